```python
import jax, jax.numpy as jnp
from jax import lax
import numpy as np

D_MODEL = 2048
BATCH = 8
SEQ = 2048
DEPTH = 2

D_MIX = D_MODEL
HEAD_DIM = 64
ATTN_WIDTH = D_MIX // 2
N_Q_HEADS = ATTN_WIDTH // HEAD_DIM
N_KV_HEADS = N_Q_HEADS // 4
KV_WIDTH = N_KV_HEADS * HEAD_DIM
WINDOW = 128
CONV_WIDTH = D_MIX // 4
CONV_KERNEL = 31
SGU_WIDTH = D_MIX // 4
SGU_HEADS = SGU_WIDTH // HEAD_DIM
CHUNK = 128
D_FF = 4 * D_MODEL
EPS = 1e-6
NEG_INF = -1e30
SPLITS = (ATTN_WIDTH,
          ATTN_WIDTH + KV_WIDTH,
          ATTN_WIDTH + 2 * KV_WIDTH,
          ATTN_WIDTH + 2 * KV_WIDTH + 2 * CONV_WIDTH)
D_IN = ATTN_WIDTH + 2 * KV_WIDTH + 2 * CONV_WIDTH + 2 * SGU_WIDTH

kernel_name = "hymba_conv_sgu_swa_hybrid"


def rms_norm(x, g):
    xf = x.astype(jnp.float32)
    y = xf * lax.rsqrt(jnp.mean(xf * xf, axis=-1, keepdims=True) + EPS)
    return (y * g.astype(jnp.float32)).astype(x.dtype)


def layer_norm(x, g, b):
    xf = x.astype(jnp.float32)
    mu = jnp.mean(xf, axis=-1, keepdims=True)
    xc = xf - mu
    y = xc * lax.rsqrt(jnp.mean(xc * xc, axis=-1, keepdims=True) + EPS)
    return (y * g.astype(jnp.float32) + b.astype(jnp.float32)).astype(x.dtype)


def sliding_window_attention(q, k, v, sinks):
    B, S = q.shape[0], q.shape[1]
    nb = S // WINDOW
    G = N_Q_HEADS // N_KV_HEADS
    qb = q.reshape(B, nb, WINDOW, N_KV_HEADS, G, HEAD_DIM)

    def with_prev(t):
        tb = t.reshape(B, nb, WINDOW, N_KV_HEADS, HEAD_DIM)
        prev = jnp.pad(tb, ((0, 0), (1, 0), (0, 0), (0, 0), (0, 0)))[:, :-1]
        return jnp.concatenate([prev, tb], axis=2)

    kb, vb = with_prev(k), with_prev(v)
    scale = HEAD_DIM ** -0.5
    logits = jnp.einsum('bnqkgd,bnskd->bnkgqs', qb, kb).astype(jnp.float32) * scale
    qi = jnp.arange(WINDOW)[None, :, None]
    sj = jnp.arange(2 * WINDOW)[None, None, :]
    blk = jnp.arange(nb)[:, None, None]
    rel = qi + WINDOW - sj
    key_pos = blk * WINDOW - WINDOW + sj
    mask = (rel >= 0) & (rel < WINDOW) & (key_pos >= 0)
    logits = jnp.where(mask[None, :, None, None], logits, NEG_INF)
    sink = sinks.astype(jnp.float32).reshape(N_KV_HEADS, G)[None, None, :, :, None, None]
    m = jnp.maximum(jnp.max(logits, axis=-1, keepdims=True), sink)
    p = jnp.exp(logits - m)
    denom = jnp.sum(p, axis=-1, keepdims=True) + jnp.exp(sink - m)
    probs = (p / denom).astype(v.dtype)
    out = jnp.einsum('bnkgqs,bnskd->bnqkgd', probs, vb)
    return out.reshape(B, S, ATTN_WIDTH)


def conv_module(xc, conv_w, conv_b, ln_g, ln_b):
    a, gate = jnp.split(xc, 2, axis=-1)
    h = a * jax.nn.sigmoid(gate)
    h = lax.conv_general_dilated(
        h, conv_w[:, None, :].astype(h.dtype), window_strides=(1,),
        padding=[(CONV_KERNEL - 1, 0)],
        dimension_numbers=('NWC', 'WIO', 'NWC'),
        feature_group_count=CONV_WIDTH) + conv_b
    h = layer_norm(h, ln_g, ln_b)
    return jax.nn.silu(h)


def spatial_gating(xs, ln_g, ln_b, w_s, b_s):
    B, S = xs.shape[0], xs.shape[1]
    u, v = jnp.split(xs, 2, axis=-1)
    v = layer_norm(v, ln_g, ln_b)
    vb = v.reshape(B, S // CHUNK, CHUNK, SGU_HEADS, HEAD_DIM)
    causal = jnp.tril(jnp.ones((CHUNK, CHUNK), dtype=bool))
    w = jnp.where(causal[None], w_s, jnp.zeros_like(w_s))
    s = jnp.einsum('hij,bnjhd->bnihd', w, vb) + b_s.T[None, None, :, :, None]
    return u * s.reshape(B, S, SGU_WIDTH)


def _fwd_setup_inputs(seed: int = 0) -> dict:
    key = jax.random.key(seed)
    ks = jax.random.split(key, 20)
    f32 = jnp.float32

    def nrm(k, shape, scale):
        return jax.random.normal(k, shape, f32) * scale

    def gain(k, shape):
        return 1.0 + 0.02 * jax.random.normal(k, shape, f32)

    return {
        "x": jax.random.normal(ks[0], (BATCH, SEQ, D_MODEL), f32),
        "ln1_g": gain(ks[1], (DEPTH, D_MODEL)),
        "w_in": nrm(ks[2], (DEPTH, D_MODEL, D_IN), D_MODEL ** -0.5),
        "q_norm_g": gain(ks[3], (DEPTH, HEAD_DIM)),
        "k_norm_g": gain(ks[4], (DEPTH, HEAD_DIM)),
        "sinks": nrm(ks[5], (DEPTH, N_Q_HEADS), 0.5),
        "conv_w": nrm(ks[6], (DEPTH, CONV_KERNEL, CONV_WIDTH), CONV_KERNEL ** -0.5),
        "conv_b": nrm(ks[7], (DEPTH, CONV_WIDTH), 0.02),
        "conv_ln_g": gain(ks[8], (DEPTH, CONV_WIDTH)),
        "conv_ln_b": nrm(ks[9], (DEPTH, CONV_WIDTH), 0.02),
        "sgu_ln_g": gain(ks[10], (DEPTH, SGU_WIDTH)),
        "sgu_ln_b": nrm(ks[11], (DEPTH, SGU_WIDTH), 0.02),
        "sgu_w": nrm(ks[12], (DEPTH, SGU_HEADS, CHUNK, CHUNK), CHUNK ** -0.5),
        "sgu_b": gain(ks[13], (DEPTH, SGU_HEADS, CHUNK)),
        "out_norm_g": gain(ks[14], (DEPTH, D_MIX)),
        "w_out": nrm(ks[15], (DEPTH, D_MIX, D_MODEL), D_MIX ** -0.5),
        "ln2_g": gain(ks[16], (DEPTH, D_MODEL)),
        "w_up": nrm(ks[17], (DEPTH, D_MODEL, D_FF), D_MODEL ** -0.5),
        "w_down": nrm(ks[18], (DEPTH, D_FF, D_MODEL), D_FF ** -0.5),
    }


def _fwd_reference(x, ln1_g, w_in, q_norm_g, k_norm_g, sinks, conv_w, conv_b, conv_ln_g,
              conv_ln_b, sgu_ln_g, sgu_ln_b, sgu_w, sgu_b, out_norm_g, w_out, ln2_g,
              w_up, w_down):
    B, S = x.shape[0], x.shape[1]
    for l in range(DEPTH):
        h = rms_norm(x, ln1_g[l])
        proj = h @ w_in[l]
        q, k, v, xc, xs = jnp.split(proj, SPLITS, axis=-1)
        q = rms_norm(q.reshape(B, S, N_Q_HEADS, HEAD_DIM), q_norm_g[l])
        k = rms_norm(k.reshape(B, S, N_KV_HEADS, HEAD_DIM), k_norm_g[l])
        v = v.reshape(B, S, N_KV_HEADS, HEAD_DIM)
        y_attn = sliding_window_attention(q, k, v, sinks[l])
        y_conv = conv_module(xc, conv_w[l], conv_b[l], conv_ln_g[l], conv_ln_b[l])
        y_sgu = spatial_gating(xs, sgu_ln_g[l], sgu_ln_b[l], sgu_w[l], sgu_b[l])
        g = out_norm_g[l]
        mix = jnp.concatenate([
            rms_norm(y_attn, g[:ATTN_WIDTH]),
            rms_norm(y_conv, g[ATTN_WIDTH:ATTN_WIDTH + CONV_WIDTH]),
            rms_norm(y_sgu, g[ATTN_WIDTH + CONV_WIDTH:]),
        ], axis=-1)
        x = x + mix @ w_out[l]
        h = rms_norm(x, ln2_g[l])
        x = x + jnp.square(jax.nn.relu(h @ w_up[l])) @ w_down[l]
    return x


import jax as _jax
import jax.numpy as _jnp

TWIN_FORMAT = 'train_step'
FWD_PARAMS = ['x', 'ln1_g', 'w_in', 'q_norm_g', 'k_norm_g', 'sinks', 'conv_w', 'conv_b', 'conv_ln_g', 'conv_ln_b', 'sgu_ln_g', 'sgu_ln_b', 'sgu_w', 'sgu_b', 'out_norm_g', 'w_out', 'ln2_g', 'w_up', 'w_down']
TWIN_WEIGHTS = ['ln1_g', 'w_in', 'q_norm_g', 'k_norm_g', 'sinks', 'conv_w', 'conv_b', 'conv_ln_g', 'conv_ln_b', 'sgu_ln_g', 'sgu_ln_b', 'sgu_w', 'sgu_b', 'out_norm_g', 'w_out', 'ln2_g', 'w_up', 'w_down']
TWIN_DIFF_INPUT = 'x'
TWIN_INPUTS = ['x', 'ln1_g', 'w_in', 'q_norm_g', 'k_norm_g', 'sinks', 'conv_w', 'conv_b', 'conv_ln_g', 'conv_ln_b', 'sgu_ln_g', 'sgu_ln_b', 'sgu_w', 'sgu_b', 'out_norm_g', 'w_out', 'ln2_g', 'w_up', 'w_down', 'loss_target', 'm_ln1_g', 'm_w_in', 'm_q_norm_g', 'm_k_norm_g', 'm_sinks', 'm_conv_w', 'm_conv_b', 'm_conv_ln_g', 'm_conv_ln_b', 'm_sgu_ln_g', 'm_sgu_ln_b', 'm_sgu_w', 'm_sgu_b', 'm_out_norm_g', 'm_w_out', 'm_ln2_g', 'm_w_up', 'm_w_down', 'v_ln1_g', 'v_w_in', 'v_q_norm_g', 'v_k_norm_g', 'v_sinks', 'v_conv_w', 'v_conv_b', 'v_conv_ln_g', 'v_conv_ln_b', 'v_sgu_ln_g', 'v_sgu_ln_b', 'v_sgu_w', 'v_sgu_b', 'v_out_norm_g', 'v_w_out', 'v_ln2_g', 'v_w_up', 'v_w_down']
TWIN_OUTPUTS = ['loss', 'grad_x', 'grad_ln1_g', 'grad_w_in', 'grad_q_norm_g', 'grad_k_norm_g', 'grad_sinks', 'grad_conv_w', 'grad_conv_b', 'grad_conv_ln_g', 'grad_conv_ln_b', 'grad_sgu_ln_g', 'grad_sgu_ln_b', 'grad_sgu_w', 'grad_sgu_b', 'grad_out_norm_g', 'grad_w_out', 'grad_ln2_g', 'grad_w_up', 'grad_w_down', 'delta_ln1_g', 'delta_w_in', 'delta_q_norm_g', 'delta_k_norm_g', 'delta_sinks', 'delta_conv_w', 'delta_conv_b', 'delta_conv_ln_g', 'delta_conv_ln_b', 'delta_sgu_ln_g', 'delta_sgu_ln_b', 'delta_sgu_w', 'delta_sgu_b', 'delta_out_norm_g', 'delta_w_out', 'delta_ln2_g', 'delta_w_up', 'delta_w_down', 'new_m_ln1_g', 'new_m_w_in', 'new_m_q_norm_g', 'new_m_k_norm_g', 'new_m_sinks', 'new_m_conv_w', 'new_m_conv_b', 'new_m_conv_ln_g', 'new_m_conv_ln_b', 'new_m_sgu_ln_g', 'new_m_sgu_ln_b', 'new_m_sgu_w', 'new_m_sgu_b', 'new_m_out_norm_g', 'new_m_w_out', 'new_m_ln2_g', 'new_m_w_up', 'new_m_w_down', 'new_v_ln1_g', 'new_v_w_in', 'new_v_q_norm_g', 'new_v_k_norm_g', 'new_v_sinks', 'new_v_conv_w', 'new_v_conv_b', 'new_v_conv_ln_g', 'new_v_conv_ln_b', 'new_v_sgu_ln_g', 'new_v_sgu_ln_b', 'new_v_sgu_w', 'new_v_sgu_b', 'new_v_out_norm_g', 'new_v_w_out', 'new_v_ln2_g', 'new_v_w_up', 'new_v_w_down']
TWIN_LEAF_KINDS = {'loss': 'loss', 'grad_x': 'grad_x', 'grad_ln1_g': 'grad_w', 'grad_w_in': 'grad_w', 'grad_q_norm_g': 'grad_w', 'grad_k_norm_g': 'grad_w', 'grad_sinks': 'grad_w', 'grad_conv_w': 'grad_w', 'grad_conv_b': 'grad_w', 'grad_conv_ln_g': 'grad_w', 'grad_conv_ln_b': 'grad_w', 'grad_sgu_ln_g': 'grad_w', 'grad_sgu_ln_b': 'grad_w', 'grad_sgu_w': 'grad_w', 'grad_sgu_b': 'grad_w', 'grad_out_norm_g': 'grad_w', 'grad_w_out': 'grad_w', 'grad_ln2_g': 'grad_w', 'grad_w_up': 'grad_w', 'grad_w_down': 'grad_w', 'delta_ln1_g': 'delta_w', 'delta_w_in': 'delta_w', 'delta_q_norm_g': 'delta_w', 'delta_k_norm_g': 'delta_w', 'delta_sinks': 'delta_w', 'delta_conv_w': 'delta_w', 'delta_conv_b': 'delta_w', 'delta_conv_ln_g': 'delta_w', 'delta_conv_ln_b': 'delta_w', 'delta_sgu_ln_g': 'delta_w', 'delta_sgu_ln_b': 'delta_w', 'delta_sgu_w': 'delta_w', 'delta_sgu_b': 'delta_w', 'delta_out_norm_g': 'delta_w', 'delta_w_out': 'delta_w', 'delta_ln2_g': 'delta_w', 'delta_w_up': 'delta_w', 'delta_w_down': 'delta_w', 'new_m_ln1_g': 'new_m', 'new_m_w_in': 'new_m', 'new_m_q_norm_g': 'new_m', 'new_m_k_norm_g': 'new_m', 'new_m_sinks': 'new_m', 'new_m_conv_w': 'new_m', 'new_m_conv_b': 'new_m', 'new_m_conv_ln_g': 'new_m', 'new_m_conv_ln_b': 'new_m', 'new_m_sgu_ln_g': 'new_m', 'new_m_sgu_ln_b': 'new_m', 'new_m_sgu_w': 'new_m', 'new_m_sgu_b': 'new_m', 'new_m_out_norm_g': 'new_m', 'new_m_w_out': 'new_m', 'new_m_ln2_g': 'new_m', 'new_m_w_up': 'new_m', 'new_m_w_down': 'new_m', 'new_v_ln1_g': 'new_v', 'new_v_w_in': 'new_v', 'new_v_q_norm_g': 'new_v', 'new_v_k_norm_g': 'new_v', 'new_v_sinks': 'new_v', 'new_v_conv_w': 'new_v', 'new_v_conv_b': 'new_v', 'new_v_conv_ln_g': 'new_v', 'new_v_conv_ln_b': 'new_v', 'new_v_sgu_ln_g': 'new_v', 'new_v_sgu_ln_b': 'new_v', 'new_v_sgu_w': 'new_v', 'new_v_sgu_b': 'new_v', 'new_v_out_norm_g': 'new_v', 'new_v_w_out': 'new_v', 'new_v_ln2_g': 'new_v', 'new_v_w_up': 'new_v', 'new_v_w_down': 'new_v'}


def _forward(args):
    return _fwd_reference(*[args[k] for k in FWD_PARAMS])


def _output_shape():
    out = _jax.eval_shape(lambda: _forward(_fwd_setup_inputs(0)))
    return out.shape, out.dtype

N_MICROBATCH = 1
ADAM_LR = 0.001
ADAM_B1 = 0.9
ADAM_B2 = 0.999
ADAM_EPS = 1e-08
ADAM_WD = 0.01
ADAM_STEP = 10
PER_EXAMPLE_BATCH_AXIS = {'x': 0, 'loss_target': 0}
SHARED_INPUTS = []
_WEIGHT_DTYPES = {'ln1_g': _jnp.float32, 'w_in': _jnp.float32, 'q_norm_g': _jnp.float32, 'k_norm_g': _jnp.float32, 'sinks': _jnp.float32, 'conv_w': _jnp.float32, 'conv_b': _jnp.float32, 'conv_ln_g': _jnp.float32, 'conv_ln_b': _jnp.float32, 'sgu_ln_g': _jnp.float32, 'sgu_ln_b': _jnp.float32, 'sgu_w': _jnp.float32, 'sgu_b': _jnp.float32, 'out_norm_g': _jnp.float32, 'w_out': _jnp.float32, 'ln2_g': _jnp.float32, 'w_up': _jnp.float32, 'w_down': _jnp.float32}
MOMENT_SCALE = {'ln1_g': 3.324660e+00, 'w_in': 2.498310e+00, 'q_norm_g': 1.288188e+00, 'k_norm_g': 1.291037e+00, 'sinks': 3.319084e-01, 'conv_w': 1.908173e+00, 'conv_b': 1.957098e+01, 'conv_ln_g': 7.677336e+00, 'conv_ln_b': 1.073025e+01, 'sgu_ln_g': 2.137819e-01, 'sgu_ln_b': 1.766068e-01, 'sgu_w': 1.243245e-01, 'sgu_b': 1.828649e-01, 'out_norm_g': 9.665348e+00, 'w_out': 3.897291e+00, 'ln2_g': 2.405082e+01, 'w_up': 1.364079e+00, 'w_down': 5.681089e+00}


def _to_microbatches(a, axis):
    t = _jnp.moveaxis(a, axis, 0)
    t = t.reshape((N_MICROBATCH, t.shape[0] // N_MICROBATCH) + t.shape[1:])
    return _jnp.moveaxis(t, 1, axis + 1)


def setup_inputs(seed: int = 0) -> dict:
    inp = _fwd_setup_inputs(seed)
    key = _jax.random.fold_in(_jax.random.key(seed), 7919)
    shape, _ = _output_shape()
    out = dict(inp)
    out["loss_target"] = _jax.random.normal(_jax.random.fold_in(key, 0), shape, _jnp.float32)
    for i, name in enumerate(TWIN_WEIGHTS):
        w = inp[name].astype(_jnp.float32)
        if MOMENT_SCALE is None:
            s = _jnp.sqrt(_jnp.mean(_jnp.square(w)) + 1e-30)
        else:
            s = MOMENT_SCALE[name]
        km, kv = _jax.random.split(_jax.random.fold_in(key, i + 1))
        out[name] = w
        out["m_" + name] = s * _jax.random.normal(km, w.shape, _jnp.float32)
        out["v_" + name] = (s * s) * _jax.random.uniform(kv, w.shape, _jnp.float32, 0.5, 1.5)
    if N_MICROBATCH > 1:
        for name, axis in PER_EXAMPLE_BATCH_AXIS.items():
            out[name] = _to_microbatches(out[name], axis)
    return {'x': out['x'], 'ln1_g': out['ln1_g'], 'w_in': out['w_in'], 'q_norm_g': out['q_norm_g'], 'k_norm_g': out['k_norm_g'], 'sinks': out['sinks'], 'conv_w': out['conv_w'], 'conv_b': out['conv_b'], 'conv_ln_g': out['conv_ln_g'], 'conv_ln_b': out['conv_ln_b'], 'sgu_ln_g': out['sgu_ln_g'], 'sgu_ln_b': out['sgu_ln_b'], 'sgu_w': out['sgu_w'], 'sgu_b': out['sgu_b'], 'out_norm_g': out['out_norm_g'], 'w_out': out['w_out'], 'ln2_g': out['ln2_g'], 'w_up': out['w_up'], 'w_down': out['w_down'], 'loss_target': out['loss_target'], 'm_ln1_g': out['m_ln1_g'], 'm_w_in': out['m_w_in'], 'm_q_norm_g': out['m_q_norm_g'], 'm_k_norm_g': out['m_k_norm_g'], 'm_sinks': out['m_sinks'], 'm_conv_w': out['m_conv_w'], 'm_conv_b': out['m_conv_b'], 'm_conv_ln_g': out['m_conv_ln_g'], 'm_conv_ln_b': out['m_conv_ln_b'], 'm_sgu_ln_g': out['m_sgu_ln_g'], 'm_sgu_ln_b': out['m_sgu_ln_b'], 'm_sgu_w': out['m_sgu_w'], 'm_sgu_b': out['m_sgu_b'], 'm_out_norm_g': out['m_out_norm_g'], 'm_w_out': out['m_w_out'], 'm_ln2_g': out['m_ln2_g'], 'm_w_up': out['m_w_up'], 'm_w_down': out['m_w_down'], 'v_ln1_g': out['v_ln1_g'], 'v_w_in': out['v_w_in'], 'v_q_norm_g': out['v_q_norm_g'], 'v_k_norm_g': out['v_k_norm_g'], 'v_sinks': out['v_sinks'], 'v_conv_w': out['v_conv_w'], 'v_conv_b': out['v_conv_b'], 'v_conv_ln_g': out['v_conv_ln_g'], 'v_conv_ln_b': out['v_conv_ln_b'], 'v_sgu_ln_g': out['v_sgu_ln_g'], 'v_sgu_ln_b': out['v_sgu_ln_b'], 'v_sgu_w': out['v_sgu_w'], 'v_sgu_b': out['v_sgu_b'], 'v_out_norm_g': out['v_out_norm_g'], 'v_w_out': out['v_w_out'], 'v_ln2_g': out['v_ln2_g'], 'v_w_up': out['v_w_up'], 'v_w_down': out['v_w_down']}


def _loss(weights, diff, rest, loss_target):
    with _jax.named_scope("forward"):
        args = {**rest, TWIN_DIFF_INPUT: diff, **{k: w.astype(_WEIGHT_DTYPES[k]) for k, w in weights.items()}}
        y = _forward(args)
    with _jax.named_scope("loss_head"):
        err = _jnp.square(y.astype(_jnp.float32) - loss_target)
        return 0.5 * _jnp.sum(_jnp.mean(err, axis=-1)) if err.ndim else 0.5 * err


def _adamw(w, g, m, v):
    m = ADAM_B1 * m + (1.0 - ADAM_B1) * g
    v = ADAM_B2 * v + (1.0 - ADAM_B2) * _jnp.square(g)
    m_hat = m / (1.0 - ADAM_B1 ** ADAM_STEP)
    v_hat = v / (1.0 - ADAM_B2 ** ADAM_STEP)
    delta = -ADAM_LR * (m_hat / (_jnp.sqrt(v_hat) + ADAM_EPS) + ADAM_WD * w)
    return delta, m, v


def reference(x, ln1_g, w_in, q_norm_g, k_norm_g, sinks, conv_w, conv_b, conv_ln_g, conv_ln_b, sgu_ln_g, sgu_ln_b, sgu_w, sgu_b, out_norm_g, w_out, ln2_g, w_up, w_down, loss_target, m_ln1_g, m_w_in, m_q_norm_g, m_k_norm_g, m_sinks, m_conv_w, m_conv_b, m_conv_ln_g, m_conv_ln_b, m_sgu_ln_g, m_sgu_ln_b, m_sgu_w, m_sgu_b, m_out_norm_g, m_w_out, m_ln2_g, m_w_up, m_w_down, v_ln1_g, v_w_in, v_q_norm_g, v_k_norm_g, v_sinks, v_conv_w, v_conv_b, v_conv_ln_g, v_conv_ln_b, v_sgu_ln_g, v_sgu_ln_b, v_sgu_w, v_sgu_b, v_out_norm_g, v_w_out, v_ln2_g, v_w_up, v_w_down):
    given = dict(x=x, ln1_g=ln1_g, w_in=w_in, q_norm_g=q_norm_g, k_norm_g=k_norm_g, sinks=sinks, conv_w=conv_w, conv_b=conv_b, conv_ln_g=conv_ln_g, conv_ln_b=conv_ln_b, sgu_ln_g=sgu_ln_g, sgu_ln_b=sgu_ln_b, sgu_w=sgu_w, sgu_b=sgu_b, out_norm_g=out_norm_g, w_out=w_out, ln2_g=ln2_g, w_up=w_up, w_down=w_down, loss_target=loss_target, m_ln1_g=m_ln1_g, m_w_in=m_w_in, m_q_norm_g=m_q_norm_g, m_k_norm_g=m_k_norm_g, m_sinks=m_sinks, m_conv_w=m_conv_w, m_conv_b=m_conv_b, m_conv_ln_g=m_conv_ln_g, m_conv_ln_b=m_conv_ln_b, m_sgu_ln_g=m_sgu_ln_g, m_sgu_ln_b=m_sgu_ln_b, m_sgu_w=m_sgu_w, m_sgu_b=m_sgu_b, m_out_norm_g=m_out_norm_g, m_w_out=m_w_out, m_ln2_g=m_ln2_g, m_w_up=m_w_up, m_w_down=m_w_down, v_ln1_g=v_ln1_g, v_w_in=v_w_in, v_q_norm_g=v_q_norm_g, v_k_norm_g=v_k_norm_g, v_sinks=v_sinks, v_conv_w=v_conv_w, v_conv_b=v_conv_b, v_conv_ln_g=v_conv_ln_g, v_conv_ln_b=v_conv_ln_b, v_sgu_ln_g=v_sgu_ln_g, v_sgu_ln_b=v_sgu_ln_b, v_sgu_w=v_sgu_w, v_sgu_b=v_sgu_b, v_out_norm_g=v_out_norm_g, v_w_out=v_w_out, v_ln2_g=v_ln2_g, v_w_up=v_w_up, v_w_down=v_w_down)
    weights = {n: given[n] for n in TWIN_WEIGHTS}
    shared = {n: given[n] for n in SHARED_INPUTS}
    per_example = {n: given[n] for n in ['x']}
    grad_fn = _jax.value_and_grad(_loss, argnums=(0, 1))

    def one_microbatch(ex, loss_target):
        ex = dict(ex)
        diff = ex.pop(TWIN_DIFF_INPUT)
        return grad_fn(weights, diff, {**shared, **ex}, loss_target)

    if N_MICROBATCH == 1:
        loss, (grad_w, grad_x) = one_microbatch(per_example, given["loss_target"])
    else:
        def body(carry, xs):
            loss_sum, grad_sum = carry
            l_k, (gw_k, gx_k) = one_microbatch(xs[0], xs[1])
            with _jax.named_scope("update"):
                return (loss_sum + l_k, _jax.tree.map(_jnp.add, grad_sum, gw_k)), gx_k

        init = (_jnp.zeros((), _jnp.float32), _jax.tree.map(_jnp.zeros_like, weights))
        (loss, grad_w), grad_x = _jax.lax.scan(body, init, (per_example, given["loss_target"]))
    with _jax.named_scope("update"):
        delta_w, new_m, new_v = {}, {}, {}
        for n in TWIN_WEIGHTS:
            delta_w[n], new_m[n], new_v[n] = _adamw(weights[n], grad_w[n], given["m_" + n], given["v_" + n])
    return (loss, grad_x, *[grad_w[n] for n in TWIN_WEIGHTS], *[delta_w[n] for n in TWIN_WEIGHTS],
            *[new_m[n] for n in TWIN_WEIGHTS], *[new_v[n] for n in TWIN_WEIGHTS])
```

```python
import functools

import jax
import jax.numpy as jnp
from jax import lax
from jax.experimental import pallas as pl
from jax.experimental.pallas import tpu as pltpu

F32 = jnp.float32
BF16 = jnp.bfloat16
EPS = 1e-6
NEG_INF = -1e30
HEAD_DIM = 64
WINDOW = 128
CONV_KERNEL = 31
HALO = 32
GQA = 4
N_CHIPS = 4
ADAM_LR, ADAM_B1, ADAM_B2, ADAM_EPS, ADAM_WD, ADAM_STEP = 0.001, 0.9, 0.999, 1e-08, 0.01, 10
VMEM_LIMIT = 56 * 1024 * 1024
MESH = pl.DeviceIdType.MESH
ANY = pl.BlockSpec(memory_space=pl.ANY)

NN = (((1,), (0,)), ((), ()))
NT = (((1,), (1,)), ((), ()))
TN = (((0,), (0,)), ((), ()))


def _cp(*sem):
    return pltpu.CompilerParams(dimension_semantics=sem, vmem_limit_bytes=VMEM_LIMIT)


def _tile(dim, pref):
    if dim <= pref:
        return dim
    for t in range(pref, 0, -128):
        if dim % t == 0:
            return t
    while dim % pref:
        pref //= 2
    return pref


def _dot(a, b, dims=NN):
    return lax.dot_general(a, b, dims, preferred_element_type=F32)


def _colsum(v):
    return jnp.sum(v, axis=0, keepdims=True)


def _sigmoid(x):
    return 1.0 / (1.0 + jnp.exp(-x))


def _matmul(name, operands, in_specs, out_shape, out_specs, grid, dims, acc_shape, epilogue):
    nk = grid[2]
    n_in = len(operands)

    def body(*refs):
        a_ref, b_ref = refs[0], refs[1]
        extra = refs[2:n_in]
        outs = refs[n_in:-1]
        acc = refs[-1]
        k = pl.program_id(2)

        @pl.when(k == 0)
        def _():
            acc[...] = jnp.zeros_like(acc)

        acc[...] += _dot(a_ref[...], b_ref[...], dims)

        @pl.when(k == nk - 1)
        def _():
            epilogue(acc[...], extra, outs)

    return pl.pallas_call(
        body, name=name, grid=grid, in_specs=in_specs, out_specs=out_specs, out_shape=out_shape,
        scratch_shapes=[pltpu.VMEM(acc_shape, F32)],
        compiler_params=_cp("parallel", "parallel", "arbitrary"),
    )(*operands)


def _ep_store(acc, extra, outs):
    outs[0][...] = acc.astype(outs[0].dtype)


def _ep_residual(acc, extra, outs):
    outs[0][...] = extra[0][...] + acc


def _ep_up(acc, extra, outs):
    outs[0][...] = acc.astype(BF16)
    r = jnp.maximum(acc, 0.0)
    outs[1][...] = (r * r).astype(BF16)


def _ep_dup(acc, extra, outs):
    outs[0][...] = (acc * (2.0 * jnp.maximum(extra[0][...].astype(F32), 0.0))).astype(BF16)


def _mm_act_w(name, a, wg, col_sharded, epilogue, extra=(), out_dtypes=(F32,)):
    m, kdim = a.shape
    _, r, c = wg.shape
    tm = _tile(m, 1024)
    if col_sharded:
        n = N_CHIPS * c
        tn = _tile(c, 1024)
        tk = _tile(kdim, 1024)
        per = c // tn
        b_spec = pl.BlockSpec((None, tk, tn), lambda i, j, k: (j // per, k, j % per))
    else:
        n = c
        tn = _tile(n, 1024)
        tk = _tile(r, 1024)
        per = r // tk
        b_spec = pl.BlockSpec((None, tk, tn), lambda i, j, k: (k // per, k % per, j))
    grid = (m // tm, n // tn, kdim // tk)
    o_spec = pl.BlockSpec((tm, tn), lambda i, j, k: (i, j))
    in_specs = [pl.BlockSpec((tm, tk), lambda i, j, k: (i, k)), b_spec] + [o_spec] * len(extra)
    return _matmul(name, (a, wg) + tuple(extra), in_specs,
                   tuple(jax.ShapeDtypeStruct((m, n), d) for d in out_dtypes),
                   tuple(o_spec for _ in out_dtypes), grid, NN, (tm, tn), epilogue)


def _mm_act_wt(name, a, wg, col_sharded, epilogue, extra=(), out_dtypes=(F32,)):
    m, kdim = a.shape
    _, r, c = wg.shape
    tm = _tile(m, 1024)
    if col_sharded:
        n = r
        tn = _tile(n, 1024)
        tk = _tile(c, 1024)
        per = c // tk
        b_spec = pl.BlockSpec((None, tn, tk), lambda i, j, k: (k // per, j, k % per))
    else:
        n = N_CHIPS * r
        tn = _tile(r, 1024)
        tk = _tile(c, 1024)
        per = r // tn
        b_spec = pl.BlockSpec((None, tn, tk), lambda i, j, k: (j // per, j % per, k))
    grid = (m // tm, n // tn, kdim // tk)
    o_spec = pl.BlockSpec((tm, tn), lambda i, j, k: (i, j))
    in_specs = [pl.BlockSpec((tm, tk), lambda i, j, k: (i, k)), b_spec] + [o_spec] * len(extra)
    return _matmul(name, (a, wg) + tuple(extra), in_specs,
                   tuple(jax.ShapeDtypeStruct((m, n), d) for d in out_dtypes),
                   tuple(o_spec for _ in out_dtypes), grid, NT, (tm, tn), epilogue)


def _mm_wgrad(name, a, g, col_sharded, c):
    s, kdim = a.shape
    _, n = g.shape
    ts = _tile(s, 1024)
    if col_sharded:
        r = kdim
        tm = _tile(kdim, 1024)
        tn = _tile(c, 1024)
        per = c // tn
        o_spec = pl.BlockSpec((None, tm, tn), lambda i, j, k: (j // per, i, j % per))
    else:
        r = kdim // N_CHIPS
        tm = _tile(r, 1024)
        tn = _tile(c, 1024)
        per = r // tm
        o_spec = pl.BlockSpec((None, tm, tn), lambda i, j, k: (i // per, i % per, j))
    grid = (kdim // tm, n // tn, s // ts)
    in_specs = [pl.BlockSpec((ts, tm), lambda i, j, k: (k, i)), pl.BlockSpec((ts, tn), lambda i, j, k: (k, j))]
    return _matmul(name, (a, g), in_specs, (jax.ShapeDtypeStruct((N_CHIPS, r, c), BF16),), (o_spec,),
                   grid, TN, (tm, tn), _ep_store)[0]


def _rms_fwd(name, x, g):
    s, d = x.shape
    tb = _tile(s, 256)

    def body(x_ref, g_ref, o_ref):
        xv = x_ref[...]
        r = lax.rsqrt(jnp.mean(xv * xv, axis=-1, keepdims=True) + EPS)
        o_ref[...] = (xv * r * g_ref[...]).astype(BF16)

    return pl.pallas_call(
        body, name=name, grid=(s // tb,),
        in_specs=[pl.BlockSpec((tb, d), lambda i: (i, 0)), pl.BlockSpec((1, d), lambda i: (0, 0))],
        out_specs=pl.BlockSpec((tb, d), lambda i: (i, 0)),
        out_shape=jax.ShapeDtypeStruct((s, d), BF16), compiler_params=_cp("parallel"),
    )(x, g.reshape(1, d))


def _rms_bwd(name, dh, x, g, dres):
    s, d = x.shape
    tb = _tile(s, 256)

    def body(dh_ref, x_ref, g_ref, dres_ref, dx_ref, dxb_ref, dg_ref):
        i = pl.program_id(0)
        xv = x_ref[...]
        r = lax.rsqrt(jnp.mean(xv * xv, axis=-1, keepdims=True) + EPS)
        xhat = xv * r
        dhv = dh_ref[...]
        dxhat = dhv * g_ref[...]
        dx = dres_ref[...] + r * (dxhat - xhat * jnp.mean(dxhat * xhat, axis=-1, keepdims=True))
        dx_ref[...] = dx
        dxb_ref[...] = dx.astype(BF16)

        @pl.when(i == 0)
        def _():
            dg_ref[...] = jnp.zeros_like(dg_ref)

        dg_ref[...] += _colsum(dhv * xhat)

    row = pl.BlockSpec((tb, d), lambda i: (i, 0))
    vec = pl.BlockSpec((1, d), lambda i: (0, 0))
    return pl.pallas_call(
        body, name=name, grid=(s // tb,), in_specs=[row, row, vec, row], out_specs=(row, row, vec),
        out_shape=(jax.ShapeDtypeStruct((s, d), F32), jax.ShapeDtypeStruct((s, d), BF16),
                   jax.ShapeDtypeStruct((1, d), F32)),
        compiler_params=_cp("arbitrary"),
    )(dh, x, g.reshape(1, d), dres)


def _loss_head(y, t):
    s, d = y.shape
    tb = _tile(s, 256)

    def body(y_ref, t_ref, dy_ref, dyb_ref, loss_ref, acc):
        i = pl.program_id(0)
        e = y_ref[...] - t_ref[...]
        dy = e * (1.0 / d)
        dy_ref[...] = dy
        dyb_ref[...] = dy.astype(BF16)

        @pl.when(i == 0)
        def _():
            acc[...] = jnp.zeros_like(acc)

        acc[...] += _colsum(e * e)

        @pl.when(i == pl.num_programs(0) - 1)
        def _():
            loss_ref[...] = jnp.sum(acc[...], axis=-1, keepdims=True) * (0.5 / d)

    row = pl.BlockSpec((tb, d), lambda i: (i, 0))
    return pl.pallas_call(
        body, name="loss_head", grid=(s // tb,), in_specs=[row, row],
        out_specs=(row, row, pl.BlockSpec((1, 1), lambda i: (0, 0))),
        out_shape=(jax.ShapeDtypeStruct((s, d), F32), jax.ShapeDtypeStruct((s, d), BF16),
                   jax.ShapeDtypeStruct((1, 1), F32)),
        scratch_shapes=[pltpu.VMEM((1, d), F32)], compiler_params=_cp("arbitrary"),
    )(y, t)


def _mixnorm_fwd(name, ya, yc, ys, g):
    s, aw = ya.shape
    cw, sw = yc.shape[1], ys.shape[1]
    d = aw + cw + sw
    tb = _tile(s, 256)

    def body(ya_ref, yc_ref, ys_ref, g_ref, o_ref):
        off = 0
        for ref, w in ((ya_ref, aw), (yc_ref, cw), (ys_ref, sw)):
            v = ref[...]
            r = lax.rsqrt(jnp.mean(v * v, axis=-1, keepdims=True) + EPS)
            o_ref[:, off:off + w] = (v * r * g_ref[:, off:off + w]).astype(BF16)
            off += w

    def row(w):
        return pl.BlockSpec((tb, w), lambda i: (i, 0))

    return pl.pallas_call(
        body, name=name, grid=(s // tb,),
        in_specs=[row(aw), row(cw), row(sw), pl.BlockSpec((1, d), lambda i: (0, 0))], out_specs=row(d),
        out_shape=jax.ShapeDtypeStruct((s, d), BF16), compiler_params=_cp("parallel"),
    )(ya, yc, ys, g.reshape(1, d))


def _mixnorm_bwd(name, dmix, ya, yc, ys, g):
    s, aw = ya.shape
    cw, sw = yc.shape[1], ys.shape[1]
    d = aw + cw + sw
    tb = _tile(s, 256)

    def body(dm_ref, ya_ref, yc_ref, ys_ref, g_ref, dya_ref, dyc_ref, dys_ref, dg_ref):
        i = pl.program_id(0)

        @pl.when(i == 0)
        def _():
            dg_ref[...] = jnp.zeros_like(dg_ref)

        off = 0
        for ref, dref, w in ((ya_ref, dya_ref, aw), (yc_ref, dyc_ref, cw), (ys_ref, dys_ref, sw)):
            v = ref[...]
            r = lax.rsqrt(jnp.mean(v * v, axis=-1, keepdims=True) + EPS)
            vhat = v * r
            dm = dm_ref[:, off:off + w]
            dvhat = dm * g_ref[:, off:off + w]
            dref[...] = r * (dvhat - vhat * jnp.mean(dvhat * vhat, axis=-1, keepdims=True))
            dg_ref[:, off:off + w] += _colsum(dm * vhat)
            off += w

    def row(w):
        return pl.BlockSpec((tb, w), lambda i: (i, 0))

    vec = pl.BlockSpec((1, d), lambda i: (0, 0))
    return pl.pallas_call(
        body, name=name, grid=(s // tb,), in_specs=[row(d), row(aw), row(cw), row(sw), vec],
        out_specs=(row(aw), row(cw), row(sw), vec),
        out_shape=(jax.ShapeDtypeStruct((s, aw), F32), jax.ShapeDtypeStruct((s, cw), F32),
                   jax.ShapeDtypeStruct((s, sw), F32), jax.ShapeDtypeStruct((1, d), F32)),
        compiler_params=_cp("arbitrary"),
    )(dmix, ya, yc, ys, g.reshape(1, d))


def _head_rms(x, g):
    r = lax.rsqrt(jnp.mean(x * x, axis=-1, keepdims=True) + EPS)
    return x * r, r


def _attn_mask(n):
    qi = lax.broadcasted_iota(jnp.int32, (WINDOW, 2 * WINDOW), 0)
    sj = lax.broadcasted_iota(jnp.int32, (WINDOW, 2 * WINDOW), 1)
    rel = qi + WINDOW - sj
    return (rel >= 0) & (rel < WINDOW) & ((sj >= WINDOW) | (n > 0))


def _attn_specs(nb):
    qspec = pl.BlockSpec((GQA, WINDOW, HEAD_DIM), lambda g, n: (g, n, 0))
    cur = pl.BlockSpec((None, WINDOW, HEAD_DIM), lambda g, n: (g, n, 0))
    prev = pl.BlockSpec((None, WINDOW, HEAD_DIM), lambda g, n: (g, jnp.maximum(n - 1, 0), 0))
    nxt = pl.BlockSpec((None, WINDOW, HEAD_DIM), lambda g, n: (g, jnp.minimum(n + 1, nb - 1), 0))
    gain = pl.BlockSpec((1, HEAD_DIM), lambda g, n: (0, 0))
    sink = pl.BlockSpec((GQA, 1, 128), lambda g, n: (g, 0, 0))
    return qspec, cur, prev, nxt, gain, sink


def _attn_probs(qn_b, kn_b, valid, sink):
    logits = _dot(qn_b, kn_b, NT) * (HEAD_DIM ** -0.5)
    logits = jnp.where(valid, logits, NEG_INF)
    m = jnp.maximum(jnp.max(logits, axis=-1, keepdims=True), sink)
    p = jnp.exp(logits - m)
    es = jnp.exp(sink - m)
    denom = jnp.sum(p, axis=-1, keepdims=True) + es
    return p / denom, es / denom


def _attn_fwd(name, q, k, v, gq, gk, sinks_b):
    nq, s, _ = q.shape
    nkv = k.shape[0]
    nb = s // WINDOW
    qspec, cur, prev, _, gain, sink = _attn_specs(nb)

    def body(q_ref, kc_ref, kp_ref, vc_ref, vp_ref, gq_ref, gk_ref, s_ref, o_ref):
        n = pl.program_id(1)
        gkv = gk_ref[...]
        kn = jnp.concatenate([_head_rms(kp_ref[...], gkv)[0] * gkv, _head_rms(kc_ref[...], gkv)[0] * gkv], axis=0)
        kn_b = kn.astype(BF16)
        vv_b = jnp.concatenate([vp_ref[...], vc_ref[...]], axis=0).astype(BF16)
        valid = _attn_mask(n)
        for i in range(GQA):
            qn_b = (_head_rms(q_ref[i], gq_ref[...])[0] * gq_ref[...]).astype(BF16)
            probs, _ = _attn_probs(qn_b, kn_b, valid, s_ref[i][:, :1])
            o_ref[i] = _dot(probs.astype(BF16), vv_b)

    return pl.pallas_call(
        body, name=name, grid=(nkv, nb), in_specs=[qspec, cur, prev, cur, prev, gain, gain, sink], out_specs=qspec,
        out_shape=jax.ShapeDtypeStruct((nq, s, HEAD_DIM), F32), compiler_params=_cp("parallel", "parallel"),
    )(q, k, k, v, v, gq.reshape(1, HEAD_DIM), gk.reshape(1, HEAD_DIM), sinks_b)


def _attn_bwd(name, q, k, v, gq, gk, sinks_b, do):
    nq, s, _ = q.shape
    nkv = k.shape[0]
    nb = s // WINDOW
    qspec, cur, prev, _, gain, sink = _attn_specs(nb)

    def body(q_ref, kc_ref, kp_ref, vc_ref, vp_ref, gq_ref, gk_ref, s_ref, do_ref,
             dq_ref, dkc_ref, dkp_ref, dvc_ref, dvp_ref, dgq_ref, ds_ref):
        g = pl.program_id(0)
        n = pl.program_id(1)

        @pl.when((g == 0) & (n == 0))
        def _():
            dgq_ref[...] = jnp.zeros_like(dgq_ref)

        @pl.when(n == 0)
        def _():
            ds_ref[...] = jnp.zeros_like(ds_ref)

        gkv = gk_ref[...]
        gqv = gq_ref[...]
        kn_b = jnp.concatenate([_head_rms(kp_ref[...], gkv)[0] * gkv, _head_rms(kc_ref[...], gkv)[0] * gkv],
                               axis=0).astype(BF16)
        vv_b = jnp.concatenate([vp_ref[...], vc_ref[...]], axis=0).astype(BF16)
        valid = _attn_mask(n)
        dkn = jnp.zeros((2 * WINDOW, HEAD_DIM), F32)
        dvv = jnp.zeros((2 * WINDOW, HEAD_DIM), F32)
        dgq = jnp.zeros((1, HEAD_DIM), F32)
        for i in range(GQA):
            qhat, r = _head_rms(q_ref[i], gqv)
            qn_b = (qhat * gqv).astype(BF16)
            probs, psink = _attn_probs(qn_b, kn_b, valid, s_ref[i][:, :1])
            do_b = do_ref[i].astype(BF16)
            dp = _dot(do_b, vv_b, NT)
            delta = jnp.sum(probs * dp, axis=-1, keepdims=True)
            dl_b = (probs * (dp - delta) * (HEAD_DIM ** -0.5)).astype(BF16)
            ds_ref[i] += jnp.broadcast_to(-jnp.sum(psink * delta, axis=0, keepdims=True), (1, 128))
            dqn = _dot(dl_b, kn_b)
            dkn += _dot(dl_b, qn_b, TN)
            dvv += _dot(probs.astype(BF16), do_b, TN)
            dgq += _colsum(dqn * qhat)
            dqhat = dqn * gqv
            dq_ref[i] = r * (dqhat - qhat * jnp.mean(dqhat * qhat, axis=-1, keepdims=True))
        dgq_ref[...] += dgq
        dkp_ref[...] = dkn[:WINDOW]
        dkc_ref[...] = dkn[WINDOW:]
        dvp_ref[...] = dvv[:WINDOW]
        dvc_ref[...] = dvv[WINDOW:]

    kv_shape = jax.ShapeDtypeStruct((nkv, s, HEAD_DIM), F32)
    return pl.pallas_call(
        body, name=name, grid=(nkv, nb), in_specs=[qspec, cur, prev, cur, prev, gain, gain, sink, qspec],
        out_specs=(qspec, cur, cur, cur, cur, gain, sink),
        out_shape=(jax.ShapeDtypeStruct((nq, s, HEAD_DIM), F32), kv_shape, kv_shape, kv_shape, kv_shape,
                   jax.ShapeDtypeStruct((1, HEAD_DIM), F32), jax.ShapeDtypeStruct((nq, 1, 128), F32)),
        compiler_params=_cp("arbitrary", "arbitrary"),
    )(q, k, k, v, v, gq.reshape(1, HEAD_DIM), gk.reshape(1, HEAD_DIM), sinks_b, do)


def _attn_bwd_kv(name, k, gk, dkc, dkp, dvc, dvp):
    nkv, s, _ = k.shape
    nb = s // WINDOW
    _, cur, _, nxt, gain, _ = _attn_specs(nb)

    def body(k_ref, gk_ref, dkc_ref, dkp_ref, dvc_ref, dvp_ref, dk_ref, dv_ref, dgk_ref):
        g = pl.program_id(0)
        n = pl.program_id(1)

        @pl.when((g == 0) & (n == 0))
        def _():
            dgk_ref[...] = jnp.zeros_like(dgk_ref)

        has_next = n < nb - 1
        dkn = dkc_ref[...] + jnp.where(has_next, dkp_ref[...], 0.0)
        dv_ref[...] = dvc_ref[...] + jnp.where(has_next, dvp_ref[...], 0.0)
        khat, r = _head_rms(k_ref[...], gk_ref[...])
        dgk_ref[...] += _colsum(dkn * khat)
        dkhat = dkn * gk_ref[...]
        dk_ref[...] = r * (dkhat - khat * jnp.mean(dkhat * khat, axis=-1, keepdims=True))

    kv_shape = jax.ShapeDtypeStruct((nkv, s, HEAD_DIM), F32)
    return pl.pallas_call(
        body, name=name, grid=(nkv, nb), in_specs=[cur, gain, cur, nxt, cur, nxt], out_specs=(cur, cur, gain),
        out_shape=(kv_shape, kv_shape, jax.ShapeDtypeStruct((1, HEAD_DIM), F32)),
        compiler_params=_cp("arbitrary", "arbitrary"),
    )(k, gk.reshape(1, HEAD_DIM), dkc, dkp, dvc, dvp)


def _conv_recompute(i, a_ref, gt_ref, ap_ref, gp_ref, w_ref, b_ref, hbuf, tb):
    hbuf[pl.ds(HALO, tb), :] = a_ref[...] * _sigmoid(gt_ref[...])
    tail = ap_ref[pl.ds(tb - HALO, HALO), :] * _sigmoid(gp_ref[pl.ds(tb - HALO, HALO), :])
    hbuf[pl.ds(0, HALO), :] = jnp.where(i > 0, tail, 0.0)
    acc = jnp.broadcast_to(b_ref[...], a_ref.shape)
    for kk in range(CONV_KERNEL):
        acc = acc + w_ref[pl.ds(kk, 1), :] * hbuf[pl.ds(HALO - (CONV_KERNEL - 1) + kk, tb), :]
    return acc


def _layer_norm_stats(c):
    mu = jnp.mean(c, axis=-1, keepdims=True)
    xc = c - mu
    r = lax.rsqrt(jnp.mean(xc * xc, axis=-1, keepdims=True) + EPS)
    return xc * r, r


def _conv_specs(s, cw, tb, a_blk):
    cur = lambda off: pl.BlockSpec((tb, cw), lambda i: (i, a_blk + off))
    prev = lambda off: pl.BlockSpec((tb, cw), lambda i: (jnp.maximum(i - 1, 0), a_blk + off))
    wspec = pl.BlockSpec((HALO, cw), lambda i: (0, 0))
    vec = pl.BlockSpec((1, cw), lambda i: (0, 0))
    row = pl.BlockSpec((tb, cw), lambda i: (i, 0))
    return cur, prev, wspec, vec, row


def _conv_fwd(name, proj, a_blk, w, b, lg, lb):
    s = proj.shape[0]
    cw = w.shape[1]
    tb = _tile(s, 256)
    cur, prev, wspec, vec, row = _conv_specs(s, cw, tb, a_blk)

    def body(a_ref, gt_ref, ap_ref, gp_ref, w_ref, b_ref, lg_ref, lb_ref, y_ref, hbuf):
        c = _conv_recompute(pl.program_id(0), a_ref, gt_ref, ap_ref, gp_ref, w_ref, b_ref, hbuf, tb)
        chat, _ = _layer_norm_stats(c)
        z = chat * lg_ref[...] + lb_ref[...]
        y_ref[...] = z * _sigmoid(z)

    return pl.pallas_call(
        body, name=name, grid=(s // tb,), in_specs=[cur(0), cur(1), prev(0), prev(1), wspec, vec, vec, vec],
        out_specs=row, out_shape=jax.ShapeDtypeStruct((s, cw), F32),
        scratch_shapes=[pltpu.VMEM((tb + HALO, cw), F32)], compiler_params=_cp("arbitrary"),
    )(proj, proj, proj, proj, w, b, lg, lb)


def _conv_bwd1(name, proj, a_blk, w, b, lg, lb, dy):
    s = proj.shape[0]
    cw = w.shape[1]
    tb = _tile(s, 256)
    cur, prev, wspec, vec, row = _conv_specs(s, cw, tb, a_blk)

    def body(a_ref, gt_ref, ap_ref, gp_ref, w_ref, b_ref, lg_ref, lb_ref, dy_ref,
             dc_ref, dw_ref, db_ref, dlg_ref, dlb_ref, hbuf):
        i = pl.program_id(0)

        @pl.when(i == 0)
        def _():
            dw_ref[...] = jnp.zeros_like(dw_ref)
            db_ref[...] = jnp.zeros_like(db_ref)
            dlg_ref[...] = jnp.zeros_like(dlg_ref)
            dlb_ref[...] = jnp.zeros_like(dlb_ref)

        c = _conv_recompute(i, a_ref, gt_ref, ap_ref, gp_ref, w_ref, b_ref, hbuf, tb)
        chat, r = _layer_norm_stats(c)
        z = chat * lg_ref[...] + lb_ref[...]
        sg = _sigmoid(z)
        dz = dy_ref[...] * (sg + z * sg * (1.0 - sg))
        dlg_ref[...] += _colsum(dz * chat)
        dlb_ref[...] += _colsum(dz)
        dzg = dz * lg_ref[...]
        dc = r * (dzg - jnp.mean(dzg, axis=-1, keepdims=True) - chat * jnp.mean(dzg * chat, axis=-1, keepdims=True))
        dc_ref[...] = dc
        db_ref[...] += _colsum(dc)
        for kk in range(CONV_KERNEL):
            dw_ref[pl.ds(kk, 1), :] += _colsum(dc * hbuf[pl.ds(HALO - (CONV_KERNEL - 1) + kk, tb), :])

    return pl.pallas_call(
        body, name=name, grid=(s // tb,), in_specs=[cur(0), cur(1), prev(0), prev(1), wspec, vec, vec, vec, row],
        out_specs=(row, wspec, vec, vec, vec),
        out_shape=(jax.ShapeDtypeStruct((s, cw), F32), jax.ShapeDtypeStruct((HALO, cw), F32),
                   jax.ShapeDtypeStruct((1, cw), F32), jax.ShapeDtypeStruct((1, cw), F32),
                   jax.ShapeDtypeStruct((1, cw), F32)),
        scratch_shapes=[pltpu.VMEM((tb + HALO, cw), F32)], compiler_params=_cp("arbitrary"),
    )(proj, proj, proj, proj, w, b, lg, lb, dy)


def _conv_bwd2(name, proj, a_blk, w, dc):
    s = proj.shape[0]
    cw = w.shape[1]
    tb = _tile(s, 256)
    nblk = s // tb
    cur, _, wspec, _, row = _conv_specs(s, cw, tb, a_blk)
    nxt = pl.BlockSpec((tb, cw), lambda i: (jnp.minimum(i + 1, nblk - 1), 0))

    def body(a_ref, gt_ref, w_ref, dc_ref, dn_ref, o_ref, dbuf):
        i = pl.program_id(0)
        dbuf[pl.ds(0, tb), :] = dc_ref[...]
        dbuf[pl.ds(tb, HALO), :] = jnp.where(i < nblk - 1, dn_ref[pl.ds(0, HALO), :], 0.0)
        dh = jnp.zeros((tb, cw), F32)
        for kk in range(CONV_KERNEL):
            dh = dh + w_ref[pl.ds(kk, 1), :] * dbuf[pl.ds(CONV_KERNEL - 1 - kk, tb), :]
        sg = _sigmoid(gt_ref[...])
        o_ref[:, 0:cw] = (dh * sg).astype(BF16)
        o_ref[:, cw:2 * cw] = (dh * a_ref[...] * sg * (1.0 - sg)).astype(BF16)

    return pl.pallas_call(
        body, name=name, grid=(nblk,), in_specs=[cur(0), cur(1), wspec, row, nxt],
        out_specs=pl.BlockSpec((tb, 2 * cw), lambda i: (i, 0)), out_shape=jax.ShapeDtypeStruct((s, 2 * cw), BF16),
        scratch_shapes=[pltpu.VMEM((tb + HALO, cw), F32)], compiler_params=_cp("arbitrary"),
    )(proj, proj, w, dc, dc)


def _sgu_common(v_ref, lg_ref, lb_ref, w_ref, bexp_ref, sw):
    vhat, r = _layer_norm_stats(v_ref[...])
    vn_b = (vhat * lg_ref[...] + lb_ref[...]).astype(BF16)
    ii = lax.broadcasted_iota(jnp.int32, (WINDOW, WINDOW), 0)
    jj = lax.broadcasted_iota(jnp.int32, (WINDOW, WINDOW), 1)
    tril = jj <= ii
    head_of = lax.broadcasted_iota(jnp.int32, (WINDOW, sw), 1) // HEAD_DIM
    wts = [jnp.where(tril, w_ref[h], 0.0).astype(BF16) for h in range(sw // HEAD_DIM)]
    sv = bexp_ref[...]
    for h, wt in enumerate(wts):
        sv = sv + jnp.where(head_of == h, _dot(wt, vn_b), 0.0)
    return vhat, r, vn_b, tril, head_of, wts, sv


def _sgu_specs(sw, u_blk):
    nh = sw // HEAD_DIM
    u = pl.BlockSpec((WINDOW, sw), lambda n: (n, u_blk))
    v = pl.BlockSpec((WINDOW, sw), lambda n: (n, u_blk + 1))
    vec = pl.BlockSpec((1, sw), lambda n: (0, 0))
    wspec = pl.BlockSpec((nh, WINDOW, WINDOW), lambda n: (0, 0, 0))
    bspec = pl.BlockSpec((WINDOW, sw), lambda n: (0, 0))
    row = pl.BlockSpec((WINDOW, sw), lambda n: (n, 0))
    return u, v, vec, wspec, bspec, row


def _sgu_fwd(name, proj, u_blk, lg, lb, w, bexp):
    s = proj.shape[0]
    sw = lg.shape[1]
    u, v, vec, wspec, bspec, row = _sgu_specs(sw, u_blk)

    def body(u_ref, v_ref, lg_ref, lb_ref, w_ref, bexp_ref, y_ref):
        sv = _sgu_common(v_ref, lg_ref, lb_ref, w_ref, bexp_ref, sw)[-1]
        y_ref[...] = u_ref[...] * sv

    return pl.pallas_call(
        body, name=name, grid=(s // WINDOW,), in_specs=[u, v, vec, vec, wspec, bspec], out_specs=row,
        out_shape=jax.ShapeDtypeStruct((s, sw), F32), compiler_params=_cp("parallel"),
    )(proj, proj, lg, lb, w, bexp)


def _sgu_bwd(name, proj, u_blk, lg, lb, w, bexp, dy):
    s = proj.shape[0]
    sw = lg.shape[1]
    nh = sw // HEAD_DIM
    u, v, vec, wspec, bspec, row = _sgu_specs(sw, u_blk)
    dbspec = pl.BlockSpec((nh, WINDOW), lambda n: (0, 0))

    def body(u_ref, v_ref, lg_ref, lb_ref, w_ref, bexp_ref, dy_ref, o_ref, dw_ref, db_ref, dlg_ref, dlb_ref):
        n = pl.program_id(0)

        @pl.when(n == 0)
        def _():
            dw_ref[...] = jnp.zeros_like(dw_ref)
            db_ref[...] = jnp.zeros_like(db_ref)
            dlg_ref[...] = jnp.zeros_like(dlg_ref)
            dlb_ref[...] = jnp.zeros_like(dlb_ref)

        vhat, r, vn_b, tril, head_of, wts, sv = _sgu_common(v_ref, lg_ref, lb_ref, w_ref, bexp_ref, sw)
        dyv = dy_ref[...]
        o_ref[:, 0:sw] = (dyv * sv).astype(BF16)
        ds = dyv * u_ref[...]
        dvn = jnp.zeros((WINDOW, sw), F32)
        for h, wt in enumerate(wts):
            dsm_b = jnp.where(head_of == h, ds, 0.0).astype(BF16)
            dvn = dvn + _dot(wt, dsm_b, TN)
            dw_ref[h] += jnp.where(tril, _dot(dsm_b, vn_b, NT), 0.0)
        hmask = (lax.broadcasted_iota(jnp.int32, (nh, sw), 1) // HEAD_DIM
                 == lax.broadcasted_iota(jnp.int32, (nh, sw), 0)).astype(F32)
        db_ref[...] += lax.dot_general(hmask, ds, NT, precision=lax.Precision.HIGHEST, preferred_element_type=F32)
        dlg_ref[...] += _colsum(dvn * vhat)
        dlb_ref[...] += _colsum(dvn)
        dvg = dvn * lg_ref[...]
        dv = r * (dvg - jnp.mean(dvg, axis=-1, keepdims=True) - vhat * jnp.mean(dvg * vhat, axis=-1, keepdims=True))
        o_ref[:, sw:2 * sw] = dv.astype(BF16)

    return pl.pallas_call(
        body, name=name, grid=(s // WINDOW,), in_specs=[u, v, vec, vec, wspec, bspec, row],
        out_specs=(pl.BlockSpec((WINDOW, 2 * sw), lambda n: (n, 0)), wspec, dbspec, vec, vec),
        out_shape=(jax.ShapeDtypeStruct((s, 2 * sw), BF16), jax.ShapeDtypeStruct((nh, WINDOW, WINDOW), F32),
                   jax.ShapeDtypeStruct((nh, WINDOW), F32), jax.ShapeDtypeStruct((1, sw), F32),
                   jax.ShapeDtypeStruct((1, sw), F32)),
        compiler_params=_cp("arbitrary"),
    )(proj, proj, lg, lb, w, bexp, dy)


def _adamw(name, w, g, m, v):
    rows, cols = w.shape
    tr = _tile(rows, 256)

    def body(w_ref, g_ref, m_ref, v_ref, d_ref, nm_ref, nv_ref):
        gv = g_ref[...]
        mv = ADAM_B1 * m_ref[...] + (1.0 - ADAM_B1) * gv
        vv = ADAM_B2 * v_ref[...] + (1.0 - ADAM_B2) * (gv * gv)
        m_hat = mv / (1.0 - ADAM_B1 ** ADAM_STEP)
        v_hat = vv / (1.0 - ADAM_B2 ** ADAM_STEP)
        d_ref[...] = -ADAM_LR * (m_hat / (jnp.sqrt(v_hat) + ADAM_EPS) + ADAM_WD * w_ref[...])
        nm_ref[...] = mv
        nv_ref[...] = vv

    spec = pl.BlockSpec((tr, cols), lambda i: (i, 0))
    shape = jax.ShapeDtypeStruct((rows, cols), F32)
    return pl.pallas_call(
        body, name=name, grid=(rows // tr,), in_specs=[spec] * 4, out_specs=(spec,) * 3, out_shape=(shape,) * 3,
        compiler_params=_cp("parallel"),
    )(w, g, m, v)


def _place():
    x, y, c = lax.axis_index("x"), lax.axis_index("y"), lax.axis_index("c")
    return x, y, c, ((1 - x, y), (x, 1 - y), (1 - x, 1 - y))


def _remote(src, dst, send_sem, recv_sem, device):
    return pltpu.make_async_remote_copy(src_ref=src, dst_ref=dst, send_sem=send_sem, recv_sem=recv_sem,
                                        device_id=device, device_id_type=MESH)


def _gather_weights(shards):
    n = len(shards)

    def body(*refs):
        ins, outs = refs[:n], refs[n:2 * n]
        send_sems, recv_sems, local_sems = refs[2 * n:]
        x, y, c, chips = _place()
        me = 2 * x + y
        sibling = (x, y, 1 - c)

        def half(t, chip, hc):
            hr = shards[t].shape[0] // 2
            return outs[t].at[chip, pl.ds(hc * hr, hr), :]

        def src_half(t):
            hr = shards[t].shape[0] // 2
            return ins[t].at[pl.ds(c * hr, hr), :]

        local = [pltpu.make_async_copy(ins[t], outs[t].at[me], local_sems.at[t]) for t in range(n)]
        for cp in local:
            cp.start()
        first = [_remote(src_half(t), half(t, me, c), send_sems.at[t, r], recv_sems.at[t, r], (*chips[r], c))
                 for t in range(n) for r in range(3)]
        for cp in first:
            cp.start()
        passed = []
        for t in range(n):
            for r in range(3):
                landed = half(t, 2 * chips[r][0] + chips[r][1], c)
                _remote(landed, landed, send_sems.at[t, r], recv_sems.at[t, r], (*chips[r], c)).wait_recv()
                cp = _remote(landed, landed, send_sems.at[t, 3 + r], recv_sems.at[t, 3 + r], sibling)
                cp.start()
                passed.append(cp)
        for t in range(n):
            for r in range(3):
                other = half(t, 2 * chips[r][0] + chips[r][1], 1 - c)
                _remote(other, other, send_sems.at[t, 3 + r], recv_sems.at[t, 3 + r], sibling).wait_recv()
        for cp in first + passed:
            cp.wait_send()
        for cp in local:
            cp.wait()

    return pl.pallas_call(
        body, name="gather_weights", in_specs=[ANY] * n, out_specs=tuple(ANY for _ in range(n)),
        out_shape=tuple(jax.ShapeDtypeStruct((N_CHIPS,) + s.shape, s.dtype) for s in shards),
        scratch_shapes=[pltpu.SemaphoreType.DMA((n, 6)), pltpu.SemaphoreType.DMA((n, 6)),
                        pltpu.SemaphoreType.DMA((n,))],
        compiler_params=pltpu.CompilerParams(has_side_effects=True),
    )(*shards)


def _pair_exchange(grads):
    n = len(grads)

    def body(*refs):
        ins, outs = refs[:n], refs[n:2 * n]
        send_sems, recv_sems = refs[2 * n:]
        x, y, c, _ = _place()
        cps = []
        for t in range(n):
            hr = grads[t].shape[1] // 2
            cps.append(_remote(ins[t].at[:, pl.ds((1 - c) * hr, hr), :], outs[t], send_sems.at[t], recv_sems.at[t],
                               (x, y, 1 - c)))
        for cp in cps:
            cp.start()
        for cp in cps:
            cp.wait()

    return pl.pallas_call(
        body, name="rs_pair_exchange", in_specs=[ANY] * n, out_specs=tuple(ANY for _ in range(n)),
        out_shape=tuple(jax.ShapeDtypeStruct((N_CHIPS, g.shape[1] // 2, g.shape[2]), g.dtype) for g in grads),
        scratch_shapes=[pltpu.SemaphoreType.DMA((n,)), pltpu.SemaphoreType.DMA((n,))],
        compiler_params=pltpu.CompilerParams(has_side_effects=True),
    )(*grads)


def _pair_sum(name, g, r1, c_idx):
    _, rows, cols = g.shape
    hr = rows // 2
    tr = _tile(hr, 512)
    per = hr // tr

    def body(c_ref, g_ref, r_ref, o_ref):
        o_ref[...] = (g_ref[...].astype(F32) + r_ref[...].astype(F32)).astype(BF16)

    blk = (None, tr, cols)
    grid_spec = pltpu.PrefetchScalarGridSpec(
        num_scalar_prefetch=1, grid=(N_CHIPS, per),
        in_specs=[pl.BlockSpec(blk, lambda j, i, c_ref: (j, c_ref[0] * per + i, 0)),
                  pl.BlockSpec(blk, lambda j, i, c_ref: (j, i, 0))],
        out_specs=pl.BlockSpec(blk, lambda j, i, c_ref: (j, i, 0)))
    return pl.pallas_call(
        body, name=name, grid_spec=grid_spec, out_shape=jax.ShapeDtypeStruct((N_CHIPS, hr, cols), BF16),
        compiler_params=_cp("parallel", "parallel"),
    )(c_idx, g, r1)


def _chip_exchange(psums):
    n = len(psums)

    def body(*refs):
        ins, outs = refs[:n], refs[n:2 * n]
        send_sems, recv_sems = refs[2 * n:]
        _, _, c, chips = _place()
        cps = [_remote(ins[t].at[2 * chips[r][0] + chips[r][1]], outs[t].at[r], send_sems.at[t, r],
                       recv_sems.at[t, r], (*chips[r], c)) for t in range(n) for r in range(3)]
        for cp in cps:
            cp.start()
        for cp in cps:
            cp.wait()

    return pl.pallas_call(
        body, name="rs_chip_exchange", in_specs=[ANY] * n, out_specs=tuple(ANY for _ in range(n)),
        out_shape=tuple(jax.ShapeDtypeStruct((3,) + p.shape[1:], p.dtype) for p in psums),
        scratch_shapes=[pltpu.SemaphoreType.DMA((n, 3)), pltpu.SemaphoreType.DMA((n, 3))],
        compiler_params=pltpu.CompilerParams(has_side_effects=True),
    )(*psums)


def _chip_sum(name, p, r2, me_idx):
    _, hr, cols = p.shape
    tr = _tile(hr, 512)

    def body(me_ref, p_ref, r_ref, o_ref):
        acc = p_ref[...].astype(F32)
        for r in range(3):
            acc = acc + r_ref[r].astype(F32)
        o_ref[...] = acc

    grid_spec = pltpu.PrefetchScalarGridSpec(
        num_scalar_prefetch=1, grid=(hr // tr,),
        in_specs=[pl.BlockSpec((None, tr, cols), lambda i, me_ref: (me_ref[0], i, 0)),
                  pl.BlockSpec((3, tr, cols), lambda i, me_ref: (0, i, 0))],
        out_specs=pl.BlockSpec((tr, cols), lambda i, me_ref: (i, 0)))
    return pl.pallas_call(
        body, name=name, grid_spec=grid_spec, out_shape=jax.ShapeDtypeStruct((hr, cols), F32),
        compiler_params=_cp("parallel"),
    )(me_idx, p, r2)


def _half_exchange(halves, n_layers):
    n = len(halves)
    n_t = n // n_layers

    def body(*refs):
        ins, outs = refs[:n], refs[n:n + n_t]
        send_sems, recv_sems, local_sems = refs[n + n_t:]
        x, y, c, _ = _place()
        local, cps = [], []
        for k in range(n):
            t, l = divmod(k, n_layers)
            hr = halves[k].shape[0]
            dst = outs[t].at[l, pl.ds(c * hr, hr), :]
            local.append(pltpu.make_async_copy(ins[k], dst, local_sems.at[k]))
            cps.append(_remote(ins[k], dst, send_sems.at[k], recv_sems.at[k], (x, y, 1 - c)))
        for cp in local + cps:
            cp.start()
        for cp in cps:
            cp.wait()
        for cp in local:
            cp.wait()

    return pl.pallas_call(
        body, name="rs_half_exchange", in_specs=[ANY] * n, out_specs=tuple(ANY for _ in range(n_t)),
        out_shape=tuple(jax.ShapeDtypeStruct((n_layers, 2 * halves[t * n_layers].shape[0],
                                              halves[t * n_layers].shape[1]), F32) for t in range(n_t)),
        scratch_shapes=[pltpu.SemaphoreType.DMA((n,)), pltpu.SemaphoreType.DMA((n,)), pltpu.SemaphoreType.DMA((n,))],
        compiler_params=pltpu.CompilerParams(has_side_effects=True),
    )(*halves)


def _small_allreduce(buf):
    rows = buf.shape[0]

    def body(in_ref, out_ref, pair, quad, send_sems, recv_sems):
        x, y, c, chips = _place()
        me = 2 * x + y
        pair[c] = in_ref[...]
        cp = _remote(in_ref, pair.at[c], send_sems.at[0], recv_sems.at[0], (x, y, 1 - c))
        cp.start()
        cp.wait()
        quad[me] = pair[0] + pair[1]
        cps = [_remote(quad.at[me], quad.at[me], send_sems.at[1 + r], recv_sems.at[1 + r], (*chips[r], c))
               for r in range(3)]
        for cp in cps:
            cp.start()
        for cp in cps:
            cp.wait()
        out_ref[...] = (quad[0] + quad[1]) + (quad[2] + quad[3])

    return pl.pallas_call(
        body, name="small_allreduce", in_specs=[pl.BlockSpec(memory_space=pltpu.VMEM)],
        out_specs=pl.BlockSpec(memory_space=pltpu.VMEM), out_shape=jax.ShapeDtypeStruct((rows, 128), F32),
        scratch_shapes=[pltpu.VMEM((2, rows, 128), F32), pltpu.VMEM((N_CHIPS, rows, 128), F32),
                        pltpu.SemaphoreType.DMA((4,)), pltpu.SemaphoreType.DMA((4,))],
        compiler_params=pltpu.CompilerParams(has_side_effects=True, vmem_limit_bytes=VMEM_LIMIT),
    )(buf)


BIG = ("w_in", "w_out", "w_up", "w_down")
COL_SHARDED = {"w_in": True, "w_out": False, "w_up": True, "w_down": False}
SMALL = ("ln1_g", "q_norm_g", "k_norm_g", "sinks", "conv_w", "conv_b", "conv_ln_g", "conv_ln_b", "sgu_ln_g",
         "sgu_ln_b", "sgu_w", "sgu_b", "out_norm_g", "ln2_g")
WEIGHTS = ("ln1_g", "w_in", "q_norm_g", "k_norm_g", "sinks", "conv_w", "conv_b", "conv_ln_g", "conv_ln_b",
           "sgu_ln_g", "sgu_ln_b", "sgu_w", "sgu_b", "out_norm_g", "w_out", "ln2_g", "w_up", "w_down")
PACK_QUANTUM = 8 * 128


def _pack(arrs):
    parts = []
    for a in arrs:
        f = a.reshape(-1)
        parts.append(jnp.pad(f, (0, -f.shape[0] % PACK_QUANTUM)).reshape(-1, 128))
    return jnp.concatenate(parts, axis=0)


def _unpack(buf, shapes):
    out, off = [], 0
    for shp in shapes:
        n = 1
        for dd in shp:
            n *= dd
        rows = (n + PACK_QUANTUM - 1) // PACK_QUANTUM * 8
        out.append(buf[off:off + rows].reshape(-1)[:n].reshape(shp))
        off += rows
    return out


def _to_heads(t, nh):
    return t.reshape(t.shape[0], nh, HEAD_DIM).transpose(1, 0, 2)


def _from_heads(t):
    return t.transpose(1, 0, 2).reshape(t.shape[1], t.shape[0] * HEAD_DIM)


def _layer_fwd(l, x, p, wg):
    d = x.shape[1]
    aw, cw = d // 2, d // 4
    nq = aw // HEAD_DIM
    nkv = nq // GQA
    kvw = nkv * HEAD_DIM
    h1 = _rms_fwd(f"ln1_fwd_{l}", x, p["ln1_g"])
    proj = _mm_act_w(f"proj_{l}", h1, wg["w_in"], True, _ep_store)[0]
    q = _to_heads(proj[:, :aw], nq)
    k = _to_heads(proj[:, aw:aw + kvw], nkv)
    v = _to_heads(proj[:, aw + kvw:aw + 2 * kvw], nkv)
    sinks_b = jnp.broadcast_to(p["sinks"][:, None, None], (nq, 1, 128))
    ya = _from_heads(_attn_fwd(f"attn_fwd_{l}", q, k, v, p["q_norm_g"], p["k_norm_g"], sinks_b))
    yc = _conv_fwd(f"conv_fwd_{l}", proj, 3, p["conv_w"], p["conv_b"], p["conv_ln_g"], p["conv_ln_b"])
    ys = _sgu_fwd(f"sgu_fwd_{l}", proj, 5, p["sgu_ln_g"], p["sgu_ln_b"], p["sgu_w"], p["sgu_bexp"])
    mix = _mixnorm_fwd(f"mixnorm_fwd_{l}", ya, yc, ys, p["out_norm_g"])
    xm = _mm_act_w(f"out_{l}", mix, wg["w_out"], False, _ep_residual, extra=(x,))[0]
    h2 = _rms_fwd(f"ln2_fwd_{l}", xm, p["ln2_g"])
    up_b, act_b = _mm_act_w(f"up_{l}", h2, wg["w_up"], True, _ep_up, out_dtypes=(BF16, BF16))
    xo = _mm_act_w(f"down_{l}", act_b, wg["w_down"], False, _ep_residual, extra=(xm,))[0]
    saved = dict(x=x, h1=h1, proj=proj, q=q, k=k, v=v, sinks_b=sinks_b, ya=ya, yc=yc, ys=ys, mix=mix, xm=xm, h2=h2,
                 up_b=up_b, act_b=act_b)
    return xo, saved


def _layer_bwd(l, dxo, dxo_b, p, wg, sv):
    d = dxo.shape[1]
    nq = (d // 2) // HEAD_DIM
    big, small = {}, {}
    big["w_down"] = _mm_wgrad(f"dw_down_{l}", sv["act_b"], dxo_b, False, d)
    dup_b = _mm_act_wt(f"dup_{l}", dxo_b, wg["w_down"], False, _ep_dup, extra=(sv["up_b"],), out_dtypes=(BF16,))[0]
    big["w_up"] = _mm_wgrad(f"dw_up_{l}", sv["h2"], dup_b, True, wg["w_up"].shape[2])
    dh2 = _mm_act_wt(f"dh2_{l}", dup_b, wg["w_up"], True, _ep_store)[0]
    dxm, dxm_b, small["ln2_g"] = _rms_bwd(f"ln2_bwd_{l}", dh2, sv["xm"], p["ln2_g"], dxo)
    big["w_out"] = _mm_wgrad(f"dw_out_{l}", sv["mix"], dxm_b, False, d)
    dmix = _mm_act_wt(f"dmix_{l}", dxm_b, wg["w_out"], False, _ep_store)[0]
    dya, dyc, dys, small["out_norm_g"] = _mixnorm_bwd(f"mixnorm_bwd_{l}", dmix, sv["ya"], sv["yc"], sv["ys"],
                                                      p["out_norm_g"])
    dq, dkc, dkp, dvc, dvp, small["q_norm_g"], dsink = _attn_bwd(
        f"attn_bwd_{l}", sv["q"], sv["k"], sv["v"], p["q_norm_g"], p["k_norm_g"], sv["sinks_b"], _to_heads(dya, nq))
    small["sinks"] = dsink[:, 0, 0]
    dk, dv, small["k_norm_g"] = _attn_bwd_kv(f"attn_bwd_kv_{l}", sv["k"], p["k_norm_g"], dkc, dkp, dvc, dvp)
    dc, dcw, small["conv_b"], small["conv_ln_g"], small["conv_ln_b"] = _conv_bwd1(
        f"conv_bwd1_{l}", sv["proj"], 3, p["conv_w"], p["conv_b"], p["conv_ln_g"], p["conv_ln_b"], dyc)
    small["conv_w"] = dcw[:CONV_KERNEL]
    dxc_b = _conv_bwd2(f"conv_bwd2_{l}", sv["proj"], 3, p["conv_w"], dc)
    dxs_b, small["sgu_w"], small["sgu_b"], small["sgu_ln_g"], small["sgu_ln_b"] = _sgu_bwd(
        f"sgu_bwd_{l}", sv["proj"], 5, p["sgu_ln_g"], p["sgu_ln_b"], p["sgu_w"], p["sgu_bexp"], dys)
    dproj_b = jnp.concatenate([_from_heads(dq).astype(BF16), _from_heads(dk).astype(BF16),
                               _from_heads(dv).astype(BF16), dxc_b, dxs_b], axis=1)
    big["w_in"] = _mm_wgrad(f"dw_in_{l}", sv["h1"], dproj_b, True, wg["w_in"].shape[2])
    dh1 = _mm_act_wt(f"dh1_{l}", dproj_b, wg["w_in"], True, _ep_store)[0]
    dx, dx_b, small["ln1_g"] = _rms_bwd(f"ln1_bwd_{l}", dh1, sv["x"], p["ln1_g"], dxm)
    return dx, dx_b, big, small


def kernel(x, ln1_g, w_in, q_norm_g, k_norm_g, sinks, conv_w, conv_b, conv_ln_g, conv_ln_b, sgu_ln_g, sgu_ln_b, sgu_w, sgu_b, out_norm_g, w_out, ln2_g, w_up, w_down, loss_target, m_ln1_g, m_w_in, m_q_norm_g, m_k_norm_g, m_sinks, m_conv_w, m_conv_b, m_conv_ln_g, m_conv_ln_b, m_sgu_ln_g, m_sgu_ln_b, m_sgu_w, m_sgu_b, m_out_norm_g, m_w_out, m_ln2_g, m_w_up, m_w_down, v_ln1_g, v_w_in, v_q_norm_g, v_k_norm_g, v_sinks, v_conv_w, v_conv_b, v_conv_ln_g, v_conv_ln_b, v_sgu_ln_g, v_sgu_ln_b, v_sgu_w, v_sgu_b, v_out_norm_g, v_w_out, v_ln2_g, v_w_up, v_w_down):
    given = dict(locals())
    n_layers = ln1_g.shape[0]
    s, d = x.shape[1], x.shape[2]
    cw = d // 4
    chip = 2 * lax.axis_index("x") + lax.axis_index("y")
    core = lax.axis_index("c")

    conv_w_pad = jnp.pad(conv_w, ((0, 0), (0, HALO - CONV_KERNEL), (0, 0))).reshape(n_layers * HALO, -1)
    shards = [conv_w_pad] + [given[nm][l].astype(BF16) for l in range(n_layers) for nm in BIG]
    gathered = _gather_weights(shards)
    cwl = conv_w_pad.shape[1]
    conv_w_full = gathered[0].reshape(N_CHIPS, n_layers, HALO, cwl).transpose(1, 2, 0, 3).reshape(n_layers, HALO, cw)
    wgs = [{nm: gathered[1 + l * len(BIG) + i] for i, nm in enumerate(BIG)} for l in range(n_layers)]
    params = []
    for l in range(n_layers):
        p = {nm: given[nm][l] for nm in SMALL if nm != "conv_w"}
        for nm in ("conv_b", "conv_ln_g", "conv_ln_b", "sgu_ln_g", "sgu_ln_b"):
            p[nm] = p[nm].reshape(1, -1)
        p["conv_w"] = conv_w_full[l]
        p["sgu_bexp"] = jnp.repeat(sgu_b[l].T, HEAD_DIM, axis=1)
        params.append(p)

    h = x.reshape(s, d)
    saved = []
    for l in range(n_layers):
        h, sv = _layer_fwd(l, h, params[l], wgs[l])
        saved.append(sv)
    dh, dh_b, loss_part = _loss_head(h, loss_target.reshape(s, d))
    loss = lax.psum(loss_part[0, 0], ("x", "y", "c"))
    big_grads, small_grads = [None] * n_layers, [None] * n_layers
    for l in reversed(range(n_layers)):
        dh, dh_b, big_grads[l], small_grads[l] = _layer_bwd(l, dh, dh_b, params[l], wgs[l], saved[l])
    grad_x = dh.reshape(x.shape)

    order = [(nm, l) for nm in BIG for l in range(n_layers)]
    partial = [big_grads[l][nm] for nm, l in order]
    from_sibling = _pair_exchange(partial)
    c_idx = core.reshape(1).astype(jnp.int32)
    me_idx = chip.reshape(1).astype(jnp.int32)
    psums = [_pair_sum(f"rs_pair_sum_{nm}_{l}", g, r1, c_idx) for (nm, l), g, r1 in zip(order, partial, from_sibling)]
    from_chips = _chip_exchange(psums)
    halves = [_chip_sum(f"rs_chip_sum_{nm}_{l}", ps, r2, me_idx) for (nm, l), ps, r2 in zip(order, psums, from_chips)]
    full = _half_exchange(halves, n_layers)
    grads = {nm: full[i] for i, nm in enumerate(BIG)}

    small_shapes = [(n_layers,) + small_grads[0][nm].shape for nm in SMALL]
    small_sum = _small_allreduce(_pack([jnp.stack([small_grads[l][nm] for l in range(n_layers)]) for nm in SMALL]))
    for nm, g in zip(SMALL, _unpack(small_sum, small_shapes)):
        grads[nm] = g.reshape((n_layers,) + given[nm].shape[1:]) if nm != "conv_w" else g
    grads["conv_w"] = lax.dynamic_slice_in_dim(grads["conv_w"], chip * cwl, cwl, axis=2)

    delta, new_m, new_v = {}, {}, {}
    for nm in BIG:
        shp = given[nm].shape
        flat = lambda a: a.reshape(shp[0] * shp[1], shp[2])
        dl, nm_, nv_ = _adamw(f"adamw_{nm}", flat(given[nm]), flat(grads[nm]), flat(given["m_" + nm]),
                              flat(given["v_" + nm]))
        delta[nm], new_m[nm], new_v[nm] = dl.reshape(shp), nm_.reshape(shp), nv_.reshape(shp)
    packed = [_pack([src[nm] for nm in SMALL]) for src in
              ({nm: given[nm] for nm in SMALL}, grads, {nm: given["m_" + nm] for nm in SMALL},
               {nm: given["v_" + nm] for nm in SMALL})]
    local_shapes = [given[nm].shape for nm in SMALL]
    for dst, buf in zip((delta, new_m, new_v), _adamw("adamw_small", *packed)):
        for nm, a in zip(SMALL, _unpack(buf, local_shapes)):
            dst[nm] = a
    return (loss, grad_x, *[grads[nm] for nm in WEIGHTS], *[delta[nm] for nm in WEIGHTS],
            *[new_m[nm] for nm in WEIGHTS], *[new_v[nm] for nm in WEIGHTS])
```

```python
import functools

import jax
import jax.numpy as jnp
from jax import lax
from jax.experimental import pallas as pl
from jax.experimental.pallas import tpu as pltpu

F32 = jnp.float32
BF16 = jnp.bfloat16
EPS = 1e-6
NEG_INF = -1e30
HEAD_DIM = 64
WINDOW = 128
CONV_KERNEL = 31
HALO = 32
GQA = 4
N_CHIPS = 4
ADAM_LR, ADAM_B1, ADAM_B2, ADAM_EPS, ADAM_WD, ADAM_STEP = 0.001, 0.9, 0.999, 1e-08, 0.01, 10
VMEM_LIMIT = 56 * 1024 * 1024
MESH = pl.DeviceIdType.MESH
ANY = pl.BlockSpec(memory_space=pl.ANY)

NN = (((1,), (0,)), ((), ()))
NT = (((1,), (1,)), ((), ()))
TN = (((0,), (0,)), ((), ()))


def _cp(*sem):
    return pltpu.CompilerParams(dimension_semantics=sem, vmem_limit_bytes=VMEM_LIMIT)


def _tile(dim, pref):
    if dim <= pref:
        return dim
    for t in range(pref, 0, -128):
        if dim % t == 0:
            return t
    while dim % pref:
        pref //= 2
    return pref


def _dot(a, b, dims=NN):
    return lax.dot_general(a, b, dims, preferred_element_type=F32)


def _colsum(v):
    return jnp.sum(v, axis=0, keepdims=True)


def _sigmoid(x):
    return 1.0 / (1.0 + jnp.exp(-x))


def _matmul(name, operands, in_specs, out_shape, out_specs, grid, dims, acc_shape, epilogue):
    nk = grid[2]
    n_in = len(operands)

    def body(*refs):
        a_ref, b_ref = refs[0], refs[1]
        extra = refs[2:n_in]
        outs = refs[n_in:-1]
        acc = refs[-1]
        k = pl.program_id(2)

        @pl.when(k == 0)
        def _():
            acc[...] = jnp.zeros_like(acc)

        acc[...] += _dot(a_ref[...], b_ref[...], dims)

        @pl.when(k == nk - 1)
        def _():
            epilogue(acc[...], extra, outs)

    return pl.pallas_call(
        body, name=name, grid=grid, in_specs=in_specs, out_specs=out_specs, out_shape=out_shape,
        scratch_shapes=[pltpu.VMEM(acc_shape, F32)],
        compiler_params=_cp("parallel", "parallel", "arbitrary"),
    )(*operands)


def _ep_store(acc, extra, outs):
    outs[0][...] = acc.astype(outs[0].dtype)


def _ep_residual(acc, extra, outs):
    outs[0][...] = extra[0][...] + acc


def _ep_up(acc, extra, outs):
    outs[0][...] = acc.astype(BF16)
    r = jnp.maximum(acc, 0.0)
    outs[1][...] = (r * r).astype(BF16)


def _ep_dup(acc, extra, outs):
    outs[0][...] = (acc * (2.0 * jnp.maximum(extra[0][...].astype(F32), 0.0))).astype(BF16)


def _mm_act_w(name, a, wg, col_sharded, epilogue, extra=(), out_dtypes=(F32,)):
    m, kdim = a.shape
    _, r, c = wg.shape
    tm = _tile(m, 1024)
    if col_sharded:
        n = N_CHIPS * c
        tn = _tile(c, 1024)
        tk = _tile(kdim, 1024)
        per = c // tn
        b_spec = pl.BlockSpec((None, tk, tn), lambda i, j, k: (j // per, k, j % per))
    else:
        n = c
        tn = _tile(n, 1024)
        tk = _tile(r, 1024)
        per = r // tk
        b_spec = pl.BlockSpec((None, tk, tn), lambda i, j, k: (k // per, k % per, j))
    grid = (m // tm, n // tn, kdim // tk)
    o_spec = pl.BlockSpec((tm, tn), lambda i, j, k: (i, j))
    in_specs = [pl.BlockSpec((tm, tk), lambda i, j, k: (i, k)), b_spec] + [o_spec] * len(extra)
    return _matmul(name, (a, wg) + tuple(extra), in_specs,
                   tuple(jax.ShapeDtypeStruct((m, n), d) for d in out_dtypes),
                   tuple(o_spec for _ in out_dtypes), grid, NN, (tm, tn), epilogue)


def _mm_act_wt(name, a, wg, col_sharded, epilogue, extra=(), out_dtypes=(F32,)):
    m, kdim = a.shape
    _, r, c = wg.shape
    tm = _tile(m, 1024)
    if col_sharded:
        n = r
        tn = _tile(n, 1024)
        tk = _tile(c, 1024)
        per = c // tk
        b_spec = pl.BlockSpec((None, tn, tk), lambda i, j, k: (k // per, j, k % per))
    else:
        n = N_CHIPS * r
        tn = _tile(r, 1024)
        tk = _tile(c, 1024)
        per = r // tn
        b_spec = pl.BlockSpec((None, tn, tk), lambda i, j, k: (j // per, j % per, k))
    grid = (m // tm, n // tn, kdim // tk)
    o_spec = pl.BlockSpec((tm, tn), lambda i, j, k: (i, j))
    in_specs = [pl.BlockSpec((tm, tk), lambda i, j, k: (i, k)), b_spec] + [o_spec] * len(extra)
    return _matmul(name, (a, wg) + tuple(extra), in_specs,
                   tuple(jax.ShapeDtypeStruct((m, n), d) for d in out_dtypes),
                   tuple(o_spec for _ in out_dtypes), grid, NT, (tm, tn), epilogue)


def _mm_wgrad(name, a, g, col_sharded, c):
    s, kdim = a.shape
    _, n = g.shape
    ts = _tile(s, 1024)
    if col_sharded:
        r = kdim
        tm = _tile(kdim, 1024)
        tn = _tile(c, 1024)
        per = c // tn
        o_spec = pl.BlockSpec((None, tm, tn), lambda i, j, k: (j // per, i, j % per))
    else:
        r = kdim // N_CHIPS
        tm = _tile(r, 1024)
        tn = _tile(c, 1024)
        per = r // tm
        o_spec = pl.BlockSpec((None, tm, tn), lambda i, j, k: (i // per, i % per, j))
    grid = (kdim // tm, n // tn, s // ts)
    in_specs = [pl.BlockSpec((ts, tm), lambda i, j, k: (k, i)), pl.BlockSpec((ts, tn), lambda i, j, k: (k, j))]
    return _matmul(name, (a, g), in_specs, (jax.ShapeDtypeStruct((N_CHIPS, r, c), BF16),), (o_spec,),
                   grid, TN, (tm, tn), _ep_store)[0]


def _rms_fwd(name, x, g):
    s, d = x.shape
    tb = _tile(s, 256)

    def body(x_ref, g_ref, o_ref):
        xv = x_ref[...]
        r = lax.rsqrt(jnp.mean(xv * xv, axis=-1, keepdims=True) + EPS)
        o_ref[...] = (xv * r * g_ref[...]).astype(BF16)

    return pl.pallas_call(
        body, name=name, grid=(s // tb,),
        in_specs=[pl.BlockSpec((tb, d), lambda i: (i, 0)), pl.BlockSpec((1, d), lambda i: (0, 0))],
        out_specs=pl.BlockSpec((tb, d), lambda i: (i, 0)),
        out_shape=jax.ShapeDtypeStruct((s, d), BF16), compiler_params=_cp("parallel"),
    )(x, g.reshape(1, d))


def _rms_bwd(name, dh, x, g, dres):
    s, d = x.shape
    tb = _tile(s, 256)

    def body(dh_ref, x_ref, g_ref, dres_ref, dx_ref, dxb_ref, dg_ref):
        i = pl.program_id(0)
        xv = x_ref[...]
        r = lax.rsqrt(jnp.mean(xv * xv, axis=-1, keepdims=True) + EPS)
        xhat = xv * r
        dhv = dh_ref[...]
        dxhat = dhv * g_ref[...]
        dx = dres_ref[...] + r * (dxhat - xhat * jnp.mean(dxhat * xhat, axis=-1, keepdims=True))
        dx_ref[...] = dx
        dxb_ref[...] = dx.astype(BF16)

        @pl.when(i == 0)
        def _():
            dg_ref[...] = jnp.zeros_like(dg_ref)

        dg_ref[...] += _colsum(dhv * xhat)

    row = pl.BlockSpec((tb, d), lambda i: (i, 0))
    vec = pl.BlockSpec((1, d), lambda i: (0, 0))
    return pl.pallas_call(
        body, name=name, grid=(s // tb,), in_specs=[row, row, vec, row], out_specs=(row, row, vec),
        out_shape=(jax.ShapeDtypeStruct((s, d), F32), jax.ShapeDtypeStruct((s, d), BF16),
                   jax.ShapeDtypeStruct((1, d), F32)),
        compiler_params=_cp("arbitrary"),
    )(dh, x, g.reshape(1, d), dres)


def _loss_head(y, t):
    s, d = y.shape
    tb = _tile(s, 256)

    def body(y_ref, t_ref, dy_ref, dyb_ref, loss_ref, acc):
        i = pl.program_id(0)
        e = y_ref[...] - t_ref[...]
        dy = e * (1.0 / d)
        dy_ref[...] = dy
        dyb_ref[...] = dy.astype(BF16)

        @pl.when(i == 0)
        def _():
            acc[...] = jnp.zeros_like(acc)

        acc[...] += _colsum(e * e)

        @pl.when(i == pl.num_programs(0) - 1)
        def _():
            loss_ref[...] = jnp.sum(acc[...], axis=-1, keepdims=True) * (0.5 / d)

    row = pl.BlockSpec((tb, d), lambda i: (i, 0))
    return pl.pallas_call(
        body, name="loss_head", grid=(s // tb,), in_specs=[row, row],
        out_specs=(row, row, pl.BlockSpec((1, 1), lambda i: (0, 0))),
        out_shape=(jax.ShapeDtypeStruct((s, d), F32), jax.ShapeDtypeStruct((s, d), BF16),
                   jax.ShapeDtypeStruct((1, 1), F32)),
        scratch_shapes=[pltpu.VMEM((1, d), F32)], compiler_params=_cp("arbitrary"),
    )(y, t)


def _mixnorm_fwd(name, ya, yc, ys, g):
    s, aw = ya.shape
    cw, sw = yc.shape[1], ys.shape[1]
    d = aw + cw + sw
    tb = _tile(s, 256)

    def body(ya_ref, yc_ref, ys_ref, g_ref, o_ref):
        off = 0
        for ref, w in ((ya_ref, aw), (yc_ref, cw), (ys_ref, sw)):
            v = ref[...]
            r = lax.rsqrt(jnp.mean(v * v, axis=-1, keepdims=True) + EPS)
            o_ref[:, off:off + w] = (v * r * g_ref[:, off:off + w]).astype(BF16)
            off += w

    def row(w):
        return pl.BlockSpec((tb, w), lambda i: (i, 0))

    return pl.pallas_call(
        body, name=name, grid=(s // tb,),
        in_specs=[row(aw), row(cw), row(sw), pl.BlockSpec((1, d), lambda i: (0, 0))], out_specs=row(d),
        out_shape=jax.ShapeDtypeStruct((s, d), BF16), compiler_params=_cp("parallel"),
    )(ya, yc, ys, g.reshape(1, d))


def _mixnorm_bwd(name, dmix, ya, yc, ys, g):
    s, aw = ya.shape
    cw, sw = yc.shape[1], ys.shape[1]
    d = aw + cw + sw
    tb = _tile(s, 256)

    def body(dm_ref, ya_ref, yc_ref, ys_ref, g_ref, dya_ref, dyc_ref, dys_ref, dg_ref):
        i = pl.program_id(0)

        @pl.when(i == 0)
        def _():
            dg_ref[...] = jnp.zeros_like(dg_ref)

        off = 0
        for ref, dref, w in ((ya_ref, dya_ref, aw), (yc_ref, dyc_ref, cw), (ys_ref, dys_ref, sw)):
            v = ref[...]
            r = lax.rsqrt(jnp.mean(v * v, axis=-1, keepdims=True) + EPS)
            vhat = v * r
            dm = dm_ref[:, off:off + w]
            dvhat = dm * g_ref[:, off:off + w]
            dref[...] = r * (dvhat - vhat * jnp.mean(dvhat * vhat, axis=-1, keepdims=True))
            dg_ref[:, off:off + w] += _colsum(dm * vhat)
            off += w

    def row(w):
        return pl.BlockSpec((tb, w), lambda i: (i, 0))

    vec = pl.BlockSpec((1, d), lambda i: (0, 0))
    return pl.pallas_call(
        body, name=name, grid=(s // tb,), in_specs=[row(d), row(aw), row(cw), row(sw), vec],
        out_specs=(row(aw), row(cw), row(sw), vec),
        out_shape=(jax.ShapeDtypeStruct((s, aw), F32), jax.ShapeDtypeStruct((s, cw), F32),
                   jax.ShapeDtypeStruct((s, sw), F32), jax.ShapeDtypeStruct((1, d), F32)),
        compiler_params=_cp("arbitrary"),
    )(dmix, ya, yc, ys, g.reshape(1, d))


def _head_rms(x, g):
    r = lax.rsqrt(jnp.mean(x * x, axis=-1, keepdims=True) + EPS)
    return x * r, r


def _attn_mask(n):
    qi = lax.broadcasted_iota(jnp.int32, (WINDOW, 2 * WINDOW), 0)
    sj = lax.broadcasted_iota(jnp.int32, (WINDOW, 2 * WINDOW), 1)
    rel = qi + WINDOW - sj
    return (rel >= 0) & (rel < WINDOW) & ((sj >= WINDOW) | (n > 0))


def _attn_specs(nb):
    qspec = pl.BlockSpec((GQA, WINDOW, HEAD_DIM), lambda g, n: (g, n, 0))
    cur = pl.BlockSpec((None, WINDOW, HEAD_DIM), lambda g, n: (g, n, 0))
    prev = pl.BlockSpec((None, WINDOW, HEAD_DIM), lambda g, n: (g, jnp.maximum(n - 1, 0), 0))
    nxt = pl.BlockSpec((None, WINDOW, HEAD_DIM), lambda g, n: (g, jnp.minimum(n + 1, nb - 1), 0))
    gain = pl.BlockSpec((1, HEAD_DIM), lambda g, n: (0, 0))
    sink = pl.BlockSpec((GQA, 1, 128), lambda g, n: (g, 0, 0))
    return qspec, cur, prev, nxt, gain, sink


def _attn_probs(qn_b, kn_b, valid, sink):
    logits = _dot(qn_b, kn_b, NT) * (HEAD_DIM ** -0.5)
    logits = jnp.where(valid, logits, NEG_INF)
    m = jnp.maximum(jnp.max(logits, axis=-1, keepdims=True), sink)
    p = jnp.exp(logits - m)
    es = jnp.exp(sink - m)
    denom = jnp.sum(p, axis=-1, keepdims=True) + es
    return p / denom, es / denom


def _attn_fwd(name, q, k, v, gq, gk, sinks_b):
    nq, s, _ = q.shape
    nkv = k.shape[0]
    nb = s // WINDOW
    qspec, cur, prev, _, gain, sink = _attn_specs(nb)

    def body(q_ref, kc_ref, kp_ref, vc_ref, vp_ref, gq_ref, gk_ref, s_ref, o_ref):
        n = pl.program_id(1)
        gkv = gk_ref[...]
        kn = jnp.concatenate([_head_rms(kp_ref[...], gkv)[0] * gkv, _head_rms(kc_ref[...], gkv)[0] * gkv], axis=0)
        kn_b = kn.astype(BF16)
        vv_b = jnp.concatenate([vp_ref[...], vc_ref[...]], axis=0).astype(BF16)
        valid = _attn_mask(n)
        for i in range(GQA):
            qn_b = (_head_rms(q_ref[i], gq_ref[...])[0] * gq_ref[...]).astype(BF16)
            probs, _ = _attn_probs(qn_b, kn_b, valid, s_ref[i][:, :1])
            o_ref[i] = _dot(probs.astype(BF16), vv_b)

    return pl.pallas_call(
        body, name=name, grid=(nkv, nb), in_specs=[qspec, cur, prev, cur, prev, gain, gain, sink], out_specs=qspec,
        out_shape=jax.ShapeDtypeStruct((nq, s, HEAD_DIM), F32), compiler_params=_cp("parallel", "parallel"),
    )(q, k, k, v, v, gq.reshape(1, HEAD_DIM), gk.reshape(1, HEAD_DIM), sinks_b)


def _attn_bwd(name, q, k, v, gq, gk, sinks_b, do):
    nq, s, _ = q.shape
    nkv = k.shape[0]
    nb = s // WINDOW
    qspec, cur, prev, _, gain, sink = _attn_specs(nb)

    def body(q_ref, kc_ref, kp_ref, vc_ref, vp_ref, gq_ref, gk_ref, s_ref, do_ref,
             dq_ref, dkc_ref, dkp_ref, dvc_ref, dvp_ref, dgq_ref, ds_ref):
        g = pl.program_id(0)
        n = pl.program_id(1)

        @pl.when((g == 0) & (n == 0))
        def _():
            dgq_ref[...] = jnp.zeros_like(dgq_ref)

        @pl.when(n == 0)
        def _():
            ds_ref[...] = jnp.zeros_like(ds_ref)

        gkv = gk_ref[...]
        gqv = gq_ref[...]
        kn_b = jnp.concatenate([_head_rms(kp_ref[...], gkv)[0] * gkv, _head_rms(kc_ref[...], gkv)[0] * gkv],
                               axis=0).astype(BF16)
        vv_b = jnp.concatenate([vp_ref[...], vc_ref[...]], axis=0).astype(BF16)
        valid = _attn_mask(n)
        dkn = jnp.zeros((2 * WINDOW, HEAD_DIM), F32)
        dvv = jnp.zeros((2 * WINDOW, HEAD_DIM), F32)
        dgq = jnp.zeros((1, HEAD_DIM), F32)
        for i in range(GQA):
            qhat, r = _head_rms(q_ref[i], gqv)
            qn_b = (qhat * gqv).astype(BF16)
            probs, psink = _attn_probs(qn_b, kn_b, valid, s_ref[i][:, :1])
            do_b = do_ref[i].astype(BF16)
            dp = _dot(do_b, vv_b, NT)
            delta = jnp.sum(probs * dp, axis=-1, keepdims=True)
            dl_b = (probs * (dp - delta) * (HEAD_DIM ** -0.5)).astype(BF16)
            ds_ref[i] += jnp.broadcast_to(-jnp.sum(psink * delta, axis=0, keepdims=True), (1, 128))
            dqn = _dot(dl_b, kn_b)
            dkn += _dot(dl_b, qn_b, TN)
            dvv += _dot(probs.astype(BF16), do_b, TN)
            dgq += _colsum(dqn * qhat)
            dqhat = dqn * gqv
            dq_ref[i] = r * (dqhat - qhat * jnp.mean(dqhat * qhat, axis=-1, keepdims=True))
        dgq_ref[...] += dgq
        dkp_ref[...] = dkn[:WINDOW]
        dkc_ref[...] = dkn[WINDOW:]
        dvp_ref[...] = dvv[:WINDOW]
        dvc_ref[...] = dvv[WINDOW:]

    kv_shape = jax.ShapeDtypeStruct((nkv, s, HEAD_DIM), F32)
    return pl.pallas_call(
        body, name=name, grid=(nkv, nb), in_specs=[qspec, cur, prev, cur, prev, gain, gain, sink, qspec],
        out_specs=(qspec, cur, cur, cur, cur, gain, sink),
        out_shape=(jax.ShapeDtypeStruct((nq, s, HEAD_DIM), F32), kv_shape, kv_shape, kv_shape, kv_shape,
                   jax.ShapeDtypeStruct((1, HEAD_DIM), F32), jax.ShapeDtypeStruct((nq, 1, 128), F32)),
        compiler_params=_cp("arbitrary", "arbitrary"),
    )(q, k, k, v, v, gq.reshape(1, HEAD_DIM), gk.reshape(1, HEAD_DIM), sinks_b, do)


def _attn_bwd_kv(name, k, gk, dkc, dkp, dvc, dvp):
    nkv, s, _ = k.shape
    nb = s // WINDOW
    _, cur, _, nxt, gain, _ = _attn_specs(nb)

    def body(k_ref, gk_ref, dkc_ref, dkp_ref, dvc_ref, dvp_ref, dk_ref, dv_ref, dgk_ref):
        g = pl.program_id(0)
        n = pl.program_id(1)

        @pl.when((g == 0) & (n == 0))
        def _():
            dgk_ref[...] = jnp.zeros_like(dgk_ref)

        has_next = n < nb - 1
        dkn = dkc_ref[...] + jnp.where(has_next, dkp_ref[...], 0.0)
        dv_ref[...] = dvc_ref[...] + jnp.where(has_next, dvp_ref[...], 0.0)
        khat, r = _head_rms(k_ref[...], gk_ref[...])
        dgk_ref[...] += _colsum(dkn * khat)
        dkhat = dkn * gk_ref[...]
        dk_ref[...] = r * (dkhat - khat * jnp.mean(dkhat * khat, axis=-1, keepdims=True))

    kv_shape = jax.ShapeDtypeStruct((nkv, s, HEAD_DIM), F32)
    return pl.pallas_call(
        body, name=name, grid=(nkv, nb), in_specs=[cur, gain, cur, nxt, cur, nxt], out_specs=(cur, cur, gain),
        out_shape=(kv_shape, kv_shape, jax.ShapeDtypeStruct((1, HEAD_DIM), F32)),
        compiler_params=_cp("arbitrary", "arbitrary"),
    )(k, gk.reshape(1, HEAD_DIM), dkc, dkp, dvc, dvp)


def _conv_recompute(i, a_ref, gt_ref, ap_ref, gp_ref, w_ref, b_ref, hbuf, tb):
    hbuf[pl.ds(HALO, tb), :] = a_ref[...] * _sigmoid(gt_ref[...])
    tail = ap_ref[pl.ds(tb - HALO, HALO), :] * _sigmoid(gp_ref[pl.ds(tb - HALO, HALO), :])
    hbuf[pl.ds(0, HALO), :] = jnp.where(i > 0, tail, 0.0)
    acc = jnp.broadcast_to(b_ref[...], a_ref.shape)
    for kk in range(CONV_KERNEL):
        acc = acc + w_ref[pl.ds(kk, 1), :] * hbuf[pl.ds(HALO - (CONV_KERNEL - 1) + kk, tb), :]
    return acc


def _layer_norm_stats(c):
    mu = jnp.mean(c, axis=-1, keepdims=True)
    xc = c - mu
    r = lax.rsqrt(jnp.mean(xc * xc, axis=-1, keepdims=True) + EPS)
    return xc * r, r


def _conv_specs(s, cw, tb, a_blk):
    cur = lambda off: pl.BlockSpec((tb, cw), lambda i: (i, a_blk + off))
    prev = lambda off: pl.BlockSpec((tb, cw), lambda i: (jnp.maximum(i - 1, 0), a_blk + off))
    wspec = pl.BlockSpec((HALO, cw), lambda i: (0, 0))
    vec = pl.BlockSpec((1, cw), lambda i: (0, 0))
    row = pl.BlockSpec((tb, cw), lambda i: (i, 0))
    return cur, prev, wspec, vec, row


def _conv_fwd(name, proj, a_blk, w, b, lg, lb):
    s = proj.shape[0]
    cw = w.shape[1]
    tb = _tile(s, 256)
    cur, prev, wspec, vec, row = _conv_specs(s, cw, tb, a_blk)

    def body(a_ref, gt_ref, ap_ref, gp_ref, w_ref, b_ref, lg_ref, lb_ref, y_ref, hbuf):
        c = _conv_recompute(pl.program_id(0), a_ref, gt_ref, ap_ref, gp_ref, w_ref, b_ref, hbuf, tb)
        chat, _ = _layer_norm_stats(c)
        z = chat * lg_ref[...] + lb_ref[...]
        y_ref[...] = z * _sigmoid(z)

    return pl.pallas_call(
        body, name=name, grid=(s // tb,), in_specs=[cur(0), cur(1), prev(0), prev(1), wspec, vec, vec, vec],
        out_specs=row, out_shape=jax.ShapeDtypeStruct((s, cw), F32),
        scratch_shapes=[pltpu.VMEM((tb + HALO, cw), F32)], compiler_params=_cp("arbitrary"),
    )(proj, proj, proj, proj, w, b, lg, lb)


def _conv_bwd1(name, proj, a_blk, w, b, lg, lb, dy):
    s = proj.shape[0]
    cw = w.shape[1]
    tb = _tile(s, 256)
    cur, prev, wspec, vec, row = _conv_specs(s, cw, tb, a_blk)

    def body(a_ref, gt_ref, ap_ref, gp_ref, w_ref, b_ref, lg_ref, lb_ref, dy_ref,
             dc_ref, dw_ref, db_ref, dlg_ref, dlb_ref, hbuf):
        i = pl.program_id(0)

        @pl.when(i == 0)
        def _():
            dw_ref[...] = jnp.zeros_like(dw_ref)
            db_ref[...] = jnp.zeros_like(db_ref)
            dlg_ref[...] = jnp.zeros_like(dlg_ref)
            dlb_ref[...] = jnp.zeros_like(dlb_ref)

        c = _conv_recompute(i, a_ref, gt_ref, ap_ref, gp_ref, w_ref, b_ref, hbuf, tb)
        chat, r = _layer_norm_stats(c)
        z = chat * lg_ref[...] + lb_ref[...]
        sg = _sigmoid(z)
        dz = dy_ref[...] * (sg + z * sg * (1.0 - sg))
        dlg_ref[...] += _colsum(dz * chat)
        dlb_ref[...] += _colsum(dz)
        dzg = dz * lg_ref[...]
        dc = r * (dzg - jnp.mean(dzg, axis=-1, keepdims=True) - chat * jnp.mean(dzg * chat, axis=-1, keepdims=True))
        dc_ref[...] = dc
        db_ref[...] += _colsum(dc)
        for kk in range(CONV_KERNEL):
            dw_ref[pl.ds(kk, 1), :] += _colsum(dc * hbuf[pl.ds(HALO - (CONV_KERNEL - 1) + kk, tb), :])

    return pl.pallas_call(
        body, name=name, grid=(s // tb,), in_specs=[cur(0), cur(1), prev(0), prev(1), wspec, vec, vec, vec, row],
        out_specs=(row, wspec, vec, vec, vec),
        out_shape=(jax.ShapeDtypeStruct((s, cw), F32), jax.ShapeDtypeStruct((HALO, cw), F32),
                   jax.ShapeDtypeStruct((1, cw), F32), jax.ShapeDtypeStruct((1, cw), F32),
                   jax.ShapeDtypeStruct((1, cw), F32)),
        scratch_shapes=[pltpu.VMEM((tb + HALO, cw), F32)], compiler_params=_cp("arbitrary"),
    )(proj, proj, proj, proj, w, b, lg, lb, dy)


def _conv_bwd2(name, proj, a_blk, w, dc):
    s = proj.shape[0]
    cw = w.shape[1]
    tb = _tile(s, 256)
    nblk = s // tb
    cur, _, wspec, _, row = _conv_specs(s, cw, tb, a_blk)
    nxt = pl.BlockSpec((tb, cw), lambda i: (jnp.minimum(i + 1, nblk - 1), 0))

    def body(a_ref, gt_ref, w_ref, dc_ref, dn_ref, o_ref, dbuf):
        i = pl.program_id(0)
        dbuf[pl.ds(0, tb), :] = dc_ref[...]
        dbuf[pl.ds(tb, HALO), :] = jnp.where(i < nblk - 1, dn_ref[pl.ds(0, HALO), :], 0.0)
        dh = jnp.zeros((tb, cw), F32)
        for kk in range(CONV_KERNEL):
            dh = dh + w_ref[pl.ds(kk, 1), :] * dbuf[pl.ds(CONV_KERNEL - 1 - kk, tb), :]
        sg = _sigmoid(gt_ref[...])
        o_ref[:, 0:cw] = (dh * sg).astype(BF16)
        o_ref[:, cw:2 * cw] = (dh * a_ref[...] * sg * (1.0 - sg)).astype(BF16)

    return pl.pallas_call(
        body, name=name, grid=(nblk,), in_specs=[cur(0), cur(1), wspec, row, nxt],
        out_specs=pl.BlockSpec((tb, 2 * cw), lambda i: (i, 0)), out_shape=jax.ShapeDtypeStruct((s, 2 * cw), BF16),
        scratch_shapes=[pltpu.VMEM((tb + HALO, cw), F32)], compiler_params=_cp("arbitrary"),
    )(proj, proj, w, dc, dc)


def _sgu_common(v_ref, lg_ref, lb_ref, w_ref, bexp_ref, sw):
    vhat, r = _layer_norm_stats(v_ref[...])
    vn_b = (vhat * lg_ref[...] + lb_ref[...]).astype(BF16)
    ii = lax.broadcasted_iota(jnp.int32, (WINDOW, WINDOW), 0)
    jj = lax.broadcasted_iota(jnp.int32, (WINDOW, WINDOW), 1)
    tril = jj <= ii
    head_of = lax.broadcasted_iota(jnp.int32, (WINDOW, sw), 1) // HEAD_DIM
    wts = [jnp.where(tril, w_ref[h], 0.0).astype(BF16) for h in range(sw // HEAD_DIM)]
    sv = bexp_ref[...]
    for h, wt in enumerate(wts):
        sv = sv + jnp.where(head_of == h, _dot(wt, vn_b), 0.0)
    return vhat, r, vn_b, tril, head_of, wts, sv


def _sgu_specs(sw, u_blk):
    nh = sw // HEAD_DIM
    u = pl.BlockSpec((WINDOW, sw), lambda n: (n, u_blk))
    v = pl.BlockSpec((WINDOW, sw), lambda n: (n, u_blk + 1))
    vec = pl.BlockSpec((1, sw), lambda n: (0, 0))
    wspec = pl.BlockSpec((nh, WINDOW, WINDOW), lambda n: (0, 0, 0))
    bspec = pl.BlockSpec((WINDOW, sw), lambda n: (0, 0))
    row = pl.BlockSpec((WINDOW, sw), lambda n: (n, 0))
    return u, v, vec, wspec, bspec, row


def _sgu_fwd(name, proj, u_blk, lg, lb, w, bexp):
    s = proj.shape[0]
    sw = lg.shape[1]
    u, v, vec, wspec, bspec, row = _sgu_specs(sw, u_blk)

    def body(u_ref, v_ref, lg_ref, lb_ref, w_ref, bexp_ref, y_ref):
        sv = _sgu_common(v_ref, lg_ref, lb_ref, w_ref, bexp_ref, sw)[-1]
        y_ref[...] = u_ref[...] * sv

    return pl.pallas_call(
        body, name=name, grid=(s // WINDOW,), in_specs=[u, v, vec, vec, wspec, bspec], out_specs=row,
        out_shape=jax.ShapeDtypeStruct((s, sw), F32), compiler_params=_cp("parallel"),
    )(proj, proj, lg, lb, w, bexp)


def _sgu_bwd(name, proj, u_blk, lg, lb, w, bexp, dy):
    s = proj.shape[0]
    sw = lg.shape[1]
    nh = sw // HEAD_DIM
    u, v, vec, wspec, bspec, row = _sgu_specs(sw, u_blk)
    dbspec = pl.BlockSpec((nh, WINDOW), lambda n: (0, 0))

    def body(u_ref, v_ref, lg_ref, lb_ref, w_ref, bexp_ref, dy_ref, o_ref, dw_ref, db_ref, dlg_ref, dlb_ref):
        n = pl.program_id(0)

        @pl.when(n == 0)
        def _():
            dw_ref[...] = jnp.zeros_like(dw_ref)
            db_ref[...] = jnp.zeros_like(db_ref)
            dlg_ref[...] = jnp.zeros_like(dlg_ref)
            dlb_ref[...] = jnp.zeros_like(dlb_ref)

        vhat, r, vn_b, tril, head_of, wts, sv = _sgu_common(v_ref, lg_ref, lb_ref, w_ref, bexp_ref, sw)
        dyv = dy_ref[...]
        o_ref[:, 0:sw] = (dyv * sv).astype(BF16)
        ds = dyv * u_ref[...]
        dvn = jnp.zeros((WINDOW, sw), F32)
        for h, wt in enumerate(wts):
            dsm_b = jnp.where(head_of == h, ds, 0.0).astype(BF16)
            dvn = dvn + _dot(wt, dsm_b, TN)
            dw_ref[h] += jnp.where(tril, _dot(dsm_b, vn_b, NT), 0.0)
        hmask = (lax.broadcasted_iota(jnp.int32, (nh, sw), 1) // HEAD_DIM
                 == lax.broadcasted_iota(jnp.int32, (nh, sw), 0)).astype(F32)
        db_ref[...] += lax.dot_general(hmask, ds, NT, precision=lax.Precision.HIGHEST, preferred_element_type=F32)
        dlg_ref[...] += _colsum(dvn * vhat)
        dlb_ref[...] += _colsum(dvn)
        dvg = dvn * lg_ref[...]
        dv = r * (dvg - jnp.mean(dvg, axis=-1, keepdims=True) - vhat * jnp.mean(dvg * vhat, axis=-1, keepdims=True))
        o_ref[:, sw:2 * sw] = dv.astype(BF16)

    return pl.pallas_call(
        body, name=name, grid=(s // WINDOW,), in_specs=[u, v, vec, vec, wspec, bspec, row],
        out_specs=(pl.BlockSpec((WINDOW, 2 * sw), lambda n: (n, 0)), wspec, dbspec, vec, vec),
        out_shape=(jax.ShapeDtypeStruct((s, 2 * sw), BF16), jax.ShapeDtypeStruct((nh, WINDOW, WINDOW), F32),
                   jax.ShapeDtypeStruct((nh, WINDOW), F32), jax.ShapeDtypeStruct((1, sw), F32),
                   jax.ShapeDtypeStruct((1, sw), F32)),
        compiler_params=_cp("arbitrary"),
    )(proj, proj, lg, lb, w, bexp, dy)


def _adamw(name, w, g, m, v):
    rows, cols = w.shape
    tr = _tile(rows, 256)

    def body(w_ref, g_ref, m_ref, v_ref, d_ref, nm_ref, nv_ref):
        gv = g_ref[...]
        mv = ADAM_B1 * m_ref[...] + (1.0 - ADAM_B1) * gv
        vv = ADAM_B2 * v_ref[...] + (1.0 - ADAM_B2) * (gv * gv)
        m_hat = mv / (1.0 - ADAM_B1 ** ADAM_STEP)
        v_hat = vv / (1.0 - ADAM_B2 ** ADAM_STEP)
        d_ref[...] = -ADAM_LR * (m_hat / (jnp.sqrt(v_hat) + ADAM_EPS) + ADAM_WD * w_ref[...])
        nm_ref[...] = mv
        nv_ref[...] = vv

    spec = pl.BlockSpec((tr, cols), lambda i: (i, 0))
    shape = jax.ShapeDtypeStruct((rows, cols), F32)
    return pl.pallas_call(
        body, name=name, grid=(rows // tr,), in_specs=[spec] * 4, out_specs=(spec,) * 3, out_shape=(shape,) * 3,
        compiler_params=_cp("parallel"),
    )(w, g, m, v)


def _route():
    x, y, c = lax.axis_index("x"), lax.axis_index("y"), lax.axis_index("c")
    n1 = (jnp.where(c == 0, 1 - x, x), jnp.where(c == 0, y, 1 - y))
    n2 = (jnp.where(c == 0, x, 1 - x), jnp.where(c == 0, 1 - y, y))
    return x, y, c, n1, n2, (1 - x, 1 - y)


def _cidx(chip):
    return 2 * chip[0] + chip[1]


def _remote(src, dst, sems, k, device):
    send_sems, recv_sems = sems
    return pltpu.make_async_remote_copy(src_ref=src, dst_ref=dst, send_sem=send_sems.at[k], recv_sem=recv_sems.at[k],
                                        device_id=device, device_id_type=MESH)


def _exchange(name, bufs, n_sems, build):
    n = len(bufs)

    def body(*refs):
        cps = build(refs[n:2 * n], (refs[2 * n], refs[2 * n + 1]))
        for cp in cps:
            cp.start()
        for cp in cps:
            cp.wait()

    return pl.pallas_call(
        body, name=name, in_specs=[ANY] * n, out_specs=tuple(ANY for _ in range(n)),
        out_shape=tuple(jax.ShapeDtypeStruct(b.shape, b.dtype) for b in bufs),
        input_output_aliases={i: i for i in range(n)},
        scratch_shapes=[pltpu.SemaphoreType.DMA((n_sems,)), pltpu.SemaphoreType.DMA((n_sems,))],
        compiler_params=pltpu.CompilerParams(has_side_effects=True),
    )(*bufs)


def _cast_place(name, w, l, me_idx, dtype):
    _, r, c = w.shape
    tr = _tile(r, 512)

    def body(me_ref, w_ref, o_ref):
        o_ref[...] = w_ref[...].astype(dtype)

    grid_spec = pltpu.PrefetchScalarGridSpec(
        num_scalar_prefetch=1, grid=(r // tr,),
        in_specs=[pl.BlockSpec((None, tr, c), lambda i, me_ref: (l, i, 0))],
        out_specs=pl.BlockSpec((None, tr, c), lambda i, me_ref: (me_ref[0], i, 0)))
    return pl.pallas_call(
        body, name=name, grid_spec=grid_spec, out_shape=jax.ShapeDtypeStruct((N_CHIPS, r, c), dtype),
        compiler_params=_cp("arbitrary"),
    )(me_idx, w)


def _my_half(ref, blk, c):
    hr = ref.shape[1] // 2
    return ref.at[blk, pl.ds(c * hr, hr), :]


def _gather(tag, bufs):
    n = len(bufs)

    def stage1(refs, sems):
        x, y, c, n1, n2, dg = _route()
        return [_remote(_my_half(refs[t], _cidx((x, y)), c), _my_half(refs[t], _cidx((x, y)), c), sems, t, (*n1, c))
                for t in range(n)]

    def stage2(refs, sems):
        x, y, c, n1, n2, dg = _route()
        return [_remote(_my_half(refs[t], blk, c), _my_half(refs[t], blk, c), sems, 2 * t + b, (*n2, c))
                for t in range(n) for b, blk in enumerate((_cidx((x, y)), _cidx(n1)))]

    def stage3(refs, sems):
        x, y, c, n1, n2, dg = _route()
        return [_remote(_my_half(refs[t], blk, c), _my_half(refs[t], blk, c), sems, 3 * t + b, (x, y, 1 - c))
                for t in range(n) for b, blk in enumerate((_cidx(n1), _cidx(n2), _cidx(dg)))]

    bufs = _exchange(f"gather1_{tag}", bufs, n, stage1)
    bufs = _exchange(f"gather2_{tag}", bufs, 2 * n, stage2)
    return _exchange(f"gather3_{tag}", bufs, 3 * n, stage3)


RI_C, RI_ME, RI_N2 = 0, 1, 2


def _pair_sum(name, g, sib, route_idx):
    _, rows, cols = g.shape
    hr = rows // 2
    tr = _tile(hr, 512)
    per = hr // tr

    def body(ri, g_ref, s_ref, o_ref):
        o_ref[...] = (g_ref[...].astype(F32) + s_ref[...].astype(F32)).astype(BF16)

    blk = (None, tr, cols)
    grid_spec = pltpu.PrefetchScalarGridSpec(
        num_scalar_prefetch=1, grid=(N_CHIPS, per),
        in_specs=[pl.BlockSpec(blk, lambda j, i, ri: (j, ri[RI_C] * per + i, 0)),
                  pl.BlockSpec(blk, lambda j, i, ri: (j, i, 0))],
        out_specs=pl.BlockSpec(blk, lambda j, i, ri: (j, i, 0)))
    return pl.pallas_call(
        body, name=name, grid_spec=grid_spec, out_shape=jax.ShapeDtypeStruct((N_CHIPS, hr, cols), BF16),
        compiler_params=_cp("parallel", "parallel"),
    )(route_idx, g, sib)


def _sum_stage1(name, p, got, route_idx):
    _, hr, cols = p.shape
    tr = _tile(hr, 512)

    def body(ri, pm_ref, pn_ref, g0_ref, g1_ref, keep_ref, send_ref):
        keep_ref[...] = pm_ref[...].astype(F32) + g0_ref[...].astype(F32)
        send_ref[...] = (pn_ref[...].astype(F32) + g1_ref[...].astype(F32)).astype(BF16)

    blk = (None, tr, cols)
    row = pl.BlockSpec((tr, cols), lambda i, ri: (i, 0))
    grid_spec = pltpu.PrefetchScalarGridSpec(
        num_scalar_prefetch=1, grid=(hr // tr,),
        in_specs=[pl.BlockSpec(blk, lambda i, ri: (ri[RI_ME], i, 0)), pl.BlockSpec(blk, lambda i, ri: (ri[RI_N2], i, 0)),
                  pl.BlockSpec(blk, lambda i, ri: (0, i, 0)), pl.BlockSpec(blk, lambda i, ri: (1, i, 0))],
        out_specs=(row, row))
    return pl.pallas_call(
        body, name=name, grid_spec=grid_spec,
        out_shape=(jax.ShapeDtypeStruct((hr, cols), F32), jax.ShapeDtypeStruct((hr, cols), BF16)),
        compiler_params=_cp("parallel"),
    )(route_idx, p, p, got, got)


def _sum_stage2(name, keep, got):
    hr, cols = keep.shape
    tr = _tile(hr, 512)

    def body(k_ref, g_ref, o_ref):
        o_ref[...] = k_ref[...] + g_ref[...].astype(F32)

    row = pl.BlockSpec((tr, cols), lambda i: (i, 0))
    return pl.pallas_call(
        body, name=name, grid=(hr // tr,), in_specs=[row, row], out_specs=row,
        out_shape=jax.ShapeDtypeStruct((hr, cols), F32), compiler_params=_cp("parallel"),
    )(keep, got)


def _reduce_scatter(tag, names, grads, route_idx):
    n = len(grads)
    hrs = [g.shape[1] // 2 for g in grads]

    def empty(t, lead, dtype):
        return lax.empty(lead + (hrs[t], grads[t].shape[2]), dtype)

    def pair_stage(refs, sems):
        x, y, c, n1, n2, dg = _route()
        return [_remote(refs[t].at[:, pl.ds((1 - c) * hrs[t], hrs[t]), :], refs[n + t], sems, t, (x, y, 1 - c))
                for t in range(n)]

    def stage1(refs, sems):
        x, y, c, n1, n2, dg = _route()
        return [_remote(refs[t].at[blk], refs[n + t].at[slot], sems, 2 * t + slot, (*n1, c))
                for t in range(n) for slot, blk in enumerate((_cidx(n1), _cidx(dg)))]

    def stage2(refs, sems):
        x, y, c, n1, n2, dg = _route()
        return [_remote(refs[t], refs[n + t], sems, t, (*n2, c)) for t in range(n)]

    def stage3(refs, sems):
        x, y, c, n1, n2, dg = _route()
        return [_remote(refs[t], refs[n + t], sems, t, (x, y, 1 - c)) for t in range(n)]

    out = _exchange(f"rs_pair_{tag}", list(grads) + [empty(t, (N_CHIPS,), BF16) for t in range(n)], n, pair_stage)
    grads, sib = out[:n], out[n:]
    psum = [_pair_sum(f"rs_psum_{names[t]}", grads[t], sib[t], route_idx) for t in range(n)]
    out = _exchange(f"rs_x1_{tag}", psum + [empty(t, (2,), BF16) for t in range(n)], 2 * n, stage1)
    psum, got1 = out[:n], out[n:]
    keep, send = zip(*[_sum_stage1(f"rs_sum1_{names[t]}", psum[t], got1[t], route_idx) for t in range(n)])
    out = _exchange(f"rs_x2_{tag}", list(send) + [empty(t, (), BF16) for t in range(n)], n, stage2)
    got2 = out[n:]
    mine = [_sum_stage2(f"rs_sum2_{names[t]}", keep[t], got2[t]) for t in range(n)]
    out = _exchange(f"rs_half_{tag}", mine + [empty(t, (), F32) for t in range(n)], n, stage3)
    return list(zip(out[:n], out[n:]))


def _adamw_big(name, w, m, v, f, h, l, c_idx, prev):
    n_l, r, cols = w.shape
    hr = r // 2
    tr = _tile(hr, 256)
    per = hr // tr

    def body(c_ref, w_ref, m_ref, v_ref, f_ref, h_ref, *rest):
        g_ref, d_ref, nm_ref, nv_ref = rest[-4:]
        gv = jnp.where(pl.program_id(0) == c_ref[0], f_ref[...], h_ref[...])
        mv = ADAM_B1 * m_ref[...] + (1.0 - ADAM_B1) * gv
        vv = ADAM_B2 * v_ref[...] + (1.0 - ADAM_B2) * (gv * gv)
        m_hat = mv / (1.0 - ADAM_B1 ** ADAM_STEP)
        v_hat = vv / (1.0 - ADAM_B2 ** ADAM_STEP)
        g_ref[...] = gv
        d_ref[...] = -ADAM_LR * (m_hat / (jnp.sqrt(v_hat) + ADAM_EPS) + ADAM_WD * w_ref[...])
        nm_ref[...] = mv
        nv_ref[...] = vv

    big = pl.BlockSpec((None, tr, cols), lambda hf, i, c_ref: (l, hf * per + i, 0))
    fspec = pl.BlockSpec((tr, cols), lambda hf, i, c_ref: (jnp.where(hf == c_ref[0], i, 0), 0))
    hspec = pl.BlockSpec((tr, cols), lambda hf, i, c_ref: (jnp.where(hf == c_ref[0], 0, i), 0))
    grid_spec = pltpu.PrefetchScalarGridSpec(
        num_scalar_prefetch=1, grid=(2, per), in_specs=[big] * 3 + [fspec, hspec] + [ANY] * len(prev),
        out_specs=(big,) * 4)
    return pl.pallas_call(
        body, name=name, grid_spec=grid_spec, out_shape=(jax.ShapeDtypeStruct(w.shape, F32),) * 4,
        input_output_aliases={6 + k: k for k in range(len(prev))}, compiler_params=_cp("arbitrary", "arbitrary"),
    )(c_idx, w, m, v, f, h, *prev)


def _small_allreduce(buf):
    rows = buf.shape[0]
    hr = rows // 2

    def body(in_ref, out_ref, pair, acc, got1, got2, send_sems, recv_sems):
        x, y, c, n1, n2, dg = _route()
        sems = (send_sems, recv_sems)
        sibling = (x, y, 1 - c)
        mine = pl.ds(pl.multiple_of(c * hr, 8), hr)
        pair[c] = in_ref[...]
        cp = _remote(in_ref, pair.at[c], sems, 0, sibling)
        cp.start()
        cp.wait()
        acc[...] = pair[0, mine, :] + pair[1, mine, :]
        cp = _remote(acc, got1, sems, 1, (*n1, c))
        cp.start()
        cp.wait()
        acc[...] = acc[...] + got1[...]
        cp = _remote(acc, got2, sems, 2, (*n2, c))
        cp.start()
        cp.wait()
        out_ref[mine, :] = acc[...] + got2[...]
        cp = _remote(out_ref.at[mine, :], out_ref.at[mine, :], sems, 3, sibling)
        cp.start()
        cp.wait()

    half = pltpu.VMEM((hr, 128), F32)
    return pl.pallas_call(
        body, name="small_allreduce", in_specs=[pl.BlockSpec(memory_space=pltpu.VMEM)],
        out_specs=pl.BlockSpec(memory_space=pltpu.VMEM), out_shape=jax.ShapeDtypeStruct((rows, 128), F32),
        scratch_shapes=[pltpu.VMEM((2, rows, 128), F32), half, half, half,
                        pltpu.SemaphoreType.DMA((4,)), pltpu.SemaphoreType.DMA((4,))],
        compiler_params=pltpu.CompilerParams(has_side_effects=True, vmem_limit_bytes=VMEM_LIMIT),
    )(buf)


BIG = ("w_in", "w_out", "w_up", "w_down")
COL_SHARDED = {"w_in": True, "w_out": False, "w_up": True, "w_down": False}
SMALL = ("ln1_g", "q_norm_g", "k_norm_g", "sinks", "conv_w", "conv_b", "conv_ln_g", "conv_ln_b", "sgu_ln_g",
         "sgu_ln_b", "sgu_w", "sgu_b", "out_norm_g", "ln2_g")
WEIGHTS = ("ln1_g", "w_in", "q_norm_g", "k_norm_g", "sinks", "conv_w", "conv_b", "conv_ln_g", "conv_ln_b",
           "sgu_ln_g", "sgu_ln_b", "sgu_w", "sgu_b", "out_norm_g", "w_out", "ln2_g", "w_up", "w_down")
PACK_QUANTUM = 8 * 128
PACK_ROWS = 512


def _pack(arrs):
    parts = []
    for a in arrs:
        f = a.reshape(-1)
        parts.append(jnp.pad(f, (0, -f.shape[0] % PACK_QUANTUM)).reshape(-1, 128))
    rows = sum(p.shape[0] for p in parts)
    parts.append(jnp.zeros((-rows % PACK_ROWS, 128), F32))
    return jnp.concatenate(parts, axis=0)


def _unpack(buf, shapes):
    out, off = [], 0
    for shp in shapes:
        n = 1
        for dd in shp:
            n *= dd
        rows = (n + PACK_QUANTUM - 1) // PACK_QUANTUM * 8
        out.append(buf[off:off + rows].reshape(-1)[:n].reshape(shp))
        off += rows
    return out


def _to_heads(t, nh):
    return t.reshape(t.shape[0], nh, HEAD_DIM).transpose(1, 0, 2)


def _from_heads(t):
    return t.transpose(1, 0, 2).reshape(t.shape[1], t.shape[0] * HEAD_DIM)


def _layer_fwd(l, x, p, wg):
    d = x.shape[1]
    aw, cw = d // 2, d // 4
    nq = aw // HEAD_DIM
    nkv = nq // GQA
    kvw = nkv * HEAD_DIM
    h1 = _rms_fwd(f"ln1_fwd_{l}", x, p["ln1_g"])
    proj = _mm_act_w(f"proj_{l}", h1, wg["w_in"], True, _ep_store)[0]
    q = _to_heads(proj[:, :aw], nq)
    k = _to_heads(proj[:, aw:aw + kvw], nkv)
    v = _to_heads(proj[:, aw + kvw:aw + 2 * kvw], nkv)
    sinks_b = jnp.broadcast_to(p["sinks"][:, None, None], (nq, 1, 128))
    ya = _from_heads(_attn_fwd(f"attn_fwd_{l}", q, k, v, p["q_norm_g"], p["k_norm_g"], sinks_b))
    yc = _conv_fwd(f"conv_fwd_{l}", proj, 3, p["conv_w"], p["conv_b"], p["conv_ln_g"], p["conv_ln_b"])
    ys = _sgu_fwd(f"sgu_fwd_{l}", proj, 5, p["sgu_ln_g"], p["sgu_ln_b"], p["sgu_w"], p["sgu_bexp"])
    mix = _mixnorm_fwd(f"mixnorm_fwd_{l}", ya, yc, ys, p["out_norm_g"])
    xm = _mm_act_w(f"out_{l}", mix, wg["w_out"], False, _ep_residual, extra=(x,))[0]
    h2 = _rms_fwd(f"ln2_fwd_{l}", xm, p["ln2_g"])
    up_b, act_b = _mm_act_w(f"up_{l}", h2, wg["w_up"], True, _ep_up, out_dtypes=(BF16, BF16))
    xo = _mm_act_w(f"down_{l}", act_b, wg["w_down"], False, _ep_residual, extra=(xm,))[0]
    saved = dict(x=x, h1=h1, proj=proj, q=q, k=k, v=v, sinks_b=sinks_b, ya=ya, yc=yc, ys=ys, mix=mix, xm=xm, h2=h2,
                 up_b=up_b, act_b=act_b)
    return xo, saved


def _layer_bwd(l, dxo, dxo_b, p, wg, sv):
    d = dxo.shape[1]
    nq = (d // 2) // HEAD_DIM
    big, small = {}, {}
    big["w_down"] = _mm_wgrad(f"dw_down_{l}", sv["act_b"], dxo_b, False, d)
    dup_b = _mm_act_wt(f"dup_{l}", dxo_b, wg["w_down"], False, _ep_dup, extra=(sv["up_b"],), out_dtypes=(BF16,))[0]
    big["w_up"] = _mm_wgrad(f"dw_up_{l}", sv["h2"], dup_b, True, wg["w_up"].shape[2])
    dh2 = _mm_act_wt(f"dh2_{l}", dup_b, wg["w_up"], True, _ep_store)[0]
    dxm, dxm_b, small["ln2_g"] = _rms_bwd(f"ln2_bwd_{l}", dh2, sv["xm"], p["ln2_g"], dxo)
    big["w_out"] = _mm_wgrad(f"dw_out_{l}", sv["mix"], dxm_b, False, d)
    dmix = _mm_act_wt(f"dmix_{l}", dxm_b, wg["w_out"], False, _ep_store)[0]
    dya, dyc, dys, small["out_norm_g"] = _mixnorm_bwd(f"mixnorm_bwd_{l}", dmix, sv["ya"], sv["yc"], sv["ys"],
                                                      p["out_norm_g"])
    dq, dkc, dkp, dvc, dvp, small["q_norm_g"], dsink = _attn_bwd(
        f"attn_bwd_{l}", sv["q"], sv["k"], sv["v"], p["q_norm_g"], p["k_norm_g"], sv["sinks_b"], _to_heads(dya, nq))
    small["sinks"] = dsink[:, 0, 0]
    dk, dv, small["k_norm_g"] = _attn_bwd_kv(f"attn_bwd_kv_{l}", sv["k"], p["k_norm_g"], dkc, dkp, dvc, dvp)
    dc, dcw, small["conv_b"], small["conv_ln_g"], small["conv_ln_b"] = _conv_bwd1(
        f"conv_bwd1_{l}", sv["proj"], 3, p["conv_w"], p["conv_b"], p["conv_ln_g"], p["conv_ln_b"], dyc)
    small["conv_w"] = dcw[:CONV_KERNEL]
    dxc_b = _conv_bwd2(f"conv_bwd2_{l}", sv["proj"], 3, p["conv_w"], dc)
    dxs_b, small["sgu_w"], small["sgu_b"], small["sgu_ln_g"], small["sgu_ln_b"] = _sgu_bwd(
        f"sgu_bwd_{l}", sv["proj"], 5, p["sgu_ln_g"], p["sgu_ln_b"], p["sgu_w"], p["sgu_bexp"], dys)
    dproj_b = jnp.concatenate([_from_heads(dq).astype(BF16), _from_heads(dk).astype(BF16),
                               _from_heads(dv).astype(BF16), dxc_b, dxs_b], axis=1)
    big["w_in"] = _mm_wgrad(f"dw_in_{l}", sv["h1"], dproj_b, True, wg["w_in"].shape[2])
    dh1 = _mm_act_wt(f"dh1_{l}", dproj_b, wg["w_in"], True, _ep_store)[0]
    dx, dx_b, small["ln1_g"] = _rms_bwd(f"ln1_bwd_{l}", dh1, sv["x"], p["ln1_g"], dxm)
    return dx, dx_b, big, small


def kernel(x, ln1_g, w_in, q_norm_g, k_norm_g, sinks, conv_w, conv_b, conv_ln_g, conv_ln_b, sgu_ln_g, sgu_ln_b, sgu_w, sgu_b, out_norm_g, w_out, ln2_g, w_up, w_down, loss_target, m_ln1_g, m_w_in, m_q_norm_g, m_k_norm_g, m_sinks, m_conv_w, m_conv_b, m_conv_ln_g, m_conv_ln_b, m_sgu_ln_g, m_sgu_ln_b, m_sgu_w, m_sgu_b, m_out_norm_g, m_w_out, m_ln2_g, m_w_up, m_w_down, v_ln1_g, v_w_in, v_q_norm_g, v_k_norm_g, v_sinks, v_conv_w, v_conv_b, v_conv_ln_g, v_conv_ln_b, v_sgu_ln_g, v_sgu_ln_b, v_sgu_w, v_sgu_b, v_out_norm_g, v_w_out, v_ln2_g, v_w_up, v_w_down):
    given = dict(locals())
    n_layers = ln1_g.shape[0]
    s, d = x.shape[1], x.shape[2]
    cw = d // 4
    xi, yi, core = lax.axis_index("x"), lax.axis_index("y"), lax.axis_index("c")
    chip = 2 * xi + yi
    second = jnp.where(core == 0, 2 * xi + (1 - yi), 2 * (1 - xi) + yi)
    route_idx = jnp.stack([core, chip, second]).astype(jnp.int32)

    conv_w_pad = jnp.pad(conv_w, ((0, 0), (0, HALO - CONV_KERNEL), (0, 0))).reshape(1, n_layers * HALO, -1)
    cwl = conv_w_pad.shape[2]
    placed = [[_cast_place(f"place_{nm}_{l}", given[nm], l, route_idx[1:2], BF16) for nm in BIG]
              for l in range(n_layers)]
    first = _gather("0", [_cast_place("place_conv_w", conv_w_pad, 0, route_idx[1:2], F32)] + placed[0])
    conv_w_full = first[0].reshape(N_CHIPS, n_layers, HALO, cwl).transpose(1, 2, 0, 3).reshape(n_layers, HALO, cw)
    gathered = [first[1:]] + [_gather(str(l), placed[l]) for l in range(1, n_layers)]
    wgs = [dict(zip(BIG, gathered[l])) for l in range(n_layers)]
    params = []
    for l in range(n_layers):
        p = {nm: given[nm][l] for nm in SMALL if nm != "conv_w"}
        for nm in ("conv_b", "conv_ln_g", "conv_ln_b", "sgu_ln_g", "sgu_ln_b"):
            p[nm] = p[nm].reshape(1, -1)
        p["conv_w"] = conv_w_full[l]
        p["sgu_bexp"] = jnp.repeat(sgu_b[l].T, HEAD_DIM, axis=1)
        params.append(p)

    h = x.reshape(s, d)
    saved = []
    for l in range(n_layers):
        h, sv = _layer_fwd(l, h, params[l], wgs[l])
        saved.append(sv)
    dh, dh_b, loss_part = _loss_head(h, loss_target.reshape(s, d))
    loss = lax.psum(loss_part[0, 0], ("x", "y", "c"))
    big_grads, small_grads = [None] * n_layers, [None] * n_layers
    for l in reversed(range(n_layers)):
        dh, dh_b, big_grads[l], small_grads[l] = _layer_bwd(l, dh, dh_b, params[l], wgs[l], saved[l])
    grad_x = dh.reshape(x.shape)

    order = [(nm, l) for l in range(n_layers) for nm in BIG]
    halves = dict(zip(order, _reduce_scatter("all", [f"{nm}_{l}" for nm, l in order],
                                             [big_grads[l][nm] for nm, l in order], route_idx)))
    grads = {}

    small_shapes = [(n_layers,) + small_grads[0][nm].shape for nm in SMALL]
    small_sum = _small_allreduce(_pack([jnp.stack([small_grads[l][nm] for l in range(n_layers)]) for nm in SMALL]))
    for nm, g in zip(SMALL, _unpack(small_sum, small_shapes)):
        grads[nm] = g.reshape((n_layers,) + given[nm].shape[1:]) if nm != "conv_w" else g
    grads["conv_w"] = lax.dynamic_slice_in_dim(grads["conv_w"], chip * cwl, cwl, axis=2)

    delta, new_m, new_v = {}, {}, {}
    for nm in BIG:
        res = ()
        for l in range(n_layers):
            f, h = halves[(nm, l)]
            res = _adamw_big(f"adamw_{nm}_{l}", given[nm], given["m_" + nm], given["v_" + nm], f, h, l,
                             route_idx[0:1], res)
        grads[nm], delta[nm], new_m[nm], new_v[nm] = res
    packed = [_pack([src[nm] for nm in SMALL]) for src in
              ({nm: given[nm] for nm in SMALL}, grads, {nm: given["m_" + nm] for nm in SMALL},
               {nm: given["v_" + nm] for nm in SMALL})]
    local_shapes = [given[nm].shape for nm in SMALL]
    for dst, buf in zip((delta, new_m, new_v), _adamw("adamw_small", *packed)):
        for nm, a in zip(SMALL, _unpack(buf, local_shapes)):
            dst[nm] = a
    return (loss, grad_x, *[grads[nm] for nm in WEIGHTS], *[delta[nm] for nm in WEIGHTS],
            *[new_m[nm] for nm in WEIGHTS], *[new_v[nm] for nm in WEIGHTS])
```

```python
import functools

import jax
import jax.numpy as jnp
from jax import lax
from jax.experimental import pallas as pl
from jax.experimental.pallas import tpu as pltpu

F32 = jnp.float32
BF16 = jnp.bfloat16
EPS = 1e-6
NEG_INF = -1e30
HEAD_DIM = 64
WINDOW = 128
CONV_KERNEL = 31
HALO = 32
GQA = 4
N_CHIPS = 4
ADAM_LR, ADAM_B1, ADAM_B2, ADAM_EPS, ADAM_WD, ADAM_STEP = 0.001, 0.9, 0.999, 1e-08, 0.01, 10
VMEM_LIMIT = 56 * 1024 * 1024
MESH = pl.DeviceIdType.MESH
ANY = pl.BlockSpec(memory_space=pl.ANY)

NN = (((1,), (0,)), ((), ()))
NT = (((1,), (1,)), ((), ()))
TN = (((0,), (0,)), ((), ()))


def _cp(*sem):
    return pltpu.CompilerParams(dimension_semantics=sem, vmem_limit_bytes=VMEM_LIMIT)


_LAST = [None]


def _ordered_call(body, *, in_specs=None, grid_spec=None, **kw):
    def run(*operands):
        dep = _LAST[0]
        n = len(operands)
        if dep is None:
            fn, specs, spec, args = body, in_specs, grid_spec, operands
        else:
            def fn(*refs):
                body(*refs[:n], *refs[n + 1:])

            specs, spec, args = in_specs, grid_spec, operands + (dep,)
            if grid_spec is not None:
                spec = pltpu.PrefetchScalarGridSpec(
                    num_scalar_prefetch=grid_spec.num_scalar_prefetch, grid=grid_spec.grid,
                    in_specs=list(grid_spec.in_specs) + [ANY], out_specs=grid_spec.out_specs)
            else:
                specs = list(in_specs) + [ANY]
        if spec is not None:
            out = pl.pallas_call(fn, grid_spec=spec, **kw)(*args)
        else:
            out = pl.pallas_call(fn, in_specs=specs, **kw)(*args)
        arrays = [o for o in (out if isinstance(out, (tuple, list)) else (out,)) if o.dtype in (F32, BF16)]
        _LAST[0] = min(arrays, key=lambda o: o.size)
        return out

    return run


def _tile(dim, pref):
    if dim <= pref:
        return dim
    for t in range(pref, 0, -128):
        if dim % t == 0:
            return t
    while dim % pref:
        pref //= 2
    return pref


def _dot(a, b, dims=NN):
    return lax.dot_general(a, b, dims, preferred_element_type=F32)


def _colsum(v):
    return jnp.sum(v, axis=0, keepdims=True)


def _sigmoid(x):
    return 1.0 / (1.0 + jnp.exp(-x))


def _matmul(name, operands, in_specs, out_shape, out_specs, grid, dims, acc_shape, epilogue):
    nk = grid[2]
    n_in = len(operands)

    def body(*refs):
        a_ref, b_ref = refs[0], refs[1]
        extra = refs[2:n_in]
        outs = refs[n_in:-1]
        acc = refs[-1]
        k = pl.program_id(2)

        @pl.when(k == 0)
        def _():
            acc[...] = jnp.zeros_like(acc)

        acc[...] += _dot(a_ref[...], b_ref[...], dims)

        @pl.when(k == nk - 1)
        def _():
            epilogue(acc[...], extra, outs)

    return _ordered_call(
        body, name=name, grid=grid, in_specs=in_specs, out_specs=out_specs, out_shape=out_shape,
        scratch_shapes=[pltpu.VMEM(acc_shape, F32)],
        compiler_params=_cp("parallel", "parallel", "arbitrary"),
    )(*operands)


def _ep_store(acc, extra, outs):
    outs[0][...] = acc.astype(outs[0].dtype)


def _ep_residual(acc, extra, outs):
    outs[0][...] = extra[0][...] + acc


def _ep_up(acc, extra, outs):
    outs[0][...] = acc.astype(BF16)
    r = jnp.maximum(acc, 0.0)
    outs[1][...] = (r * r).astype(BF16)


def _ep_dup(acc, extra, outs):
    outs[0][...] = (acc * (2.0 * jnp.maximum(extra[0][...].astype(F32), 0.0))).astype(BF16)


def _mm_act_w(name, a, wg, col_sharded, epilogue, extra=(), out_dtypes=(F32,)):
    m, kdim = a.shape
    _, r, c = wg.shape
    tm = _tile(m, 1024)
    if col_sharded:
        n = N_CHIPS * c
        tn = _tile(c, 1024)
        tk = _tile(kdim, 1024)
        per = c // tn
        b_spec = pl.BlockSpec((None, tk, tn), lambda i, j, k: (j // per, k, j % per))
    else:
        n = c
        tn = _tile(n, 1024)
        tk = _tile(r, 1024)
        per = r // tk
        b_spec = pl.BlockSpec((None, tk, tn), lambda i, j, k: (k // per, k % per, j))
    grid = (m // tm, n // tn, kdim // tk)
    o_spec = pl.BlockSpec((tm, tn), lambda i, j, k: (i, j))
    in_specs = [pl.BlockSpec((tm, tk), lambda i, j, k: (i, k)), b_spec] + [o_spec] * len(extra)
    return _matmul(name, (a, wg) + tuple(extra), in_specs,
                   tuple(jax.ShapeDtypeStruct((m, n), d) for d in out_dtypes),
                   tuple(o_spec for _ in out_dtypes), grid, NN, (tm, tn), epilogue)


def _mm_act_wt(name, a, wg, col_sharded, epilogue, extra=(), out_dtypes=(F32,)):
    m, kdim = a.shape
    _, r, c = wg.shape
    tm = _tile(m, 1024)
    if col_sharded:
        n = r
        tn = _tile(n, 1024)
        tk = _tile(c, 1024)
        per = c // tk
        b_spec = pl.BlockSpec((None, tn, tk), lambda i, j, k: (k // per, j, k % per))
    else:
        n = N_CHIPS * r
        tn = _tile(r, 1024)
        tk = _tile(c, 1024)
        per = r // tn
        b_spec = pl.BlockSpec((None, tn, tk), lambda i, j, k: (j // per, j % per, k))
    grid = (m // tm, n // tn, kdim // tk)
    o_spec = pl.BlockSpec((tm, tn), lambda i, j, k: (i, j))
    in_specs = [pl.BlockSpec((tm, tk), lambda i, j, k: (i, k)), b_spec] + [o_spec] * len(extra)
    return _matmul(name, (a, wg) + tuple(extra), in_specs,
                   tuple(jax.ShapeDtypeStruct((m, n), d) for d in out_dtypes),
                   tuple(o_spec for _ in out_dtypes), grid, NT, (tm, tn), epilogue)


def _mm_wgrad(name, a, g, col_sharded, c):
    s, kdim = a.shape
    _, n = g.shape
    ts = _tile(s, 1024)
    if col_sharded:
        r = kdim
        tm = _tile(kdim, 1024)
        tn = _tile(c, 1024)
        per = c // tn
        o_spec = pl.BlockSpec((None, tm, tn), lambda i, j, k: (j // per, i, j % per))
    else:
        r = kdim // N_CHIPS
        tm = _tile(r, 1024)
        tn = _tile(c, 1024)
        per = r // tm
        o_spec = pl.BlockSpec((None, tm, tn), lambda i, j, k: (i // per, i % per, j))
    grid = (kdim // tm, n // tn, s // ts)
    in_specs = [pl.BlockSpec((ts, tm), lambda i, j, k: (k, i)), pl.BlockSpec((ts, tn), lambda i, j, k: (k, j))]
    return _matmul(name, (a, g), in_specs, (jax.ShapeDtypeStruct((N_CHIPS, r, c), BF16),), (o_spec,),
                   grid, TN, (tm, tn), _ep_store)[0]


def _rms_fwd(name, x, g):
    s, d = x.shape
    tb = _tile(s, 256)

    def body(x_ref, g_ref, o_ref):
        xv = x_ref[...]
        r = lax.rsqrt(jnp.mean(xv * xv, axis=-1, keepdims=True) + EPS)
        o_ref[...] = (xv * r * g_ref[...]).astype(BF16)

    return _ordered_call(
        body, name=name, grid=(s // tb,),
        in_specs=[pl.BlockSpec((tb, d), lambda i: (i, 0)), pl.BlockSpec((1, d), lambda i: (0, 0))],
        out_specs=pl.BlockSpec((tb, d), lambda i: (i, 0)),
        out_shape=jax.ShapeDtypeStruct((s, d), BF16), compiler_params=_cp("parallel"),
    )(x, g.reshape(1, d))


def _rms_bwd(name, dh, x, g, dres):
    s, d = x.shape
    tb = _tile(s, 256)

    def body(dh_ref, x_ref, g_ref, dres_ref, dx_ref, dxb_ref, dg_ref):
        i = pl.program_id(0)
        xv = x_ref[...]
        r = lax.rsqrt(jnp.mean(xv * xv, axis=-1, keepdims=True) + EPS)
        xhat = xv * r
        dhv = dh_ref[...]
        dxhat = dhv * g_ref[...]
        dx = dres_ref[...] + r * (dxhat - xhat * jnp.mean(dxhat * xhat, axis=-1, keepdims=True))
        dx_ref[...] = dx
        dxb_ref[...] = dx.astype(BF16)

        @pl.when(i == 0)
        def _():
            dg_ref[...] = jnp.zeros_like(dg_ref)

        dg_ref[...] += _colsum(dhv * xhat)

    row = pl.BlockSpec((tb, d), lambda i: (i, 0))
    vec = pl.BlockSpec((1, d), lambda i: (0, 0))
    return _ordered_call(
        body, name=name, grid=(s // tb,), in_specs=[row, row, vec, row], out_specs=(row, row, vec),
        out_shape=(jax.ShapeDtypeStruct((s, d), F32), jax.ShapeDtypeStruct((s, d), BF16),
                   jax.ShapeDtypeStruct((1, d), F32)),
        compiler_params=_cp("arbitrary"),
    )(dh, x, g.reshape(1, d), dres)


def _loss_head(y, t):
    s, d = y.shape
    tb = _tile(s, 256)

    def body(y_ref, t_ref, dy_ref, dyb_ref, loss_ref, acc):
        i = pl.program_id(0)
        e = y_ref[...] - t_ref[...]
        dy = e * (1.0 / d)
        dy_ref[...] = dy
        dyb_ref[...] = dy.astype(BF16)

        @pl.when(i == 0)
        def _():
            acc[...] = jnp.zeros_like(acc)

        acc[...] += _colsum(e * e)

        @pl.when(i == pl.num_programs(0) - 1)
        def _():
            loss_ref[...] = jnp.sum(acc[...], axis=-1, keepdims=True) * (0.5 / d)

    row = pl.BlockSpec((tb, d), lambda i: (i, 0))
    return _ordered_call(
        body, name="loss_head", grid=(s // tb,), in_specs=[row, row],
        out_specs=(row, row, pl.BlockSpec((1, 1), lambda i: (0, 0))),
        out_shape=(jax.ShapeDtypeStruct((s, d), F32), jax.ShapeDtypeStruct((s, d), BF16),
                   jax.ShapeDtypeStruct((1, 1), F32)),
        scratch_shapes=[pltpu.VMEM((1, d), F32)], compiler_params=_cp("arbitrary"),
    )(y, t)


def _mixnorm_fwd(name, ya, yc, ys, g):
    s, aw = ya.shape
    cw, sw = yc.shape[1], ys.shape[1]
    d = aw + cw + sw
    tb = _tile(s, 256)

    def body(ya_ref, yc_ref, ys_ref, g_ref, o_ref):
        off = 0
        for ref, w in ((ya_ref, aw), (yc_ref, cw), (ys_ref, sw)):
            v = ref[...]
            r = lax.rsqrt(jnp.mean(v * v, axis=-1, keepdims=True) + EPS)
            o_ref[:, off:off + w] = (v * r * g_ref[:, off:off + w]).astype(BF16)
            off += w

    def row(w):
        return pl.BlockSpec((tb, w), lambda i: (i, 0))

    return _ordered_call(
        body, name=name, grid=(s // tb,),
        in_specs=[row(aw), row(cw), row(sw), pl.BlockSpec((1, d), lambda i: (0, 0))], out_specs=row(d),
        out_shape=jax.ShapeDtypeStruct((s, d), BF16), compiler_params=_cp("parallel"),
    )(ya, yc, ys, g.reshape(1, d))


def _mixnorm_bwd(name, dmix, ya, yc, ys, g):
    s, aw = ya.shape
    cw, sw = yc.shape[1], ys.shape[1]
    d = aw + cw + sw
    tb = _tile(s, 256)

    def body(dm_ref, ya_ref, yc_ref, ys_ref, g_ref, dya_ref, dyc_ref, dys_ref, dg_ref):
        i = pl.program_id(0)

        @pl.when(i == 0)
        def _():
            dg_ref[...] = jnp.zeros_like(dg_ref)

        off = 0
        for ref, dref, w in ((ya_ref, dya_ref, aw), (yc_ref, dyc_ref, cw), (ys_ref, dys_ref, sw)):
            v = ref[...]
            r = lax.rsqrt(jnp.mean(v * v, axis=-1, keepdims=True) + EPS)
            vhat = v * r
            dm = dm_ref[:, off:off + w]
            dvhat = dm * g_ref[:, off:off + w]
            dref[...] = r * (dvhat - vhat * jnp.mean(dvhat * vhat, axis=-1, keepdims=True))
            dg_ref[:, off:off + w] += _colsum(dm * vhat)
            off += w

    def row(w):
        return pl.BlockSpec((tb, w), lambda i: (i, 0))

    vec = pl.BlockSpec((1, d), lambda i: (0, 0))
    return _ordered_call(
        body, name=name, grid=(s // tb,), in_specs=[row(d), row(aw), row(cw), row(sw), vec],
        out_specs=(row(aw), row(cw), row(sw), vec),
        out_shape=(jax.ShapeDtypeStruct((s, aw), F32), jax.ShapeDtypeStruct((s, cw), F32),
                   jax.ShapeDtypeStruct((s, sw), F32), jax.ShapeDtypeStruct((1, d), F32)),
        compiler_params=_cp("arbitrary"),
    )(dmix, ya, yc, ys, g.reshape(1, d))


def _head_rms(x):
    r = lax.rsqrt(jnp.mean(x * x, axis=-1, keepdims=True) + EPS)
    return x * r, r


def _attn_mask(n):
    qi = lax.broadcasted_iota(jnp.int32, (WINDOW, 2 * WINDOW), 0)
    sj = lax.broadcasted_iota(jnp.int32, (WINDOW, 2 * WINDOW), 1)
    rel = qi + WINDOW - sj
    return (rel >= 0) & (rel < WINDOW) & ((sj >= WINDOW) | (n > 0))


def _attn_specs(nq, nkv, nb):
    qspec = pl.BlockSpec((nq, WINDOW, HEAD_DIM), lambda n: (0, n, 0))
    cur = pl.BlockSpec((nkv, WINDOW, HEAD_DIM), lambda n: (0, n, 0))
    prev = pl.BlockSpec((nkv, WINDOW, HEAD_DIM), lambda n: (0, jnp.maximum(n - 1, 0), 0))
    nxt = pl.BlockSpec((nkv, WINDOW, HEAD_DIM), lambda n: (0, jnp.minimum(n + 1, nb - 1), 0))
    gain = pl.BlockSpec((1, HEAD_DIM), lambda n: (0, 0))
    sink = pl.BlockSpec((nq, 1, 128), lambda n: (0, 0, 0))
    return qspec, cur, prev, nxt, gain, sink


def _attn_probs(qn_b, kn_b, valid, sink):
    logits = _dot(qn_b, kn_b, NT) * (HEAD_DIM ** -0.5)
    logits = jnp.where(valid, logits, NEG_INF)
    m = jnp.maximum(jnp.max(logits, axis=-1, keepdims=True), sink)
    p = jnp.exp(logits - m)
    es = jnp.exp(sink - m)
    denom = jnp.sum(p, axis=-1, keepdims=True) + es
    return p / denom, es / denom


def _attn_fwd(name, q, k, v, gq, gk, sinks_b):
    nq, s, _ = q.shape
    nkv = k.shape[0]
    nb = s // WINDOW
    qspec, cur, prev, _, gain, sink = _attn_specs(nq, nkv, nb)

    def body(q_ref, kc_ref, kp_ref, vc_ref, vp_ref, gq_ref, gk_ref, s_ref, o_ref):
        gkv = gk_ref[...]
        valid = _attn_mask(pl.program_id(0))
        for g in range(nkv):
            kn_b = jnp.concatenate([_head_rms(kp_ref[g])[0] * gkv, _head_rms(kc_ref[g])[0] * gkv],
                                   axis=0).astype(BF16)
            vv_b = jnp.concatenate([vp_ref[g], vc_ref[g]], axis=0).astype(BF16)
            for i in range(g * GQA, (g + 1) * GQA):
                qn_b = (_head_rms(q_ref[i])[0] * gq_ref[...]).astype(BF16)
                probs, _ = _attn_probs(qn_b, kn_b, valid, s_ref[i][:, :1])
                o_ref[i] = _dot(probs.astype(BF16), vv_b)

    return _ordered_call(
        body, name=name, grid=(nb,), in_specs=[qspec, cur, prev, cur, prev, gain, gain, sink], out_specs=qspec,
        out_shape=jax.ShapeDtypeStruct((nq, s, HEAD_DIM), F32), compiler_params=_cp("parallel"),
    )(q, k, k, v, v, gq.reshape(1, HEAD_DIM), gk.reshape(1, HEAD_DIM), sinks_b)


def _attn_bwd(name, q, k, v, gq, gk, sinks_b, do):
    nq, s, _ = q.shape
    nkv = k.shape[0]
    nb = s // WINDOW
    qspec, cur, prev, _, gain, sink = _attn_specs(nq, nkv, nb)

    def body(q_ref, kc_ref, kp_ref, vc_ref, vp_ref, gq_ref, gk_ref, s_ref, do_ref,
             dq_ref, dkc_ref, dkp_ref, dvc_ref, dvp_ref, dgq_ref, ds_ref):
        n = pl.program_id(0)

        @pl.when(n == 0)
        def _():
            dgq_ref[...] = jnp.zeros_like(dgq_ref)
            ds_ref[...] = jnp.zeros_like(ds_ref)

        gkv = gk_ref[...]
        gqv = gq_ref[...]
        valid = _attn_mask(n)
        dgq = jnp.zeros((1, HEAD_DIM), F32)
        for g in range(nkv):
            kn_b = jnp.concatenate([_head_rms(kp_ref[g])[0] * gkv, _head_rms(kc_ref[g])[0] * gkv],
                                   axis=0).astype(BF16)
            vv_b = jnp.concatenate([vp_ref[g], vc_ref[g]], axis=0).astype(BF16)
            dkn = jnp.zeros((2 * WINDOW, HEAD_DIM), F32)
            dvv = jnp.zeros((2 * WINDOW, HEAD_DIM), F32)
            for i in range(g * GQA, (g + 1) * GQA):
                qhat, r = _head_rms(q_ref[i])
                qn_b = (qhat * gqv).astype(BF16)
                probs, psink = _attn_probs(qn_b, kn_b, valid, s_ref[i][:, :1])
                do_b = do_ref[i].astype(BF16)
                dp = _dot(do_b, vv_b, NT)
                delta = jnp.sum(probs * dp, axis=-1, keepdims=True)
                dl_b = (probs * (dp - delta) * (HEAD_DIM ** -0.5)).astype(BF16)
                ds_ref[i] += jnp.broadcast_to(-jnp.sum(psink * delta, axis=0, keepdims=True), (1, 128))
                dqn = _dot(dl_b, kn_b)
                dkn += _dot(dl_b, qn_b, TN)
                dvv += _dot(probs.astype(BF16), do_b, TN)
                dgq += _colsum(dqn * qhat)
                dqhat = dqn * gqv
                dq_ref[i] = r * (dqhat - qhat * jnp.mean(dqhat * qhat, axis=-1, keepdims=True))
            dkp_ref[g] = dkn[:WINDOW]
            dkc_ref[g] = dkn[WINDOW:]
            dvp_ref[g] = dvv[:WINDOW]
            dvc_ref[g] = dvv[WINDOW:]
        dgq_ref[...] += dgq

    kv_shape = jax.ShapeDtypeStruct((nkv, s, HEAD_DIM), F32)
    return _ordered_call(
        body, name=name, grid=(nb,), in_specs=[qspec, cur, prev, cur, prev, gain, gain, sink, qspec],
        out_specs=(qspec, cur, cur, cur, cur, gain, sink),
        out_shape=(jax.ShapeDtypeStruct((nq, s, HEAD_DIM), F32), kv_shape, kv_shape, kv_shape, kv_shape,
                   jax.ShapeDtypeStruct((1, HEAD_DIM), F32), jax.ShapeDtypeStruct((nq, 1, 128), F32)),
        compiler_params=_cp("arbitrary"),
    )(q, k, k, v, v, gq.reshape(1, HEAD_DIM), gk.reshape(1, HEAD_DIM), sinks_b, do)


def _attn_bwd_kv(name, k, gk, dkc, dkp, dvc, dvp):
    nkv, s, _ = k.shape
    nb = s // WINDOW
    _, cur, _, nxt, gain, _ = _attn_specs(GQA * nkv, nkv, nb)

    def body(k_ref, gk_ref, dkc_ref, dkp_ref, dvc_ref, dvp_ref, dk_ref, dv_ref, dgk_ref):
        n = pl.program_id(0)

        @pl.when(n == 0)
        def _():
            dgk_ref[...] = jnp.zeros_like(dgk_ref)

        has_next = n < nb - 1
        dgk = jnp.zeros((1, HEAD_DIM), F32)
        for g in range(nkv):
            dkn = dkc_ref[g] + jnp.where(has_next, dkp_ref[g], 0.0)
            dv_ref[g] = dvc_ref[g] + jnp.where(has_next, dvp_ref[g], 0.0)
            khat, r = _head_rms(k_ref[g])
            dgk += _colsum(dkn * khat)
            dkhat = dkn * gk_ref[...]
            dk_ref[g] = r * (dkhat - khat * jnp.mean(dkhat * khat, axis=-1, keepdims=True))
        dgk_ref[...] += dgk

    kv_shape = jax.ShapeDtypeStruct((nkv, s, HEAD_DIM), F32)
    return _ordered_call(
        body, name=name, grid=(nb,), in_specs=[cur, gain, cur, nxt, cur, nxt], out_specs=(cur, cur, gain),
        out_shape=(kv_shape, kv_shape, jax.ShapeDtypeStruct((1, HEAD_DIM), F32)),
        compiler_params=_cp("arbitrary"),
    )(k, gk.reshape(1, HEAD_DIM), dkc, dkp, dvc, dvp)


def _conv_recompute(i, a_ref, gt_ref, ap_ref, gp_ref, w_ref, b_ref, hbuf, tb):
    hbuf[pl.ds(HALO, tb), :] = a_ref[...] * _sigmoid(gt_ref[...])
    tail = ap_ref[pl.ds(tb - HALO, HALO), :] * _sigmoid(gp_ref[pl.ds(tb - HALO, HALO), :])
    hbuf[pl.ds(0, HALO), :] = jnp.where(i > 0, tail, 0.0)
    acc = jnp.broadcast_to(b_ref[...], a_ref.shape)
    for kk in range(CONV_KERNEL):
        acc = acc + w_ref[pl.ds(kk, 1), :] * hbuf[pl.ds(HALO - (CONV_KERNEL - 1) + kk, tb), :]
    return acc


def _layer_norm_stats(c):
    mu = jnp.mean(c, axis=-1, keepdims=True)
    xc = c - mu
    r = lax.rsqrt(jnp.mean(xc * xc, axis=-1, keepdims=True) + EPS)
    return xc * r, r


def _conv_specs(s, cw, tb, a_blk):
    cur = lambda off: pl.BlockSpec((tb, cw), lambda i: (i, a_blk + off))
    prev = lambda off: pl.BlockSpec((tb, cw), lambda i: (jnp.maximum(i - 1, 0), a_blk + off))
    wspec = pl.BlockSpec((HALO, cw), lambda i: (0, 0))
    vec = pl.BlockSpec((1, cw), lambda i: (0, 0))
    row = pl.BlockSpec((tb, cw), lambda i: (i, 0))
    return cur, prev, wspec, vec, row


def _conv_fwd(name, proj, a_blk, w, b, lg, lb):
    s = proj.shape[0]
    cw = w.shape[1]
    tb = _tile(s, 256)
    cur, prev, wspec, vec, row = _conv_specs(s, cw, tb, a_blk)

    def body(a_ref, gt_ref, ap_ref, gp_ref, w_ref, b_ref, lg_ref, lb_ref, y_ref, hbuf):
        c = _conv_recompute(pl.program_id(0), a_ref, gt_ref, ap_ref, gp_ref, w_ref, b_ref, hbuf, tb)
        chat, _ = _layer_norm_stats(c)
        z = chat * lg_ref[...] + lb_ref[...]
        y_ref[...] = z * _sigmoid(z)

    return _ordered_call(
        body, name=name, grid=(s // tb,), in_specs=[cur(0), cur(1), prev(0), prev(1), wspec, vec, vec, vec],
        out_specs=row, out_shape=jax.ShapeDtypeStruct((s, cw), F32),
        scratch_shapes=[pltpu.VMEM((tb + HALO, cw), F32)], compiler_params=_cp("arbitrary"),
    )(proj, proj, proj, proj, w, b, lg, lb)


def _conv_bwd1(name, proj, a_blk, w, b, lg, lb, dy):
    s = proj.shape[0]
    cw = w.shape[1]
    tb = _tile(s, 256)
    cur, prev, wspec, vec, row = _conv_specs(s, cw, tb, a_blk)

    def body(a_ref, gt_ref, ap_ref, gp_ref, w_ref, b_ref, lg_ref, lb_ref, dy_ref,
             dc_ref, dw_ref, db_ref, dlg_ref, dlb_ref, hbuf):
        i = pl.program_id(0)

        @pl.when(i == 0)
        def _():
            dw_ref[...] = jnp.zeros_like(dw_ref)
            db_ref[...] = jnp.zeros_like(db_ref)
            dlg_ref[...] = jnp.zeros_like(dlg_ref)
            dlb_ref[...] = jnp.zeros_like(dlb_ref)

        c = _conv_recompute(i, a_ref, gt_ref, ap_ref, gp_ref, w_ref, b_ref, hbuf, tb)
        chat, r = _layer_norm_stats(c)
        z = chat * lg_ref[...] + lb_ref[...]
        sg = _sigmoid(z)
        dz = dy_ref[...] * (sg + z * sg * (1.0 - sg))
        dlg_ref[...] += _colsum(dz * chat)
        dlb_ref[...] += _colsum(dz)
        dzg = dz * lg_ref[...]
        dc = r * (dzg - jnp.mean(dzg, axis=-1, keepdims=True) - chat * jnp.mean(dzg * chat, axis=-1, keepdims=True))
        dc_ref[...] = dc
        db_ref[...] += _colsum(dc)
        for kk in range(CONV_KERNEL):
            dw_ref[pl.ds(kk, 1), :] += _colsum(dc * hbuf[pl.ds(HALO - (CONV_KERNEL - 1) + kk, tb), :])

    return _ordered_call(
        body, name=name, grid=(s // tb,), in_specs=[cur(0), cur(1), prev(0), prev(1), wspec, vec, vec, vec, row],
        out_specs=(row, wspec, vec, vec, vec),
        out_shape=(jax.ShapeDtypeStruct((s, cw), F32), jax.ShapeDtypeStruct((HALO, cw), F32),
                   jax.ShapeDtypeStruct((1, cw), F32), jax.ShapeDtypeStruct((1, cw), F32),
                   jax.ShapeDtypeStruct((1, cw), F32)),
        scratch_shapes=[pltpu.VMEM((tb + HALO, cw), F32)], compiler_params=_cp("arbitrary"),
    )(proj, proj, proj, proj, w, b, lg, lb, dy)


def _conv_bwd2(name, proj, a_blk, w, dc):
    s = proj.shape[0]
    cw = w.shape[1]
    tb = _tile(s, 256)
    nblk = s // tb
    cur, _, wspec, _, row = _conv_specs(s, cw, tb, a_blk)
    nxt = pl.BlockSpec((tb, cw), lambda i: (jnp.minimum(i + 1, nblk - 1), 0))

    def body(a_ref, gt_ref, w_ref, dc_ref, dn_ref, o_ref, dbuf):
        i = pl.program_id(0)
        dbuf[pl.ds(0, tb), :] = dc_ref[...]
        dbuf[pl.ds(tb, HALO), :] = jnp.where(i < nblk - 1, dn_ref[pl.ds(0, HALO), :], 0.0)
        dh = jnp.zeros((tb, cw), F32)
        for kk in range(CONV_KERNEL):
            dh = dh + w_ref[pl.ds(kk, 1), :] * dbuf[pl.ds(CONV_KERNEL - 1 - kk, tb), :]
        sg = _sigmoid(gt_ref[...])
        o_ref[:, 0:cw] = (dh * sg).astype(BF16)
        o_ref[:, cw:2 * cw] = (dh * a_ref[...] * sg * (1.0 - sg)).astype(BF16)

    return _ordered_call(
        body, name=name, grid=(nblk,), in_specs=[cur(0), cur(1), wspec, row, nxt],
        out_specs=pl.BlockSpec((tb, 2 * cw), lambda i: (i, 0)), out_shape=jax.ShapeDtypeStruct((s, 2 * cw), BF16),
        scratch_shapes=[pltpu.VMEM((tb + HALO, cw), F32)], compiler_params=_cp("arbitrary"),
    )(proj, proj, w, dc, dc)


def _sgu_common(v_ref, lg_ref, lb_ref, w_ref, bexp_ref, sw):
    vhat, r = _layer_norm_stats(v_ref[...])
    vn_b = (vhat * lg_ref[...] + lb_ref[...]).astype(BF16)
    ii = lax.broadcasted_iota(jnp.int32, (WINDOW, WINDOW), 0)
    jj = lax.broadcasted_iota(jnp.int32, (WINDOW, WINDOW), 1)
    tril = jj <= ii
    head_of = lax.broadcasted_iota(jnp.int32, (WINDOW, sw), 1) // HEAD_DIM
    wts = [jnp.where(tril, w_ref[h], 0.0).astype(BF16) for h in range(sw // HEAD_DIM)]
    sv = bexp_ref[...]
    for h, wt in enumerate(wts):
        sv = sv + jnp.where(head_of == h, _dot(wt, vn_b), 0.0)
    return vhat, r, vn_b, tril, head_of, wts, sv


def _sgu_specs(sw, u_blk):
    nh = sw // HEAD_DIM
    u = pl.BlockSpec((WINDOW, sw), lambda n: (n, u_blk))
    v = pl.BlockSpec((WINDOW, sw), lambda n: (n, u_blk + 1))
    vec = pl.BlockSpec((1, sw), lambda n: (0, 0))
    wspec = pl.BlockSpec((nh, WINDOW, WINDOW), lambda n: (0, 0, 0))
    bspec = pl.BlockSpec((WINDOW, sw), lambda n: (0, 0))
    row = pl.BlockSpec((WINDOW, sw), lambda n: (n, 0))
    return u, v, vec, wspec, bspec, row


def _sgu_fwd(name, proj, u_blk, lg, lb, w, bexp):
    s = proj.shape[0]
    sw = lg.shape[1]
    u, v, vec, wspec, bspec, row = _sgu_specs(sw, u_blk)

    def body(u_ref, v_ref, lg_ref, lb_ref, w_ref, bexp_ref, y_ref):
        sv = _sgu_common(v_ref, lg_ref, lb_ref, w_ref, bexp_ref, sw)[-1]
        y_ref[...] = u_ref[...] * sv

    return _ordered_call(
        body, name=name, grid=(s // WINDOW,), in_specs=[u, v, vec, vec, wspec, bspec], out_specs=row,
        out_shape=jax.ShapeDtypeStruct((s, sw), F32), compiler_params=_cp("parallel"),
    )(proj, proj, lg, lb, w, bexp)


def _sgu_bwd(name, proj, u_blk, lg, lb, w, bexp, dy):
    s = proj.shape[0]
    sw = lg.shape[1]
    nh = sw // HEAD_DIM
    u, v, vec, wspec, bspec, row = _sgu_specs(sw, u_blk)
    dbspec = pl.BlockSpec((nh, WINDOW), lambda n: (0, 0))

    def body(u_ref, v_ref, lg_ref, lb_ref, w_ref, bexp_ref, dy_ref, o_ref, dw_ref, db_ref, dlg_ref, dlb_ref):
        n = pl.program_id(0)

        @pl.when(n == 0)
        def _():
            dw_ref[...] = jnp.zeros_like(dw_ref)
            db_ref[...] = jnp.zeros_like(db_ref)
            dlg_ref[...] = jnp.zeros_like(dlg_ref)
            dlb_ref[...] = jnp.zeros_like(dlb_ref)

        vhat, r, vn_b, tril, head_of, wts, sv = _sgu_common(v_ref, lg_ref, lb_ref, w_ref, bexp_ref, sw)
        dyv = dy_ref[...]
        o_ref[:, 0:sw] = (dyv * sv).astype(BF16)
        ds = dyv * u_ref[...]
        dvn = jnp.zeros((WINDOW, sw), F32)
        for h, wt in enumerate(wts):
            dsm_b = jnp.where(head_of == h, ds, 0.0).astype(BF16)
            dvn = dvn + _dot(wt, dsm_b, TN)
            dw_ref[h] += jnp.where(tril, _dot(dsm_b, vn_b, NT), 0.0)
        hmask = (lax.broadcasted_iota(jnp.int32, (nh, sw), 1) // HEAD_DIM
                 == lax.broadcasted_iota(jnp.int32, (nh, sw), 0)).astype(F32)
        db_ref[...] += lax.dot_general(hmask, ds, NT, precision=lax.Precision.HIGHEST, preferred_element_type=F32)
        dlg_ref[...] += _colsum(dvn * vhat)
        dlb_ref[...] += _colsum(dvn)
        dvg = dvn * lg_ref[...]
        dv = r * (dvg - jnp.mean(dvg, axis=-1, keepdims=True) - vhat * jnp.mean(dvg * vhat, axis=-1, keepdims=True))
        o_ref[:, sw:2 * sw] = dv.astype(BF16)

    return _ordered_call(
        body, name=name, grid=(s // WINDOW,), in_specs=[u, v, vec, vec, wspec, bspec, row],
        out_specs=(pl.BlockSpec((WINDOW, 2 * sw), lambda n: (n, 0)), wspec, dbspec, vec, vec),
        out_shape=(jax.ShapeDtypeStruct((s, 2 * sw), BF16), jax.ShapeDtypeStruct((nh, WINDOW, WINDOW), F32),
                   jax.ShapeDtypeStruct((nh, WINDOW), F32), jax.ShapeDtypeStruct((1, sw), F32),
                   jax.ShapeDtypeStruct((1, sw), F32)),
        compiler_params=_cp("arbitrary"),
    )(proj, proj, lg, lb, w, bexp, dy)


def _adamw(name, w, g, m, v):
    rows, cols = w.shape
    tr = _tile(rows, 256)

    def body(w_ref, g_ref, m_ref, v_ref, d_ref, nm_ref, nv_ref):
        gv = g_ref[...]
        mv = ADAM_B1 * m_ref[...] + (1.0 - ADAM_B1) * gv
        vv = ADAM_B2 * v_ref[...] + (1.0 - ADAM_B2) * (gv * gv)
        m_hat = mv / (1.0 - ADAM_B1 ** ADAM_STEP)
        v_hat = vv / (1.0 - ADAM_B2 ** ADAM_STEP)
        d_ref[...] = -ADAM_LR * (m_hat / (jnp.sqrt(v_hat) + ADAM_EPS) + ADAM_WD * w_ref[...])
        nm_ref[...] = mv
        nv_ref[...] = vv

    spec = pl.BlockSpec((tr, cols), lambda i: (i, 0))
    shape = jax.ShapeDtypeStruct((rows, cols), F32)
    return _ordered_call(
        body, name=name, grid=(rows // tr,), in_specs=[spec] * 4, out_specs=(spec,) * 3, out_shape=(shape,) * 3,
        compiler_params=_cp("parallel"),
    )(w, g, m, v)


def _route():
    x, y, c = lax.axis_index("x"), lax.axis_index("y"), lax.axis_index("c")
    n1 = (jnp.where(c == 0, 1 - x, x), jnp.where(c == 0, y, 1 - y))
    n2 = (jnp.where(c == 0, x, 1 - x), jnp.where(c == 0, 1 - y, y))
    return x, y, c, n1, n2, (1 - x, 1 - y)


def _cidx(chip):
    return 2 * chip[0] + chip[1]


def _remote(src, dst, sems, k, device):
    send_sems, recv_sems = sems
    return pltpu.make_async_remote_copy(src_ref=src, dst_ref=dst, send_sem=send_sems.at[k], recv_sem=recv_sems.at[k],
                                        device_id=device, device_id_type=MESH)


def _exchange(name, bufs, n_sems, build):
    n = len(bufs)

    def body(*refs):
        cps = build(refs[n:2 * n], (refs[2 * n], refs[2 * n + 1]))
        for cp in cps:
            cp.start()
        for cp in cps:
            cp.wait()

    return _ordered_call(
        body, name=name, in_specs=[ANY] * n, out_specs=tuple(ANY for _ in range(n)),
        out_shape=tuple(jax.ShapeDtypeStruct(b.shape, b.dtype) for b in bufs),
        input_output_aliases={i: i for i in range(n)},
        scratch_shapes=[pltpu.SemaphoreType.DMA((n_sems,)), pltpu.SemaphoreType.DMA((n_sems,))],
        compiler_params=pltpu.CompilerParams(has_side_effects=True),
    )(*bufs)


HBM_SPEC = pl.BlockSpec(memory_space=pltpu.HBM)
SEM_SPEC = pl.BlockSpec(memory_space=pltpu.SEMAPHORE)
DATAFLOW = pltpu.SideEffectType.DATAFLOW_SIDE_EFFECTING


def _exchange_start(name, bufs, n_sems, build):
    n = len(bufs)

    def body(*refs):
        for cp in build(refs[:n], (refs[n], refs[n + 1])):
            cp.start()
        refs[-1][...] = jnp.zeros_like(refs[-1])

    out = _ordered_call(
        body, name=name,
        out_shape=(pltpu.SemaphoreType.DMA((n_sems,)), pltpu.SemaphoreType.DMA((n_sems,)))
        + tuple(pltpu.HBM(b.shape, b.dtype) for b in bufs) + (jax.ShapeDtypeStruct((8, 128), F32),),
        in_specs=[HBM_SPEC] * n,
        out_specs=(SEM_SPEC, SEM_SPEC) + (HBM_SPEC,) * n + (pl.BlockSpec(memory_space=pltpu.VMEM),),
        input_output_aliases={i: 2 + i for i in range(n)},
        compiler_params=pltpu.CompilerParams(has_side_effects=DATAFLOW),
    )(*[pltpu.with_memory_space_constraint(b, pltpu.HBM) for b in bufs])
    return dict(name=name, send=out[0], recv=out[1], bufs=list(out[2:2 + n]), token=out[-1], build=build)


def _exchange_wait(handle):
    n = len(handle["bufs"])

    def body(*refs):
        for cp in handle["build"](refs[:n], (refs[n], refs[n + 1])):
            cp.wait_send()
            cp.wait_recv()

    return list(_ordered_call(
        body, name=handle["name"] + "_wait", out_shape=tuple(pltpu.HBM(b.shape, b.dtype) for b in handle["bufs"]),
        in_specs=[HBM_SPEC] * n + [SEM_SPEC, SEM_SPEC], out_specs=(HBM_SPEC,) * n,
        input_output_aliases={i: i for i in range(n)},
        compiler_params=pltpu.CompilerParams(has_side_effects=DATAFLOW),
    )(*handle["bufs"], handle["send"], handle["recv"]))


def _cast_place(name, w, l, me_idx, dtype):
    _, r, c = w.shape
    tr = _tile(r, 512)

    def body(me_ref, w_ref, o_ref):
        o_ref[...] = w_ref[...].astype(dtype)

    grid_spec = pltpu.PrefetchScalarGridSpec(
        num_scalar_prefetch=1, grid=(r // tr,),
        in_specs=[pl.BlockSpec((None, tr, c), lambda i, me_ref: (l, i, 0))],
        out_specs=pl.BlockSpec((None, tr, c), lambda i, me_ref: (me_ref[0], i, 0)))
    return _ordered_call(
        body, name=name, grid_spec=grid_spec, out_shape=jax.ShapeDtypeStruct((N_CHIPS, r, c), dtype),
        compiler_params=_cp("arbitrary"),
    )(me_idx, w)


def _my_half(ref, blk, c):
    hr = ref.shape[1] // 2
    return ref.at[blk, pl.ds(c * hr, hr), :]


def _gather_stages(tag, n):
    def stage1(refs, sems):
        x, y, c, n1, n2, dg = _route()
        return [_remote(_my_half(refs[t], _cidx((x, y)), c), _my_half(refs[t], _cidx((x, y)), c), sems, t, (*n1, c))
                for t in range(n)]

    def stage2(refs, sems):
        x, y, c, n1, n2, dg = _route()
        return [_remote(_my_half(refs[t], blk, c), _my_half(refs[t], blk, c), sems, 2 * t + b, (*n2, c))
                for t in range(n) for b, blk in enumerate((_cidx((x, y)), _cidx(n1)))]

    def stage3(refs, sems):
        x, y, c, n1, n2, dg = _route()
        return [_remote(_my_half(refs[t], blk, c), _my_half(refs[t], blk, c), sems, 3 * t + b, (x, y, 1 - c))
                for t in range(n) for b, blk in enumerate((_cidx(n1), _cidx(n2), _cidx(dg)))]

    return ((f"gather1_{tag}", n, stage1), (f"gather2_{tag}", 2 * n, stage2), (f"gather3_{tag}", 3 * n, stage3))


RI_C, RI_ME, RI_N2 = 0, 1, 2


def _pair_sum(name, g, sib, route_idx):
    _, rows, cols = g.shape
    hr = rows // 2
    tr = _tile(hr, 512)
    per = hr // tr

    def body(ri, g_ref, s_ref, o_ref):
        o_ref[...] = (g_ref[...].astype(F32) + s_ref[...].astype(F32)).astype(BF16)

    blk = (None, tr, cols)
    grid_spec = pltpu.PrefetchScalarGridSpec(
        num_scalar_prefetch=1, grid=(N_CHIPS, per),
        in_specs=[pl.BlockSpec(blk, lambda j, i, ri: (j, ri[RI_C] * per + i, 0)),
                  pl.BlockSpec(blk, lambda j, i, ri: (j, i, 0))],
        out_specs=pl.BlockSpec(blk, lambda j, i, ri: (j, i, 0)))
    return _ordered_call(
        body, name=name, grid_spec=grid_spec, out_shape=jax.ShapeDtypeStruct((N_CHIPS, hr, cols), BF16),
        compiler_params=_cp("parallel", "parallel"),
    )(route_idx, g, sib)


def _sum_stage1(name, p, got, route_idx):
    _, hr, cols = p.shape
    tr = _tile(hr, 512)

    def body(ri, pm_ref, pn_ref, g0_ref, g1_ref, keep_ref, send_ref):
        keep_ref[...] = pm_ref[...].astype(F32) + g0_ref[...].astype(F32)
        send_ref[...] = (pn_ref[...].astype(F32) + g1_ref[...].astype(F32)).astype(BF16)

    blk = (None, tr, cols)
    row = pl.BlockSpec((tr, cols), lambda i, ri: (i, 0))
    grid_spec = pltpu.PrefetchScalarGridSpec(
        num_scalar_prefetch=1, grid=(hr // tr,),
        in_specs=[pl.BlockSpec(blk, lambda i, ri: (ri[RI_ME], i, 0)), pl.BlockSpec(blk, lambda i, ri: (ri[RI_N2], i, 0)),
                  pl.BlockSpec(blk, lambda i, ri: (0, i, 0)), pl.BlockSpec(blk, lambda i, ri: (1, i, 0))],
        out_specs=(row, row))
    return _ordered_call(
        body, name=name, grid_spec=grid_spec,
        out_shape=(jax.ShapeDtypeStruct((hr, cols), F32), jax.ShapeDtypeStruct((hr, cols), BF16)),
        compiler_params=_cp("parallel"),
    )(route_idx, p, p, got, got)


def _sum_stage2(name, keep, got):
    hr, cols = keep.shape
    tr = _tile(hr, 512)

    def body(k_ref, g_ref, o_ref):
        o_ref[...] = k_ref[...] + g_ref[...].astype(F32)

    row = pl.BlockSpec((tr, cols), lambda i: (i, 0))
    return _ordered_call(
        body, name=name, grid=(hr // tr,), in_specs=[row, row], out_specs=row,
        out_shape=jax.ShapeDtypeStruct((hr, cols), F32), compiler_params=_cp("parallel"),
    )(keep, got)


def _reduce_scatter(tag, names, grads, route_idx):
    n = len(grads)
    hrs = [g.shape[1] // 2 for g in grads]

    def empty(t, lead, dtype):
        return lax.empty(lead + (hrs[t], grads[t].shape[2]), dtype)

    def pair_stage(refs, sems):
        x, y, c, n1, n2, dg = _route()
        return [_remote(refs[t].at[:, pl.ds((1 - c) * hrs[t], hrs[t]), :], refs[n + t], sems, t, (x, y, 1 - c))
                for t in range(n)]

    def stage1(refs, sems):
        x, y, c, n1, n2, dg = _route()
        return [_remote(refs[t].at[blk], refs[n + t].at[slot], sems, 2 * t + slot, (*n1, c))
                for t in range(n) for slot, blk in enumerate((_cidx(n1), _cidx(dg)))]

    def stage2(refs, sems):
        x, y, c, n1, n2, dg = _route()
        return [_remote(refs[t], refs[n + t], sems, t, (*n2, c)) for t in range(n)]

    def stage3(refs, sems):
        x, y, c, n1, n2, dg = _route()
        return [_remote(refs[t], refs[n + t], sems, t, (x, y, 1 - c)) for t in range(n)]

    state = {}

    def start():
        state["h"] = _exchange_start(f"rs_pair_{tag}", list(grads) + [empty(t, (N_CHIPS,), BF16) for t in range(n)],
                                     n, pair_stage)

    def pair_done():
        out = _exchange_wait(state["h"])
        psum = [_pair_sum(f"rs_psum_{names[t]}", out[t], out[n + t], route_idx) for t in range(n)]
        state["h"] = _exchange_start(f"rs_x1_{tag}", psum + [empty(t, (2,), BF16) for t in range(n)], 2 * n, stage1)

    def x1_done():
        out = _exchange_wait(state["h"])
        state["keep"], send = zip(*[_sum_stage1(f"rs_sum1_{names[t]}", out[t], out[n + t], route_idx)
                                    for t in range(n)])
        state["h"] = _exchange_start(f"rs_x2_{tag}", list(send) + [empty(t, (), BF16) for t in range(n)], n, stage2)

    def x2_done():
        out = _exchange_wait(state["h"])
        mine = [_sum_stage2(f"rs_sum2_{names[t]}", state["keep"][t], out[n + t]) for t in range(n)]
        state["h"] = _exchange_start(f"rs_half_{tag}", mine + [empty(t, (), F32) for t in range(n)], n, stage3)

    def finish():
        out = _exchange_wait(state["h"])
        return list(zip(out[:n], out[n:]))

    return start, pair_done, x1_done, x2_done, finish


def _adamw_big(name, w, m, v, f, h, l, c_idx, prev):
    n_l, r, cols = w.shape
    hr = r // 2
    tr = _tile(hr, 256)
    per = hr // tr

    def body(c_ref, w_ref, m_ref, v_ref, f_ref, h_ref, *rest):
        g_ref, d_ref, nm_ref, nv_ref = rest[-4:]
        gv = jnp.where(pl.program_id(0) == c_ref[0], f_ref[...], h_ref[...])
        mv = ADAM_B1 * m_ref[...] + (1.0 - ADAM_B1) * gv
        vv = ADAM_B2 * v_ref[...] + (1.0 - ADAM_B2) * (gv * gv)
        m_hat = mv / (1.0 - ADAM_B1 ** ADAM_STEP)
        v_hat = vv / (1.0 - ADAM_B2 ** ADAM_STEP)
        g_ref[...] = gv
        d_ref[...] = -ADAM_LR * (m_hat / (jnp.sqrt(v_hat) + ADAM_EPS) + ADAM_WD * w_ref[...])
        nm_ref[...] = mv
        nv_ref[...] = vv

    big = pl.BlockSpec((None, tr, cols), lambda hf, i, c_ref: (l, hf * per + i, 0))
    fspec = pl.BlockSpec((tr, cols), lambda hf, i, c_ref: (jnp.where(hf == c_ref[0], i, 0), 0))
    hspec = pl.BlockSpec((tr, cols), lambda hf, i, c_ref: (jnp.where(hf == c_ref[0], 0, i), 0))
    grid_spec = pltpu.PrefetchScalarGridSpec(
        num_scalar_prefetch=1, grid=(2, per), in_specs=[big] * 3 + [fspec, hspec] + [ANY] * len(prev),
        out_specs=(big,) * 4)
    return _ordered_call(
        body, name=name, grid_spec=grid_spec, out_shape=(jax.ShapeDtypeStruct(w.shape, F32),) * 4,
        input_output_aliases={6 + k: k for k in range(len(prev))}, compiler_params=_cp("arbitrary", "arbitrary"),
    )(c_idx, w, m, v, f, h, *prev)


def _small_allreduce(buf):
    rows = buf.shape[0]
    hr = rows // 2

    def body(in_ref, out_ref, pair, acc, got1, got2, send_sems, recv_sems):
        x, y, c, n1, n2, dg = _route()
        sems = (send_sems, recv_sems)
        sibling = (x, y, 1 - c)
        mine = pl.ds(pl.multiple_of(c * hr, 8), hr)
        pair[c] = in_ref[...]
        cp = _remote(in_ref, pair.at[c], sems, 0, sibling)
        cp.start()
        cp.wait()
        acc[...] = pair[0, mine, :] + pair[1, mine, :]
        cp = _remote(acc, got1, sems, 1, (*n1, c))
        cp.start()
        cp.wait()
        acc[...] = acc[...] + got1[...]
        cp = _remote(acc, got2, sems, 2, (*n2, c))
        cp.start()
        cp.wait()
        out_ref[mine, :] = acc[...] + got2[...]
        cp = _remote(out_ref.at[mine, :], out_ref.at[mine, :], sems, 3, sibling)
        cp.start()
        cp.wait()

    half = pltpu.VMEM((hr, 128), F32)
    return _ordered_call(
        body, name="small_allreduce", in_specs=[pl.BlockSpec(memory_space=pltpu.VMEM)],
        out_specs=pl.BlockSpec(memory_space=pltpu.VMEM), out_shape=jax.ShapeDtypeStruct((rows, 128), F32),
        scratch_shapes=[pltpu.VMEM((2, rows, 128), F32), half, half, half,
                        pltpu.SemaphoreType.DMA((4,)), pltpu.SemaphoreType.DMA((4,))],
        compiler_params=pltpu.CompilerParams(has_side_effects=True, vmem_limit_bytes=VMEM_LIMIT),
    )(buf)


BIG = ("w_in", "w_out", "w_up", "w_down")
COL_SHARDED = {"w_in": True, "w_out": False, "w_up": True, "w_down": False}
SMALL = ("ln1_g", "q_norm_g", "k_norm_g", "sinks", "conv_w", "conv_b", "conv_ln_g", "conv_ln_b", "sgu_ln_g",
         "sgu_ln_b", "sgu_w", "sgu_b", "out_norm_g", "ln2_g")
WEIGHTS = ("ln1_g", "w_in", "q_norm_g", "k_norm_g", "sinks", "conv_w", "conv_b", "conv_ln_g", "conv_ln_b",
           "sgu_ln_g", "sgu_ln_b", "sgu_w", "sgu_b", "out_norm_g", "w_out", "ln2_g", "w_up", "w_down")
PACK_QUANTUM = 8 * 128
PACK_ROWS = 512


def _pack(arrs):
    parts = []
    for a in arrs:
        f = a.reshape(-1)
        parts.append(jnp.pad(f, (0, -f.shape[0] % PACK_QUANTUM)).reshape(-1, 128))
    rows = sum(p.shape[0] for p in parts)
    parts.append(jnp.zeros((-rows % PACK_ROWS, 128), F32))
    return jnp.concatenate(parts, axis=0)


def _unpack(buf, shapes):
    out, off = [], 0
    for shp in shapes:
        n = 1
        for dd in shp:
            n *= dd
        rows = (n + PACK_QUANTUM - 1) // PACK_QUANTUM * 8
        out.append(buf[off:off + rows].reshape(-1)[:n].reshape(shp))
        off += rows
    return out


def _to_heads(t, nh):
    return t.reshape(t.shape[0], nh, HEAD_DIM).transpose(1, 0, 2)


def _from_heads(t):
    return t.transpose(1, 0, 2).reshape(t.shape[1], t.shape[0] * HEAD_DIM)


def _no_hook(point, carry):
    return carry


def _layer_fwd(l, x, p, wg, hook=_no_hook):
    d = x.shape[1]
    aw, cw = d // 2, d // 4
    nq = aw // HEAD_DIM
    nkv = nq // GQA
    kvw = nkv * HEAD_DIM
    x = hook("fwd_start", x)
    h1 = _rms_fwd(f"ln1_fwd_{l}", x, p["ln1_g"])
    proj = _mm_act_w(f"proj_{l}", h1, wg["w_in"], True, _ep_store)[0]
    q = _to_heads(proj[:, :aw], nq)
    k = _to_heads(proj[:, aw:aw + kvw], nkv)
    v = _to_heads(proj[:, aw + kvw:aw + 2 * kvw], nkv)
    sinks_b = jnp.broadcast_to(p["sinks"][:, None, None], (nq, 1, 128))
    ya = _from_heads(_attn_fwd(f"attn_fwd_{l}", q, k, v, p["q_norm_g"], p["k_norm_g"], sinks_b))
    yc = _conv_fwd(f"conv_fwd_{l}", proj, 3, p["conv_w"], p["conv_b"], p["conv_ln_g"], p["conv_ln_b"])
    ys = _sgu_fwd(f"sgu_fwd_{l}", proj, 5, p["sgu_ln_g"], p["sgu_ln_b"], p["sgu_w"], p["sgu_bexp"])
    mix = _mixnorm_fwd(f"mixnorm_fwd_{l}", ya, yc, ys, p["out_norm_g"])
    mix = hook("fwd_mid", mix)
    xm = _mm_act_w(f"out_{l}", mix, wg["w_out"], False, _ep_residual, extra=(x,))[0]
    h2 = _rms_fwd(f"ln2_fwd_{l}", xm, p["ln2_g"])
    up_b, act_b = _mm_act_w(f"up_{l}", h2, wg["w_up"], True, _ep_up, out_dtypes=(BF16, BF16))
    xo = _mm_act_w(f"down_{l}", act_b, wg["w_down"], False, _ep_residual, extra=(xm,))[0]
    xo = hook("fwd_end", xo)
    saved = dict(x=x, h1=h1, proj=proj, q=q, k=k, v=v, sinks_b=sinks_b, ya=ya, yc=yc, ys=ys, mix=mix, xm=xm, h2=h2,
                 up_b=up_b, act_b=act_b)
    return xo, saved


def _layer_bwd(l, dxo, dxo_b, p, wg, sv, big, hook=_no_hook):
    d = dxo.shape[1]
    nq = (d // 2) // HEAD_DIM
    small = {}
    dxo_b = hook("bwd_start", dxo_b)
    big["w_down"] = _mm_wgrad(f"dw_down_{l}", sv["act_b"], dxo_b, False, d)
    dup_b = _mm_act_wt(f"dup_{l}", dxo_b, wg["w_down"], False, _ep_dup, extra=(sv["up_b"],), out_dtypes=(BF16,))[0]
    dup_b = hook("bwd_dup", dup_b)
    big["w_up"] = _mm_wgrad(f"dw_up_{l}", sv["h2"], dup_b, True, wg["w_up"].shape[2])
    dh2 = _mm_act_wt(f"dh2_{l}", dup_b, wg["w_up"], True, _ep_store)[0]
    dh2 = hook("bwd_dh2", dh2)
    dxm, dxm_b, small["ln2_g"] = _rms_bwd(f"ln2_bwd_{l}", dh2, sv["xm"], p["ln2_g"], dxo)
    big["w_out"] = _mm_wgrad(f"dw_out_{l}", sv["mix"], dxm_b, False, d)
    dmix = _mm_act_wt(f"dmix_{l}", dxm_b, wg["w_out"], False, _ep_store)[0]
    dya, dyc, dys, small["out_norm_g"] = _mixnorm_bwd(f"mixnorm_bwd_{l}", dmix, sv["ya"], sv["yc"], sv["ys"],
                                                      p["out_norm_g"])
    dya = hook("bwd_mix", dya)
    dq, dkc, dkp, dvc, dvp, small["q_norm_g"], dsink = _attn_bwd(
        f"attn_bwd_{l}", sv["q"], sv["k"], sv["v"], p["q_norm_g"], p["k_norm_g"], sv["sinks_b"], _to_heads(dya, nq))
    dkc = hook("bwd_attn", dkc)
    small["sinks"] = dsink[:, 0, 0]
    dk, dv, small["k_norm_g"] = _attn_bwd_kv(f"attn_bwd_kv_{l}", sv["k"], p["k_norm_g"], dkc, dkp, dvc, dvp)
    dc, dcw, small["conv_b"], small["conv_ln_g"], small["conv_ln_b"] = _conv_bwd1(
        f"conv_bwd1_{l}", sv["proj"], 3, p["conv_w"], p["conv_b"], p["conv_ln_g"], p["conv_ln_b"], dyc)
    small["conv_w"] = dcw[:CONV_KERNEL]
    dxc_b = _conv_bwd2(f"conv_bwd2_{l}", sv["proj"], 3, p["conv_w"], dc)
    dxs_b, small["sgu_w"], small["sgu_b"], small["sgu_ln_g"], small["sgu_ln_b"] = _sgu_bwd(
        f"sgu_bwd_{l}", sv["proj"], 5, p["sgu_ln_g"], p["sgu_ln_b"], p["sgu_w"], p["sgu_bexp"], dys)
    dxs_b = hook("bwd_sgu", dxs_b)
    dproj_b = jnp.concatenate([_from_heads(dq).astype(BF16), _from_heads(dk).astype(BF16),
                               _from_heads(dv).astype(BF16), dxc_b, dxs_b], axis=1)
    big["w_in"] = _mm_wgrad(f"dw_in_{l}", sv["h1"], dproj_b, True, wg["w_in"].shape[2])
    dh1 = _mm_act_wt(f"dh1_{l}", dproj_b, wg["w_in"], True, _ep_store)[0]
    dx, dx_b, small["ln1_g"] = _rms_bwd(f"ln1_bwd_{l}", dh1, sv["x"], p["ln1_g"], dxm)
    dx_b = hook("bwd_end", dx_b)
    return dx, dx_b, small


def kernel(x, ln1_g, w_in, q_norm_g, k_norm_g, sinks, conv_w, conv_b, conv_ln_g, conv_ln_b, sgu_ln_g, sgu_ln_b, sgu_w, sgu_b, out_norm_g, w_out, ln2_g, w_up, w_down, loss_target, m_ln1_g, m_w_in, m_q_norm_g, m_k_norm_g, m_sinks, m_conv_w, m_conv_b, m_conv_ln_g, m_conv_ln_b, m_sgu_ln_g, m_sgu_ln_b, m_sgu_w, m_sgu_b, m_out_norm_g, m_w_out, m_ln2_g, m_w_up, m_w_down, v_ln1_g, v_w_in, v_q_norm_g, v_k_norm_g, v_sinks, v_conv_w, v_conv_b, v_conv_ln_g, v_conv_ln_b, v_sgu_ln_g, v_sgu_ln_b, v_sgu_w, v_sgu_b, v_out_norm_g, v_w_out, v_ln2_g, v_w_up, v_w_down):
    given = dict(locals())
    _LAST[0] = None
    n_layers = ln1_g.shape[0]
    s, d = x.shape[1], x.shape[2]
    cw = d // 4
    xi, yi, core = lax.axis_index("x"), lax.axis_index("y"), lax.axis_index("c")
    chip = 2 * xi + yi
    second = jnp.where(core == 0, 2 * xi + (1 - yi), 2 * (1 - xi) + yi)
    route_idx = jnp.stack([core, chip, second]).astype(jnp.int32)

    conv_w_pad = jnp.pad(conv_w, ((0, 0), (0, HALO - CONV_KERNEL), (0, 0))).reshape(1, n_layers * HALO, -1)
    cwl = conv_w_pad.shape[2]
    placed = [[_cast_place(f"place_{nm}_{l}", given[nm], l, route_idx[1:2], BF16) for nm in BIG]
              for l in range(n_layers)]
    first = [_cast_place("place_conv_w", conv_w_pad, 0, route_idx[1:2], F32)] + placed[0]
    for name, n_sems, build in _gather_stages("0", len(first)):
        first = _exchange(name, first, n_sems, build)
    conv_w_full = first[0].reshape(N_CHIPS, n_layers, HALO, cwl).transpose(1, 2, 0, 3).reshape(n_layers, HALO, cw)
    wgs = [dict(zip(BIG, first[1:]))] + [None] * (n_layers - 1)
    params = []
    for l in range(n_layers):
        p = {nm: given[nm][l] for nm in SMALL if nm != "conv_w"}
        for nm in ("conv_b", "conv_ln_g", "conv_ln_b", "sgu_ln_g", "sgu_ln_b"):
            p[nm] = p[nm].reshape(1, -1)
        p["conv_w"] = conv_w_full[l]
        p["sgu_bexp"] = jnp.repeat(sgu_b[l].T, HEAD_DIM, axis=1)
        params.append(p)

    def make_hook(table):
        def hook(point, carry):
            for fn in table.get(point, ()):
                fn()
            return carry
        return hook

    def gather_hooks(l):
        stages = _gather_stages(str(l), len(BIG))
        state = {"bufs": placed[l]}

        def begin(k):
            def fn():
                if k > 0:
                    state["bufs"] = _exchange_wait(state["h"])
                state["h"] = _exchange_start(stages[k][0], state["bufs"], stages[k][1], stages[k][2])
            return fn

        def end():
            bufs = _exchange_wait(state["h"])
            wgs[l] = dict(zip(BIG, _exchange(stages[2][0], bufs, stages[2][1], stages[2][2])))

        return {"fwd_start": [begin(0)], "fwd_mid": [begin(1)], "fwd_end": [end]}

    h = x.reshape(s, d)
    saved = []
    for l in range(n_layers):
        hook = make_hook(gather_hooks(l + 1)) if l + 1 < n_layers else _no_hook
        h, sv = _layer_fwd(l, h, params[l], wgs[l], hook)
        saved.append(sv)
    dh, dh_b, loss_part = _loss_head(h, loss_target.reshape(s, d))
    loss = lax.psum(loss_part[0, 0], ("x", "y", "c"))

    big_grads = [{} for _ in range(n_layers)]
    small_grads = [None] * n_layers
    halves = {}

    def rs_group(tag, l, names):
        phases = {}

        def start():
            phases["p"] = _reduce_scatter(tag, [f"{nm}_{l}" for nm in names], [big_grads[l][nm] for nm in names],
                                          route_idx)
            phases["p"][0]()

        def step(k):
            return lambda: phases["p"][k]()

        def finish():
            for nm, fh in zip(names, phases["p"][4]()):
                halves[(nm, l)] = fh

        return [start, step(1), step(2), step(3), finish]

    early = rs_group("l0a", 0, ("w_down", "w_up"))
    for l in reversed(range(n_layers)):
        table = {}
        if l + 1 < n_layers:
            above = rs_group(f"l{l + 1}", l + 1, BIG)
            for point, fn in zip(("bwd_start", "bwd_dup", "bwd_mix", "bwd_attn", "bwd_sgu"), above):
                table.setdefault(point, []).append(fn)
        if l == 0:
            for point, fn in zip(("bwd_dh2", "bwd_mix", "bwd_sgu", "bwd_end"), early[:4]):
                table.setdefault(point, []).append(fn)
        dh, dh_b, small_grads[l] = _layer_bwd(l, dh, dh_b, params[l], wgs[l], saved[l], big_grads[l],
                                              make_hook(table))
    grad_x = dh.reshape(x.shape)
    early[4]()
    for fn in rs_group("l0b", 0, ("w_out", "w_in")):
        fn()
    grads = {}

    small_shapes = [(n_layers,) + small_grads[0][nm].shape for nm in SMALL]
    small_sum = _small_allreduce(_pack([jnp.stack([small_grads[l][nm] for l in range(n_layers)]) for nm in SMALL]))
    for nm, g in zip(SMALL, _unpack(small_sum, small_shapes)):
        grads[nm] = g.reshape((n_layers,) + given[nm].shape[1:]) if nm != "conv_w" else g
    grads["conv_w"] = lax.dynamic_slice_in_dim(grads["conv_w"], chip * cwl, cwl, axis=2)

    delta, new_m, new_v = {}, {}, {}
    for nm in BIG:
        res = ()
        for l in range(n_layers):
            f, h = halves[(nm, l)]
            res = _adamw_big(f"adamw_{nm}_{l}", given[nm], given["m_" + nm], given["v_" + nm], f, h, l,
                             route_idx[0:1], res)
        grads[nm], delta[nm], new_m[nm], new_v[nm] = res
    packed = [_pack([src[nm] for nm in SMALL]) for src in
              ({nm: given[nm] for nm in SMALL}, grads, {nm: given["m_" + nm] for nm in SMALL},
               {nm: given["v_" + nm] for nm in SMALL})]
    local_shapes = [given[nm].shape for nm in SMALL]
    for dst, buf in zip((delta, new_m, new_v), _adamw("adamw_small", *packed)):
        for nm, a in zip(SMALL, _unpack(buf, local_shapes)):
            dst[nm] = a
    return (loss, grad_x, *[grads[nm] for nm in WEIGHTS], *[delta[nm] for nm in WEIGHTS],
            *[new_m[nm] for nm in WEIGHTS], *[new_v[nm] for nm in WEIGHTS])
```

```python
import functools

import jax
import jax.numpy as jnp
from jax import lax
from jax.experimental import pallas as pl
from jax.experimental.pallas import tpu as pltpu

F32 = jnp.float32
BF16 = jnp.bfloat16
EPS = 1e-6
NEG_INF = -1e30
HEAD_DIM = 64
WINDOW = 128
CONV_KERNEL = 31
HALO = 32
GQA = 4
N_CHIPS = 4
ADAM_LR, ADAM_B1, ADAM_B2, ADAM_EPS, ADAM_WD, ADAM_STEP = 0.001, 0.9, 0.999, 1e-08, 0.01, 10
VMEM_LIMIT = 56 * 1024 * 1024
TILE_K = 2048
MESH = pl.DeviceIdType.MESH
ANY = pl.BlockSpec(memory_space=pl.ANY)

NN = (((1,), (0,)), ((), ()))
NT = (((1,), (1,)), ((), ()))
TN = (((0,), (0,)), ((), ()))


def _cp(*sem):
    return pltpu.CompilerParams(dimension_semantics=sem, vmem_limit_bytes=VMEM_LIMIT)


_LAST = [None]


def _ordered_call(body, *, in_specs=None, grid_spec=None, **kw):
    def run(*operands):
        dep = _LAST[0]
        n = len(operands)
        if dep is None or any(dep is o for o in operands):
            fn, specs, spec, args = body, in_specs, grid_spec, operands
        else:
            def fn(*refs):
                body(*refs[:n], *refs[n + 1:])

            specs, spec, args = in_specs, grid_spec, operands + (dep,)
            if grid_spec is not None:
                spec = pltpu.PrefetchScalarGridSpec(
                    num_scalar_prefetch=grid_spec.num_scalar_prefetch, grid=grid_spec.grid,
                    in_specs=list(grid_spec.in_specs) + [ANY], out_specs=grid_spec.out_specs)
            else:
                specs = list(in_specs) + [ANY]
        if spec is not None:
            out = pl.pallas_call(fn, grid_spec=spec, **kw)(*args)
        else:
            out = pl.pallas_call(fn, in_specs=specs, **kw)(*args)
        arrays = [o for o in (out if isinstance(out, (tuple, list)) else (out,)) if o.dtype in (F32, BF16)]
        _LAST[0] = min(arrays, key=lambda o: o.size)
        return out

    return run


def _tile(dim, pref):
    if dim <= pref:
        return dim
    for t in range(pref, 0, -128):
        if dim % t == 0:
            return t
    while dim % pref:
        pref //= 2
    return pref


def _dot(a, b, dims=NN):
    return lax.dot_general(a, b, dims, preferred_element_type=F32)


def _colsum(v):
    return jnp.sum(v, axis=0, keepdims=True)


def _sigmoid(x):
    return 1.0 / (1.0 + jnp.exp(-x))


def _matmul(name, operands, in_specs, out_shape, out_specs, grid, dims, acc_shape, epilogue):
    nk = grid[2]
    n_in = len(operands)

    def body(*refs):
        a_ref, b_ref = refs[0], refs[1]
        extra = refs[2:n_in]
        if nk == 1:
            epilogue(_dot(a_ref[...], b_ref[...], dims), extra, refs[n_in:])
            return
        outs = refs[n_in:-1]
        acc = refs[-1]
        k = pl.program_id(2)

        @pl.when(k == 0)
        def _():
            acc[...] = _dot(a_ref[...], b_ref[...], dims)

        @pl.when((k > 0) & (k < nk - 1))
        def _():
            acc[...] += _dot(a_ref[...], b_ref[...], dims)

        @pl.when(k == nk - 1)
        def _():
            epilogue(acc[...] + _dot(a_ref[...], b_ref[...], dims), extra, outs)

    return _ordered_call(
        body, name=name, grid=grid, in_specs=in_specs, out_specs=out_specs, out_shape=out_shape,
        scratch_shapes=[pltpu.VMEM(acc_shape, F32)] if nk > 1 else [],
        compiler_params=_cp("parallel", "parallel", "arbitrary"),
    )(*operands)


def _ep_store(acc, extra, outs):
    outs[0][...] = acc.astype(outs[0].dtype)


def _ep_residual(acc, extra, outs):
    outs[0][...] = extra[0][...] + acc


def _ep_up(acc, extra, outs):
    outs[0][...] = acc.astype(BF16)
    r = jnp.maximum(acc, 0.0)
    outs[1][...] = (r * r).astype(BF16)


def _ep_dup(acc, extra, outs):
    outs[0][...] = (acc * (2.0 * jnp.maximum(extra[0][...].astype(F32), 0.0))).astype(BF16)


def _mm_act_w(name, a, wg, col_sharded, epilogue, extra=(), out_dtypes=(F32,)):
    m, kdim = a.shape
    _, r, c = wg.shape
    tm = _tile(m, 1024)
    if col_sharded:
        n = N_CHIPS * c
        tn = _tile(c, 1024)
        tk = _tile(kdim, TILE_K)
        per = c // tn
        b_spec = pl.BlockSpec((None, tk, tn), lambda i, j, k: (j // per, k, j % per))
    else:
        n = c
        tn = _tile(n, 1024)
        tk = _tile(r, TILE_K)
        per = r // tk
        b_spec = pl.BlockSpec((None, tk, tn), lambda i, j, k: (k // per, k % per, j))
    grid = (m // tm, n // tn, kdim // tk)
    o_spec = pl.BlockSpec((tm, tn), lambda i, j, k: (i, j))
    in_specs = [pl.BlockSpec((tm, tk), lambda i, j, k: (i, k)), b_spec] + [o_spec] * len(extra)
    return _matmul(name, (a, wg) + tuple(extra), in_specs,
                   tuple(jax.ShapeDtypeStruct((m, n), d) for d in out_dtypes),
                   tuple(o_spec for _ in out_dtypes), grid, NN, (tm, tn), epilogue)


def _mm_act_wt(name, a, wg, col_sharded, epilogue, extra=(), out_dtypes=(F32,)):
    m, kdim = a.shape
    _, r, c = wg.shape
    tm = _tile(m, 1024)
    if col_sharded:
        n = r
        tn = _tile(n, 1024)
        tk = _tile(c, TILE_K)
        per = c // tk
        b_spec = pl.BlockSpec((None, tn, tk), lambda i, j, k: (k // per, j, k % per))
    else:
        n = N_CHIPS * r
        tn = _tile(r, 1024)
        tk = _tile(c, TILE_K)
        per = r // tn
        b_spec = pl.BlockSpec((None, tn, tk), lambda i, j, k: (j // per, j % per, k))
    grid = (m // tm, n // tn, kdim // tk)
    o_spec = pl.BlockSpec((tm, tn), lambda i, j, k: (i, j))
    in_specs = [pl.BlockSpec((tm, tk), lambda i, j, k: (i, k)), b_spec] + [o_spec] * len(extra)
    return _matmul(name, (a, wg) + tuple(extra), in_specs,
                   tuple(jax.ShapeDtypeStruct((m, n), d) for d in out_dtypes),
                   tuple(o_spec for _ in out_dtypes), grid, NT, (tm, tn), epilogue)


def _mm_wgrad(name, a, g, col_sharded, c):
    s, kdim = a.shape
    _, n = g.shape
    ts = _tile(s, TILE_K)
    if col_sharded:
        r = kdim
        tm = _tile(kdim, 1024)
        tn = _tile(c, 1024)
        per = c // tn
        o_spec = pl.BlockSpec((None, tm, tn), lambda i, j, k: (j // per, i, j % per))
    else:
        r = kdim // N_CHIPS
        tm = _tile(r, 1024)
        tn = _tile(c, 1024)
        per = r // tm
        o_spec = pl.BlockSpec((None, tm, tn), lambda i, j, k: (i // per, i % per, j))
    grid = (kdim // tm, n // tn, s // ts)
    in_specs = [pl.BlockSpec((ts, tm), lambda i, j, k: (k, i)), pl.BlockSpec((ts, tn), lambda i, j, k: (k, j))]
    return _matmul(name, (a, g), in_specs, (jax.ShapeDtypeStruct((N_CHIPS, r, c), BF16),), (o_spec,),
                   grid, TN, (tm, tn), _ep_store)[0]


def _rms_fwd(name, x, g):
    s, d = x.shape
    tb = _tile(s, 256)

    def body(x_ref, g_ref, o_ref):
        xv = x_ref[...]
        r = lax.rsqrt(jnp.mean(xv * xv, axis=-1, keepdims=True) + EPS)
        o_ref[...] = (xv * r * g_ref[...]).astype(BF16)

    return _ordered_call(
        body, name=name, grid=(s // tb,),
        in_specs=[pl.BlockSpec((tb, d), lambda i: (i, 0)), pl.BlockSpec((1, d), lambda i: (0, 0))],
        out_specs=pl.BlockSpec((tb, d), lambda i: (i, 0)),
        out_shape=jax.ShapeDtypeStruct((s, d), BF16), compiler_params=_cp("parallel"),
    )(x, g.reshape(1, d))


def _rms_bwd(name, dh, x, g, dres):
    s, d = x.shape
    tb = _tile(s, 256)

    def body(dh_ref, x_ref, g_ref, dres_ref, dx_ref, dxb_ref, dg_ref):
        i = pl.program_id(0)
        xv = x_ref[...]
        r = lax.rsqrt(jnp.mean(xv * xv, axis=-1, keepdims=True) + EPS)
        xhat = xv * r
        dhv = dh_ref[...]
        dxhat = dhv * g_ref[...]
        dx = dres_ref[...] + r * (dxhat - xhat * jnp.mean(dxhat * xhat, axis=-1, keepdims=True))
        dx_ref[...] = dx
        dxb_ref[...] = dx.astype(BF16)

        @pl.when(i == 0)
        def _():
            dg_ref[...] = jnp.zeros_like(dg_ref)

        dg_ref[...] += _colsum(dhv * xhat)

    row = pl.BlockSpec((tb, d), lambda i: (i, 0))
    vec = pl.BlockSpec((1, d), lambda i: (0, 0))
    return _ordered_call(
        body, name=name, grid=(s // tb,), in_specs=[row, row, vec, row], out_specs=(row, row, vec),
        out_shape=(jax.ShapeDtypeStruct((s, d), F32), jax.ShapeDtypeStruct((s, d), BF16),
                   jax.ShapeDtypeStruct((1, d), F32)),
        compiler_params=_cp("arbitrary"),
    )(dh, x, g.reshape(1, d), dres)


def _loss_head(y, t):
    s, d = y.shape
    tb = _tile(s, 256)

    def body(y_ref, t_ref, dy_ref, dyb_ref, loss_ref, acc):
        i = pl.program_id(0)
        e = y_ref[...] - t_ref[...]
        dy = e * (1.0 / d)
        dy_ref[...] = dy
        dyb_ref[...] = dy.astype(BF16)

        @pl.when(i == 0)
        def _():
            acc[...] = jnp.zeros_like(acc)

        acc[...] += _colsum(e * e)

        @pl.when(i == pl.num_programs(0) - 1)
        def _():
            loss_ref[...] = jnp.sum(acc[...], axis=-1, keepdims=True) * (0.5 / d)

    row = pl.BlockSpec((tb, d), lambda i: (i, 0))
    return _ordered_call(
        body, name="loss_head", grid=(s // tb,), in_specs=[row, row],
        out_specs=(row, row, pl.BlockSpec((1, 1), lambda i: (0, 0))),
        out_shape=(jax.ShapeDtypeStruct((s, d), F32), jax.ShapeDtypeStruct((s, d), BF16),
                   jax.ShapeDtypeStruct((1, 1), F32)),
        scratch_shapes=[pltpu.VMEM((1, d), F32)], compiler_params=_cp("arbitrary"),
    )(y, t)


def _mixnorm_fwd(name, ya, yc, ys, g):
    s, aw = ya.shape
    cw, sw = yc.shape[1], ys.shape[1]
    d = aw + cw + sw
    tb = _tile(s, 256)

    def body(ya_ref, yc_ref, ys_ref, g_ref, o_ref):
        off = 0
        for ref, w in ((ya_ref, aw), (yc_ref, cw), (ys_ref, sw)):
            v = ref[...]
            r = lax.rsqrt(jnp.mean(v * v, axis=-1, keepdims=True) + EPS)
            o_ref[:, off:off + w] = (v * r * g_ref[:, off:off + w]).astype(BF16)
            off += w

    def row(w):
        return pl.BlockSpec((tb, w), lambda i: (i, 0))

    return _ordered_call(
        body, name=name, grid=(s // tb,),
        in_specs=[row(aw), row(cw), row(sw), pl.BlockSpec((1, d), lambda i: (0, 0))], out_specs=row(d),
        out_shape=jax.ShapeDtypeStruct((s, d), BF16), compiler_params=_cp("parallel"),
    )(ya, yc, ys, g.reshape(1, d))


def _mixnorm_bwd(name, dmix, ya, yc, ys, g):
    s, aw = ya.shape
    cw, sw = yc.shape[1], ys.shape[1]
    d = aw + cw + sw
    tb = _tile(s, 256)

    def body(dm_ref, ya_ref, yc_ref, ys_ref, g_ref, dya_ref, dyc_ref, dys_ref, dg_ref):
        i = pl.program_id(0)

        @pl.when(i == 0)
        def _():
            dg_ref[...] = jnp.zeros_like(dg_ref)

        off = 0
        for ref, dref, w in ((ya_ref, dya_ref, aw), (yc_ref, dyc_ref, cw), (ys_ref, dys_ref, sw)):
            v = ref[...]
            r = lax.rsqrt(jnp.mean(v * v, axis=-1, keepdims=True) + EPS)
            vhat = v * r
            dm = dm_ref[:, off:off + w]
            dvhat = dm * g_ref[:, off:off + w]
            dref[...] = r * (dvhat - vhat * jnp.mean(dvhat * vhat, axis=-1, keepdims=True))
            dg_ref[:, off:off + w] += _colsum(dm * vhat)
            off += w

    def row(w):
        return pl.BlockSpec((tb, w), lambda i: (i, 0))

    vec = pl.BlockSpec((1, d), lambda i: (0, 0))
    return _ordered_call(
        body, name=name, grid=(s // tb,), in_specs=[row(d), row(aw), row(cw), row(sw), vec],
        out_specs=(row(aw), row(cw), row(sw), vec),
        out_shape=(jax.ShapeDtypeStruct((s, aw), F32), jax.ShapeDtypeStruct((s, cw), F32),
                   jax.ShapeDtypeStruct((s, sw), F32), jax.ShapeDtypeStruct((1, d), F32)),
        compiler_params=_cp("arbitrary"),
    )(dmix, ya, yc, ys, g.reshape(1, d))


def _head_rms(x):
    r = lax.rsqrt(jnp.mean(x * x, axis=-1, keepdims=True) + EPS)
    return x * r, r


def _attn_mask(n):
    qi = lax.broadcasted_iota(jnp.int32, (WINDOW, 2 * WINDOW), 0)
    sj = lax.broadcasted_iota(jnp.int32, (WINDOW, 2 * WINDOW), 1)
    rel = qi + WINDOW - sj
    return (rel >= 0) & (rel < WINDOW) & ((sj >= WINDOW) | (n > 0))


def _attn_specs(nq, nkv, nb):
    qspec = pl.BlockSpec((nq, WINDOW, HEAD_DIM), lambda n: (0, n, 0))
    cur = pl.BlockSpec((nkv, WINDOW, HEAD_DIM), lambda n: (0, n, 0))
    prev = pl.BlockSpec((nkv, WINDOW, HEAD_DIM), lambda n: (0, jnp.maximum(n - 1, 0), 0))
    nxt = pl.BlockSpec((nkv, WINDOW, HEAD_DIM), lambda n: (0, jnp.minimum(n + 1, nb - 1), 0))
    gain = pl.BlockSpec((1, HEAD_DIM), lambda n: (0, 0))
    sink = pl.BlockSpec((nq, 1, 128), lambda n: (0, 0, 0))
    return qspec, cur, prev, nxt, gain, sink


def _attn_probs(qn_b, kn_b, valid, sink):
    logits = _dot(qn_b, kn_b, NT) * (HEAD_DIM ** -0.5)
    logits = jnp.where(valid, logits, NEG_INF)
    m = jnp.maximum(jnp.max(logits, axis=-1, keepdims=True), sink)
    p = jnp.exp(logits - m)
    es = jnp.exp(sink - m)
    denom = jnp.sum(p, axis=-1, keepdims=True) + es
    return p / denom, es / denom


def _attn_fwd(name, q, k, v, gq, gk, sinks_b):
    nq, s, _ = q.shape
    nkv = k.shape[0]
    nb = s // WINDOW
    qspec, cur, prev, _, gain, sink = _attn_specs(nq, nkv, nb)

    def body(q_ref, kc_ref, kp_ref, vc_ref, vp_ref, gq_ref, gk_ref, s_ref, o_ref):
        gkv = gk_ref[...]
        valid = _attn_mask(pl.program_id(0))
        for g in range(nkv):
            kn_b = jnp.concatenate([_head_rms(kp_ref[g])[0] * gkv, _head_rms(kc_ref[g])[0] * gkv],
                                   axis=0).astype(BF16)
            vv_b = jnp.concatenate([vp_ref[g], vc_ref[g]], axis=0).astype(BF16)
            for i in range(g * GQA, (g + 1) * GQA):
                qn_b = (_head_rms(q_ref[i])[0] * gq_ref[...]).astype(BF16)
                probs, _ = _attn_probs(qn_b, kn_b, valid, s_ref[i][:, :1])
                o_ref[i] = _dot(probs.astype(BF16), vv_b)

    return _ordered_call(
        body, name=name, grid=(nb,), in_specs=[qspec, cur, prev, cur, prev, gain, gain, sink], out_specs=qspec,
        out_shape=jax.ShapeDtypeStruct((nq, s, HEAD_DIM), F32), compiler_params=_cp("parallel"),
    )(q, k, k, v, v, gq.reshape(1, HEAD_DIM), gk.reshape(1, HEAD_DIM), sinks_b)


def _attn_bwd(name, q, k, v, gq, gk, sinks_b, do):
    nq, s, _ = q.shape
    nkv = k.shape[0]
    nb = s // WINDOW
    qspec, cur, prev, _, gain, sink = _attn_specs(nq, nkv, nb)

    def body(q_ref, kc_ref, kp_ref, vc_ref, vp_ref, gq_ref, gk_ref, s_ref, do_ref,
             dq_ref, dkc_ref, dkp_ref, dvc_ref, dvp_ref, dgq_ref, ds_ref):
        n = pl.program_id(0)

        @pl.when(n == 0)
        def _():
            dgq_ref[...] = jnp.zeros_like(dgq_ref)
            ds_ref[...] = jnp.zeros_like(ds_ref)

        gkv = gk_ref[...]
        gqv = gq_ref[...]
        valid = _attn_mask(n)
        dgq = jnp.zeros((1, HEAD_DIM), F32)
        for g in range(nkv):
            kn_b = jnp.concatenate([_head_rms(kp_ref[g])[0] * gkv, _head_rms(kc_ref[g])[0] * gkv],
                                   axis=0).astype(BF16)
            vv_b = jnp.concatenate([vp_ref[g], vc_ref[g]], axis=0).astype(BF16)
            dkn = jnp.zeros((2 * WINDOW, HEAD_DIM), F32)
            dvv = jnp.zeros((2 * WINDOW, HEAD_DIM), F32)
            for i in range(g * GQA, (g + 1) * GQA):
                qhat, r = _head_rms(q_ref[i])
                qn_b = (qhat * gqv).astype(BF16)
                probs, psink = _attn_probs(qn_b, kn_b, valid, s_ref[i][:, :1])
                do_b = do_ref[i].astype(BF16)
                dp = _dot(do_b, vv_b, NT)
                delta = jnp.sum(probs * dp, axis=-1, keepdims=True)
                dl_b = (probs * (dp - delta) * (HEAD_DIM ** -0.5)).astype(BF16)
                ds_ref[i] += jnp.broadcast_to(-jnp.sum(psink * delta, axis=0, keepdims=True), (1, 128))
                dqn = _dot(dl_b, kn_b)
                dkn += _dot(dl_b, qn_b, TN)
                dvv += _dot(probs.astype(BF16), do_b, TN)
                dgq += _colsum(dqn * qhat)
                dqhat = dqn * gqv
                dq_ref[i] = r * (dqhat - qhat * jnp.mean(dqhat * qhat, axis=-1, keepdims=True))
            dkp_ref[g] = dkn[:WINDOW]
            dkc_ref[g] = dkn[WINDOW:]
            dvp_ref[g] = dvv[:WINDOW]
            dvc_ref[g] = dvv[WINDOW:]
        dgq_ref[...] += dgq

    kv_shape = jax.ShapeDtypeStruct((nkv, s, HEAD_DIM), F32)
    return _ordered_call(
        body, name=name, grid=(nb,), in_specs=[qspec, cur, prev, cur, prev, gain, gain, sink, qspec],
        out_specs=(qspec, cur, cur, cur, cur, gain, sink),
        out_shape=(jax.ShapeDtypeStruct((nq, s, HEAD_DIM), F32), kv_shape, kv_shape, kv_shape, kv_shape,
                   jax.ShapeDtypeStruct((1, HEAD_DIM), F32), jax.ShapeDtypeStruct((nq, 1, 128), F32)),
        compiler_params=_cp("arbitrary"),
    )(q, k, k, v, v, gq.reshape(1, HEAD_DIM), gk.reshape(1, HEAD_DIM), sinks_b, do)


def _attn_bwd_kv(name, k, gk, dkc, dkp, dvc, dvp):
    nkv, s, _ = k.shape
    nb = s // WINDOW
    _, cur, _, nxt, gain, _ = _attn_specs(GQA * nkv, nkv, nb)

    def body(k_ref, gk_ref, dkc_ref, dkp_ref, dvc_ref, dvp_ref, dk_ref, dv_ref, dgk_ref):
        n = pl.program_id(0)

        @pl.when(n == 0)
        def _():
            dgk_ref[...] = jnp.zeros_like(dgk_ref)

        has_next = n < nb - 1
        dgk = jnp.zeros((1, HEAD_DIM), F32)
        for g in range(nkv):
            dkn = dkc_ref[g] + jnp.where(has_next, dkp_ref[g], 0.0)
            dv_ref[g] = dvc_ref[g] + jnp.where(has_next, dvp_ref[g], 0.0)
            khat, r = _head_rms(k_ref[g])
            dgk += _colsum(dkn * khat)
            dkhat = dkn * gk_ref[...]
            dk_ref[g] = r * (dkhat - khat * jnp.mean(dkhat * khat, axis=-1, keepdims=True))
        dgk_ref[...] += dgk

    kv_shape = jax.ShapeDtypeStruct((nkv, s, HEAD_DIM), F32)
    return _ordered_call(
        body, name=name, grid=(nb,), in_specs=[cur, gain, cur, nxt, cur, nxt], out_specs=(cur, cur, gain),
        out_shape=(kv_shape, kv_shape, jax.ShapeDtypeStruct((1, HEAD_DIM), F32)),
        compiler_params=_cp("arbitrary"),
    )(k, gk.reshape(1, HEAD_DIM), dkc, dkp, dvc, dvp)


def _conv_recompute(i, a_ref, gt_ref, ap_ref, gp_ref, w_ref, b_ref, hbuf, tb):
    hbuf[pl.ds(HALO, tb), :] = a_ref[...] * _sigmoid(gt_ref[...])
    tail = ap_ref[pl.ds(tb - HALO, HALO), :] * _sigmoid(gp_ref[pl.ds(tb - HALO, HALO), :])
    hbuf[pl.ds(0, HALO), :] = jnp.where(i > 0, tail, 0.0)
    acc = jnp.broadcast_to(b_ref[...], a_ref.shape)
    for kk in range(CONV_KERNEL):
        acc = acc + w_ref[pl.ds(kk, 1), :] * hbuf[pl.ds(HALO - (CONV_KERNEL - 1) + kk, tb), :]
    return acc


def _layer_norm_stats(c):
    mu = jnp.mean(c, axis=-1, keepdims=True)
    xc = c - mu
    r = lax.rsqrt(jnp.mean(xc * xc, axis=-1, keepdims=True) + EPS)
    return xc * r, r


def _conv_specs(s, cw, tb, a_blk):
    cur = lambda off: pl.BlockSpec((tb, cw), lambda i: (i, a_blk + off))
    prev = lambda off: pl.BlockSpec((tb, cw), lambda i: (jnp.maximum(i - 1, 0), a_blk + off))
    wspec = pl.BlockSpec((HALO, cw), lambda i: (0, 0))
    vec = pl.BlockSpec((1, cw), lambda i: (0, 0))
    row = pl.BlockSpec((tb, cw), lambda i: (i, 0))
    return cur, prev, wspec, vec, row


def _conv_fwd(name, proj, a_blk, w, b, lg, lb):
    s = proj.shape[0]
    cw = w.shape[1]
    tb = _tile(s, 256)
    cur, prev, wspec, vec, row = _conv_specs(s, cw, tb, a_blk)

    def body(a_ref, gt_ref, ap_ref, gp_ref, w_ref, b_ref, lg_ref, lb_ref, y_ref, hbuf):
        c = _conv_recompute(pl.program_id(0), a_ref, gt_ref, ap_ref, gp_ref, w_ref, b_ref, hbuf, tb)
        chat, _ = _layer_norm_stats(c)
        z = chat * lg_ref[...] + lb_ref[...]
        y_ref[...] = z * _sigmoid(z)

    return _ordered_call(
        body, name=name, grid=(s // tb,), in_specs=[cur(0), cur(1), prev(0), prev(1), wspec, vec, vec, vec],
        out_specs=row, out_shape=jax.ShapeDtypeStruct((s, cw), F32),
        scratch_shapes=[pltpu.VMEM((tb + HALO, cw), F32)], compiler_params=_cp("arbitrary"),
    )(proj, proj, proj, proj, w, b, lg, lb)


def _conv_bwd1(name, proj, a_blk, w, b, lg, lb, dy):
    s = proj.shape[0]
    cw = w.shape[1]
    tb = _tile(s, 256)
    cur, prev, wspec, vec, row = _conv_specs(s, cw, tb, a_blk)

    def body(a_ref, gt_ref, ap_ref, gp_ref, w_ref, b_ref, lg_ref, lb_ref, dy_ref,
             dc_ref, dw_ref, db_ref, dlg_ref, dlb_ref, hbuf):
        i = pl.program_id(0)

        @pl.when(i == 0)
        def _():
            dw_ref[...] = jnp.zeros_like(dw_ref)
            db_ref[...] = jnp.zeros_like(db_ref)
            dlg_ref[...] = jnp.zeros_like(dlg_ref)
            dlb_ref[...] = jnp.zeros_like(dlb_ref)

        c = _conv_recompute(i, a_ref, gt_ref, ap_ref, gp_ref, w_ref, b_ref, hbuf, tb)
        chat, r = _layer_norm_stats(c)
        z = chat * lg_ref[...] + lb_ref[...]
        sg = _sigmoid(z)
        dz = dy_ref[...] * (sg + z * sg * (1.0 - sg))
        dlg_ref[...] += _colsum(dz * chat)
        dlb_ref[...] += _colsum(dz)
        dzg = dz * lg_ref[...]
        dc = r * (dzg - jnp.mean(dzg, axis=-1, keepdims=True) - chat * jnp.mean(dzg * chat, axis=-1, keepdims=True))
        dc_ref[...] = dc
        db_ref[...] += _colsum(dc)
        for kk in range(CONV_KERNEL):
            dw_ref[pl.ds(kk, 1), :] += _colsum(dc * hbuf[pl.ds(HALO - (CONV_KERNEL - 1) + kk, tb), :])

    return _ordered_call(
        body, name=name, grid=(s // tb,), in_specs=[cur(0), cur(1), prev(0), prev(1), wspec, vec, vec, vec, row],
        out_specs=(row, wspec, vec, vec, vec),
        out_shape=(jax.ShapeDtypeStruct((s, cw), F32), jax.ShapeDtypeStruct((HALO, cw), F32),
                   jax.ShapeDtypeStruct((1, cw), F32), jax.ShapeDtypeStruct((1, cw), F32),
                   jax.ShapeDtypeStruct((1, cw), F32)),
        scratch_shapes=[pltpu.VMEM((tb + HALO, cw), F32)], compiler_params=_cp("arbitrary"),
    )(proj, proj, proj, proj, w, b, lg, lb, dy)


def _conv_bwd2(name, proj, a_blk, w, dc):
    s = proj.shape[0]
    cw = w.shape[1]
    tb = _tile(s, 256)
    nblk = s // tb
    cur, _, wspec, _, row = _conv_specs(s, cw, tb, a_blk)
    nxt = pl.BlockSpec((tb, cw), lambda i: (jnp.minimum(i + 1, nblk - 1), 0))

    def body(a_ref, gt_ref, w_ref, dc_ref, dn_ref, o_ref, dbuf):
        i = pl.program_id(0)
        dbuf[pl.ds(0, tb), :] = dc_ref[...]
        dbuf[pl.ds(tb, HALO), :] = jnp.where(i < nblk - 1, dn_ref[pl.ds(0, HALO), :], 0.0)
        dh = jnp.zeros((tb, cw), F32)
        for kk in range(CONV_KERNEL):
            dh = dh + w_ref[pl.ds(kk, 1), :] * dbuf[pl.ds(CONV_KERNEL - 1 - kk, tb), :]
        sg = _sigmoid(gt_ref[...])
        o_ref[:, 0:cw] = (dh * sg).astype(BF16)
        o_ref[:, cw:2 * cw] = (dh * a_ref[...] * sg * (1.0 - sg)).astype(BF16)

    return _ordered_call(
        body, name=name, grid=(nblk,), in_specs=[cur(0), cur(1), wspec, row, nxt],
        out_specs=pl.BlockSpec((tb, 2 * cw), lambda i: (i, 0)), out_shape=jax.ShapeDtypeStruct((s, 2 * cw), BF16),
        scratch_shapes=[pltpu.VMEM((tb + HALO, cw), F32)], compiler_params=_cp("arbitrary"),
    )(proj, proj, w, dc, dc)


def _sgu_common(v_ref, lg_ref, lb_ref, w_ref, bexp_ref, sw):
    vhat, r = _layer_norm_stats(v_ref[...])
    vn_b = (vhat * lg_ref[...] + lb_ref[...]).astype(BF16)
    ii = lax.broadcasted_iota(jnp.int32, (WINDOW, WINDOW), 0)
    jj = lax.broadcasted_iota(jnp.int32, (WINDOW, WINDOW), 1)
    tril = jj <= ii
    head_of = lax.broadcasted_iota(jnp.int32, (WINDOW, sw), 1) // HEAD_DIM
    wts = [jnp.where(tril, w_ref[h], 0.0).astype(BF16) for h in range(sw // HEAD_DIM)]
    sv = bexp_ref[...]
    for h, wt in enumerate(wts):
        sv = sv + jnp.where(head_of == h, _dot(wt, vn_b), 0.0)
    return vhat, r, vn_b, tril, head_of, wts, sv


def _sgu_specs(sw, u_blk):
    nh = sw // HEAD_DIM
    u = pl.BlockSpec((WINDOW, sw), lambda n: (n, u_blk))
    v = pl.BlockSpec((WINDOW, sw), lambda n: (n, u_blk + 1))
    vec = pl.BlockSpec((1, sw), lambda n: (0, 0))
    wspec = pl.BlockSpec((nh, WINDOW, WINDOW), lambda n: (0, 0, 0))
    bspec = pl.BlockSpec((WINDOW, sw), lambda n: (0, 0))
    row = pl.BlockSpec((WINDOW, sw), lambda n: (n, 0))
    return u, v, vec, wspec, bspec, row


def _sgu_fwd(name, proj, u_blk, lg, lb, w, bexp):
    s = proj.shape[0]
    sw = lg.shape[1]
    u, v, vec, wspec, bspec, row = _sgu_specs(sw, u_blk)

    def body(u_ref, v_ref, lg_ref, lb_ref, w_ref, bexp_ref, y_ref):
        sv = _sgu_common(v_ref, lg_ref, lb_ref, w_ref, bexp_ref, sw)[-1]
        y_ref[...] = u_ref[...] * sv

    return _ordered_call(
        body, name=name, grid=(s // WINDOW,), in_specs=[u, v, vec, vec, wspec, bspec], out_specs=row,
        out_shape=jax.ShapeDtypeStruct((s, sw), F32), compiler_params=_cp("parallel"),
    )(proj, proj, lg, lb, w, bexp)


def _sgu_bwd(name, proj, u_blk, lg, lb, w, bexp, dy):
    s = proj.shape[0]
    sw = lg.shape[1]
    nh = sw // HEAD_DIM
    u, v, vec, wspec, bspec, row = _sgu_specs(sw, u_blk)
    dbspec = pl.BlockSpec((nh, WINDOW), lambda n: (0, 0))

    def body(u_ref, v_ref, lg_ref, lb_ref, w_ref, bexp_ref, dy_ref, o_ref, dw_ref, db_ref, dlg_ref, dlb_ref):
        n = pl.program_id(0)

        @pl.when(n == 0)
        def _():
            dw_ref[...] = jnp.zeros_like(dw_ref)
            db_ref[...] = jnp.zeros_like(db_ref)
            dlg_ref[...] = jnp.zeros_like(dlg_ref)
            dlb_ref[...] = jnp.zeros_like(dlb_ref)

        vhat, r, vn_b, tril, head_of, wts, sv = _sgu_common(v_ref, lg_ref, lb_ref, w_ref, bexp_ref, sw)
        dyv = dy_ref[...]
        o_ref[:, 0:sw] = (dyv * sv).astype(BF16)
        ds = dyv * u_ref[...]
        dvn = jnp.zeros((WINDOW, sw), F32)
        for h, wt in enumerate(wts):
            dsm_b = jnp.where(head_of == h, ds, 0.0).astype(BF16)
            dvn = dvn + _dot(wt, dsm_b, TN)
            dw_ref[h] += jnp.where(tril, _dot(dsm_b, vn_b, NT), 0.0)
        hmask = (lax.broadcasted_iota(jnp.int32, (nh, sw), 1) // HEAD_DIM
                 == lax.broadcasted_iota(jnp.int32, (nh, sw), 0)).astype(F32)
        db_ref[...] += lax.dot_general(hmask, ds, NT, precision=lax.Precision.HIGHEST, preferred_element_type=F32)
        dlg_ref[...] += _colsum(dvn * vhat)
        dlb_ref[...] += _colsum(dvn)
        dvg = dvn * lg_ref[...]
        dv = r * (dvg - jnp.mean(dvg, axis=-1, keepdims=True) - vhat * jnp.mean(dvg * vhat, axis=-1, keepdims=True))
        o_ref[:, sw:2 * sw] = dv.astype(BF16)

    return _ordered_call(
        body, name=name, grid=(s // WINDOW,), in_specs=[u, v, vec, vec, wspec, bspec, row],
        out_specs=(pl.BlockSpec((WINDOW, 2 * sw), lambda n: (n, 0)), wspec, dbspec, vec, vec),
        out_shape=(jax.ShapeDtypeStruct((s, 2 * sw), BF16), jax.ShapeDtypeStruct((nh, WINDOW, WINDOW), F32),
                   jax.ShapeDtypeStruct((nh, WINDOW), F32), jax.ShapeDtypeStruct((1, sw), F32),
                   jax.ShapeDtypeStruct((1, sw), F32)),
        compiler_params=_cp("arbitrary"),
    )(proj, proj, lg, lb, w, bexp, dy)


def _adamw(name, w, g, m, v):
    rows, cols = w.shape
    tr = _tile(rows, 256)

    def body(w_ref, g_ref, m_ref, v_ref, d_ref, nm_ref, nv_ref):
        gv = g_ref[...]
        mv = ADAM_B1 * m_ref[...] + (1.0 - ADAM_B1) * gv
        vv = ADAM_B2 * v_ref[...] + (1.0 - ADAM_B2) * (gv * gv)
        m_hat = mv / (1.0 - ADAM_B1 ** ADAM_STEP)
        v_hat = vv / (1.0 - ADAM_B2 ** ADAM_STEP)
        d_ref[...] = -ADAM_LR * (m_hat / (jnp.sqrt(v_hat) + ADAM_EPS) + ADAM_WD * w_ref[...])
        nm_ref[...] = mv
        nv_ref[...] = vv

    spec = pl.BlockSpec((tr, cols), lambda i: (i, 0))
    shape = jax.ShapeDtypeStruct((rows, cols), F32)
    return _ordered_call(
        body, name=name, grid=(rows // tr,), in_specs=[spec] * 4, out_specs=(spec,) * 3, out_shape=(shape,) * 3,
        compiler_params=_cp("parallel"),
    )(w, g, m, v)


def _route():
    x, y, c = lax.axis_index("x"), lax.axis_index("y"), lax.axis_index("c")
    n1 = (jnp.where(c == 0, 1 - x, x), jnp.where(c == 0, y, 1 - y))
    n2 = (jnp.where(c == 0, x, 1 - x), jnp.where(c == 0, 1 - y, y))
    return x, y, c, n1, n2, (1 - x, 1 - y)


def _cidx(chip):
    return 2 * chip[0] + chip[1]


def _remote(src, dst, sems, k, device):
    send_sems, recv_sems = sems
    return pltpu.make_async_remote_copy(src_ref=src, dst_ref=dst, send_sem=send_sems.at[k], recv_sem=recv_sems.at[k],
                                        device_id=device, device_id_type=MESH)


def _exchange(name, bufs, n_sems, build):
    n = len(bufs)

    def body(*refs):
        cps = build(refs[n:2 * n], (refs[2 * n], refs[2 * n + 1]))
        for cp in cps:
            cp.start()
        for cp in cps:
            cp.wait()

    return _ordered_call(
        body, name=name, in_specs=[ANY] * n, out_specs=tuple(ANY for _ in range(n)),
        out_shape=tuple(jax.ShapeDtypeStruct(b.shape, b.dtype) for b in bufs),
        input_output_aliases={i: i for i in range(n)},
        scratch_shapes=[pltpu.SemaphoreType.DMA((n_sems,)), pltpu.SemaphoreType.DMA((n_sems,))],
        compiler_params=pltpu.CompilerParams(has_side_effects=True),
    )(*bufs)


HBM_SPEC = pl.BlockSpec(memory_space=pltpu.HBM)
SEM_SPEC = pl.BlockSpec(memory_space=pltpu.SEMAPHORE)
DATAFLOW = pltpu.SideEffectType.DATAFLOW_SIDE_EFFECTING


def _exchange_start(name, bufs, n_sems, build):
    n = len(bufs)

    def body(*refs):
        for cp in build(refs[:n], (refs[n], refs[n + 1])):
            cp.start()
        refs[-1][...] = jnp.zeros_like(refs[-1])

    out = _ordered_call(
        body, name=name,
        out_shape=(pltpu.SemaphoreType.DMA((n_sems,)), pltpu.SemaphoreType.DMA((n_sems,)))
        + tuple(pltpu.HBM(b.shape, b.dtype) for b in bufs) + (jax.ShapeDtypeStruct((8, 128), F32),),
        in_specs=[HBM_SPEC] * n,
        out_specs=(SEM_SPEC, SEM_SPEC) + (HBM_SPEC,) * n + (pl.BlockSpec(memory_space=pltpu.VMEM),),
        input_output_aliases={i: 2 + i for i in range(n)},
        compiler_params=pltpu.CompilerParams(has_side_effects=DATAFLOW),
    )(*[pltpu.with_memory_space_constraint(b, pltpu.HBM) for b in bufs])
    return dict(name=name, send=out[0], recv=out[1], bufs=list(out[2:2 + n]), token=out[-1], build=build)


def _exchange_wait(handle):
    n = len(handle["bufs"])

    def body(*refs):
        for cp in handle["build"](refs[:n], (refs[n], refs[n + 1])):
            cp.wait_send()
            cp.wait_recv()

    return list(_ordered_call(
        body, name=handle["name"] + "_wait", out_shape=tuple(pltpu.HBM(b.shape, b.dtype) for b in handle["bufs"]),
        in_specs=[HBM_SPEC] * n + [SEM_SPEC, SEM_SPEC], out_specs=(HBM_SPEC,) * n,
        input_output_aliases={i: i for i in range(n)},
        compiler_params=pltpu.CompilerParams(has_side_effects=DATAFLOW),
    )(*handle["bufs"], handle["send"], handle["recv"]))


def _cast_place(name, w, l, me_idx, dtype):
    _, r, c = w.shape
    tr = _tile(r, 512)

    def body(me_ref, w_ref, o_ref):
        o_ref[...] = w_ref[...].astype(dtype)

    grid_spec = pltpu.PrefetchScalarGridSpec(
        num_scalar_prefetch=1, grid=(r // tr,),
        in_specs=[pl.BlockSpec((None, tr, c), lambda i, me_ref: (l, i, 0))],
        out_specs=pl.BlockSpec((None, tr, c), lambda i, me_ref: (me_ref[0], i, 0)))
    return _ordered_call(
        body, name=name, grid_spec=grid_spec, out_shape=jax.ShapeDtypeStruct((N_CHIPS, r, c), dtype),
        compiler_params=_cp("arbitrary"),
    )(me_idx, w)


def _my_half(ref, blk, c):
    hr = ref.shape[1] // 2
    return ref.at[blk, pl.ds(c * hr, hr), :]


def _gather_step(entering):
    lens = [len(e) for e in entering]
    flat = [b for e in entering for b in e]

    def build(refs, sems):
        x, y, c, n1, n2, dg = _route()
        me = _cidx((x, y))
        plan = ([(r, (me,), (*n1, c)) for r in refs[:lens[0]]]
                + [(r, (me, _cidx(n1)), (*n2, c)) for r in refs[lens[0]:lens[0] + lens[1]]]
                + [(r, (_cidx(n1), _cidx(n2), _cidx(dg)), (x, y, 1 - c)) for r in refs[lens[0] + lens[1]:]])
        cps = []
        for ref, blocks, peer in plan:
            for blk in blocks:
                cps.append(_remote(_my_half(ref, blk, c), _my_half(ref, blk, c), sems, len(cps), peer))
        return cps

    return flat, lens[0] + 2 * lens[1] + 3 * lens[2], build


RI_C, RI_ME, RI_N2 = 0, 1, 2


def _pair_sum(name, g, sib, route_idx):
    _, rows, cols = g.shape
    hr = rows // 2
    tr = _tile(hr, 512)
    per = hr // tr

    def body(ri, g_ref, s_ref, o_ref):
        o_ref[...] = (g_ref[...].astype(F32) + s_ref[...].astype(F32)).astype(BF16)

    blk = (None, tr, cols)
    grid_spec = pltpu.PrefetchScalarGridSpec(
        num_scalar_prefetch=1, grid=(N_CHIPS, per),
        in_specs=[pl.BlockSpec(blk, lambda j, i, ri: (j, ri[RI_C] * per + i, 0)),
                  pl.BlockSpec(blk, lambda j, i, ri: (j, i, 0))],
        out_specs=pl.BlockSpec(blk, lambda j, i, ri: (j, i, 0)))
    return _ordered_call(
        body, name=name, grid_spec=grid_spec, out_shape=jax.ShapeDtypeStruct((N_CHIPS, hr, cols), BF16),
        compiler_params=_cp("parallel", "parallel"),
    )(route_idx, g, sib)


def _sum_stage1(name, p, got, route_idx):
    _, hr, cols = p.shape
    tr = _tile(hr, 512)

    def body(ri, pm_ref, pn_ref, g0_ref, g1_ref, keep_ref, send_ref):
        keep_ref[...] = pm_ref[...].astype(F32) + g0_ref[...].astype(F32)
        send_ref[...] = (pn_ref[...].astype(F32) + g1_ref[...].astype(F32)).astype(BF16)

    blk = (None, tr, cols)
    row = pl.BlockSpec((tr, cols), lambda i, ri: (i, 0))
    grid_spec = pltpu.PrefetchScalarGridSpec(
        num_scalar_prefetch=1, grid=(hr // tr,),
        in_specs=[pl.BlockSpec(blk, lambda i, ri: (ri[RI_ME], i, 0)), pl.BlockSpec(blk, lambda i, ri: (ri[RI_N2], i, 0)),
                  pl.BlockSpec(blk, lambda i, ri: (0, i, 0)), pl.BlockSpec(blk, lambda i, ri: (1, i, 0))],
        out_specs=(row, row))
    return _ordered_call(
        body, name=name, grid_spec=grid_spec,
        out_shape=(jax.ShapeDtypeStruct((hr, cols), F32), jax.ShapeDtypeStruct((hr, cols), BF16)),
        compiler_params=_cp("parallel"),
    )(route_idx, p, p, got, got)


def _sum_stage2(name, keep, got):
    hr, cols = keep.shape
    tr = _tile(hr, 512)

    def body(k_ref, g_ref, o_ref):
        o_ref[...] = k_ref[...] + g_ref[...].astype(F32)

    row = pl.BlockSpec((tr, cols), lambda i: (i, 0))
    return _ordered_call(
        body, name=name, grid=(hr // tr,), in_specs=[row, row], out_specs=row,
        out_shape=jax.ShapeDtypeStruct((hr, cols), F32), compiler_params=_cp("parallel"),
    )(keep, got)


def _reduce_scatter(tag, names, grads, route_idx):
    n = len(grads)
    hrs = [g.shape[1] // 2 for g in grads]

    def empty(t, lead, dtype):
        return lax.empty(lead + (hrs[t], grads[t].shape[2]), dtype)

    def pair_stage(refs, sems):
        x, y, c, n1, n2, dg = _route()
        return [_remote(refs[t].at[:, pl.ds((1 - c) * hrs[t], hrs[t]), :], refs[n + t], sems, t, (x, y, 1 - c))
                for t in range(n)]

    def stage1(refs, sems):
        x, y, c, n1, n2, dg = _route()
        return [_remote(refs[t].at[blk], refs[n + t].at[slot], sems, 2 * t + slot, (*n1, c))
                for t in range(n) for slot, blk in enumerate((_cidx(n1), _cidx(dg)))]

    def stage2(refs, sems):
        x, y, c, n1, n2, dg = _route()
        return [_remote(refs[t], refs[n + t], sems, t, (*n2, c)) for t in range(n)]

    def stage3(refs, sems):
        x, y, c, n1, n2, dg = _route()
        return [_remote(refs[t], refs[n + t], sems, t, (x, y, 1 - c)) for t in range(n)]

    state = {}

    def start():
        state["h"] = _exchange_start(f"rs_pair_{tag}", list(grads) + [empty(t, (N_CHIPS,), BF16) for t in range(n)],
                                     n, pair_stage)

    def pair_done():
        out = _exchange_wait(state["h"])
        psum = [_pair_sum(f"rs_psum_{names[t]}", out[t], out[n + t], route_idx) for t in range(n)]
        state["h"] = _exchange_start(f"rs_x1_{tag}", psum + [empty(t, (2,), BF16) for t in range(n)], 2 * n, stage1)

    def x1_done():
        out = _exchange_wait(state["h"])
        state["keep"], send = zip(*[_sum_stage1(f"rs_sum1_{names[t]}", out[t], out[n + t], route_idx)
                                    for t in range(n)])
        state["h"] = _exchange_start(f"rs_x2_{tag}", list(send) + [empty(t, (), BF16) for t in range(n)], n, stage2)

    def x2_done():
        out = _exchange_wait(state["h"])
        mine = [_sum_stage2(f"rs_sum2_{names[t]}", state["keep"][t], out[n + t]) for t in range(n)]
        state["h"] = _exchange_start(f"rs_half_{tag}", mine + [empty(t, (), F32) for t in range(n)], n, stage3)

    def finish():
        out = _exchange_wait(state["h"])
        return list(zip(out[:n], out[n:]))

    return start, pair_done, x1_done, x2_done, finish


def _adamw_big(name, w, m, v, f, h, l, c_idx, prev):
    n_l, r, cols = w.shape
    hr = r // 2
    tr = _tile(hr, 256)
    per = hr // tr

    def body(c_ref, w_ref, m_ref, v_ref, f_ref, h_ref, *rest):
        g_ref, d_ref, nm_ref, nv_ref = rest[-4:]
        gv = jnp.where(pl.program_id(0) == c_ref[0], f_ref[...], h_ref[...])
        mv = ADAM_B1 * m_ref[...] + (1.0 - ADAM_B1) * gv
        vv = ADAM_B2 * v_ref[...] + (1.0 - ADAM_B2) * (gv * gv)
        m_hat = mv / (1.0 - ADAM_B1 ** ADAM_STEP)
        v_hat = vv / (1.0 - ADAM_B2 ** ADAM_STEP)
        g_ref[...] = gv
        d_ref[...] = -ADAM_LR * (m_hat / (jnp.sqrt(v_hat) + ADAM_EPS) + ADAM_WD * w_ref[...])
        nm_ref[...] = mv
        nv_ref[...] = vv

    big = pl.BlockSpec((None, tr, cols), lambda hf, i, c_ref: (l, hf * per + i, 0))
    fspec = pl.BlockSpec((tr, cols), lambda hf, i, c_ref: (jnp.where(hf == c_ref[0], i, 0), 0))
    hspec = pl.BlockSpec((tr, cols), lambda hf, i, c_ref: (jnp.where(hf == c_ref[0], 0, i), 0))
    grid_spec = pltpu.PrefetchScalarGridSpec(
        num_scalar_prefetch=1, grid=(2, per), in_specs=[big] * 3 + [fspec, hspec] + [ANY] * len(prev),
        out_specs=(big,) * 4)
    return _ordered_call(
        body, name=name, grid_spec=grid_spec, out_shape=(jax.ShapeDtypeStruct(w.shape, F32),) * 4,
        input_output_aliases={6 + k: k for k in range(len(prev))}, compiler_params=_cp("arbitrary", "arbitrary"),
    )(c_idx, w, m, v, f, h, *prev)


def _small_allreduce(buf):
    rows = buf.shape[0]
    hr = rows // 2

    def body(in_ref, out_ref, pair, acc, got1, got2, send_sems, recv_sems):
        x, y, c, n1, n2, dg = _route()
        sems = (send_sems, recv_sems)
        sibling = (x, y, 1 - c)
        mine = pl.ds(pl.multiple_of(c * hr, 8), hr)
        pair[c] = in_ref[...]
        cp = _remote(in_ref, pair.at[c], sems, 0, sibling)
        cp.start()
        cp.wait()
        acc[...] = pair[0, mine, :] + pair[1, mine, :]
        cp = _remote(acc, got1, sems, 1, (*n1, c))
        cp.start()
        cp.wait()
        acc[...] = acc[...] + got1[...]
        cp = _remote(acc, got2, sems, 2, (*n2, c))
        cp.start()
        cp.wait()
        out_ref[mine, :] = acc[...] + got2[...]
        cp = _remote(out_ref.at[mine, :], out_ref.at[mine, :], sems, 3, sibling)
        cp.start()
        cp.wait()

    half = pltpu.VMEM((hr, 128), F32)
    return _ordered_call(
        body, name="small_allreduce", in_specs=[pl.BlockSpec(memory_space=pltpu.VMEM)],
        out_specs=pl.BlockSpec(memory_space=pltpu.VMEM), out_shape=jax.ShapeDtypeStruct((rows, 128), F32),
        scratch_shapes=[pltpu.VMEM((2, rows, 128), F32), half, half, half,
                        pltpu.SemaphoreType.DMA((4,)), pltpu.SemaphoreType.DMA((4,))],
        compiler_params=pltpu.CompilerParams(has_side_effects=True, vmem_limit_bytes=VMEM_LIMIT),
    )(buf)


BIG = ("w_in", "w_out", "w_up", "w_down")
COL_SHARDED = {"w_in": True, "w_out": False, "w_up": True, "w_down": False}
SMALL = ("ln1_g", "q_norm_g", "k_norm_g", "sinks", "conv_w", "conv_b", "conv_ln_g", "conv_ln_b", "sgu_ln_g",
         "sgu_ln_b", "sgu_w", "sgu_b", "out_norm_g", "ln2_g")
WEIGHTS = ("ln1_g", "w_in", "q_norm_g", "k_norm_g", "sinks", "conv_w", "conv_b", "conv_ln_g", "conv_ln_b",
           "sgu_ln_g", "sgu_ln_b", "sgu_w", "sgu_b", "out_norm_g", "w_out", "ln2_g", "w_up", "w_down")
PACK_QUANTUM = 8 * 128
PACK_ROWS = 512


def _pack(arrs):
    parts = []
    for a in arrs:
        f = a.reshape(-1)
        parts.append(jnp.pad(f, (0, -f.shape[0] % PACK_QUANTUM)).reshape(-1, 128))
    rows = sum(p.shape[0] for p in parts)
    parts.append(jnp.zeros((-rows % PACK_ROWS, 128), F32))
    return jnp.concatenate(parts, axis=0)


def _unpack(buf, shapes):
    out, off = [], 0
    for shp in shapes:
        n = 1
        for dd in shp:
            n *= dd
        rows = (n + PACK_QUANTUM - 1) // PACK_QUANTUM * 8
        out.append(buf[off:off + rows].reshape(-1)[:n].reshape(shp))
        off += rows
    return out


def _to_heads(t, nh):
    return t.reshape(t.shape[0], nh, HEAD_DIM).transpose(1, 0, 2)


def _from_heads(t):
    return t.transpose(1, 0, 2).reshape(t.shape[1], t.shape[0] * HEAD_DIM)


def _no_hook(point, carry):
    return carry


def _layer_fwd(l, x, p, wg, hook=_no_hook):
    d = x.shape[1]
    aw, cw = d // 2, d // 4
    nq = aw // HEAD_DIM
    nkv = nq // GQA
    kvw = nkv * HEAD_DIM
    x = hook("fwd_start", x)
    h1 = _rms_fwd(f"ln1_fwd_{l}", x, p["ln1_g"])
    proj = _mm_act_w(f"proj_{l}", h1, wg["w_in"], True, _ep_store)[0]
    q = _to_heads(proj[:, :aw], nq)
    k = _to_heads(proj[:, aw:aw + kvw], nkv)
    v = _to_heads(proj[:, aw + kvw:aw + 2 * kvw], nkv)
    sinks_b = jnp.broadcast_to(p["sinks"][:, None, None], (nq, 1, 128))
    ya = _from_heads(_attn_fwd(f"attn_fwd_{l}", q, k, v, p["q_norm_g"], p["k_norm_g"], sinks_b))
    ya = hook("fwd_attn", ya)
    yc = _conv_fwd(f"conv_fwd_{l}", proj, 3, p["conv_w"], p["conv_b"], p["conv_ln_g"], p["conv_ln_b"])
    ys = _sgu_fwd(f"sgu_fwd_{l}", proj, 5, p["sgu_ln_g"], p["sgu_ln_b"], p["sgu_w"], p["sgu_bexp"])
    mix = _mixnorm_fwd(f"mixnorm_fwd_{l}", ya, yc, ys, p["out_norm_g"])
    mix = hook("fwd_mid", mix)
    xm = _mm_act_w(f"out_{l}", mix, wg["w_out"], False, _ep_residual, extra=(x,))[0]
    h2 = _rms_fwd(f"ln2_fwd_{l}", xm, p["ln2_g"])
    up_b, act_b = _mm_act_w(f"up_{l}", h2, wg["w_up"], True, _ep_up, out_dtypes=(BF16, BF16))
    act_b = hook("fwd_up", act_b)
    xo = _mm_act_w(f"down_{l}", act_b, wg["w_down"], False, _ep_residual, extra=(xm,))[0]
    xo = hook("fwd_end", xo)
    saved = dict(x=x, h1=h1, proj=proj, q=q, k=k, v=v, sinks_b=sinks_b, ya=ya, yc=yc, ys=ys, mix=mix, xm=xm, h2=h2,
                 up_b=up_b, act_b=act_b)
    return xo, saved


def _layer_bwd(l, dxo, dxo_b, p, wg, sv, big, hook=_no_hook):
    d = dxo.shape[1]
    nq = (d // 2) // HEAD_DIM
    small = {}
    dxo_b = hook("bwd_start", dxo_b)
    big["w_down"] = _mm_wgrad(f"dw_down_{l}", sv["act_b"], dxo_b, False, d)
    dup_b = _mm_act_wt(f"dup_{l}", dxo_b, wg["w_down"], False, _ep_dup, extra=(sv["up_b"],), out_dtypes=(BF16,))[0]
    dup_b = hook("bwd_dup", dup_b)
    big["w_up"] = _mm_wgrad(f"dw_up_{l}", sv["h2"], dup_b, True, wg["w_up"].shape[2])
    dh2 = _mm_act_wt(f"dh2_{l}", dup_b, wg["w_up"], True, _ep_store)[0]
    dh2 = hook("bwd_dh2", dh2)
    dxm, dxm_b, small["ln2_g"] = _rms_bwd(f"ln2_bwd_{l}", dh2, sv["xm"], p["ln2_g"], dxo)
    big["w_out"] = _mm_wgrad(f"dw_out_{l}", sv["mix"], dxm_b, False, d)
    dmix = _mm_act_wt(f"dmix_{l}", dxm_b, wg["w_out"], False, _ep_store)[0]
    dya, dyc, dys, small["out_norm_g"] = _mixnorm_bwd(f"mixnorm_bwd_{l}", dmix, sv["ya"], sv["yc"], sv["ys"],
                                                      p["out_norm_g"])
    dya = hook("bwd_mix", dya)
    dq, dkc, dkp, dvc, dvp, small["q_norm_g"], dsink = _attn_bwd(
        f"attn_bwd_{l}", sv["q"], sv["k"], sv["v"], p["q_norm_g"], p["k_norm_g"], sv["sinks_b"], _to_heads(dya, nq))
    dkc = hook("bwd_attn", dkc)
    small["sinks"] = dsink[:, 0, 0]
    dk, dv, small["k_norm_g"] = _attn_bwd_kv(f"attn_bwd_kv_{l}", sv["k"], p["k_norm_g"], dkc, dkp, dvc, dvp)
    dc, dcw, small["conv_b"], small["conv_ln_g"], small["conv_ln_b"] = _conv_bwd1(
        f"conv_bwd1_{l}", sv["proj"], 3, p["conv_w"], p["conv_b"], p["conv_ln_g"], p["conv_ln_b"], dyc)
    small["conv_w"] = dcw[:CONV_KERNEL]
    dxc_b = _conv_bwd2(f"conv_bwd2_{l}", sv["proj"], 3, p["conv_w"], dc)
    dxs_b, small["sgu_w"], small["sgu_b"], small["sgu_ln_g"], small["sgu_ln_b"] = _sgu_bwd(
        f"sgu_bwd_{l}", sv["proj"], 5, p["sgu_ln_g"], p["sgu_ln_b"], p["sgu_w"], p["sgu_bexp"], dys)
    dxs_b = hook("bwd_sgu", dxs_b)
    dproj_b = jnp.concatenate([_from_heads(dq).astype(BF16), _from_heads(dk).astype(BF16),
                               _from_heads(dv).astype(BF16), dxc_b, dxs_b], axis=1)
    big["w_in"] = _mm_wgrad(f"dw_in_{l}", sv["h1"], dproj_b, True, wg["w_in"].shape[2])
    dh1 = _mm_act_wt(f"dh1_{l}", dproj_b, wg["w_in"], True, _ep_store)[0]
    dx, dx_b, small["ln1_g"] = _rms_bwd(f"ln1_bwd_{l}", dh1, sv["x"], p["ln1_g"], dxm)
    dx_b = hook("bwd_end", dx_b)
    return dx, dx_b, small


def kernel(x, ln1_g, w_in, q_norm_g, k_norm_g, sinks, conv_w, conv_b, conv_ln_g, conv_ln_b, sgu_ln_g, sgu_ln_b, sgu_w, sgu_b, out_norm_g, w_out, ln2_g, w_up, w_down, loss_target, m_ln1_g, m_w_in, m_q_norm_g, m_k_norm_g, m_sinks, m_conv_w, m_conv_b, m_conv_ln_g, m_conv_ln_b, m_sgu_ln_g, m_sgu_ln_b, m_sgu_w, m_sgu_b, m_out_norm_g, m_w_out, m_ln2_g, m_w_up, m_w_down, v_ln1_g, v_w_in, v_q_norm_g, v_k_norm_g, v_sinks, v_conv_w, v_conv_b, v_conv_ln_g, v_conv_ln_b, v_sgu_ln_g, v_sgu_ln_b, v_sgu_w, v_sgu_b, v_out_norm_g, v_w_out, v_ln2_g, v_w_up, v_w_down):
    given = dict(locals())
    _LAST[0] = None
    n_layers = ln1_g.shape[0]
    s, d = x.shape[1], x.shape[2]
    cw = d // 4
    xi, yi, core = lax.axis_index("x"), lax.axis_index("y"), lax.axis_index("c")
    chip = 2 * xi + yi
    second = jnp.where(core == 0, 2 * xi + (1 - yi), 2 * (1 - xi) + yi)
    route_idx = jnp.stack([core, chip, second]).astype(jnp.int32)

    conv_w_pad = jnp.pad(conv_w, ((0, 0), (0, HALO - CONV_KERNEL), (0, 0))).reshape(1, n_layers * HALO, -1)
    cwl = conv_w_pad.shape[2]
    buf = {(nm, l): _cast_place(f"place_{nm}_{l}", given[nm], l, route_idx[1:2], BF16)
           for l in range(n_layers) for nm in BIG}
    buf["conv_w"] = _cast_place("place_conv_w", conv_w_pad, 0, route_idx[1:2], F32)
    groups = [["conv_w", ("w_in", 0)]] + [[(nm, l)] for l in range(n_layers) for nm in BIG if (nm, l) != ("w_in", 0)]
    n_steps = len(groups) + 2

    def step_parts(st):
        keys = [groups[st - j] if 0 <= st - j < len(groups) else [] for j in range(3)]
        flat, n_sems, build = _gather_step([[buf[k] for k in ks] for ks in keys])
        return [k for ks in keys for k in ks], flat, n_sems, build

    pending = {}

    def start_step(st):
        keys, flat, n_sems, build = step_parts(st)
        pending["keys"], pending["h"] = keys, _exchange_start(f"gather_step{st}", flat, n_sems, build)

    def wait_step():
        for k, b in zip(pending["keys"], _exchange_wait(pending["h"])):
            buf[k] = b

    for st in range(3):
        keys, flat, n_sems, build = step_parts(st)
        for k, b in zip(keys, _exchange(f"gather_step{st}", flat, n_sems, build)):
            buf[k] = b
    conv_w_full = buf["conv_w"].reshape(N_CHIPS, n_layers, HALO, cwl).transpose(1, 2, 0, 3).reshape(
        n_layers, HALO, cw)

    class LayerWeights:
        def __init__(self, l):
            self.l = l

        def __getitem__(self, nm):
            return buf[(nm, self.l)]

    wgs = [LayerWeights(l) for l in range(n_layers)]
    fwd_points = [(l, pt) for l in range(n_layers) for pt in ("fwd_start", "fwd_attn", "fwd_mid", "fwd_up", "fwd_end")
                  if (pt != "fwd_start" or l == 0) and (pt != "fwd_end" or l + 1 < n_layers)]
    assert len(fwd_points) == n_steps - 3 + 1, "one hook point per pipeline step, and one to wait for the last"
    fwd_tables = [{} for _ in range(n_layers)]
    for i, (l, pt) in enumerate(fwd_points):
        if i > 0:
            fwd_tables[l].setdefault(pt, []).append(wait_step)
        if 3 + i < n_steps:
            fwd_tables[l].setdefault(pt, []).append(functools.partial(start_step, 3 + i))
    params = []
    for l in range(n_layers):
        p = {nm: given[nm][l] for nm in SMALL if nm != "conv_w"}
        for nm in ("conv_b", "conv_ln_g", "conv_ln_b", "sgu_ln_g", "sgu_ln_b"):
            p[nm] = p[nm].reshape(1, -1)
        p["conv_w"] = conv_w_full[l]
        p["sgu_bexp"] = jnp.repeat(sgu_b[l].T, HEAD_DIM, axis=1)
        params.append(p)

    def make_hook(table):
        def hook(point, carry):
            for fn in table.get(point, ()):
                fn()
            return carry
        return hook

    h = x.reshape(s, d)
    saved = []
    for l in range(n_layers):
        h, sv = _layer_fwd(l, h, params[l], wgs[l], make_hook(fwd_tables[l]))
        saved.append(sv)
    dh, dh_b, loss_part = _loss_head(h, loss_target.reshape(s, d))
    loss = lax.psum(loss_part[0, 0], ("x", "y", "c"))

    big_grads = [{} for _ in range(n_layers)]
    small_grads = [None] * n_layers
    halves = {}

    def rs_group(tag, l, names):
        phases = {}

        def start():
            phases["p"] = _reduce_scatter(tag, [f"{nm}_{l}" for nm in names], [big_grads[l][nm] for nm in names],
                                          route_idx)
            phases["p"][0]()

        def step(k):
            return lambda: phases["p"][k]()

        def finish():
            for nm, fh in zip(names, phases["p"][4]()):
                halves[(nm, l)] = fh

        return [start, step(1), step(2), step(3), finish]

    early = rs_group("l0a", 0, ("w_down", "w_up", "w_out"))
    for l in reversed(range(n_layers)):
        table = {}
        if l + 1 < n_layers:
            above = rs_group(f"l{l + 1}", l + 1, BIG)
            for point, fn in zip(("bwd_start", "bwd_dup", "bwd_mix", "bwd_attn", "bwd_sgu"), above):
                table.setdefault(point, []).append(fn)
        if l == 0:
            for point, fn in zip(("bwd_mix", "bwd_attn", "bwd_end"), early[:3]):
                table.setdefault(point, []).append(fn)
        dh, dh_b, small_grads[l] = _layer_bwd(l, dh, dh_b, params[l], wgs[l], saved[l], big_grads[l],
                                              make_hook(table))
    grad_x = dh.reshape(x.shape)

    grads, delta, new_m, new_v = {}, {}, {}, {}
    adam_state = {nm: () for nm in BIG}

    def adam(nm, l):
        f, h = halves[(nm, l)]
        adam_state[nm] = _adamw_big(f"adamw_{nm}_{l}", given[nm], given["m_" + nm], given["v_" + nm], f, h, l,
                                    route_idx[0:1], adam_state[nm])

    def small_update():
        small_shapes = [(n_layers,) + small_grads[0][nm].shape for nm in SMALL]
        small_sum = _small_allreduce(_pack([jnp.stack([small_grads[l][nm] for l in range(n_layers)])
                                            for nm in SMALL]))
        for nm, g in zip(SMALL, _unpack(small_sum, small_shapes)):
            grads[nm] = g.reshape((n_layers,) + given[nm].shape[1:]) if nm != "conv_w" else g
        grads["conv_w"] = lax.dynamic_slice_in_dim(grads["conv_w"], chip * cwl, cwl, axis=2)
        packed = [_pack([src[nm] for nm in SMALL]) for src in
                  ({nm: given[nm] for nm in SMALL}, grads, {nm: given["m_" + nm] for nm in SMALL},
                   {nm: given["v_" + nm] for nm in SMALL})]
        local_shapes = [given[nm].shape for nm in SMALL]
        for dst, buf in zip((delta, new_m, new_v), _adamw("adamw_small", *packed)):
            for nm, a in zip(SMALL, _unpack(buf, local_shapes)):
                dst[nm] = a

    upper = [(nm, l) for l in reversed(range(1, n_layers)) for nm in reversed(BIG)]
    late = rs_group("l0b", 0, ("w_in",))
    late[0]()
    for task in upper[:1]:
        adam(*task)
    late[1]()
    for task in upper[1:]:
        adam(*task)
    early[3]()
    late[2]()
    small_update()
    early[4]()
    for nm in ("w_down", "w_up"):
        adam(nm, 0)
    late[3]()
    adam("w_out", 0)
    late[4]()
    adam("w_in", 0)
    for nm in BIG:
        grads[nm], delta[nm], new_m[nm], new_v[nm] = adam_state[nm]
    return (loss, grad_x, *[grads[nm] for nm in WEIGHTS], *[delta[nm] for nm in WEIGHTS],
            *[new_m[nm] for nm in WEIGHTS], *[new_v[nm] for nm in WEIGHTS])
```

```python
import functools

import jax
import jax.numpy as jnp
from jax import lax
from jax.experimental import pallas as pl
from jax.experimental.pallas import tpu as pltpu

F32 = jnp.float32
BF16 = jnp.bfloat16
EPS = 1e-6
NEG_INF = -1e30
HEAD_DIM = 64
WINDOW = 128
CONV_KERNEL = 31
HALO = 32
GQA = 4
N_CHIPS = 4
ADAM_LR, ADAM_B1, ADAM_B2, ADAM_EPS, ADAM_WD, ADAM_STEP = 0.001, 0.9, 0.999, 1e-08, 0.01, 10
VMEM_LIMIT = 56 * 1024 * 1024
TILE_K = 2048
MESH = pl.DeviceIdType.MESH
ANY = pl.BlockSpec(memory_space=pl.ANY)

NN = (((1,), (0,)), ((), ()))
NT = (((1,), (1,)), ((), ()))
TN = (((0,), (0,)), ((), ()))


def _cp(*sem):
    return pltpu.CompilerParams(dimension_semantics=sem, vmem_limit_bytes=VMEM_LIMIT)


_LAST = [None]


def _ordered_call(body, *, in_specs=None, grid_spec=None, **kw):
    def run(*operands):
        dep = _LAST[0]
        n = len(operands)
        if dep is None or any(dep is o for o in operands):
            fn, specs, spec, args = body, in_specs, grid_spec, operands
        else:
            def fn(*refs):
                body(*refs[:n], *refs[n + 1:])

            specs, spec, args = in_specs, grid_spec, operands + (dep,)
            if grid_spec is not None:
                spec = pltpu.PrefetchScalarGridSpec(
                    num_scalar_prefetch=grid_spec.num_scalar_prefetch, grid=grid_spec.grid,
                    in_specs=list(grid_spec.in_specs) + [ANY], out_specs=grid_spec.out_specs)
            else:
                specs = list(in_specs) + [ANY]
        if spec is not None:
            out = pl.pallas_call(fn, grid_spec=spec, **kw)(*args)
        else:
            out = pl.pallas_call(fn, in_specs=specs, **kw)(*args)
        arrays = [o for o in (out if isinstance(out, (tuple, list)) else (out,)) if o.dtype in (F32, BF16)]
        _LAST[0] = min(arrays, key=lambda o: o.size)
        return out

    return run


def _tile(dim, pref):
    if dim <= pref:
        return dim
    for t in range(pref, 0, -128):
        if dim % t == 0:
            return t
    while dim % pref:
        pref //= 2
    return pref


def _dot(a, b, dims=NN):
    return lax.dot_general(a, b, dims, preferred_element_type=F32)


def _colsum(v):
    return jnp.sum(v, axis=0, keepdims=True)


def _sigmoid(x):
    return 1.0 / (1.0 + jnp.exp(-x))


def _matmul(name, operands, in_specs, out_shape, out_specs, grid, dims, acc_shape, epilogue, split_k=False):
    nk = grid[2]
    n_in = len(operands)

    def product(a_ref, b_ref):
        if split_k:
            ck = b_ref.shape[2]
            out = _dot(a_ref[:, 0:ck], b_ref[0], dims)
            for j in range(1, N_CHIPS):
                out = out + _dot(a_ref[:, j * ck:(j + 1) * ck], b_ref[j], dims)
            return out
        bv = b_ref[...]
        return _dot(a_ref[...], bv.reshape(-1, bv.shape[-1]) if bv.ndim == 3 else bv, dims)

    def body(*refs):
        a_ref, b_ref = refs[0], refs[1]
        extra = refs[2:n_in]
        if nk == 1:
            epilogue(product(a_ref, b_ref), extra, refs[n_in:])
            return
        outs = refs[n_in:-1]
        acc = refs[-1]
        k = pl.program_id(2)

        @pl.when(k == 0)
        def _():
            acc[...] = product(a_ref, b_ref)

        @pl.when((k > 0) & (k < nk - 1))
        def _():
            acc[...] += product(a_ref, b_ref)

        @pl.when(k == nk - 1)
        def _():
            epilogue(acc[...] + product(a_ref, b_ref), extra, outs)

    return _ordered_call(
        body, name=name, grid=grid, in_specs=in_specs, out_specs=out_specs, out_shape=out_shape,
        scratch_shapes=[pltpu.VMEM(acc_shape, F32)] if nk > 1 else [],
        compiler_params=_cp("parallel", "parallel", "arbitrary"),
    )(*operands)


def _ep_store(acc, extra, outs):
    outs[0][...] = acc.astype(outs[0].dtype)


def _ep_residual(acc, extra, outs):
    outs[0][...] = extra[0][...] + acc


def _ep_up(acc, extra, outs):
    outs[0][...] = acc.astype(BF16)
    r = jnp.maximum(acc, 0.0)
    outs[1][...] = (r * r).astype(BF16)


def _ep_dup(acc, extra, outs):
    outs[0][...] = (acc * (2.0 * jnp.maximum(extra[0][...].astype(F32), 0.0))).astype(BF16)


def _mm_act_w(name, a, wg, col_sharded, epilogue, extra=(), out_dtypes=(F32,)):
    m, kdim = a.shape
    _, r, c = wg.shape
    tm = _tile(m, 1024)
    if col_sharded:
        n = N_CHIPS * c
        tn = _tile(c, 1024)
        tk = _tile(kdim, TILE_K)
        per = c // tn
        b_spec = pl.BlockSpec((None, tk, tn), lambda i, j, k: (j // per, k, j % per))
    elif N_CHIPS * r <= TILE_K:
        n = c
        tm, tn, tk = _tile(m, 512), n, kdim
        b_spec = pl.BlockSpec((N_CHIPS, r, tn), lambda i, j, k: (0, 0, j))
    else:
        n = c
        tn = _tile(n, 1024)
        tk = _tile(r, TILE_K)
        per = r // tk
        b_spec = pl.BlockSpec((None, tk, tn), lambda i, j, k: (k // per, k % per, j))
    grid = (m // tm, n // tn, kdim // tk)
    o_spec = pl.BlockSpec((tm, tn), lambda i, j, k: (i, j))
    in_specs = [pl.BlockSpec((tm, tk), lambda i, j, k: (i, k)), b_spec] + [o_spec] * len(extra)
    return _matmul(name, (a, wg) + tuple(extra), in_specs,
                   tuple(jax.ShapeDtypeStruct((m, n), d) for d in out_dtypes),
                   tuple(o_spec for _ in out_dtypes), grid, NN, (tm, tn), epilogue)


def _mm_act_wt(name, a, wg, col_sharded, epilogue, extra=(), out_dtypes=(F32,)):
    m, kdim = a.shape
    _, r, c = wg.shape
    tm = _tile(m, 1024)
    split_k = False
    if col_sharded and N_CHIPS * c <= 2 * TILE_K:
        n = r
        tm, tn, tk, split_k = _tile(m, 512), n, kdim, True
        b_spec = pl.BlockSpec((N_CHIPS, tn, c), lambda i, j, k: (0, j, 0))
    elif col_sharded:
        n = r
        tn = _tile(n, 1024)
        tk = _tile(c, TILE_K)
        per = c // tk
        b_spec = pl.BlockSpec((None, tn, tk), lambda i, j, k: (k // per, j, k % per))
    elif N_CHIPS * r <= TILE_K:
        n = N_CHIPS * r
        tm, tn, tk = _tile(m, 512), n, _tile(c, TILE_K)
        b_spec = pl.BlockSpec((N_CHIPS, r, tk), lambda i, j, k: (0, 0, k))
    else:
        n = N_CHIPS * r
        tn = _tile(r, 1024)
        tk = _tile(c, TILE_K)
        per = r // tn
        b_spec = pl.BlockSpec((None, tn, tk), lambda i, j, k: (j // per, j % per, k))
    grid = (m // tm, n // tn, kdim // tk)
    o_spec = pl.BlockSpec((tm, tn), lambda i, j, k: (i, j))
    in_specs = [pl.BlockSpec((tm, tk), lambda i, j, k: (i, k)), b_spec] + [o_spec] * len(extra)
    return _matmul(name, (a, wg) + tuple(extra), in_specs,
                   tuple(jax.ShapeDtypeStruct((m, n), d) for d in out_dtypes),
                   tuple(o_spec for _ in out_dtypes), grid, NT, (tm, tn), epilogue, split_k)


def _mm_wgrad(name, a, g, col_sharded, c):
    s, kdim = a.shape
    _, n = g.shape
    ts = _tile(s, TILE_K)
    if col_sharded:
        r = kdim
        tm = _tile(kdim, 1024)
        tn = _tile(c, 1024)
        per = c // tn
        o_spec = pl.BlockSpec((None, tm, tn), lambda i, j, k: (j // per, i, j % per))
    else:
        r = kdim // N_CHIPS
        tm = _tile(r, 512)
        tn = _tile(c, 2048)
        per = r // tm
        o_spec = pl.BlockSpec((None, tm, tn), lambda i, j, k: (i // per, i % per, j))
    grid = (kdim // tm, n // tn, s // ts)
    in_specs = [pl.BlockSpec((ts, tm), lambda i, j, k: (k, i)), pl.BlockSpec((ts, tn), lambda i, j, k: (k, j))]
    return _matmul(name, (a, g), in_specs, (jax.ShapeDtypeStruct((N_CHIPS, r, c), BF16),), (o_spec,),
                   grid, TN, (tm, tn), _ep_store)[0]


def _rms_fwd(name, x, g):
    s, d = x.shape
    tb = _tile(s, 256)

    def body(x_ref, g_ref, o_ref):
        xv = x_ref[...]
        r = lax.rsqrt(jnp.mean(xv * xv, axis=-1, keepdims=True) + EPS)
        o_ref[...] = (xv * r * g_ref[...]).astype(BF16)

    return _ordered_call(
        body, name=name, grid=(s // tb,),
        in_specs=[pl.BlockSpec((tb, d), lambda i: (i, 0)), pl.BlockSpec((1, d), lambda i: (0, 0))],
        out_specs=pl.BlockSpec((tb, d), lambda i: (i, 0)),
        out_shape=jax.ShapeDtypeStruct((s, d), BF16), compiler_params=_cp("parallel"),
    )(x, g.reshape(1, d))


def _rms_bwd(name, dh, x, g, dres):
    s, d = x.shape
    tb = _tile(s, 256)

    def body(dh_ref, x_ref, g_ref, dres_ref, dx_ref, dxb_ref, dg_ref):
        i = pl.program_id(0)
        xv = x_ref[...]
        r = lax.rsqrt(jnp.mean(xv * xv, axis=-1, keepdims=True) + EPS)
        xhat = xv * r
        dhv = dh_ref[...]
        dxhat = dhv * g_ref[...]
        dx = dres_ref[...] + r * (dxhat - xhat * jnp.mean(dxhat * xhat, axis=-1, keepdims=True))
        dx_ref[...] = dx
        dxb_ref[...] = dx.astype(BF16)

        @pl.when(i == 0)
        def _():
            dg_ref[...] = jnp.zeros_like(dg_ref)

        dg_ref[...] += _colsum(dhv * xhat)

    row = pl.BlockSpec((tb, d), lambda i: (i, 0))
    vec = pl.BlockSpec((1, d), lambda i: (0, 0))
    return _ordered_call(
        body, name=name, grid=(s // tb,), in_specs=[row, row, vec, row], out_specs=(row, row, vec),
        out_shape=(jax.ShapeDtypeStruct((s, d), F32), jax.ShapeDtypeStruct((s, d), BF16),
                   jax.ShapeDtypeStruct((1, d), F32)),
        compiler_params=_cp("arbitrary"),
    )(dh, x, g.reshape(1, d), dres)


def _loss_head(y, t):
    s, d = y.shape
    tb = _tile(s, 256)

    def body(y_ref, t_ref, dy_ref, dyb_ref, loss_ref, acc):
        i = pl.program_id(0)
        e = y_ref[...] - t_ref[...]
        dy = e * (1.0 / d)
        dy_ref[...] = dy
        dyb_ref[...] = dy.astype(BF16)

        @pl.when(i == 0)
        def _():
            acc[...] = jnp.zeros_like(acc)

        acc[...] += _colsum(e * e)

        @pl.when(i == pl.num_programs(0) - 1)
        def _():
            loss_ref[...] = jnp.sum(acc[...], axis=-1, keepdims=True) * (0.5 / d)

    row = pl.BlockSpec((tb, d), lambda i: (i, 0))
    return _ordered_call(
        body, name="loss_head", grid=(s // tb,), in_specs=[row, row],
        out_specs=(row, row, pl.BlockSpec((1, 1), lambda i: (0, 0))),
        out_shape=(jax.ShapeDtypeStruct((s, d), F32), jax.ShapeDtypeStruct((s, d), BF16),
                   jax.ShapeDtypeStruct((1, 1), F32)),
        scratch_shapes=[pltpu.VMEM((1, d), F32)], compiler_params=_cp("arbitrary"),
    )(y, t)


def _mixnorm_fwd(name, ya, yc, ys, g):
    s, aw = ya.shape
    cw, sw = yc.shape[1], ys.shape[1]
    d = aw + cw + sw
    tb = _tile(s, 256)

    def body(ya_ref, yc_ref, ys_ref, g_ref, o_ref):
        off = 0
        for ref, w in ((ya_ref, aw), (yc_ref, cw), (ys_ref, sw)):
            v = ref[...]
            r = lax.rsqrt(jnp.mean(v * v, axis=-1, keepdims=True) + EPS)
            o_ref[:, off:off + w] = (v * r * g_ref[:, off:off + w]).astype(BF16)
            off += w

    def row(w):
        return pl.BlockSpec((tb, w), lambda i: (i, 0))

    return _ordered_call(
        body, name=name, grid=(s // tb,),
        in_specs=[row(aw), row(cw), row(sw), pl.BlockSpec((1, d), lambda i: (0, 0))], out_specs=row(d),
        out_shape=jax.ShapeDtypeStruct((s, d), BF16), compiler_params=_cp("parallel"),
    )(ya, yc, ys, g.reshape(1, d))


def _mixnorm_bwd(name, dmix, ya, yc, ys, g):
    s, aw = ya.shape
    cw, sw = yc.shape[1], ys.shape[1]
    d = aw + cw + sw
    tb = _tile(s, 256)

    def body(dm_ref, ya_ref, yc_ref, ys_ref, g_ref, dya_ref, dyc_ref, dys_ref, dg_ref):
        i = pl.program_id(0)

        @pl.when(i == 0)
        def _():
            dg_ref[...] = jnp.zeros_like(dg_ref)

        off = 0
        for ref, dref, w in ((ya_ref, dya_ref, aw), (yc_ref, dyc_ref, cw), (ys_ref, dys_ref, sw)):
            v = ref[...]
            r = lax.rsqrt(jnp.mean(v * v, axis=-1, keepdims=True) + EPS)
            vhat = v * r
            dm = dm_ref[:, off:off + w]
            dvhat = dm * g_ref[:, off:off + w]
            dref[...] = r * (dvhat - vhat * jnp.mean(dvhat * vhat, axis=-1, keepdims=True))
            dg_ref[:, off:off + w] += _colsum(dm * vhat)
            off += w

    def row(w):
        return pl.BlockSpec((tb, w), lambda i: (i, 0))

    vec = pl.BlockSpec((1, d), lambda i: (0, 0))
    return _ordered_call(
        body, name=name, grid=(s // tb,), in_specs=[row(d), row(aw), row(cw), row(sw), vec],
        out_specs=(row(aw), row(cw), row(sw), vec),
        out_shape=(jax.ShapeDtypeStruct((s, aw), F32), jax.ShapeDtypeStruct((s, cw), F32),
                   jax.ShapeDtypeStruct((s, sw), F32), jax.ShapeDtypeStruct((1, d), F32)),
        compiler_params=_cp("arbitrary"),
    )(dmix, ya, yc, ys, g.reshape(1, d))


def _head_rms(x):
    r = lax.rsqrt(jnp.mean(x * x, axis=-1, keepdims=True) + EPS)
    return x * r, r


def _attn_mask(n):
    qi = lax.broadcasted_iota(jnp.int32, (GQA * WINDOW, 2 * WINDOW), 0) & (WINDOW - 1)
    sj = lax.broadcasted_iota(jnp.int32, (GQA * WINDOW, 2 * WINDOW), 1)
    rel = qi + WINDOW - sj
    return (rel >= 0) & (rel < WINDOW) & ((sj >= WINDOW) | (n > 0))


def _attn_specs(nq, nkv, nb):
    qspec = pl.BlockSpec((nq, WINDOW, HEAD_DIM), lambda n: (0, n, 0))
    cur = pl.BlockSpec((nkv, WINDOW, HEAD_DIM), lambda n: (0, n, 0))
    prev = pl.BlockSpec((nkv, WINDOW, HEAD_DIM), lambda n: (0, jnp.maximum(n - 1, 0), 0))
    nxt = pl.BlockSpec((nkv, WINDOW, HEAD_DIM), lambda n: (0, jnp.minimum(n + 1, nb - 1), 0))
    gain = pl.BlockSpec((1, HEAD_DIM), lambda n: (0, 0))
    sink = pl.BlockSpec((nq, 1, 128), lambda n: (0, 0, 0))
    return qspec, cur, prev, nxt, gain, sink


def _group_sinks(s_ref, g):
    return jnp.concatenate([jnp.broadcast_to(s_ref[g * GQA + i][:, :1], (WINDOW, 1)) for i in range(GQA)], axis=0)


def _attn_probs(qn_b, kn_b, valid, sink):
    logits = _dot(qn_b, kn_b, NT) * (HEAD_DIM ** -0.5)
    logits = jnp.where(valid, logits, NEG_INF)
    m = jnp.maximum(jnp.max(logits, axis=-1, keepdims=True), sink)
    p = jnp.exp(logits - m)
    es = jnp.exp(sink - m)
    denom = jnp.sum(p, axis=-1, keepdims=True) + es
    return p / denom, es / denom


def _attn_fwd(name, q, k, v, gq, gk, sinks_b):
    nq, s, _ = q.shape
    nkv = k.shape[0]
    nb = s // WINDOW
    qspec, cur, prev, _, gain, sink = _attn_specs(nq, nkv, nb)

    def body(q_ref, kc_ref, kp_ref, vc_ref, vp_ref, gq_ref, gk_ref, s_ref, o_ref):
        gkv = gk_ref[...]
        valid = _attn_mask(pl.program_id(0))
        for g in range(nkv):
            kn_b = jnp.concatenate([_head_rms(kp_ref[g])[0] * gkv, _head_rms(kc_ref[g])[0] * gkv],
                                   axis=0).astype(BF16)
            vv_b = jnp.concatenate([vp_ref[g], vc_ref[g]], axis=0).astype(BF16)
            heads = pl.ds(g * GQA, GQA)
            q4 = q_ref[heads].reshape(GQA * WINDOW, HEAD_DIM)
            qn_b = (_head_rms(q4)[0] * gq_ref[...]).astype(BF16)
            probs, _ = _attn_probs(qn_b, kn_b, valid, _group_sinks(s_ref, g))
            o_ref[heads] = _dot(probs.astype(BF16), vv_b).reshape(GQA, WINDOW, HEAD_DIM)

    return _ordered_call(
        body, name=name, grid=(nb,), in_specs=[qspec, cur, prev, cur, prev, gain, gain, sink], out_specs=qspec,
        out_shape=jax.ShapeDtypeStruct((nq, s, HEAD_DIM), F32), compiler_params=_cp("parallel"),
    )(q, k, k, v, v, gq.reshape(1, HEAD_DIM), gk.reshape(1, HEAD_DIM), sinks_b)


def _attn_bwd(name, q, k, v, gq, gk, sinks_b, do):
    nq, s, _ = q.shape
    nkv = k.shape[0]
    nb = s // WINDOW
    qspec, cur, prev, _, gain, sink = _attn_specs(nq, nkv, nb)

    def body(q_ref, kc_ref, kp_ref, vc_ref, vp_ref, gq_ref, gk_ref, s_ref, do_ref,
             dq_ref, dkc_ref, dkp_ref, dvc_ref, dvp_ref, dgq_ref, ds_ref):
        n = pl.program_id(0)

        @pl.when(n == 0)
        def _():
            dgq_ref[...] = jnp.zeros_like(dgq_ref)
            ds_ref[...] = jnp.zeros_like(ds_ref)

        gkv = gk_ref[...]
        gqv = gq_ref[...]
        valid = _attn_mask(n)
        dgq = jnp.zeros((1, HEAD_DIM), F32)
        for g in range(nkv):
            kn_b = jnp.concatenate([_head_rms(kp_ref[g])[0] * gkv, _head_rms(kc_ref[g])[0] * gkv],
                                   axis=0).astype(BF16)
            vv_b = jnp.concatenate([vp_ref[g], vc_ref[g]], axis=0).astype(BF16)
            heads = pl.ds(g * GQA, GQA)
            qhat, r = _head_rms(q_ref[heads].reshape(GQA * WINDOW, HEAD_DIM))
            qn_b = (qhat * gqv).astype(BF16)
            probs, psink = _attn_probs(qn_b, kn_b, valid, _group_sinks(s_ref, g))
            do_b = do_ref[heads].reshape(GQA * WINDOW, HEAD_DIM).astype(BF16)
            dp = _dot(do_b, vv_b, NT)
            delta = jnp.sum(probs * dp, axis=-1, keepdims=True)
            dl_b = (probs * (dp - delta) * (HEAD_DIM ** -0.5)).astype(BF16)
            sink_term = psink * delta
            for i in range(GQA):
                ds_ref[g * GQA + i] += jnp.broadcast_to(
                    -jnp.sum(sink_term[i * WINDOW:(i + 1) * WINDOW], axis=0, keepdims=True), (1, 128))
            dqn = _dot(dl_b, kn_b)
            dkn = _dot(dl_b, qn_b, TN)
            dvv = _dot(probs.astype(BF16), do_b, TN)
            dgq += _colsum(dqn * qhat)
            dqhat = dqn * gqv
            dq_ref[heads] = (r * (dqhat - qhat * jnp.mean(dqhat * qhat, axis=-1, keepdims=True))).reshape(
                GQA, WINDOW, HEAD_DIM)
            dkp_ref[g] = dkn[:WINDOW]
            dkc_ref[g] = dkn[WINDOW:]
            dvp_ref[g] = dvv[:WINDOW]
            dvc_ref[g] = dvv[WINDOW:]
        dgq_ref[...] += dgq

    kv_shape = jax.ShapeDtypeStruct((nkv, s, HEAD_DIM), F32)
    return _ordered_call(
        body, name=name, grid=(nb,), in_specs=[qspec, cur, prev, cur, prev, gain, gain, sink, qspec],
        out_specs=(qspec, cur, cur, cur, cur, gain, sink),
        out_shape=(jax.ShapeDtypeStruct((nq, s, HEAD_DIM), F32), kv_shape, kv_shape, kv_shape, kv_shape,
                   jax.ShapeDtypeStruct((1, HEAD_DIM), F32), jax.ShapeDtypeStruct((nq, 1, 128), F32)),
        compiler_params=_cp("arbitrary"),
    )(q, k, k, v, v, gq.reshape(1, HEAD_DIM), gk.reshape(1, HEAD_DIM), sinks_b, do)


def _attn_bwd_kv(name, k, gk, dkc, dkp, dvc, dvp):
    nkv, s, _ = k.shape
    nb = s // WINDOW
    _, cur, _, nxt, gain, _ = _attn_specs(GQA * nkv, nkv, nb)

    def body(k_ref, gk_ref, dkc_ref, dkp_ref, dvc_ref, dvp_ref, dk_ref, dv_ref, dgk_ref):
        n = pl.program_id(0)

        @pl.when(n == 0)
        def _():
            dgk_ref[...] = jnp.zeros_like(dgk_ref)

        has_next = n < nb - 1
        dgk = jnp.zeros((1, HEAD_DIM), F32)
        for g in range(nkv):
            dkn = dkc_ref[g] + jnp.where(has_next, dkp_ref[g], 0.0)
            dv_ref[g] = dvc_ref[g] + jnp.where(has_next, dvp_ref[g], 0.0)
            khat, r = _head_rms(k_ref[g])
            dgk += _colsum(dkn * khat)
            dkhat = dkn * gk_ref[...]
            dk_ref[g] = r * (dkhat - khat * jnp.mean(dkhat * khat, axis=-1, keepdims=True))
        dgk_ref[...] += dgk

    kv_shape = jax.ShapeDtypeStruct((nkv, s, HEAD_DIM), F32)
    return _ordered_call(
        body, name=name, grid=(nb,), in_specs=[cur, gain, cur, nxt, cur, nxt], out_specs=(cur, cur, gain),
        out_shape=(kv_shape, kv_shape, jax.ShapeDtypeStruct((1, HEAD_DIM), F32)),
        compiler_params=_cp("arbitrary"),
    )(k, gk.reshape(1, HEAD_DIM), dkc, dkp, dvc, dvp)


SUBLANES = 8


def _fill_shifts(buf, shifts, tb):
    rows = tb + HALO - SUBLANES
    for b in range(1, SUBLANES):
        shifts[b - 1, pl.ds(0, rows), :] = buf[pl.ds(b, rows), :]


def _window(buf, shifts, off, tb):
    b = off % SUBLANES
    return buf[pl.ds(off, tb), :] if b == 0 else shifts[b - 1, pl.ds(off - b, tb), :]


def _conv_recompute(i, a_ref, gt_ref, ap_ref, gp_ref, w_ref, b_ref, hbuf, shifts, tb):
    hbuf[pl.ds(HALO, tb), :] = a_ref[...] * _sigmoid(gt_ref[...])
    tail = ap_ref[pl.ds(tb - HALO, HALO), :] * _sigmoid(gp_ref[pl.ds(tb - HALO, HALO), :])
    hbuf[pl.ds(0, HALO), :] = jnp.where(i > 0, tail, 0.0)
    _fill_shifts(hbuf, shifts, tb)
    acc = jnp.broadcast_to(b_ref[...], a_ref.shape)
    for kk in range(CONV_KERNEL):
        acc = acc + w_ref[pl.ds(kk, 1), :] * _window(hbuf, shifts, HALO - (CONV_KERNEL - 1) + kk, tb)
    return acc


def _layer_norm_stats(c):
    mu = jnp.mean(c, axis=-1, keepdims=True)
    xc = c - mu
    r = lax.rsqrt(jnp.mean(xc * xc, axis=-1, keepdims=True) + EPS)
    return xc * r, r


def _conv_specs(s, cw, tb, a_blk):
    cur = lambda off: pl.BlockSpec((tb, cw), lambda i: (i, a_blk + off))
    prev = lambda off: pl.BlockSpec((tb, cw), lambda i: (jnp.maximum(i - 1, 0), a_blk + off))
    wspec = pl.BlockSpec((HALO, cw), lambda i: (0, 0))
    vec = pl.BlockSpec((1, cw), lambda i: (0, 0))
    row = pl.BlockSpec((tb, cw), lambda i: (i, 0))
    return cur, prev, wspec, vec, row


def _conv_fwd(name, proj, a_blk, w, b, lg, lb):
    s = proj.shape[0]
    cw = w.shape[1]
    tb = _tile(s, 256)
    cur, prev, wspec, vec, row = _conv_specs(s, cw, tb, a_blk)

    def body(a_ref, gt_ref, ap_ref, gp_ref, w_ref, b_ref, lg_ref, lb_ref, y_ref, hbuf, shifts):
        c = _conv_recompute(pl.program_id(0), a_ref, gt_ref, ap_ref, gp_ref, w_ref, b_ref, hbuf, shifts, tb)
        chat, _ = _layer_norm_stats(c)
        z = chat * lg_ref[...] + lb_ref[...]
        y_ref[...] = z * _sigmoid(z)

    return _ordered_call(
        body, name=name, grid=(s // tb,), in_specs=[cur(0), cur(1), prev(0), prev(1), wspec, vec, vec, vec],
        out_specs=row, out_shape=jax.ShapeDtypeStruct((s, cw), F32),
        scratch_shapes=[pltpu.VMEM((tb + HALO, cw), F32), pltpu.VMEM((SUBLANES - 1, tb + HALO, cw), F32)],
        compiler_params=_cp("arbitrary"),
    )(proj, proj, proj, proj, w, b, lg, lb)


def _conv_bwd1(name, proj, a_blk, w, b, lg, lb, dy):
    s = proj.shape[0]
    cw = w.shape[1]
    tb = _tile(s, 256)
    cur, prev, wspec, vec, row = _conv_specs(s, cw, tb, a_blk)

    def body(a_ref, gt_ref, ap_ref, gp_ref, w_ref, b_ref, lg_ref, lb_ref, dy_ref,
             dc_ref, dw_ref, db_ref, dlg_ref, dlb_ref, hbuf, shifts):
        i = pl.program_id(0)

        @pl.when(i == 0)
        def _():
            dw_ref[...] = jnp.zeros_like(dw_ref)
            db_ref[...] = jnp.zeros_like(db_ref)
            dlg_ref[...] = jnp.zeros_like(dlg_ref)
            dlb_ref[...] = jnp.zeros_like(dlb_ref)

        c = _conv_recompute(i, a_ref, gt_ref, ap_ref, gp_ref, w_ref, b_ref, hbuf, shifts, tb)
        chat, r = _layer_norm_stats(c)
        z = chat * lg_ref[...] + lb_ref[...]
        sg = _sigmoid(z)
        dz = dy_ref[...] * (sg + z * sg * (1.0 - sg))
        dlg_ref[...] += _colsum(dz * chat)
        dlb_ref[...] += _colsum(dz)
        dzg = dz * lg_ref[...]
        dc = r * (dzg - jnp.mean(dzg, axis=-1, keepdims=True) - chat * jnp.mean(dzg * chat, axis=-1, keepdims=True))
        dc_ref[...] = dc
        db_ref[...] += _colsum(dc)
        for kk in range(CONV_KERNEL):
            dw_ref[pl.ds(kk, 1), :] += _colsum(dc * _window(hbuf, shifts, HALO - (CONV_KERNEL - 1) + kk, tb))

    return _ordered_call(
        body, name=name, grid=(s // tb,), in_specs=[cur(0), cur(1), prev(0), prev(1), wspec, vec, vec, vec, row],
        out_specs=(row, wspec, vec, vec, vec),
        out_shape=(jax.ShapeDtypeStruct((s, cw), F32), jax.ShapeDtypeStruct((HALO, cw), F32),
                   jax.ShapeDtypeStruct((1, cw), F32), jax.ShapeDtypeStruct((1, cw), F32),
                   jax.ShapeDtypeStruct((1, cw), F32)),
        scratch_shapes=[pltpu.VMEM((tb + HALO, cw), F32), pltpu.VMEM((SUBLANES - 1, tb + HALO, cw), F32)],
        compiler_params=_cp("arbitrary"),
    )(proj, proj, proj, proj, w, b, lg, lb, dy)


def _conv_bwd2(name, proj, a_blk, w, dc):
    s = proj.shape[0]
    cw = w.shape[1]
    tb = _tile(s, 256)
    nblk = s // tb
    cur, _, wspec, _, row = _conv_specs(s, cw, tb, a_blk)
    nxt = pl.BlockSpec((tb, cw), lambda i: (jnp.minimum(i + 1, nblk - 1), 0))

    def body(a_ref, gt_ref, w_ref, dc_ref, dn_ref, o_ref, dbuf, shifts):
        i = pl.program_id(0)
        dbuf[pl.ds(0, tb), :] = dc_ref[...]
        dbuf[pl.ds(tb, HALO), :] = jnp.where(i < nblk - 1, dn_ref[pl.ds(0, HALO), :], 0.0)
        _fill_shifts(dbuf, shifts, tb)
        dh = jnp.zeros((tb, cw), F32)
        for kk in range(CONV_KERNEL):
            dh = dh + w_ref[pl.ds(kk, 1), :] * _window(dbuf, shifts, CONV_KERNEL - 1 - kk, tb)
        sg = _sigmoid(gt_ref[...])
        o_ref[:, 0:cw] = (dh * sg).astype(BF16)
        o_ref[:, cw:2 * cw] = (dh * a_ref[...] * sg * (1.0 - sg)).astype(BF16)

    return _ordered_call(
        body, name=name, grid=(nblk,), in_specs=[cur(0), cur(1), wspec, row, nxt],
        out_specs=pl.BlockSpec((tb, 2 * cw), lambda i: (i, 0)), out_shape=jax.ShapeDtypeStruct((s, 2 * cw), BF16),
        scratch_shapes=[pltpu.VMEM((tb + HALO, cw), F32), pltpu.VMEM((SUBLANES - 1, tb + HALO, cw), F32)],
        compiler_params=_cp("arbitrary"),
    )(proj, proj, w, dc, dc)


def _sgu_common(v_ref, lg_ref, lb_ref, w_ref, bexp_ref, sw):
    vhat, r = _layer_norm_stats(v_ref[...])
    vn_b = (vhat * lg_ref[...] + lb_ref[...]).astype(BF16)
    ii = lax.broadcasted_iota(jnp.int32, (WINDOW, WINDOW), 0)
    jj = lax.broadcasted_iota(jnp.int32, (WINDOW, WINDOW), 1)
    tril = jj <= ii
    head_of = lax.broadcasted_iota(jnp.int32, (WINDOW, sw), 1) // HEAD_DIM
    wts = [jnp.where(tril, w_ref[h], 0.0).astype(BF16) for h in range(sw // HEAD_DIM)]
    sv = bexp_ref[...]
    for h, wt in enumerate(wts):
        sv = sv + jnp.where(head_of == h, _dot(wt, vn_b), 0.0)
    return vhat, r, vn_b, tril, head_of, wts, sv


def _sgu_specs(sw, u_blk):
    nh = sw // HEAD_DIM
    u = pl.BlockSpec((WINDOW, sw), lambda n: (n, u_blk))
    v = pl.BlockSpec((WINDOW, sw), lambda n: (n, u_blk + 1))
    vec = pl.BlockSpec((1, sw), lambda n: (0, 0))
    wspec = pl.BlockSpec((nh, WINDOW, WINDOW), lambda n: (0, 0, 0))
    bspec = pl.BlockSpec((WINDOW, sw), lambda n: (0, 0))
    row = pl.BlockSpec((WINDOW, sw), lambda n: (n, 0))
    return u, v, vec, wspec, bspec, row


def _sgu_fwd(name, proj, u_blk, lg, lb, w, bexp):
    s = proj.shape[0]
    sw = lg.shape[1]
    u, v, vec, wspec, bspec, row = _sgu_specs(sw, u_blk)

    def body(u_ref, v_ref, lg_ref, lb_ref, w_ref, bexp_ref, y_ref):
        sv = _sgu_common(v_ref, lg_ref, lb_ref, w_ref, bexp_ref, sw)[-1]
        y_ref[...] = u_ref[...] * sv

    return _ordered_call(
        body, name=name, grid=(s // WINDOW,), in_specs=[u, v, vec, vec, wspec, bspec], out_specs=row,
        out_shape=jax.ShapeDtypeStruct((s, sw), F32), compiler_params=_cp("parallel"),
    )(proj, proj, lg, lb, w, bexp)


def _sgu_bwd(name, proj, u_blk, lg, lb, w, bexp, dy):
    s = proj.shape[0]
    sw = lg.shape[1]
    nh = sw // HEAD_DIM
    u, v, vec, wspec, bspec, row = _sgu_specs(sw, u_blk)
    dbspec = pl.BlockSpec((nh, WINDOW), lambda n: (0, 0))

    def body(u_ref, v_ref, lg_ref, lb_ref, w_ref, bexp_ref, dy_ref, o_ref, dw_ref, db_ref, dlg_ref, dlb_ref):
        n = pl.program_id(0)

        @pl.when(n == 0)
        def _():
            dw_ref[...] = jnp.zeros_like(dw_ref)
            db_ref[...] = jnp.zeros_like(db_ref)
            dlg_ref[...] = jnp.zeros_like(dlg_ref)
            dlb_ref[...] = jnp.zeros_like(dlb_ref)

        vhat, r, vn_b, tril, head_of, wts, sv = _sgu_common(v_ref, lg_ref, lb_ref, w_ref, bexp_ref, sw)
        dyv = dy_ref[...]
        o_ref[:, 0:sw] = (dyv * sv).astype(BF16)
        ds = dyv * u_ref[...]
        dvn = jnp.zeros((WINDOW, sw), F32)
        for h, wt in enumerate(wts):
            dsm_b = jnp.where(head_of == h, ds, 0.0).astype(BF16)
            dvn = dvn + _dot(wt, dsm_b, TN)
            dw_ref[h] += jnp.where(tril, _dot(dsm_b, vn_b, NT), 0.0)
        hmask = (lax.broadcasted_iota(jnp.int32, (nh, sw), 1) // HEAD_DIM
                 == lax.broadcasted_iota(jnp.int32, (nh, sw), 0)).astype(F32)
        db_ref[...] += lax.dot_general(hmask, ds, NT, precision=lax.Precision.HIGHEST, preferred_element_type=F32)
        dlg_ref[...] += _colsum(dvn * vhat)
        dlb_ref[...] += _colsum(dvn)
        dvg = dvn * lg_ref[...]
        dv = r * (dvg - jnp.mean(dvg, axis=-1, keepdims=True) - vhat * jnp.mean(dvg * vhat, axis=-1, keepdims=True))
        o_ref[:, sw:2 * sw] = dv.astype(BF16)

    return _ordered_call(
        body, name=name, grid=(s // WINDOW,), in_specs=[u, v, vec, vec, wspec, bspec, row],
        out_specs=(pl.BlockSpec((WINDOW, 2 * sw), lambda n: (n, 0)), wspec, dbspec, vec, vec),
        out_shape=(jax.ShapeDtypeStruct((s, 2 * sw), BF16), jax.ShapeDtypeStruct((nh, WINDOW, WINDOW), F32),
                   jax.ShapeDtypeStruct((nh, WINDOW), F32), jax.ShapeDtypeStruct((1, sw), F32),
                   jax.ShapeDtypeStruct((1, sw), F32)),
        compiler_params=_cp("arbitrary"),
    )(proj, proj, lg, lb, w, bexp, dy)


def _adamw(name, w, g, m, v):
    rows, cols = w.shape
    tr = _tile(rows, 256)

    def body(w_ref, g_ref, m_ref, v_ref, d_ref, nm_ref, nv_ref):
        gv = g_ref[...]
        mv = ADAM_B1 * m_ref[...] + (1.0 - ADAM_B1) * gv
        vv = ADAM_B2 * v_ref[...] + (1.0 - ADAM_B2) * (gv * gv)
        m_hat = mv / (1.0 - ADAM_B1 ** ADAM_STEP)
        v_hat = vv / (1.0 - ADAM_B2 ** ADAM_STEP)
        d_ref[...] = -ADAM_LR * (m_hat / (jnp.sqrt(v_hat) + ADAM_EPS) + ADAM_WD * w_ref[...])
        nm_ref[...] = mv
        nv_ref[...] = vv

    spec = pl.BlockSpec((tr, cols), lambda i: (i, 0))
    shape = jax.ShapeDtypeStruct((rows, cols), F32)
    return _ordered_call(
        body, name=name, grid=(rows // tr,), in_specs=[spec] * 4, out_specs=(spec,) * 3, out_shape=(shape,) * 3,
        compiler_params=_cp("parallel"),
    )(w, g, m, v)


def _route():
    x, y, c = lax.axis_index("x"), lax.axis_index("y"), lax.axis_index("c")
    n1 = (jnp.where(c == 0, 1 - x, x), jnp.where(c == 0, y, 1 - y))
    n2 = (jnp.where(c == 0, x, 1 - x), jnp.where(c == 0, 1 - y, y))
    return x, y, c, n1, n2, (1 - x, 1 - y)


def _cidx(chip):
    return 2 * chip[0] + chip[1]


def _remote(src, dst, sems, k, device):
    send_sems, recv_sems = sems
    return pltpu.make_async_remote_copy(src_ref=src, dst_ref=dst, send_sem=send_sems.at[k], recv_sem=recv_sems.at[k],
                                        device_id=device, device_id_type=MESH)


def _exchange(name, bufs, n_sems, build):
    n = len(bufs)

    def body(*refs):
        cps = build(refs[n:2 * n], (refs[2 * n], refs[2 * n + 1]))
        for cp in cps:
            cp.start()
        for cp in cps:
            cp.wait()

    return _ordered_call(
        body, name=name, in_specs=[ANY] * n, out_specs=tuple(ANY for _ in range(n)),
        out_shape=tuple(jax.ShapeDtypeStruct(b.shape, b.dtype) for b in bufs),
        input_output_aliases={i: i for i in range(n)},
        scratch_shapes=[pltpu.SemaphoreType.DMA((n_sems,)), pltpu.SemaphoreType.DMA((n_sems,))],
        compiler_params=pltpu.CompilerParams(has_side_effects=True),
    )(*bufs)


HBM_SPEC = pl.BlockSpec(memory_space=pltpu.HBM)
SEM_SPEC = pl.BlockSpec(memory_space=pltpu.SEMAPHORE)
DATAFLOW = pltpu.SideEffectType.DATAFLOW_SIDE_EFFECTING


def _exchange_start(name, bufs, n_sems, build):
    n = len(bufs)

    def body(*refs):
        for cp in build(refs[:n], (refs[n], refs[n + 1])):
            cp.start()
        refs[-1][...] = jnp.zeros_like(refs[-1])

    out = _ordered_call(
        body, name=name,
        out_shape=(pltpu.SemaphoreType.DMA((n_sems,)), pltpu.SemaphoreType.DMA((n_sems,)))
        + tuple(pltpu.HBM(b.shape, b.dtype) for b in bufs) + (jax.ShapeDtypeStruct((8, 128), F32),),
        in_specs=[HBM_SPEC] * n,
        out_specs=(SEM_SPEC, SEM_SPEC) + (HBM_SPEC,) * n + (pl.BlockSpec(memory_space=pltpu.VMEM),),
        input_output_aliases={i: 2 + i for i in range(n)},
        compiler_params=pltpu.CompilerParams(has_side_effects=DATAFLOW),
    )(*[pltpu.with_memory_space_constraint(b, pltpu.HBM) for b in bufs])
    return dict(name=name, send=out[0], recv=out[1], bufs=list(out[2:2 + n]), token=out[-1], build=build)


def _exchange_wait(handle):
    n = len(handle["bufs"])

    def body(*refs):
        for cp in handle["build"](refs[:n], (refs[n], refs[n + 1])):
            cp.wait_send()
            cp.wait_recv()

    return list(_ordered_call(
        body, name=handle["name"] + "_wait", out_shape=tuple(pltpu.HBM(b.shape, b.dtype) for b in handle["bufs"]),
        in_specs=[HBM_SPEC] * n + [SEM_SPEC, SEM_SPEC], out_specs=(HBM_SPEC,) * n,
        input_output_aliases={i: i for i in range(n)},
        compiler_params=pltpu.CompilerParams(has_side_effects=DATAFLOW),
    )(*handle["bufs"], handle["send"], handle["recv"]))


def _cast_place(name, w, l, me_idx, dtype):
    _, r, c = w.shape
    tr = _tile(r, 512)

    def body(me_ref, w_ref, o_ref):
        o_ref[...] = w_ref[...].astype(dtype)

    grid_spec = pltpu.PrefetchScalarGridSpec(
        num_scalar_prefetch=1, grid=(r // tr,),
        in_specs=[pl.BlockSpec((None, tr, c), lambda i, me_ref: (l, i, 0))],
        out_specs=pl.BlockSpec((None, tr, c), lambda i, me_ref: (me_ref[0], i, 0)))
    return _ordered_call(
        body, name=name, grid_spec=grid_spec, out_shape=jax.ShapeDtypeStruct((N_CHIPS, r, c), dtype),
        compiler_params=_cp("arbitrary"),
    )(me_idx, w)


def _my_half(ref, blk, c):
    hr = ref.shape[1] // 2
    return ref.at[blk, pl.ds(c * hr, hr), :]


def _gather_step(entering):
    lens = [len(e) for e in entering]
    flat = [b for e in entering for b in e]

    def build(refs, sems):
        x, y, c, n1, n2, dg = _route()
        me = _cidx((x, y))
        plan = ([(r, (me,), (*n1, c)) for r in refs[:lens[0]]]
                + [(r, (me, _cidx(n1)), (*n2, c)) for r in refs[lens[0]:lens[0] + lens[1]]]
                + [(r, (_cidx(n1), _cidx(n2), _cidx(dg)), (x, y, 1 - c)) for r in refs[lens[0] + lens[1]:]])
        cps = []
        for ref, blocks, peer in plan:
            for blk in blocks:
                cps.append(_remote(_my_half(ref, blk, c), _my_half(ref, blk, c), sems, len(cps), peer))
        return cps

    return flat, lens[0] + 2 * lens[1] + 3 * lens[2], build


RI_C, RI_ME, RI_N2 = 0, 1, 2


def _pair_sum(name, g, sib, route_idx):
    _, rows, cols = g.shape
    hr = rows // 2
    tr = _tile(hr, 512)
    per = hr // tr

    def body(ri, g_ref, s_ref, o_ref):
        o_ref[...] = (g_ref[...].astype(F32) + s_ref[...].astype(F32)).astype(BF16)

    blk = (None, tr, cols)
    grid_spec = pltpu.PrefetchScalarGridSpec(
        num_scalar_prefetch=1, grid=(N_CHIPS, per),
        in_specs=[pl.BlockSpec(blk, lambda j, i, ri: (j, ri[RI_C] * per + i, 0)),
                  pl.BlockSpec(blk, lambda j, i, ri: (j, i, 0))],
        out_specs=pl.BlockSpec(blk, lambda j, i, ri: (j, i, 0)))
    return _ordered_call(
        body, name=name, grid_spec=grid_spec, out_shape=jax.ShapeDtypeStruct((N_CHIPS, hr, cols), BF16),
        compiler_params=_cp("parallel", "parallel"),
    )(route_idx, g, sib)


def _sum_stage1(name, p, got, route_idx):
    _, hr, cols = p.shape
    tr = _tile(hr, 512)

    def body(ri, pm_ref, pn_ref, g0_ref, g1_ref, keep_ref, send_ref):
        keep_ref[...] = pm_ref[...].astype(F32) + g0_ref[...].astype(F32)
        send_ref[...] = (pn_ref[...].astype(F32) + g1_ref[...].astype(F32)).astype(BF16)

    blk = (None, tr, cols)
    row = pl.BlockSpec((tr, cols), lambda i, ri: (i, 0))
    grid_spec = pltpu.PrefetchScalarGridSpec(
        num_scalar_prefetch=1, grid=(hr // tr,),
        in_specs=[pl.BlockSpec(blk, lambda i, ri: (ri[RI_ME], i, 0)), pl.BlockSpec(blk, lambda i, ri: (ri[RI_N2], i, 0)),
                  pl.BlockSpec(blk, lambda i, ri: (0, i, 0)), pl.BlockSpec(blk, lambda i, ri: (1, i, 0))],
        out_specs=(row, row))
    return _ordered_call(
        body, name=name, grid_spec=grid_spec,
        out_shape=(jax.ShapeDtypeStruct((hr, cols), F32), jax.ShapeDtypeStruct((hr, cols), BF16)),
        compiler_params=_cp("parallel"),
    )(route_idx, p, p, got, got)


def _sum_stage2(name, keep, got):
    hr, cols = keep.shape
    tr = _tile(hr, 512)

    def body(k_ref, g_ref, o_ref):
        o_ref[...] = k_ref[...] + g_ref[...].astype(F32)

    row = pl.BlockSpec((tr, cols), lambda i: (i, 0))
    return _ordered_call(
        body, name=name, grid=(hr // tr,), in_specs=[row, row], out_specs=row,
        out_shape=jax.ShapeDtypeStruct((hr, cols), F32), compiler_params=_cp("parallel"),
    )(keep, got)


def _reduce_scatter(tag, names, grads, route_idx):
    n = len(grads)
    hrs = [g.shape[1] // 2 for g in grads]

    def empty(t, lead, dtype):
        return lax.empty(lead + (hrs[t], grads[t].shape[2]), dtype)

    def pair_stage(refs, sems):
        x, y, c, n1, n2, dg = _route()
        return [_remote(refs[t].at[:, pl.ds((1 - c) * hrs[t], hrs[t]), :], refs[n + t], sems, t, (x, y, 1 - c))
                for t in range(n)]

    def stage1(refs, sems):
        x, y, c, n1, n2, dg = _route()
        return [_remote(refs[t].at[blk], refs[n + t].at[slot], sems, 2 * t + slot, (*n1, c))
                for t in range(n) for slot, blk in enumerate((_cidx(n1), _cidx(dg)))]

    def stage2(refs, sems):
        x, y, c, n1, n2, dg = _route()
        return [_remote(refs[t], refs[n + t], sems, t, (*n2, c)) for t in range(n)]

    def stage3(refs, sems):
        x, y, c, n1, n2, dg = _route()
        return [_remote(refs[t], refs[n + t], sems, t, (x, y, 1 - c)) for t in range(n)]

    state = {}

    def start():
        state["h"] = _exchange_start(f"rs_pair_{tag}", list(grads) + [empty(t, (N_CHIPS,), BF16) for t in range(n)],
                                     n, pair_stage)

    def pair_done():
        out = _exchange_wait(state["h"])
        psum = [_pair_sum(f"rs_psum_{names[t]}", out[t], out[n + t], route_idx) for t in range(n)]
        state["h"] = _exchange_start(f"rs_x1_{tag}", psum + [empty(t, (2,), BF16) for t in range(n)], 2 * n, stage1)

    def x1_done():
        out = _exchange_wait(state["h"])
        state["keep"], send = zip(*[_sum_stage1(f"rs_sum1_{names[t]}", out[t], out[n + t], route_idx)
                                    for t in range(n)])
        state["h"] = _exchange_start(f"rs_x2_{tag}", list(send) + [empty(t, (), BF16) for t in range(n)], n, stage2)

    def x2_done():
        out = _exchange_wait(state["h"])
        mine = [_sum_stage2(f"rs_sum2_{names[t]}", state["keep"][t], out[n + t]) for t in range(n)]
        state["h"] = _exchange_start(f"rs_half_{tag}", mine + [empty(t, (), F32) for t in range(n)], n, stage3)

    def finish():
        out = _exchange_wait(state["h"])
        return list(zip(out[:n], out[n:]))

    return start, pair_done, x1_done, x2_done, finish


def _adamw_big(name, w, m, v, f, h, l, c_idx, prev):
    n_l, r, cols = w.shape
    hr = r // 2
    tr = _tile(hr, 256)
    per = hr // tr

    def body(c_ref, w_ref, m_ref, v_ref, f_ref, h_ref, *rest):
        g_ref, d_ref, nm_ref, nv_ref = rest[-4:]
        gv = jnp.where(pl.program_id(0) == c_ref[0], f_ref[...], h_ref[...])
        mv = ADAM_B1 * m_ref[...] + (1.0 - ADAM_B1) * gv
        vv = ADAM_B2 * v_ref[...] + (1.0 - ADAM_B2) * (gv * gv)
        m_hat = mv / (1.0 - ADAM_B1 ** ADAM_STEP)
        v_hat = vv / (1.0 - ADAM_B2 ** ADAM_STEP)
        g_ref[...] = gv
        d_ref[...] = -ADAM_LR * (m_hat / (jnp.sqrt(v_hat) + ADAM_EPS) + ADAM_WD * w_ref[...])
        nm_ref[...] = mv
        nv_ref[...] = vv

    big = pl.BlockSpec((None, tr, cols), lambda hf, i, c_ref: (l, hf * per + i, 0))
    fspec = pl.BlockSpec((tr, cols), lambda hf, i, c_ref: (jnp.where(hf == c_ref[0], i, 0), 0))
    hspec = pl.BlockSpec((tr, cols), lambda hf, i, c_ref: (jnp.where(hf == c_ref[0], 0, i), 0))
    grid_spec = pltpu.PrefetchScalarGridSpec(
        num_scalar_prefetch=1, grid=(2, per), in_specs=[big] * 3 + [fspec, hspec] + [ANY] * len(prev),
        out_specs=(big,) * 4)
    return _ordered_call(
        body, name=name, grid_spec=grid_spec, out_shape=(jax.ShapeDtypeStruct(w.shape, F32),) * 4,
        input_output_aliases={6 + k: k for k in range(len(prev))}, compiler_params=_cp("arbitrary", "arbitrary"),
    )(c_idx, w, m, v, f, h, *prev)


def _small_allreduce(buf):
    rows = buf.shape[0]
    hr = rows // 2

    def body(in_ref, out_ref, pair, acc, got1, got2, send_sems, recv_sems):
        x, y, c, n1, n2, dg = _route()
        sems = (send_sems, recv_sems)
        sibling = (x, y, 1 - c)
        mine = pl.ds(pl.multiple_of(c * hr, 8), hr)
        pair[c] = in_ref[...]
        cp = _remote(in_ref, pair.at[c], sems, 0, sibling)
        cp.start()
        cp.wait()
        acc[...] = pair[0, mine, :] + pair[1, mine, :]
        cp = _remote(acc, got1, sems, 1, (*n1, c))
        cp.start()
        cp.wait()
        acc[...] = acc[...] + got1[...]
        cp = _remote(acc, got2, sems, 2, (*n2, c))
        cp.start()
        cp.wait()
        out_ref[mine, :] = acc[...] + got2[...]
        cp = _remote(out_ref.at[mine, :], out_ref.at[mine, :], sems, 3, sibling)
        cp.start()
        cp.wait()

    half = pltpu.VMEM((hr, 128), F32)
    return _ordered_call(
        body, name="small_allreduce", in_specs=[pl.BlockSpec(memory_space=pltpu.VMEM)],
        out_specs=pl.BlockSpec(memory_space=pltpu.VMEM), out_shape=jax.ShapeDtypeStruct((rows, 128), F32),
        scratch_shapes=[pltpu.VMEM((2, rows, 128), F32), half, half, half,
                        pltpu.SemaphoreType.DMA((4,)), pltpu.SemaphoreType.DMA((4,))],
        compiler_params=pltpu.CompilerParams(has_side_effects=True, vmem_limit_bytes=VMEM_LIMIT),
    )(buf)


BIG = ("w_in", "w_out", "w_up", "w_down")
COL_SHARDED = {"w_in": True, "w_out": False, "w_up": True, "w_down": False}
SMALL = ("ln1_g", "q_norm_g", "k_norm_g", "sinks", "conv_w", "conv_b", "conv_ln_g", "conv_ln_b", "sgu_ln_g",
         "sgu_ln_b", "sgu_w", "sgu_b", "out_norm_g", "ln2_g")
WEIGHTS = ("ln1_g", "w_in", "q_norm_g", "k_norm_g", "sinks", "conv_w", "conv_b", "conv_ln_g", "conv_ln_b",
           "sgu_ln_g", "sgu_ln_b", "sgu_w", "sgu_b", "out_norm_g", "w_out", "ln2_g", "w_up", "w_down")
PACK_QUANTUM = 8 * 128
PACK_ROWS = 512


def _pack(arrs):
    parts = []
    for a in arrs:
        f = a.reshape(-1)
        parts.append(jnp.pad(f, (0, -f.shape[0] % PACK_QUANTUM)).reshape(-1, 128))
    rows = sum(p.shape[0] for p in parts)
    parts.append(jnp.zeros((-rows % PACK_ROWS, 128), F32))
    return jnp.concatenate(parts, axis=0)


def _unpack(buf, shapes):
    out, off = [], 0
    for shp in shapes:
        n = 1
        for dd in shp:
            n *= dd
        rows = (n + PACK_QUANTUM - 1) // PACK_QUANTUM * 8
        out.append(buf[off:off + rows].reshape(-1)[:n].reshape(shp))
        off += rows
    return out


def _to_heads(t, nh):
    return t.reshape(t.shape[0], nh, HEAD_DIM).transpose(1, 0, 2)


def _from_heads(t):
    return t.transpose(1, 0, 2).reshape(t.shape[1], t.shape[0] * HEAD_DIM)


def _no_hook(point, carry):
    return carry


def _layer_fwd(l, x, p, wg, hook=_no_hook):
    d = x.shape[1]
    aw, cw = d // 2, d // 4
    nq = aw // HEAD_DIM
    nkv = nq // GQA
    kvw = nkv * HEAD_DIM
    x = hook("fwd_start", x)
    h1 = _rms_fwd(f"ln1_fwd_{l}", x, p["ln1_g"])
    proj = _mm_act_w(f"proj_{l}", h1, wg["w_in"], True, _ep_store)[0]
    q = _to_heads(proj[:, :aw], nq)
    k = _to_heads(proj[:, aw:aw + kvw], nkv)
    v = _to_heads(proj[:, aw + kvw:aw + 2 * kvw], nkv)
    sinks_b = jnp.broadcast_to(p["sinks"][:, None, None], (nq, 1, 128))
    ya = _from_heads(_attn_fwd(f"attn_fwd_{l}", q, k, v, p["q_norm_g"], p["k_norm_g"], sinks_b))
    ya = hook("fwd_attn", ya)
    yc = _conv_fwd(f"conv_fwd_{l}", proj, 3, p["conv_w"], p["conv_b"], p["conv_ln_g"], p["conv_ln_b"])
    ys = _sgu_fwd(f"sgu_fwd_{l}", proj, 5, p["sgu_ln_g"], p["sgu_ln_b"], p["sgu_w"], p["sgu_bexp"])
    mix = _mixnorm_fwd(f"mixnorm_fwd_{l}", ya, yc, ys, p["out_norm_g"])
    mix = hook("fwd_mid", mix)
    xm = _mm_act_w(f"out_{l}", mix, wg["w_out"], False, _ep_residual, extra=(x,))[0]
    h2 = _rms_fwd(f"ln2_fwd_{l}", xm, p["ln2_g"])
    up_b, act_b = _mm_act_w(f"up_{l}", h2, wg["w_up"], True, _ep_up, out_dtypes=(BF16, BF16))
    act_b = hook("fwd_up", act_b)
    xo = _mm_act_w(f"down_{l}", act_b, wg["w_down"], False, _ep_residual, extra=(xm,))[0]
    xo = hook("fwd_end", xo)
    saved = dict(x=x, h1=h1, proj=proj, q=q, k=k, v=v, sinks_b=sinks_b, ya=ya, yc=yc, ys=ys, mix=mix, xm=xm, h2=h2,
                 up_b=up_b, act_b=act_b)
    return xo, saved


def _layer_bwd(l, dxo, dxo_b, p, wg, sv, big, hook=_no_hook):
    d = dxo.shape[1]
    nq = (d // 2) // HEAD_DIM
    small = {}
    dxo_b = hook("bwd_start", dxo_b)
    big["w_down"] = _mm_wgrad(f"dw_down_{l}", sv["act_b"], dxo_b, False, d)
    dup_b = _mm_act_wt(f"dup_{l}", dxo_b, wg["w_down"], False, _ep_dup, extra=(sv["up_b"],), out_dtypes=(BF16,))[0]
    dup_b = hook("bwd_dup", dup_b)
    big["w_up"] = _mm_wgrad(f"dw_up_{l}", sv["h2"], dup_b, True, wg["w_up"].shape[2])
    dh2 = _mm_act_wt(f"dh2_{l}", dup_b, wg["w_up"], True, _ep_store)[0]
    dh2 = hook("bwd_dh2", dh2)
    dxm, dxm_b, small["ln2_g"] = _rms_bwd(f"ln2_bwd_{l}", dh2, sv["xm"], p["ln2_g"], dxo)
    big["w_out"] = _mm_wgrad(f"dw_out_{l}", sv["mix"], dxm_b, False, d)
    dmix = _mm_act_wt(f"dmix_{l}", dxm_b, wg["w_out"], False, _ep_store)[0]
    dya, dyc, dys, small["out_norm_g"] = _mixnorm_bwd(f"mixnorm_bwd_{l}", dmix, sv["ya"], sv["yc"], sv["ys"],
                                                      p["out_norm_g"])
    dya = hook("bwd_mix", dya)
    dq, dkc, dkp, dvc, dvp, small["q_norm_g"], dsink = _attn_bwd(
        f"attn_bwd_{l}", sv["q"], sv["k"], sv["v"], p["q_norm_g"], p["k_norm_g"], sv["sinks_b"], _to_heads(dya, nq))
    dkc = hook("bwd_attn", dkc)
    small["sinks"] = dsink[:, 0, 0]
    dk, dv, small["k_norm_g"] = _attn_bwd_kv(f"attn_bwd_kv_{l}", sv["k"], p["k_norm_g"], dkc, dkp, dvc, dvp)
    dc, dcw, small["conv_b"], small["conv_ln_g"], small["conv_ln_b"] = _conv_bwd1(
        f"conv_bwd1_{l}", sv["proj"], 3, p["conv_w"], p["conv_b"], p["conv_ln_g"], p["conv_ln_b"], dyc)
    small["conv_w"] = dcw[:CONV_KERNEL]
    dxc_b = _conv_bwd2(f"conv_bwd2_{l}", sv["proj"], 3, p["conv_w"], dc)
    dxs_b, small["sgu_w"], small["sgu_b"], small["sgu_ln_g"], small["sgu_ln_b"] = _sgu_bwd(
        f"sgu_bwd_{l}", sv["proj"], 5, p["sgu_ln_g"], p["sgu_ln_b"], p["sgu_w"], p["sgu_bexp"], dys)
    dxs_b = hook("bwd_sgu", dxs_b)
    dproj_b = jnp.concatenate([_from_heads(dq).astype(BF16), _from_heads(dk).astype(BF16),
                               _from_heads(dv).astype(BF16), dxc_b, dxs_b], axis=1)
    big["w_in"] = _mm_wgrad(f"dw_in_{l}", sv["h1"], dproj_b, True, wg["w_in"].shape[2])
    dh1 = _mm_act_wt(f"dh1_{l}", dproj_b, wg["w_in"], True, _ep_store)[0]
    dx, dx_b, small["ln1_g"] = _rms_bwd(f"ln1_bwd_{l}", dh1, sv["x"], p["ln1_g"], dxm)
    dx_b = hook("bwd_end", dx_b)
    return dx, dx_b, small


def kernel(x, ln1_g, w_in, q_norm_g, k_norm_g, sinks, conv_w, conv_b, conv_ln_g, conv_ln_b, sgu_ln_g, sgu_ln_b, sgu_w, sgu_b, out_norm_g, w_out, ln2_g, w_up, w_down, loss_target, m_ln1_g, m_w_in, m_q_norm_g, m_k_norm_g, m_sinks, m_conv_w, m_conv_b, m_conv_ln_g, m_conv_ln_b, m_sgu_ln_g, m_sgu_ln_b, m_sgu_w, m_sgu_b, m_out_norm_g, m_w_out, m_ln2_g, m_w_up, m_w_down, v_ln1_g, v_w_in, v_q_norm_g, v_k_norm_g, v_sinks, v_conv_w, v_conv_b, v_conv_ln_g, v_conv_ln_b, v_sgu_ln_g, v_sgu_ln_b, v_sgu_w, v_sgu_b, v_out_norm_g, v_w_out, v_ln2_g, v_w_up, v_w_down):
    given = dict(locals())
    _LAST[0] = None
    n_layers = ln1_g.shape[0]
    s, d = x.shape[1], x.shape[2]
    cw = d // 4
    xi, yi, core = lax.axis_index("x"), lax.axis_index("y"), lax.axis_index("c")
    chip = 2 * xi + yi
    second = jnp.where(core == 0, 2 * xi + (1 - yi), 2 * (1 - xi) + yi)
    route_idx = jnp.stack([core, chip, second]).astype(jnp.int32)

    conv_w_pad = jnp.pad(conv_w, ((0, 0), (0, HALO - CONV_KERNEL), (0, 0))).reshape(1, n_layers * HALO, -1)
    cwl = conv_w_pad.shape[2]
    buf = {(nm, l): _cast_place(f"place_{nm}_{l}", given[nm], l, route_idx[1:2], BF16)
           for l in range(n_layers) for nm in BIG}
    buf["conv_w"] = _cast_place("place_conv_w", conv_w_pad, 0, route_idx[1:2], F32)
    groups = [["conv_w", ("w_in", 0)]] + [[(nm, l)] for l in range(n_layers) for nm in BIG if (nm, l) != ("w_in", 0)]
    n_steps = len(groups) + 2

    def step_parts(st):
        keys = [groups[st - j] if 0 <= st - j < len(groups) else [] for j in range(3)]
        flat, n_sems, build = _gather_step([[buf[k] for k in ks] for ks in keys])
        return [k for ks in keys for k in ks], flat, n_sems, build

    pending = {}

    def start_step(st):
        keys, flat, n_sems, build = step_parts(st)
        pending["keys"], pending["h"] = keys, _exchange_start(f"gather_step{st}", flat, n_sems, build)

    def wait_step():
        for k, b in zip(pending["keys"], _exchange_wait(pending["h"])):
            buf[k] = b

    for st in range(3):
        keys, flat, n_sems, build = step_parts(st)
        for k, b in zip(keys, _exchange(f"gather_step{st}", flat, n_sems, build)):
            buf[k] = b
    conv_w_full = buf["conv_w"].reshape(N_CHIPS, n_layers, HALO, cwl).transpose(1, 2, 0, 3).reshape(
        n_layers, HALO, cw)

    class LayerWeights:
        def __init__(self, l):
            self.l = l

        def __getitem__(self, nm):
            return buf[(nm, self.l)]

    wgs = [LayerWeights(l) for l in range(n_layers)]
    fwd_points = [(l, pt) for l in range(n_layers) for pt in ("fwd_start", "fwd_attn", "fwd_mid", "fwd_up", "fwd_end")
                  if (pt != "fwd_start" or l == 0) and (pt != "fwd_end" or l + 1 < n_layers)]
    assert len(fwd_points) == n_steps - 3 + 1, "one hook point per pipeline step, and one to wait for the last"
    fwd_tables = [{} for _ in range(n_layers)]
    for i, (l, pt) in enumerate(fwd_points):
        if i > 0:
            fwd_tables[l].setdefault(pt, []).append(wait_step)
        if 3 + i < n_steps:
            fwd_tables[l].setdefault(pt, []).append(functools.partial(start_step, 3 + i))
    params = []
    for l in range(n_layers):
        p = {nm: given[nm][l] for nm in SMALL if nm != "conv_w"}
        for nm in ("conv_b", "conv_ln_g", "conv_ln_b", "sgu_ln_g", "sgu_ln_b"):
            p[nm] = p[nm].reshape(1, -1)
        p["conv_w"] = conv_w_full[l]
        p["sgu_bexp"] = jnp.repeat(sgu_b[l].T, HEAD_DIM, axis=1)
        params.append(p)

    def make_hook(table):
        def hook(point, carry):
            for fn in table.get(point, ()):
                fn()
            return carry
        return hook

    h = x.reshape(s, d)
    saved = []
    for l in range(n_layers):
        h, sv = _layer_fwd(l, h, params[l], wgs[l], make_hook(fwd_tables[l]))
        saved.append(sv)
    dh, dh_b, loss_part = _loss_head(h, loss_target.reshape(s, d))
    loss = lax.psum(loss_part[0, 0], ("x", "y", "c"))

    big_grads = [{} for _ in range(n_layers)]
    small_grads = [None] * n_layers
    halves = {}

    def rs_group(tag, l, names):
        phases = {}

        def start():
            phases["p"] = _reduce_scatter(tag, [f"{nm}_{l}" for nm in names], [big_grads[l][nm] for nm in names],
                                          route_idx)
            phases["p"][0]()

        def step(k):
            return lambda: phases["p"][k]()

        def finish():
            for nm, fh in zip(names, phases["p"][4]()):
                halves[(nm, l)] = fh

        return [start, step(1), step(2), step(3), finish]

    early = rs_group("l0a", 0, ("w_down", "w_up", "w_out"))
    for l in reversed(range(n_layers)):
        table = {}
        if l + 1 < n_layers:
            above = rs_group(f"l{l + 1}", l + 1, BIG)
            for point, fn in zip(("bwd_start", "bwd_dup", "bwd_mix", "bwd_attn", "bwd_sgu"), above):
                table.setdefault(point, []).append(fn)
        if l == 0:
            for point, fn in zip(("bwd_mix", "bwd_attn", "bwd_end"), early[:3]):
                table.setdefault(point, []).append(fn)
        dh, dh_b, small_grads[l] = _layer_bwd(l, dh, dh_b, params[l], wgs[l], saved[l], big_grads[l],
                                              make_hook(table))
    grad_x = dh.reshape(x.shape)

    grads, delta, new_m, new_v = {}, {}, {}, {}
    adam_state = {nm: () for nm in BIG}

    def adam(nm, l):
        f, h = halves[(nm, l)]
        adam_state[nm] = _adamw_big(f"adamw_{nm}_{l}", given[nm], given["m_" + nm], given["v_" + nm], f, h, l,
                                    route_idx[0:1], adam_state[nm])

    def small_update():
        small_shapes = [(n_layers,) + small_grads[0][nm].shape for nm in SMALL]
        small_sum = _small_allreduce(_pack([jnp.stack([small_grads[l][nm] for l in range(n_layers)])
                                            for nm in SMALL]))
        for nm, g in zip(SMALL, _unpack(small_sum, small_shapes)):
            grads[nm] = g.reshape((n_layers,) + given[nm].shape[1:]) if nm != "conv_w" else g
        grads["conv_w"] = lax.dynamic_slice_in_dim(grads["conv_w"], chip * cwl, cwl, axis=2)
        packed = [_pack([src[nm] for nm in SMALL]) for src in
                  ({nm: given[nm] for nm in SMALL}, grads, {nm: given["m_" + nm] for nm in SMALL},
                   {nm: given["v_" + nm] for nm in SMALL})]
        local_shapes = [given[nm].shape for nm in SMALL]
        for dst, buf in zip((delta, new_m, new_v), _adamw("adamw_small", *packed)):
            for nm, a in zip(SMALL, _unpack(buf, local_shapes)):
                dst[nm] = a

    upper = [(nm, l) for l in reversed(range(1, n_layers)) for nm in reversed(BIG)]
    late = rs_group("l0b", 0, ("w_in",))
    late[0]()
    for task in upper[:1]:
        adam(*task)
    late[1]()
    for task in upper[1:]:
        adam(*task)
    early[3]()
    late[2]()
    small_update()
    early[4]()
    for nm in ("w_down", "w_up"):
        adam(nm, 0)
    late[3]()
    adam("w_out", 0)
    late[4]()
    adam("w_in", 0)
    for nm in BIG:
        grads[nm], delta[nm], new_m[nm], new_v[nm] = adam_state[nm]
    return (loss, grad_x, *[grads[nm] for nm in WEIGHTS], *[delta[nm] for nm in WEIGHTS],
            *[new_m[nm] for nm in WEIGHTS], *[new_v[nm] for nm in WEIGHTS])
```

```python
import functools

import jax
import jax.numpy as jnp
from jax import lax
from jax.experimental import pallas as pl
from jax.experimental.pallas import tpu as pltpu

F32 = jnp.float32
BF16 = jnp.bfloat16
EPS = 1e-6
NEG_INF = -1e30
HEAD_DIM = 64
WINDOW = 128
CONV_KERNEL = 31
HALO = 32
GQA = 4
N_CHIPS = 4
ADAM_LR, ADAM_B1, ADAM_B2, ADAM_EPS, ADAM_WD, ADAM_STEP = 0.001, 0.9, 0.999, 1e-08, 0.01, 10
VMEM_LIMIT = 56 * 1024 * 1024
TILE_K = 2048
MESH = pl.DeviceIdType.MESH
ANY = pl.BlockSpec(memory_space=pl.ANY)

NN = (((1,), (0,)), ((), ()))
NT = (((1,), (1,)), ((), ()))
TN = (((0,), (0,)), ((), ()))


def _cp(*sem):
    return pltpu.CompilerParams(dimension_semantics=sem, vmem_limit_bytes=VMEM_LIMIT)


_LAST = [None]


def _ordered_call(body, *, in_specs=None, grid_spec=None, **kw):
    def run(*operands):
        dep = _LAST[0]
        n = len(operands)
        if dep is None or any(dep is o for o in operands):
            fn, specs, spec, args = body, in_specs, grid_spec, operands
        else:
            def fn(*refs):
                body(*refs[:n], *refs[n + 1:])

            specs, spec, args = in_specs, grid_spec, operands + (dep,)
            if grid_spec is not None:
                spec = pltpu.PrefetchScalarGridSpec(
                    num_scalar_prefetch=grid_spec.num_scalar_prefetch, grid=grid_spec.grid,
                    in_specs=list(grid_spec.in_specs) + [ANY], out_specs=grid_spec.out_specs)
            else:
                specs = list(in_specs) + [ANY]
        if spec is not None:
            out = pl.pallas_call(fn, grid_spec=spec, **kw)(*args)
        else:
            out = pl.pallas_call(fn, in_specs=specs, **kw)(*args)
        arrays = [o for o in (out if isinstance(out, (tuple, list)) else (out,)) if o.dtype in (F32, BF16)]
        _LAST[0] = min(arrays, key=lambda o: o.size)
        return out

    return run


def _tile(dim, pref):
    if dim <= pref:
        return dim
    for t in range(pref, 0, -128):
        if dim % t == 0:
            return t
    while dim % pref:
        pref //= 2
    return pref


def _dot(a, b, dims=NN):
    return lax.dot_general(a, b, dims, preferred_element_type=F32)


def _colsum(v):
    return jnp.sum(v, axis=0, keepdims=True)


def _sigmoid(x):
    return 1.0 / (1.0 + jnp.exp(-x))


def _matmul(name, operands, in_specs, out_shape, out_specs, grid, dims, acc_shape, epilogue, split_k=False):
    nk = grid[2]
    n_in = len(operands)

    def product(a_ref, b_ref):
        if split_k:
            ck = b_ref.shape[2]
            out = _dot(a_ref[:, 0:ck], b_ref[0], dims)
            for j in range(1, N_CHIPS):
                out = out + _dot(a_ref[:, j * ck:(j + 1) * ck], b_ref[j], dims)
            return out
        bv = b_ref[...]
        return _dot(a_ref[...], bv.reshape(-1, bv.shape[-1]) if bv.ndim == 3 else bv, dims)

    def body(*refs):
        a_ref, b_ref = refs[0], refs[1]
        extra = refs[2:n_in]
        if nk == 1:
            epilogue(product(a_ref, b_ref), extra, refs[n_in:])
            return
        outs = refs[n_in:-1]
        acc = refs[-1]
        k = pl.program_id(2)

        @pl.when(k == 0)
        def _():
            acc[...] = product(a_ref, b_ref)

        @pl.when((k > 0) & (k < nk - 1))
        def _():
            acc[...] += product(a_ref, b_ref)

        @pl.when(k == nk - 1)
        def _():
            epilogue(acc[...] + product(a_ref, b_ref), extra, outs)

    return _ordered_call(
        body, name=name, grid=grid, in_specs=in_specs, out_specs=out_specs, out_shape=out_shape,
        scratch_shapes=[pltpu.VMEM(acc_shape, F32)] if nk > 1 else [],
        compiler_params=_cp("parallel", "parallel", "arbitrary"),
    )(*operands)


def _ep_store(acc, extra, outs):
    outs[0][...] = acc.astype(outs[0].dtype)


def _ep_residual(acc, extra, outs):
    outs[0][...] = extra[0][...] + acc


def _ep_up(acc, extra, outs):
    outs[0][...] = acc.astype(BF16)
    r = jnp.maximum(acc, 0.0)
    outs[1][...] = (r * r).astype(BF16)


def _ep_dup(acc, extra, outs):
    outs[0][...] = (acc * (2.0 * jnp.maximum(extra[0][...].astype(F32), 0.0))).astype(BF16)


def _mm_act_w(name, a, wg, col_sharded, epilogue, extra=(), out_dtypes=(F32,)):
    m, kdim = a.shape
    _, r, c = wg.shape
    tm = _tile(m, 1024)
    if col_sharded:
        n = N_CHIPS * c
        tn = _tile(c, 1024)
        tk = _tile(kdim, TILE_K)
        per = c // tn
        b_spec = pl.BlockSpec((None, tk, tn), lambda i, j, k: (j // per, k, j % per))
    elif N_CHIPS * r <= TILE_K:
        n = c
        tm, tn, tk = _tile(m, 512), n, kdim
        b_spec = pl.BlockSpec((N_CHIPS, r, tn), lambda i, j, k: (0, 0, j))
    else:
        n = c
        tn = _tile(n, 1024)
        tk = _tile(r, TILE_K)
        per = r // tk
        b_spec = pl.BlockSpec((None, tk, tn), lambda i, j, k: (k // per, k % per, j))
    grid = (m // tm, n // tn, kdim // tk)
    o_spec = pl.BlockSpec((tm, tn), lambda i, j, k: (i, j))
    in_specs = [pl.BlockSpec((tm, tk), lambda i, j, k: (i, k)), b_spec] + [o_spec] * len(extra)
    return _matmul(name, (a, wg) + tuple(extra), in_specs,
                   tuple(jax.ShapeDtypeStruct((m, n), d) for d in out_dtypes),
                   tuple(o_spec for _ in out_dtypes), grid, NN, (tm, tn), epilogue)


def _mm_act_wt(name, a, wg, col_sharded, epilogue, extra=(), out_dtypes=(F32,)):
    m, kdim = a.shape
    _, r, c = wg.shape
    tm = _tile(m, 1024)
    split_k = False
    if col_sharded and N_CHIPS * c <= 2 * TILE_K:
        n = r
        tm, tn, tk, split_k = _tile(m, 512), n, kdim, True
        b_spec = pl.BlockSpec((N_CHIPS, tn, c), lambda i, j, k: (0, j, 0))
    elif col_sharded:
        n = r
        tn = _tile(n, 1024)
        tk = _tile(c, TILE_K)
        per = c // tk
        b_spec = pl.BlockSpec((None, tn, tk), lambda i, j, k: (k // per, j, k % per))
    elif N_CHIPS * r <= TILE_K:
        n = N_CHIPS * r
        tm, tn, tk = _tile(m, 512), n, _tile(c, TILE_K)
        b_spec = pl.BlockSpec((N_CHIPS, r, tk), lambda i, j, k: (0, 0, k))
    else:
        n = N_CHIPS * r
        tn = _tile(r, 1024)
        tk = _tile(c, TILE_K)
        per = r // tn
        b_spec = pl.BlockSpec((None, tn, tk), lambda i, j, k: (j // per, j % per, k))
    grid = (m // tm, n // tn, kdim // tk)
    o_spec = pl.BlockSpec((tm, tn), lambda i, j, k: (i, j))
    in_specs = [pl.BlockSpec((tm, tk), lambda i, j, k: (i, k)), b_spec] + [o_spec] * len(extra)
    return _matmul(name, (a, wg) + tuple(extra), in_specs,
                   tuple(jax.ShapeDtypeStruct((m, n), d) for d in out_dtypes),
                   tuple(o_spec for _ in out_dtypes), grid, NT, (tm, tn), epilogue, split_k)


def _mm_wgrad(name, a, g, col_sharded, c):
    s, kdim = a.shape
    _, n = g.shape
    ts = _tile(s, TILE_K)
    if col_sharded:
        r = kdim
        tm = _tile(kdim, 1024)
        tn = _tile(c, 1024)
        per = c // tn
        o_spec = pl.BlockSpec((None, tm, tn), lambda i, j, k: (j // per, i, j % per))
    else:
        r = kdim // N_CHIPS
        tm = _tile(r, 512)
        tn = _tile(c, 2048)
        per = r // tm
        o_spec = pl.BlockSpec((None, tm, tn), lambda i, j, k: (i // per, i % per, j))
    grid = (kdim // tm, n // tn, s // ts)
    in_specs = [pl.BlockSpec((ts, tm), lambda i, j, k: (k, i)), pl.BlockSpec((ts, tn), lambda i, j, k: (k, j))]
    return _matmul(name, (a, g), in_specs, (jax.ShapeDtypeStruct((N_CHIPS, r, c), BF16),), (o_spec,),
                   grid, TN, (tm, tn), _ep_store)[0]


def _rms_fwd(name, x, g):
    s, d = x.shape
    tb = _tile(s, 256)

    def body(x_ref, g_ref, o_ref):
        xv = x_ref[...]
        r = lax.rsqrt(jnp.mean(xv * xv, axis=-1, keepdims=True) + EPS)
        o_ref[...] = (xv * r * g_ref[...]).astype(BF16)

    return _ordered_call(
        body, name=name, grid=(s // tb,),
        in_specs=[pl.BlockSpec((tb, d), lambda i: (i, 0)), pl.BlockSpec((1, d), lambda i: (0, 0))],
        out_specs=pl.BlockSpec((tb, d), lambda i: (i, 0)),
        out_shape=jax.ShapeDtypeStruct((s, d), BF16), compiler_params=_cp("parallel"),
    )(x, g.reshape(1, d))


def _rms_bwd(name, dh, x, g, dres):
    s, d = x.shape
    tb = _tile(s, 256)

    def body(dh_ref, x_ref, g_ref, dres_ref, dx_ref, dxb_ref, dg_ref):
        i = pl.program_id(0)
        xv = x_ref[...]
        r = lax.rsqrt(jnp.mean(xv * xv, axis=-1, keepdims=True) + EPS)
        xhat = xv * r
        dhv = dh_ref[...]
        dxhat = dhv * g_ref[...]
        dx = dres_ref[...] + r * (dxhat - xhat * jnp.mean(dxhat * xhat, axis=-1, keepdims=True))
        dx_ref[...] = dx
        dxb_ref[...] = dx.astype(BF16)

        @pl.when(i == 0)
        def _():
            dg_ref[...] = jnp.zeros_like(dg_ref)

        dg_ref[...] += _colsum(dhv * xhat)

    row = pl.BlockSpec((tb, d), lambda i: (i, 0))
    vec = pl.BlockSpec((1, d), lambda i: (0, 0))
    return _ordered_call(
        body, name=name, grid=(s // tb,), in_specs=[row, row, vec, row], out_specs=(row, row, vec),
        out_shape=(jax.ShapeDtypeStruct((s, d), F32), jax.ShapeDtypeStruct((s, d), BF16),
                   jax.ShapeDtypeStruct((1, d), F32)),
        compiler_params=_cp("arbitrary"),
    )(dh, x, g.reshape(1, d), dres)


def _loss_head(y, t):
    s, d = y.shape
    tb = _tile(s, 256)

    def body(y_ref, t_ref, dy_ref, dyb_ref, loss_ref, acc):
        i = pl.program_id(0)
        e = y_ref[...] - t_ref[...]
        dy = e * (1.0 / d)
        dy_ref[...] = dy
        dyb_ref[...] = dy.astype(BF16)

        @pl.when(i == 0)
        def _():
            acc[...] = jnp.zeros_like(acc)

        acc[...] += _colsum(e * e)

        @pl.when(i == pl.num_programs(0) - 1)
        def _():
            loss_ref[...] = jnp.sum(acc[...], axis=-1, keepdims=True) * (0.5 / d)

    row = pl.BlockSpec((tb, d), lambda i: (i, 0))
    return _ordered_call(
        body, name="loss_head", grid=(s // tb,), in_specs=[row, row],
        out_specs=(row, row, pl.BlockSpec((1, 1), lambda i: (0, 0))),
        out_shape=(jax.ShapeDtypeStruct((s, d), F32), jax.ShapeDtypeStruct((s, d), BF16),
                   jax.ShapeDtypeStruct((1, 1), F32)),
        scratch_shapes=[pltpu.VMEM((1, d), F32)], compiler_params=_cp("arbitrary"),
    )(y, t)


def _mixnorm_fwd(name, ya, yc, ys, g):
    s, aw = ya.shape
    cw, sw = yc.shape[1], ys.shape[1]
    d = aw + cw + sw
    tb = _tile(s, 256)

    def body(ya_ref, yc_ref, ys_ref, g_ref, o_ref):
        off = 0
        for ref, w in ((ya_ref, aw), (yc_ref, cw), (ys_ref, sw)):
            v = ref[...]
            r = lax.rsqrt(jnp.mean(v * v, axis=-1, keepdims=True) + EPS)
            o_ref[:, off:off + w] = (v * r * g_ref[:, off:off + w]).astype(BF16)
            off += w

    def row(w):
        return pl.BlockSpec((tb, w), lambda i: (i, 0))

    return _ordered_call(
        body, name=name, grid=(s // tb,),
        in_specs=[row(aw), row(cw), row(sw), pl.BlockSpec((1, d), lambda i: (0, 0))], out_specs=row(d),
        out_shape=jax.ShapeDtypeStruct((s, d), BF16), compiler_params=_cp("parallel"),
    )(ya, yc, ys, g.reshape(1, d))


def _mixnorm_bwd(name, dmix, ya, yc, ys, g):
    s, aw = ya.shape
    cw, sw = yc.shape[1], ys.shape[1]
    d = aw + cw + sw
    tb = _tile(s, 256)

    def body(dm_ref, ya_ref, yc_ref, ys_ref, g_ref, dya_ref, dyc_ref, dys_ref, dg_ref):
        i = pl.program_id(0)

        @pl.when(i == 0)
        def _():
            dg_ref[...] = jnp.zeros_like(dg_ref)

        off = 0
        for ref, dref, w in ((ya_ref, dya_ref, aw), (yc_ref, dyc_ref, cw), (ys_ref, dys_ref, sw)):
            v = ref[...]
            r = lax.rsqrt(jnp.mean(v * v, axis=-1, keepdims=True) + EPS)
            vhat = v * r
            dm = dm_ref[:, off:off + w]
            dvhat = dm * g_ref[:, off:off + w]
            dref[...] = r * (dvhat - vhat * jnp.mean(dvhat * vhat, axis=-1, keepdims=True))
            dg_ref[:, off:off + w] += _colsum(dm * vhat)
            off += w

    def row(w):
        return pl.BlockSpec((tb, w), lambda i: (i, 0))

    vec = pl.BlockSpec((1, d), lambda i: (0, 0))
    return _ordered_call(
        body, name=name, grid=(s // tb,), in_specs=[row(d), row(aw), row(cw), row(sw), vec],
        out_specs=(row(aw), row(cw), row(sw), vec),
        out_shape=(jax.ShapeDtypeStruct((s, aw), F32), jax.ShapeDtypeStruct((s, cw), F32),
                   jax.ShapeDtypeStruct((s, sw), F32), jax.ShapeDtypeStruct((1, d), F32)),
        compiler_params=_cp("arbitrary"),
    )(dmix, ya, yc, ys, g.reshape(1, d))


def _head_rms(x):
    r = lax.rsqrt(jnp.mean(x * x, axis=-1, keepdims=True) + EPS)
    return x * r, r


def _attn_mask(n):
    qi = lax.broadcasted_iota(jnp.int32, (GQA * WINDOW, 2 * WINDOW), 0) & (WINDOW - 1)
    sj = lax.broadcasted_iota(jnp.int32, (GQA * WINDOW, 2 * WINDOW), 1)
    rel = qi + WINDOW - sj
    return (rel >= 0) & (rel < WINDOW) & ((sj >= WINDOW) | (n > 0))


def _attn_specs(nq, nkv, nb):
    qspec = pl.BlockSpec((nq, WINDOW, HEAD_DIM), lambda n: (0, n, 0))
    cur = pl.BlockSpec((nkv, WINDOW, HEAD_DIM), lambda n: (0, n, 0))
    prev = pl.BlockSpec((nkv, WINDOW, HEAD_DIM), lambda n: (0, jnp.maximum(n - 1, 0), 0))
    nxt = pl.BlockSpec((nkv, WINDOW, HEAD_DIM), lambda n: (0, jnp.minimum(n + 1, nb - 1), 0))
    gain = pl.BlockSpec((1, HEAD_DIM), lambda n: (0, 0))
    sink = pl.BlockSpec((nq, 1, 128), lambda n: (0, 0, 0))
    return qspec, cur, prev, nxt, gain, sink


def _group_sinks(s_ref, g):
    return jnp.concatenate([jnp.broadcast_to(s_ref[g * GQA + i][:, :1], (WINDOW, 1)) for i in range(GQA)], axis=0)


def _attn_probs(qn_b, kn_b, valid, sink):
    logits = _dot(qn_b, kn_b, NT) * (HEAD_DIM ** -0.5)
    logits = jnp.where(valid, logits, NEG_INF)
    m = jnp.maximum(jnp.max(logits, axis=-1, keepdims=True), sink)
    p = jnp.exp(logits - m)
    es = jnp.exp(sink - m)
    denom = jnp.sum(p, axis=-1, keepdims=True) + es
    return p / denom, es / denom


def _attn_fwd(name, q, k, v, gq, gk, sinks_b):
    nq, s, _ = q.shape
    nkv = k.shape[0]
    nb = s // WINDOW
    qspec, cur, prev, _, gain, sink = _attn_specs(nq, nkv, nb)

    def body(q_ref, kc_ref, kp_ref, vc_ref, vp_ref, gq_ref, gk_ref, s_ref, o_ref):
        gkv = gk_ref[...]
        valid = _attn_mask(pl.program_id(0))
        for g in range(nkv):
            kn_b = jnp.concatenate([_head_rms(kp_ref[g])[0] * gkv, _head_rms(kc_ref[g])[0] * gkv],
                                   axis=0).astype(BF16)
            vv_b = jnp.concatenate([vp_ref[g], vc_ref[g]], axis=0).astype(BF16)
            heads = pl.ds(g * GQA, GQA)
            q4 = q_ref[heads].reshape(GQA * WINDOW, HEAD_DIM)
            qn_b = (_head_rms(q4)[0] * gq_ref[...]).astype(BF16)
            probs, _ = _attn_probs(qn_b, kn_b, valid, _group_sinks(s_ref, g))
            o_ref[heads] = _dot(probs.astype(BF16), vv_b).reshape(GQA, WINDOW, HEAD_DIM)

    return _ordered_call(
        body, name=name, grid=(nb,), in_specs=[qspec, cur, prev, cur, prev, gain, gain, sink], out_specs=qspec,
        out_shape=jax.ShapeDtypeStruct((nq, s, HEAD_DIM), F32), compiler_params=_cp("parallel"),
    )(q, k, k, v, v, gq.reshape(1, HEAD_DIM), gk.reshape(1, HEAD_DIM), sinks_b)


def _attn_bwd(name, q, k, v, gq, gk, sinks_b, do):
    nq, s, _ = q.shape
    nkv = k.shape[0]
    nb = s // WINDOW
    qspec, cur, prev, _, gain, sink = _attn_specs(nq, nkv, nb)

    def body(q_ref, kc_ref, kp_ref, vc_ref, vp_ref, gq_ref, gk_ref, s_ref, do_ref,
             dq_ref, dkc_ref, dkp_ref, dvc_ref, dvp_ref, dgq_ref, ds_ref):
        n = pl.program_id(0)

        @pl.when(n == 0)
        def _():
            dgq_ref[...] = jnp.zeros_like(dgq_ref)
            ds_ref[...] = jnp.zeros_like(ds_ref)

        gkv = gk_ref[...]
        gqv = gq_ref[...]
        valid = _attn_mask(n)
        dgq = jnp.zeros((1, HEAD_DIM), F32)
        for g in range(nkv):
            kn_b = jnp.concatenate([_head_rms(kp_ref[g])[0] * gkv, _head_rms(kc_ref[g])[0] * gkv],
                                   axis=0).astype(BF16)
            vv_b = jnp.concatenate([vp_ref[g], vc_ref[g]], axis=0).astype(BF16)
            heads = pl.ds(g * GQA, GQA)
            qhat, r = _head_rms(q_ref[heads].reshape(GQA * WINDOW, HEAD_DIM))
            qn_b = (qhat * gqv).astype(BF16)
            probs, psink = _attn_probs(qn_b, kn_b, valid, _group_sinks(s_ref, g))
            do_b = do_ref[heads].reshape(GQA * WINDOW, HEAD_DIM).astype(BF16)
            dp = _dot(do_b, vv_b, NT)
            delta = jnp.sum(probs * dp, axis=-1, keepdims=True)
            dl_b = (probs * (dp - delta) * (HEAD_DIM ** -0.5)).astype(BF16)
            sink_term = psink * delta
            for i in range(GQA):
                ds_ref[g * GQA + i] += jnp.broadcast_to(
                    -jnp.sum(sink_term[i * WINDOW:(i + 1) * WINDOW], axis=0, keepdims=True), (1, 128))
            dqn = _dot(dl_b, kn_b)
            dkn = _dot(dl_b, qn_b, TN)
            dvv = _dot(probs.astype(BF16), do_b, TN)
            dgq += _colsum(dqn * qhat)
            dqhat = dqn * gqv
            dq_ref[heads] = (r * (dqhat - qhat * jnp.mean(dqhat * qhat, axis=-1, keepdims=True))).astype(
                BF16).reshape(GQA, WINDOW, HEAD_DIM)
            dkp_ref[g] = dkn[:WINDOW]
            dkc_ref[g] = dkn[WINDOW:]
            dvp_ref[g] = dvv[:WINDOW]
            dvc_ref[g] = dvv[WINDOW:]
        dgq_ref[...] += dgq

    kv_shape = jax.ShapeDtypeStruct((nkv, s, HEAD_DIM), F32)
    return _ordered_call(
        body, name=name, grid=(nb,), in_specs=[qspec, cur, prev, cur, prev, gain, gain, sink, qspec],
        out_specs=(qspec, cur, cur, cur, cur, gain, sink),
        out_shape=(jax.ShapeDtypeStruct((nq, s, HEAD_DIM), BF16), kv_shape, kv_shape, kv_shape, kv_shape,
                   jax.ShapeDtypeStruct((1, HEAD_DIM), F32), jax.ShapeDtypeStruct((nq, 1, 128), F32)),
        compiler_params=_cp("arbitrary"),
    )(q, k, k, v, v, gq.reshape(1, HEAD_DIM), gk.reshape(1, HEAD_DIM), sinks_b, do)


def _attn_bwd_kv(name, k, gk, dkc, dkp, dvc, dvp):
    nkv, s, _ = k.shape
    nb = s // WINDOW
    _, cur, _, nxt, gain, _ = _attn_specs(GQA * nkv, nkv, nb)

    def body(k_ref, gk_ref, dkc_ref, dkp_ref, dvc_ref, dvp_ref, dk_ref, dv_ref, dgk_ref):
        n = pl.program_id(0)

        @pl.when(n == 0)
        def _():
            dgk_ref[...] = jnp.zeros_like(dgk_ref)

        has_next = n < nb - 1
        dgk = jnp.zeros((1, HEAD_DIM), F32)
        for g in range(nkv):
            dkn = dkc_ref[g] + jnp.where(has_next, dkp_ref[g], 0.0)
            dv_ref[g] = (dvc_ref[g] + jnp.where(has_next, dvp_ref[g], 0.0)).astype(BF16)
            khat, r = _head_rms(k_ref[g])
            dgk += _colsum(dkn * khat)
            dkhat = dkn * gk_ref[...]
            dk_ref[g] = (r * (dkhat - khat * jnp.mean(dkhat * khat, axis=-1, keepdims=True))).astype(BF16)
        dgk_ref[...] += dgk

    kv_shape = jax.ShapeDtypeStruct((nkv, s, HEAD_DIM), BF16)
    return _ordered_call(
        body, name=name, grid=(nb,), in_specs=[cur, gain, cur, nxt, cur, nxt], out_specs=(cur, cur, gain),
        out_shape=(kv_shape, kv_shape, jax.ShapeDtypeStruct((1, HEAD_DIM), F32)),
        compiler_params=_cp("arbitrary"),
    )(k, gk.reshape(1, HEAD_DIM), dkc, dkp, dvc, dvp)


SUBLANES = 8


def _fill_shifts(buf, shifts, tb):
    rows = tb + HALO - SUBLANES
    for b in range(1, SUBLANES):
        shifts[b - 1, pl.ds(0, rows), :] = buf[pl.ds(b, rows), :]


def _window(buf, shifts, off, tb):
    b = off % SUBLANES
    return buf[pl.ds(off, tb), :] if b == 0 else shifts[b - 1, pl.ds(off - b, tb), :]


def _conv_recompute(i, a_ref, gt_ref, ap_ref, gp_ref, w_ref, b_ref, hbuf, shifts, tb):
    hbuf[pl.ds(HALO, tb), :] = a_ref[...] * _sigmoid(gt_ref[...])
    tail = ap_ref[pl.ds(tb - HALO, HALO), :] * _sigmoid(gp_ref[pl.ds(tb - HALO, HALO), :])
    hbuf[pl.ds(0, HALO), :] = jnp.where(i > 0, tail, 0.0)
    _fill_shifts(hbuf, shifts, tb)
    acc = jnp.broadcast_to(b_ref[...], a_ref.shape)
    for kk in range(CONV_KERNEL):
        acc = acc + w_ref[pl.ds(kk, 1), :] * _window(hbuf, shifts, HALO - (CONV_KERNEL - 1) + kk, tb)
    return acc


def _layer_norm_stats(c):
    mu = jnp.mean(c, axis=-1, keepdims=True)
    xc = c - mu
    r = lax.rsqrt(jnp.mean(xc * xc, axis=-1, keepdims=True) + EPS)
    return xc * r, r


def _conv_specs(s, cw, tb, a_blk):
    cur = lambda off: pl.BlockSpec((tb, cw), lambda i: (i, a_blk + off))
    prev = lambda off: pl.BlockSpec((tb, cw), lambda i: (jnp.maximum(i - 1, 0), a_blk + off))
    wspec = pl.BlockSpec((HALO, cw), lambda i: (0, 0))
    vec = pl.BlockSpec((1, cw), lambda i: (0, 0))
    row = pl.BlockSpec((tb, cw), lambda i: (i, 0))
    return cur, prev, wspec, vec, row


def _conv_fwd(name, proj, a_blk, w, b, lg, lb):
    s = proj.shape[0]
    cw = w.shape[1]
    tb = _tile(s, 256)
    cur, prev, wspec, vec, row = _conv_specs(s, cw, tb, a_blk)

    def body(a_ref, gt_ref, ap_ref, gp_ref, w_ref, b_ref, lg_ref, lb_ref, y_ref, hbuf, shifts):
        c = _conv_recompute(pl.program_id(0), a_ref, gt_ref, ap_ref, gp_ref, w_ref, b_ref, hbuf, shifts, tb)
        chat, _ = _layer_norm_stats(c)
        z = chat * lg_ref[...] + lb_ref[...]
        y_ref[...] = z * _sigmoid(z)

    return _ordered_call(
        body, name=name, grid=(s // tb,), in_specs=[cur(0), cur(1), prev(0), prev(1), wspec, vec, vec, vec],
        out_specs=row, out_shape=jax.ShapeDtypeStruct((s, cw), F32),
        scratch_shapes=[pltpu.VMEM((tb + HALO, cw), F32), pltpu.VMEM((SUBLANES - 1, tb + HALO, cw), F32)],
        compiler_params=_cp("arbitrary"),
    )(proj, proj, proj, proj, w, b, lg, lb)


def _conv_bwd1(name, proj, a_blk, w, b, lg, lb, dy):
    s = proj.shape[0]
    cw = w.shape[1]
    tb = _tile(s, 256)
    cur, prev, wspec, vec, row = _conv_specs(s, cw, tb, a_blk)

    def body(a_ref, gt_ref, ap_ref, gp_ref, w_ref, b_ref, lg_ref, lb_ref, dy_ref,
             dc_ref, dw_ref, db_ref, dlg_ref, dlb_ref, hbuf, shifts):
        i = pl.program_id(0)

        @pl.when(i == 0)
        def _():
            dw_ref[...] = jnp.zeros_like(dw_ref)
            db_ref[...] = jnp.zeros_like(db_ref)
            dlg_ref[...] = jnp.zeros_like(dlg_ref)
            dlb_ref[...] = jnp.zeros_like(dlb_ref)

        c = _conv_recompute(i, a_ref, gt_ref, ap_ref, gp_ref, w_ref, b_ref, hbuf, shifts, tb)
        chat, r = _layer_norm_stats(c)
        z = chat * lg_ref[...] + lb_ref[...]
        sg = _sigmoid(z)
        dz = dy_ref[...] * (sg + z * sg * (1.0 - sg))
        dlg_ref[...] += _colsum(dz * chat)
        dlb_ref[...] += _colsum(dz)
        dzg = dz * lg_ref[...]
        dc = r * (dzg - jnp.mean(dzg, axis=-1, keepdims=True) - chat * jnp.mean(dzg * chat, axis=-1, keepdims=True))
        dc_ref[...] = dc
        db_ref[...] += _colsum(dc)
        for kk in range(CONV_KERNEL):
            dw_ref[pl.ds(kk, 1), :] += _colsum(dc * _window(hbuf, shifts, HALO - (CONV_KERNEL - 1) + kk, tb))

    return _ordered_call(
        body, name=name, grid=(s // tb,), in_specs=[cur(0), cur(1), prev(0), prev(1), wspec, vec, vec, vec, row],
        out_specs=(row, wspec, vec, vec, vec),
        out_shape=(jax.ShapeDtypeStruct((s, cw), F32), jax.ShapeDtypeStruct((HALO, cw), F32),
                   jax.ShapeDtypeStruct((1, cw), F32), jax.ShapeDtypeStruct((1, cw), F32),
                   jax.ShapeDtypeStruct((1, cw), F32)),
        scratch_shapes=[pltpu.VMEM((tb + HALO, cw), F32), pltpu.VMEM((SUBLANES - 1, tb + HALO, cw), F32)],
        compiler_params=_cp("arbitrary"),
    )(proj, proj, proj, proj, w, b, lg, lb, dy)


def _conv_bwd2(name, proj, a_blk, w, dc):
    s = proj.shape[0]
    cw = w.shape[1]
    tb = _tile(s, 256)
    nblk = s // tb
    cur, _, wspec, _, row = _conv_specs(s, cw, tb, a_blk)
    nxt = pl.BlockSpec((tb, cw), lambda i: (jnp.minimum(i + 1, nblk - 1), 0))

    def body(a_ref, gt_ref, w_ref, dc_ref, dn_ref, o_ref, dbuf, shifts):
        i = pl.program_id(0)
        dbuf[pl.ds(0, tb), :] = dc_ref[...]
        dbuf[pl.ds(tb, HALO), :] = jnp.where(i < nblk - 1, dn_ref[pl.ds(0, HALO), :], 0.0)
        _fill_shifts(dbuf, shifts, tb)
        dh = jnp.zeros((tb, cw), F32)
        for kk in range(CONV_KERNEL):
            dh = dh + w_ref[pl.ds(kk, 1), :] * _window(dbuf, shifts, CONV_KERNEL - 1 - kk, tb)
        sg = _sigmoid(gt_ref[...])
        o_ref[:, 0:cw] = (dh * sg).astype(BF16)
        o_ref[:, cw:2 * cw] = (dh * a_ref[...] * sg * (1.0 - sg)).astype(BF16)

    return _ordered_call(
        body, name=name, grid=(nblk,), in_specs=[cur(0), cur(1), wspec, row, nxt],
        out_specs=pl.BlockSpec((tb, 2 * cw), lambda i: (i, 0)), out_shape=jax.ShapeDtypeStruct((s, 2 * cw), BF16),
        scratch_shapes=[pltpu.VMEM((tb + HALO, cw), F32), pltpu.VMEM((SUBLANES - 1, tb + HALO, cw), F32)],
        compiler_params=_cp("arbitrary"),
    )(proj, proj, w, dc, dc)


def _sgu_common(v_ref, lg_ref, lb_ref, w_ref, bexp_ref, sw):
    vhat, r = _layer_norm_stats(v_ref[...])
    vn_b = (vhat * lg_ref[...] + lb_ref[...]).astype(BF16)
    ii = lax.broadcasted_iota(jnp.int32, (WINDOW, WINDOW), 0)
    jj = lax.broadcasted_iota(jnp.int32, (WINDOW, WINDOW), 1)
    tril = jj <= ii
    head_of = lax.broadcasted_iota(jnp.int32, (WINDOW, sw), 1) // HEAD_DIM
    wts = [jnp.where(tril, w_ref[h], 0.0).astype(BF16) for h in range(sw // HEAD_DIM)]
    sv = bexp_ref[...]
    for h, wt in enumerate(wts):
        sv = sv + jnp.where(head_of == h, _dot(wt, vn_b), 0.0)
    return vhat, r, vn_b, tril, head_of, wts, sv


def _sgu_specs(sw, u_blk):
    nh = sw // HEAD_DIM
    u = pl.BlockSpec((WINDOW, sw), lambda n: (n, u_blk))
    v = pl.BlockSpec((WINDOW, sw), lambda n: (n, u_blk + 1))
    vec = pl.BlockSpec((1, sw), lambda n: (0, 0))
    wspec = pl.BlockSpec((nh, WINDOW, WINDOW), lambda n: (0, 0, 0))
    bspec = pl.BlockSpec((WINDOW, sw), lambda n: (0, 0))
    row = pl.BlockSpec((WINDOW, sw), lambda n: (n, 0))
    return u, v, vec, wspec, bspec, row


def _sgu_fwd(name, proj, u_blk, lg, lb, w, bexp):
    s = proj.shape[0]
    sw = lg.shape[1]
    u, v, vec, wspec, bspec, row = _sgu_specs(sw, u_blk)

    def body(u_ref, v_ref, lg_ref, lb_ref, w_ref, bexp_ref, y_ref):
        sv = _sgu_common(v_ref, lg_ref, lb_ref, w_ref, bexp_ref, sw)[-1]
        y_ref[...] = u_ref[...] * sv

    return _ordered_call(
        body, name=name, grid=(s // WINDOW,), in_specs=[u, v, vec, vec, wspec, bspec], out_specs=row,
        out_shape=jax.ShapeDtypeStruct((s, sw), F32), compiler_params=_cp("parallel"),
    )(proj, proj, lg, lb, w, bexp)


def _sgu_bwd(name, proj, u_blk, lg, lb, w, bexp, dy):
    s = proj.shape[0]
    sw = lg.shape[1]
    nh = sw // HEAD_DIM
    u, v, vec, wspec, bspec, row = _sgu_specs(sw, u_blk)
    dbspec = pl.BlockSpec((nh, WINDOW), lambda n: (0, 0))

    def body(u_ref, v_ref, lg_ref, lb_ref, w_ref, bexp_ref, dy_ref, o_ref, dw_ref, db_ref, dlg_ref, dlb_ref):
        n = pl.program_id(0)

        @pl.when(n == 0)
        def _():
            dw_ref[...] = jnp.zeros_like(dw_ref)
            db_ref[...] = jnp.zeros_like(db_ref)
            dlg_ref[...] = jnp.zeros_like(dlg_ref)
            dlb_ref[...] = jnp.zeros_like(dlb_ref)

        vhat, r, vn_b, tril, head_of, wts, sv = _sgu_common(v_ref, lg_ref, lb_ref, w_ref, bexp_ref, sw)
        dyv = dy_ref[...]
        o_ref[:, 0:sw] = (dyv * sv).astype(BF16)
        ds = dyv * u_ref[...]
        dvn = jnp.zeros((WINDOW, sw), F32)
        for h, wt in enumerate(wts):
            dsm_b = jnp.where(head_of == h, ds, 0.0).astype(BF16)
            dvn = dvn + _dot(wt, dsm_b, TN)
            dw_ref[h] += jnp.where(tril, _dot(dsm_b, vn_b, NT), 0.0)
        hmask = (lax.broadcasted_iota(jnp.int32, (nh, sw), 1) // HEAD_DIM
                 == lax.broadcasted_iota(jnp.int32, (nh, sw), 0)).astype(F32)
        db_ref[...] += lax.dot_general(hmask, ds, NT, precision=lax.Precision.HIGHEST, preferred_element_type=F32)
        dlg_ref[...] += _colsum(dvn * vhat)
        dlb_ref[...] += _colsum(dvn)
        dvg = dvn * lg_ref[...]
        dv = r * (dvg - jnp.mean(dvg, axis=-1, keepdims=True) - vhat * jnp.mean(dvg * vhat, axis=-1, keepdims=True))
        o_ref[:, sw:2 * sw] = dv.astype(BF16)

    return _ordered_call(
        body, name=name, grid=(s // WINDOW,), in_specs=[u, v, vec, vec, wspec, bspec, row],
        out_specs=(pl.BlockSpec((WINDOW, 2 * sw), lambda n: (n, 0)), wspec, dbspec, vec, vec),
        out_shape=(jax.ShapeDtypeStruct((s, 2 * sw), BF16), jax.ShapeDtypeStruct((nh, WINDOW, WINDOW), F32),
                   jax.ShapeDtypeStruct((nh, WINDOW), F32), jax.ShapeDtypeStruct((1, sw), F32),
                   jax.ShapeDtypeStruct((1, sw), F32)),
        compiler_params=_cp("arbitrary"),
    )(proj, proj, lg, lb, w, bexp, dy)


def _adamw(name, w, g, m, v):
    rows, cols = w.shape
    tr = _tile(rows, 256)

    def body(w_ref, g_ref, m_ref, v_ref, d_ref, nm_ref, nv_ref):
        gv = g_ref[...]
        mv = ADAM_B1 * m_ref[...] + (1.0 - ADAM_B1) * gv
        vv = ADAM_B2 * v_ref[...] + (1.0 - ADAM_B2) * (gv * gv)
        m_hat = mv / (1.0 - ADAM_B1 ** ADAM_STEP)
        v_hat = vv / (1.0 - ADAM_B2 ** ADAM_STEP)
        d_ref[...] = -ADAM_LR * (m_hat / (jnp.sqrt(v_hat) + ADAM_EPS) + ADAM_WD * w_ref[...])
        nm_ref[...] = mv
        nv_ref[...] = vv

    spec = pl.BlockSpec((tr, cols), lambda i: (i, 0))
    shape = jax.ShapeDtypeStruct((rows, cols), F32)
    return _ordered_call(
        body, name=name, grid=(rows // tr,), in_specs=[spec] * 4, out_specs=(spec,) * 3, out_shape=(shape,) * 3,
        compiler_params=_cp("parallel"),
    )(w, g, m, v)


def _route():
    x, y, c = lax.axis_index("x"), lax.axis_index("y"), lax.axis_index("c")
    n1 = (jnp.where(c == 0, 1 - x, x), jnp.where(c == 0, y, 1 - y))
    n2 = (jnp.where(c == 0, x, 1 - x), jnp.where(c == 0, 1 - y, y))
    return x, y, c, n1, n2, (1 - x, 1 - y)


def _cidx(chip):
    return 2 * chip[0] + chip[1]


def _remote(src, dst, sems, k, device):
    send_sems, recv_sems = sems
    return pltpu.make_async_remote_copy(src_ref=src, dst_ref=dst, send_sem=send_sems.at[k], recv_sem=recv_sems.at[k],
                                        device_id=device, device_id_type=MESH)


def _exchange(name, bufs, n_sems, build):
    n = len(bufs)

    def body(*refs):
        cps = build(refs[n:2 * n], (refs[2 * n], refs[2 * n + 1]))
        for cp in cps:
            cp.start()
        for cp in cps:
            cp.wait()

    return _ordered_call(
        body, name=name, in_specs=[ANY] * n, out_specs=tuple(ANY for _ in range(n)),
        out_shape=tuple(jax.ShapeDtypeStruct(b.shape, b.dtype) for b in bufs),
        input_output_aliases={i: i for i in range(n)},
        scratch_shapes=[pltpu.SemaphoreType.DMA((n_sems,)), pltpu.SemaphoreType.DMA((n_sems,))],
        compiler_params=pltpu.CompilerParams(has_side_effects=True),
    )(*bufs)


HBM_SPEC = pl.BlockSpec(memory_space=pltpu.HBM)
SEM_SPEC = pl.BlockSpec(memory_space=pltpu.SEMAPHORE)
DATAFLOW = pltpu.SideEffectType.DATAFLOW_SIDE_EFFECTING


def _exchange_start(name, bufs, n_sems, build):
    n = len(bufs)

    def body(*refs):
        for cp in build(refs[:n], (refs[n], refs[n + 1])):
            cp.start()
        refs[-1][...] = jnp.zeros_like(refs[-1])

    out = _ordered_call(
        body, name=name,
        out_shape=(pltpu.SemaphoreType.DMA((n_sems,)), pltpu.SemaphoreType.DMA((n_sems,)))
        + tuple(pltpu.HBM(b.shape, b.dtype) for b in bufs) + (jax.ShapeDtypeStruct((8, 128), F32),),
        in_specs=[HBM_SPEC] * n,
        out_specs=(SEM_SPEC, SEM_SPEC) + (HBM_SPEC,) * n + (pl.BlockSpec(memory_space=pltpu.VMEM),),
        input_output_aliases={i: 2 + i for i in range(n)},
        compiler_params=pltpu.CompilerParams(has_side_effects=DATAFLOW),
    )(*[pltpu.with_memory_space_constraint(b, pltpu.HBM) for b in bufs])
    return dict(name=name, send=out[0], recv=out[1], bufs=list(out[2:2 + n]), token=out[-1], build=build)


def _exchange_wait(handle):
    n = len(handle["bufs"])

    def body(*refs):
        for cp in handle["build"](refs[:n], (refs[n], refs[n + 1])):
            cp.wait_send()
            cp.wait_recv()

    return list(_ordered_call(
        body, name=handle["name"] + "_wait", out_shape=tuple(pltpu.HBM(b.shape, b.dtype) for b in handle["bufs"]),
        in_specs=[HBM_SPEC] * n + [SEM_SPEC, SEM_SPEC], out_specs=(HBM_SPEC,) * n,
        input_output_aliases={i: i for i in range(n)},
        compiler_params=pltpu.CompilerParams(has_side_effects=DATAFLOW),
    )(*handle["bufs"], handle["send"], handle["recv"]))


def _cast_place(name, w, l, me_idx, dtype):
    _, r, c = w.shape
    tr = _tile(r, 512)

    def body(me_ref, w_ref, o_ref):
        o_ref[...] = w_ref[...].astype(dtype)

    grid_spec = pltpu.PrefetchScalarGridSpec(
        num_scalar_prefetch=1, grid=(r // tr,),
        in_specs=[pl.BlockSpec((None, tr, c), lambda i, me_ref: (l, i, 0))],
        out_specs=pl.BlockSpec((None, tr, c), lambda i, me_ref: (me_ref[0], i, 0)))
    return _ordered_call(
        body, name=name, grid_spec=grid_spec, out_shape=jax.ShapeDtypeStruct((N_CHIPS, r, c), dtype),
        compiler_params=_cp("arbitrary"),
    )(me_idx, w)


def _my_half(ref, blk, c):
    hr = ref.shape[1] // 2
    return ref.at[blk, pl.ds(c * hr, hr), :]


def _gather_step(entering):
    lens = [len(e) for e in entering]
    flat = [b for e in entering for b in e]

    def build(refs, sems):
        x, y, c, n1, n2, dg = _route()
        me = _cidx((x, y))
        plan = ([(r, (me,), (*n1, c)) for r in refs[:lens[0]]]
                + [(r, (me, _cidx(n1)), (*n2, c)) for r in refs[lens[0]:lens[0] + lens[1]]]
                + [(r, (_cidx(n1), _cidx(n2), _cidx(dg)), (x, y, 1 - c)) for r in refs[lens[0] + lens[1]:]])
        cps = []
        for ref, blocks, peer in plan:
            for blk in blocks:
                cps.append(_remote(_my_half(ref, blk, c), _my_half(ref, blk, c), sems, len(cps), peer))
        return cps

    return flat, lens[0] + 2 * lens[1] + 3 * lens[2], build


RI_C, RI_ME, RI_N2 = 0, 1, 2


def _pair_sum(name, g, sib, route_idx):
    _, rows, cols = g.shape
    hr = rows // 2
    tr = _tile(hr, 512)
    per = hr // tr

    def body(ri, g_ref, s_ref, o_ref):
        o_ref[...] = (g_ref[...].astype(F32) + s_ref[...].astype(F32)).astype(BF16)

    blk = (None, tr, cols)
    grid_spec = pltpu.PrefetchScalarGridSpec(
        num_scalar_prefetch=1, grid=(N_CHIPS, per),
        in_specs=[pl.BlockSpec(blk, lambda j, i, ri: (j, ri[RI_C] * per + i, 0)),
                  pl.BlockSpec(blk, lambda j, i, ri: (j, i, 0))],
        out_specs=pl.BlockSpec(blk, lambda j, i, ri: (j, i, 0)))
    return _ordered_call(
        body, name=name, grid_spec=grid_spec, out_shape=jax.ShapeDtypeStruct((N_CHIPS, hr, cols), BF16),
        compiler_params=_cp("parallel", "parallel"),
    )(route_idx, g, sib)


def _sum_stage1(name, p, got, route_idx):
    _, hr, cols = p.shape
    tr = _tile(hr, 512)

    def body(ri, pm_ref, pn_ref, g0_ref, g1_ref, keep_ref, send_ref):
        keep_ref[...] = pm_ref[...].astype(F32) + g0_ref[...].astype(F32)
        send_ref[...] = (pn_ref[...].astype(F32) + g1_ref[...].astype(F32)).astype(BF16)

    blk = (None, tr, cols)
    row = pl.BlockSpec((tr, cols), lambda i, ri: (i, 0))
    grid_spec = pltpu.PrefetchScalarGridSpec(
        num_scalar_prefetch=1, grid=(hr // tr,),
        in_specs=[pl.BlockSpec(blk, lambda i, ri: (ri[RI_ME], i, 0)), pl.BlockSpec(blk, lambda i, ri: (ri[RI_N2], i, 0)),
                  pl.BlockSpec(blk, lambda i, ri: (0, i, 0)), pl.BlockSpec(blk, lambda i, ri: (1, i, 0))],
        out_specs=(row, row))
    return _ordered_call(
        body, name=name, grid_spec=grid_spec,
        out_shape=(jax.ShapeDtypeStruct((hr, cols), F32), jax.ShapeDtypeStruct((hr, cols), BF16)),
        compiler_params=_cp("parallel"),
    )(route_idx, p, p, got, got)


def _sum_stage2(name, keep, got):
    hr, cols = keep.shape
    tr = _tile(hr, 512)

    def body(k_ref, g_ref, o_ref):
        o_ref[...] = k_ref[...] + g_ref[...].astype(F32)

    row = pl.BlockSpec((tr, cols), lambda i: (i, 0))
    return _ordered_call(
        body, name=name, grid=(hr // tr,), in_specs=[row, row], out_specs=row,
        out_shape=jax.ShapeDtypeStruct((hr, cols), F32), compiler_params=_cp("parallel"),
    )(keep, got)


def _reduce_scatter(tag, names, grads, route_idx):
    n = len(grads)
    hrs = [g.shape[1] // 2 for g in grads]

    def empty(t, lead, dtype):
        return lax.empty(lead + (hrs[t], grads[t].shape[2]), dtype)

    def pair_stage(refs, sems):
        x, y, c, n1, n2, dg = _route()
        return [_remote(refs[t].at[:, pl.ds((1 - c) * hrs[t], hrs[t]), :], refs[n + t], sems, t, (x, y, 1 - c))
                for t in range(n)]

    def stage1(refs, sems):
        x, y, c, n1, n2, dg = _route()
        return [_remote(refs[t].at[blk], refs[n + t].at[slot], sems, 2 * t + slot, (*n1, c))
                for t in range(n) for slot, blk in enumerate((_cidx(n1), _cidx(dg)))]

    def stage2(refs, sems):
        x, y, c, n1, n2, dg = _route()
        return [_remote(refs[t], refs[n + t], sems, t, (*n2, c)) for t in range(n)]

    def stage3(refs, sems):
        x, y, c, n1, n2, dg = _route()
        return [_remote(refs[t], refs[n + t], sems, t, (x, y, 1 - c)) for t in range(n)]

    state = {}

    def start():
        state["h"] = _exchange_start(f"rs_pair_{tag}", list(grads) + [empty(t, (N_CHIPS,), BF16) for t in range(n)],
                                     n, pair_stage)

    def pair_done():
        out = _exchange_wait(state["h"])
        psum = [_pair_sum(f"rs_psum_{names[t]}", out[t], out[n + t], route_idx) for t in range(n)]
        state["h"] = _exchange_start(f"rs_x1_{tag}", psum + [empty(t, (2,), BF16) for t in range(n)], 2 * n, stage1)

    def x1_done():
        out = _exchange_wait(state["h"])
        state["keep"], send = zip(*[_sum_stage1(f"rs_sum1_{names[t]}", out[t], out[n + t], route_idx)
                                    for t in range(n)])
        state["h"] = _exchange_start(f"rs_x2_{tag}", list(send) + [empty(t, (), BF16) for t in range(n)], n, stage2)

    def x2_done():
        out = _exchange_wait(state["h"])
        mine = [_sum_stage2(f"rs_sum2_{names[t]}", state["keep"][t], out[n + t]) for t in range(n)]
        state["h"] = _exchange_start(f"rs_half_{tag}", mine + [empty(t, (), F32) for t in range(n)], n, stage3)

    def finish():
        out = _exchange_wait(state["h"])
        return list(zip(out[:n], out[n:]))

    return start, pair_done, x1_done, x2_done, finish


def _adamw_big(name, w, m, v, f, h, l, c_idx, prev):
    n_l, r, cols = w.shape
    hr = r // 2
    tr = _tile(hr, 256)
    per = hr // tr

    def body(c_ref, w_ref, m_ref, v_ref, f_ref, h_ref, *rest):
        g_ref, d_ref, nm_ref, nv_ref = rest[-4:]
        gv = jnp.where(pl.program_id(0) == c_ref[0], f_ref[...], h_ref[...])
        mv = ADAM_B1 * m_ref[...] + (1.0 - ADAM_B1) * gv
        vv = ADAM_B2 * v_ref[...] + (1.0 - ADAM_B2) * (gv * gv)
        m_hat = mv / (1.0 - ADAM_B1 ** ADAM_STEP)
        v_hat = vv / (1.0 - ADAM_B2 ** ADAM_STEP)
        g_ref[...] = gv
        d_ref[...] = -ADAM_LR * (m_hat / (jnp.sqrt(v_hat) + ADAM_EPS) + ADAM_WD * w_ref[...])
        nm_ref[...] = mv
        nv_ref[...] = vv

    big = pl.BlockSpec((None, tr, cols), lambda hf, i, c_ref: (l, hf * per + i, 0))
    fspec = pl.BlockSpec((tr, cols), lambda hf, i, c_ref: (jnp.where(hf == c_ref[0], i, 0), 0))
    hspec = pl.BlockSpec((tr, cols), lambda hf, i, c_ref: (jnp.where(hf == c_ref[0], 0, i), 0))
    grid_spec = pltpu.PrefetchScalarGridSpec(
        num_scalar_prefetch=1, grid=(2, per), in_specs=[big] * 3 + [fspec, hspec] + [ANY] * len(prev),
        out_specs=(big,) * 4)
    return _ordered_call(
        body, name=name, grid_spec=grid_spec, out_shape=(jax.ShapeDtypeStruct(w.shape, F32),) * 4,
        input_output_aliases={6 + k: k for k in range(len(prev))}, compiler_params=_cp("arbitrary", "arbitrary"),
    )(c_idx, w, m, v, f, h, *prev)


def _small_allreduce(buf):
    rows = buf.shape[0]
    hr = rows // 2

    def body(in_ref, out_ref, pair, acc, got1, got2, send_sems, recv_sems):
        x, y, c, n1, n2, dg = _route()
        sems = (send_sems, recv_sems)
        sibling = (x, y, 1 - c)
        mine = pl.ds(pl.multiple_of(c * hr, 8), hr)
        pair[c] = in_ref[...]
        cp = _remote(in_ref, pair.at[c], sems, 0, sibling)
        cp.start()
        cp.wait()
        acc[...] = pair[0, mine, :] + pair[1, mine, :]
        cp = _remote(acc, got1, sems, 1, (*n1, c))
        cp.start()
        cp.wait()
        acc[...] = acc[...] + got1[...]
        cp = _remote(acc, got2, sems, 2, (*n2, c))
        cp.start()
        cp.wait()
        out_ref[mine, :] = acc[...] + got2[...]
        cp = _remote(out_ref.at[mine, :], out_ref.at[mine, :], sems, 3, sibling)
        cp.start()
        cp.wait()

    half = pltpu.VMEM((hr, 128), F32)
    return _ordered_call(
        body, name="small_allreduce", in_specs=[pl.BlockSpec(memory_space=pltpu.VMEM)],
        out_specs=pl.BlockSpec(memory_space=pltpu.VMEM), out_shape=jax.ShapeDtypeStruct((rows, 128), F32),
        scratch_shapes=[pltpu.VMEM((2, rows, 128), F32), half, half, half,
                        pltpu.SemaphoreType.DMA((4,)), pltpu.SemaphoreType.DMA((4,))],
        compiler_params=pltpu.CompilerParams(has_side_effects=True, vmem_limit_bytes=VMEM_LIMIT),
    )(buf)


BIG = ("w_in", "w_out", "w_up", "w_down")
COL_SHARDED = {"w_in": True, "w_out": False, "w_up": True, "w_down": False}
SMALL = ("ln1_g", "q_norm_g", "k_norm_g", "sinks", "conv_w", "conv_b", "conv_ln_g", "conv_ln_b", "sgu_ln_g",
         "sgu_ln_b", "sgu_w", "sgu_b", "out_norm_g", "ln2_g")
WEIGHTS = ("ln1_g", "w_in", "q_norm_g", "k_norm_g", "sinks", "conv_w", "conv_b", "conv_ln_g", "conv_ln_b",
           "sgu_ln_g", "sgu_ln_b", "sgu_w", "sgu_b", "out_norm_g", "w_out", "ln2_g", "w_up", "w_down")
PACK_QUANTUM = 8 * 128
PACK_ROWS = 512


def _pack(arrs):
    parts = []
    for a in arrs:
        f = a.reshape(-1)
        parts.append(jnp.pad(f, (0, -f.shape[0] % PACK_QUANTUM)).reshape(-1, 128))
    rows = sum(p.shape[0] for p in parts)
    parts.append(jnp.zeros((-rows % PACK_ROWS, 128), F32))
    return jnp.concatenate(parts, axis=0)


def _unpack(buf, shapes):
    out, off = [], 0
    for shp in shapes:
        n = 1
        for dd in shp:
            n *= dd
        rows = (n + PACK_QUANTUM - 1) // PACK_QUANTUM * 8
        out.append(buf[off:off + rows].reshape(-1)[:n].reshape(shp))
        off += rows
    return out


def _to_heads(t, nh):
    return t.reshape(t.shape[0], nh, HEAD_DIM).transpose(1, 0, 2)


def _from_heads(t):
    return t.transpose(1, 0, 2).reshape(t.shape[1], t.shape[0] * HEAD_DIM)


def _no_hook(point, carry):
    return carry


def _layer_fwd(l, x, p, wg, hook=_no_hook):
    d = x.shape[1]
    aw, cw = d // 2, d // 4
    nq = aw // HEAD_DIM
    nkv = nq // GQA
    kvw = nkv * HEAD_DIM
    x = hook("fwd_start", x)
    h1 = _rms_fwd(f"ln1_fwd_{l}", x, p["ln1_g"])
    proj = _mm_act_w(f"proj_{l}", h1, wg["w_in"], True, _ep_store)[0]
    proj = hook("fwd_proj", proj)
    q = _to_heads(proj[:, :aw], nq)
    k = _to_heads(proj[:, aw:aw + kvw], nkv)
    v = _to_heads(proj[:, aw + kvw:aw + 2 * kvw], nkv)
    sinks_b = jnp.broadcast_to(p["sinks"][:, None, None], (nq, 1, 128))
    ya = _from_heads(_attn_fwd(f"attn_fwd_{l}", q, k, v, p["q_norm_g"], p["k_norm_g"], sinks_b))
    ya = hook("fwd_attn", ya)
    yc = _conv_fwd(f"conv_fwd_{l}", proj, 3, p["conv_w"], p["conv_b"], p["conv_ln_g"], p["conv_ln_b"])
    ys = _sgu_fwd(f"sgu_fwd_{l}", proj, 5, p["sgu_ln_g"], p["sgu_ln_b"], p["sgu_w"], p["sgu_bexp"])
    mix = _mixnorm_fwd(f"mixnorm_fwd_{l}", ya, yc, ys, p["out_norm_g"])
    mix = hook("fwd_mid", mix)
    xm = _mm_act_w(f"out_{l}", mix, wg["w_out"], False, _ep_residual, extra=(x,))[0]
    h2 = _rms_fwd(f"ln2_fwd_{l}", xm, p["ln2_g"])
    h2 = hook("fwd_ln2", h2)
    up_b, act_b = _mm_act_w(f"up_{l}", h2, wg["w_up"], True, _ep_up, out_dtypes=(BF16, BF16))
    act_b = hook("fwd_up", act_b)
    xo = _mm_act_w(f"down_{l}", act_b, wg["w_down"], False, _ep_residual, extra=(xm,))[0]
    xo = hook("fwd_end", xo)
    saved = dict(x=x, h1=h1, proj=proj, q=q, k=k, v=v, sinks_b=sinks_b, ya=ya, yc=yc, ys=ys, mix=mix, xm=xm, h2=h2,
                 up_b=up_b, act_b=act_b)
    return xo, saved


def _layer_bwd(l, dxo, dxo_b, p, wg, sv, big, hook=_no_hook):
    d = dxo.shape[1]
    nq = (d // 2) // HEAD_DIM
    small = {}
    dxo_b = hook("bwd_start", dxo_b)
    big["w_down"] = _mm_wgrad(f"dw_down_{l}", sv["act_b"], dxo_b, False, d)
    dup_b = _mm_act_wt(f"dup_{l}", dxo_b, wg["w_down"], False, _ep_dup, extra=(sv["up_b"],), out_dtypes=(BF16,))[0]
    dup_b = hook("bwd_dup", dup_b)
    big["w_up"] = _mm_wgrad(f"dw_up_{l}", sv["h2"], dup_b, True, wg["w_up"].shape[2])
    dh2 = _mm_act_wt(f"dh2_{l}", dup_b, wg["w_up"], True, _ep_store)[0]
    dh2 = hook("bwd_dh2", dh2)
    dxm, dxm_b, small["ln2_g"] = _rms_bwd(f"ln2_bwd_{l}", dh2, sv["xm"], p["ln2_g"], dxo)
    big["w_out"] = _mm_wgrad(f"dw_out_{l}", sv["mix"], dxm_b, False, d)
    dmix = _mm_act_wt(f"dmix_{l}", dxm_b, wg["w_out"], False, _ep_store)[0]
    dya, dyc, dys, small["out_norm_g"] = _mixnorm_bwd(f"mixnorm_bwd_{l}", dmix, sv["ya"], sv["yc"], sv["ys"],
                                                      p["out_norm_g"])
    dya = hook("bwd_mix", dya)
    dq, dkc, dkp, dvc, dvp, small["q_norm_g"], dsink = _attn_bwd(
        f"attn_bwd_{l}", sv["q"], sv["k"], sv["v"], p["q_norm_g"], p["k_norm_g"], sv["sinks_b"], _to_heads(dya, nq))
    dkc = hook("bwd_attn", dkc)
    small["sinks"] = dsink[:, 0, 0]
    dk, dv, small["k_norm_g"] = _attn_bwd_kv(f"attn_bwd_kv_{l}", sv["k"], p["k_norm_g"], dkc, dkp, dvc, dvp)
    dc, dcw, small["conv_b"], small["conv_ln_g"], small["conv_ln_b"] = _conv_bwd1(
        f"conv_bwd1_{l}", sv["proj"], 3, p["conv_w"], p["conv_b"], p["conv_ln_g"], p["conv_ln_b"], dyc)
    small["conv_w"] = dcw[:CONV_KERNEL]
    dxc_b = _conv_bwd2(f"conv_bwd2_{l}", sv["proj"], 3, p["conv_w"], dc)
    dxs_b, small["sgu_w"], small["sgu_b"], small["sgu_ln_g"], small["sgu_ln_b"] = _sgu_bwd(
        f"sgu_bwd_{l}", sv["proj"], 5, p["sgu_ln_g"], p["sgu_ln_b"], p["sgu_w"], p["sgu_bexp"], dys)
    dxs_b = hook("bwd_sgu", dxs_b)
    dproj_b = jnp.concatenate([_from_heads(dq), _from_heads(dk), _from_heads(dv), dxc_b, dxs_b], axis=1)
    big["w_in"] = _mm_wgrad(f"dw_in_{l}", sv["h1"], dproj_b, True, wg["w_in"].shape[2])
    dh1 = _mm_act_wt(f"dh1_{l}", dproj_b, wg["w_in"], True, _ep_store)[0]
    dx, dx_b, small["ln1_g"] = _rms_bwd(f"ln1_bwd_{l}", dh1, sv["x"], p["ln1_g"], dxm)
    dx_b = hook("bwd_end", dx_b)
    return dx, dx_b, small


def kernel(x, ln1_g, w_in, q_norm_g, k_norm_g, sinks, conv_w, conv_b, conv_ln_g, conv_ln_b, sgu_ln_g, sgu_ln_b, sgu_w, sgu_b, out_norm_g, w_out, ln2_g, w_up, w_down, loss_target, m_ln1_g, m_w_in, m_q_norm_g, m_k_norm_g, m_sinks, m_conv_w, m_conv_b, m_conv_ln_g, m_conv_ln_b, m_sgu_ln_g, m_sgu_ln_b, m_sgu_w, m_sgu_b, m_out_norm_g, m_w_out, m_ln2_g, m_w_up, m_w_down, v_ln1_g, v_w_in, v_q_norm_g, v_k_norm_g, v_sinks, v_conv_w, v_conv_b, v_conv_ln_g, v_conv_ln_b, v_sgu_ln_g, v_sgu_ln_b, v_sgu_w, v_sgu_b, v_out_norm_g, v_w_out, v_ln2_g, v_w_up, v_w_down):
    given = dict(locals())
    _LAST[0] = None
    n_layers = ln1_g.shape[0]
    s, d = x.shape[1], x.shape[2]
    cw = d // 4
    xi, yi, core = lax.axis_index("x"), lax.axis_index("y"), lax.axis_index("c")
    chip = 2 * xi + yi
    second = jnp.where(core == 0, 2 * xi + (1 - yi), 2 * (1 - xi) + yi)
    route_idx = jnp.stack([core, chip, second]).astype(jnp.int32)

    conv_w_pad = jnp.pad(conv_w, ((0, 0), (0, HALO - CONV_KERNEL), (0, 0))).reshape(1, n_layers * HALO, -1)
    cwl = conv_w_pad.shape[2]
    buf = {(nm, l): _cast_place(f"place_{nm}_{l}", given[nm], l, route_idx[1:2], BF16)
           for l in range(n_layers) for nm in BIG}
    buf["conv_w"] = _cast_place("place_conv_w", conv_w_pad, 0, route_idx[1:2], F32)
    first = ["conv_w", ("w_in", 0)]
    groups = [[(nm, l)] for l in range(n_layers) for nm in BIG if (nm, l) != ("w_in", 0)]
    n_steps = len(groups) + 2

    def step_parts(keys):
        flat, n_sems, build = _gather_step([[buf[k] for k in ks] for ks in keys])
        return [k for ks in keys for k in ks], flat, n_sems, build

    pending = {}

    def start_step(st):
        keys, flat, n_sems, build = step_parts([groups[st - j] if 0 <= st - j < len(groups) else []
                                                for j in range(3)])
        pending["keys"], pending["h"] = keys, _exchange_start(f"gather_step{st}", flat, n_sems, build)

    def wait_step():
        for k, b in zip(pending["keys"], _exchange_wait(pending["h"])):
            buf[k] = b

    for stage in range(3):
        keys, flat, n_sems, build = step_parts([first if j == stage else [] for j in range(3)])
        for k, b in zip(keys, _exchange(f"gather_first{stage}", flat, n_sems, build)):
            buf[k] = b
    conv_w_full = buf["conv_w"].reshape(N_CHIPS, n_layers, HALO, cwl).transpose(1, 2, 0, 3).reshape(
        n_layers, HALO, cw)

    class LayerWeights:
        def __init__(self, l):
            self.l = l

        def __getitem__(self, nm):
            return buf[(nm, self.l)]

    wgs = [LayerWeights(l) for l in range(n_layers)]
    fwd_points = [(l, pt) for l in range(n_layers)
                  for pt in ("fwd_start", "fwd_proj", "fwd_attn", "fwd_mid", "fwd_ln2", "fwd_up", "fwd_end")
                  if (l == 0 or pt in ("fwd_mid", "fwd_ln2", "fwd_up", "fwd_end"))
                  and (pt != "fwd_end" or l + 1 < n_layers)]
    assert len(fwd_points) == n_steps + 1, "one hook point per pipeline step, and one to wait for the last"
    fwd_tables = [{} for _ in range(n_layers)]
    for i, (l, pt) in enumerate(fwd_points):
        if i > 0:
            fwd_tables[l].setdefault(pt, []).append(wait_step)
        if i < n_steps:
            fwd_tables[l].setdefault(pt, []).append(functools.partial(start_step, i))
    params = []
    for l in range(n_layers):
        p = {nm: given[nm][l] for nm in SMALL if nm != "conv_w"}
        for nm in ("conv_b", "conv_ln_g", "conv_ln_b", "sgu_ln_g", "sgu_ln_b"):
            p[nm] = p[nm].reshape(1, -1)
        p["conv_w"] = conv_w_full[l]
        p["sgu_bexp"] = jnp.repeat(sgu_b[l].T, HEAD_DIM, axis=1)
        params.append(p)

    def make_hook(table):
        def hook(point, carry):
            for fn in table.get(point, ()):
                fn()
            return carry
        return hook

    h = x.reshape(s, d)
    saved = []
    for l in range(n_layers):
        h, sv = _layer_fwd(l, h, params[l], wgs[l], make_hook(fwd_tables[l]))
        saved.append(sv)
    dh, dh_b, loss_part = _loss_head(h, loss_target.reshape(s, d))
    loss = lax.psum(loss_part[0, 0], ("x", "y", "c"))

    big_grads = [{} for _ in range(n_layers)]
    small_grads = [None] * n_layers
    halves = {}

    def rs_group(tag, l, names):
        phases = {}

        def start():
            phases["p"] = _reduce_scatter(tag, [f"{nm}_{l}" for nm in names], [big_grads[l][nm] for nm in names],
                                          route_idx)
            phases["p"][0]()

        def step(k):
            return lambda: phases["p"][k]()

        def finish():
            for nm, fh in zip(names, phases["p"][4]()):
                halves[(nm, l)] = fh

        return [start, step(1), step(2), step(3), finish]

    early = rs_group("l0a", 0, ("w_down", "w_up", "w_out"))
    for l in reversed(range(n_layers)):
        table = {}
        if l + 1 < n_layers:
            above = rs_group(f"l{l + 1}", l + 1, BIG)
            for point, fn in zip(("bwd_start", "bwd_dup", "bwd_mix", "bwd_attn", "bwd_sgu"), above):
                table.setdefault(point, []).append(fn)
        if l == 0:
            for point, fn in zip(("bwd_mix", "bwd_attn", "bwd_end"), early[:3]):
                table.setdefault(point, []).append(fn)
        dh, dh_b, small_grads[l] = _layer_bwd(l, dh, dh_b, params[l], wgs[l], saved[l], big_grads[l],
                                              make_hook(table))
    grad_x = dh.reshape(x.shape)

    grads, delta, new_m, new_v = {}, {}, {}, {}
    adam_state = {nm: () for nm in BIG}

    def adam(nm, l):
        f, h = halves[(nm, l)]
        adam_state[nm] = _adamw_big(f"adamw_{nm}_{l}", given[nm], given["m_" + nm], given["v_" + nm], f, h, l,
                                    route_idx[0:1], adam_state[nm])

    def small_update():
        small_shapes = [(n_layers,) + small_grads[0][nm].shape for nm in SMALL]
        small_sum = _small_allreduce(_pack([jnp.stack([small_grads[l][nm] for l in range(n_layers)])
                                            for nm in SMALL]))
        for nm, g in zip(SMALL, _unpack(small_sum, small_shapes)):
            grads[nm] = g.reshape((n_layers,) + given[nm].shape[1:]) if nm != "conv_w" else g
        grads["conv_w"] = lax.dynamic_slice_in_dim(grads["conv_w"], chip * cwl, cwl, axis=2)
        packed = [_pack([src[nm] for nm in SMALL]) for src in
                  ({nm: given[nm] for nm in SMALL}, grads, {nm: given["m_" + nm] for nm in SMALL},
                   {nm: given["v_" + nm] for nm in SMALL})]
        local_shapes = [given[nm].shape for nm in SMALL]
        for dst, buf in zip((delta, new_m, new_v), _adamw("adamw_small", *packed)):
            for nm, a in zip(SMALL, _unpack(buf, local_shapes)):
                dst[nm] = a

    upper = [(nm, l) for l in reversed(range(1, n_layers)) for nm in reversed(BIG)]
    late = rs_group("l0b", 0, ("w_in",))
    late[0]()
    for task in upper[:1]:
        adam(*task)
    late[1]()
    for task in upper[1:]:
        adam(*task)
    early[3]()
    late[2]()
    small_update()
    early[4]()
    for nm in ("w_down", "w_up"):
        adam(nm, 0)
    late[3]()
    adam("w_out", 0)
    late[4]()
    adam("w_in", 0)
    for nm in BIG:
        grads[nm], delta[nm], new_m[nm], new_v[nm] = adam_state[nm]
    return (loss, grad_x, *[grads[nm] for nm in WEIGHTS], *[delta[nm] for nm in WEIGHTS],
            *[new_m[nm] for nm in WEIGHTS], *[new_v[nm] for nm in WEIGHTS])
```

```python
import functools

import jax
import jax.numpy as jnp
from jax import lax
from jax.experimental import pallas as pl
from jax.experimental.pallas import tpu as pltpu

F32 = jnp.float32
BF16 = jnp.bfloat16
EPS = 1e-6
NEG_INF = -1e30
HEAD_DIM = 64
WINDOW = 128
CONV_KERNEL = 31
HALO = 32
GQA = 4
N_CHIPS = 4
ADAM_LR, ADAM_B1, ADAM_B2, ADAM_EPS, ADAM_WD, ADAM_STEP = 0.001, 0.9, 0.999, 1e-08, 0.01, 10
VMEM_LIMIT = 56 * 1024 * 1024
TILE_K = 2048
MESH = pl.DeviceIdType.MESH
ANY = pl.BlockSpec(memory_space=pl.ANY)

NN = (((1,), (0,)), ((), ()))
NT = (((1,), (1,)), ((), ()))
TN = (((0,), (0,)), ((), ()))


def _cp(*sem):
    return pltpu.CompilerParams(dimension_semantics=sem, vmem_limit_bytes=VMEM_LIMIT)


_LAST = [None]


def _ordered_call(body, *, in_specs=None, grid_spec=None, **kw):
    def run(*operands):
        dep = _LAST[0]
        n = len(operands)
        if dep is None or any(dep is o for o in operands):
            fn, specs, spec, args = body, in_specs, grid_spec, operands
        else:
            def fn(*refs):
                body(*refs[:n], *refs[n + 1:])

            specs, spec, args = in_specs, grid_spec, operands + (dep,)
            if grid_spec is not None:
                spec = pltpu.PrefetchScalarGridSpec(
                    num_scalar_prefetch=grid_spec.num_scalar_prefetch, grid=grid_spec.grid,
                    in_specs=list(grid_spec.in_specs) + [ANY], out_specs=grid_spec.out_specs)
            else:
                specs = list(in_specs) + [ANY]
        if spec is not None:
            out = pl.pallas_call(fn, grid_spec=spec, **kw)(*args)
        else:
            out = pl.pallas_call(fn, in_specs=specs, **kw)(*args)
        arrays = [o for o in (out if isinstance(out, (tuple, list)) else (out,)) if o.dtype in (F32, BF16)]
        _LAST[0] = min(arrays, key=lambda o: o.size)
        return out

    return run


def _tile(dim, pref):
    if dim <= pref:
        return dim
    for t in range(pref, 0, -128):
        if dim % t == 0:
            return t
    while dim % pref:
        pref //= 2
    return pref


def _dot(a, b, dims=NN):
    return lax.dot_general(a, b, dims, preferred_element_type=F32)


def _colsum(v):
    return jnp.sum(v, axis=0, keepdims=True)


def _sigmoid(x):
    return 1.0 / (1.0 + jnp.exp(-x))


def _matmul(name, operands, in_specs, out_shape, out_specs, grid, dims, acc_shape, epilogue, split_k=False):
    nk = grid[2]
    n_in = len(operands)

    def product(a_ref, b_ref):
        if split_k:
            ck = b_ref.shape[2]
            out = _dot(a_ref[:, 0:ck], b_ref[0], dims)
            for j in range(1, N_CHIPS):
                out = out + _dot(a_ref[:, j * ck:(j + 1) * ck], b_ref[j], dims)
            return out
        bv = b_ref[...]
        return _dot(a_ref[...], bv.reshape(-1, bv.shape[-1]) if bv.ndim == 3 else bv, dims)

    def body(*refs):
        a_ref, b_ref = refs[0], refs[1]
        extra = refs[2:n_in]
        if nk == 1:
            epilogue(product(a_ref, b_ref), extra, refs[n_in:])
            return
        outs = refs[n_in:-1]
        acc = refs[-1]
        k = pl.program_id(2)

        @pl.when(k == 0)
        def _():
            acc[...] = product(a_ref, b_ref)

        @pl.when((k > 0) & (k < nk - 1))
        def _():
            acc[...] += product(a_ref, b_ref)

        @pl.when(k == nk - 1)
        def _():
            epilogue(acc[...] + product(a_ref, b_ref), extra, outs)

    return _ordered_call(
        body, name=name, grid=grid, in_specs=in_specs, out_specs=out_specs, out_shape=out_shape,
        scratch_shapes=[pltpu.VMEM(acc_shape, F32)] if nk > 1 else [],
        compiler_params=_cp("parallel", "parallel", "arbitrary"),
    )(*operands)


def _ep_store(acc, extra, outs):
    outs[0][...] = acc.astype(outs[0].dtype)


def _ep_residual(acc, extra, outs):
    outs[0][...] = extra[0][...] + acc


def _ep_up(acc, extra, outs):
    outs[0][...] = acc.astype(BF16)
    r = jnp.maximum(acc, 0.0)
    outs[1][...] = (r * r).astype(BF16)


def _ep_dup(acc, extra, outs):
    outs[0][...] = (acc * (2.0 * jnp.maximum(extra[0][...].astype(F32), 0.0))).astype(BF16)


def _mm_act_w(name, a, wg, col_sharded, epilogue, extra=(), out_dtypes=(F32,)):
    m, kdim = a.shape
    _, r, c = wg.shape
    tm = _tile(m, 1024)
    if col_sharded:
        n = N_CHIPS * c
        tn = _tile(c, 1024)
        tk = _tile(kdim, TILE_K)
        per = c // tn
        b_spec = pl.BlockSpec((None, tk, tn), lambda i, j, k: (j // per, k, j % per))
    elif N_CHIPS * r <= TILE_K:
        n = c
        tm, tn, tk = _tile(m, 512), n, kdim
        b_spec = pl.BlockSpec((N_CHIPS, r, tn), lambda i, j, k: (0, 0, j))
    else:
        n = c
        tn = _tile(n, 1024)
        tk = _tile(r, TILE_K)
        per = r // tk
        b_spec = pl.BlockSpec((None, tk, tn), lambda i, j, k: (k // per, k % per, j))
    grid = (m // tm, n // tn, kdim // tk)
    o_spec = pl.BlockSpec((tm, tn), lambda i, j, k: (i, j))
    in_specs = [pl.BlockSpec((tm, tk), lambda i, j, k: (i, k)), b_spec] + [o_spec] * len(extra)
    return _matmul(name, (a, wg) + tuple(extra), in_specs,
                   tuple(jax.ShapeDtypeStruct((m, n), d) for d in out_dtypes),
                   tuple(o_spec for _ in out_dtypes), grid, NN, (tm, tn), epilogue)


def _mm_act_wt(name, a, wg, col_sharded, epilogue, extra=(), out_dtypes=(F32,)):
    m, kdim = a.shape
    _, r, c = wg.shape
    tm = _tile(m, 1024)
    split_k = False
    if col_sharded and N_CHIPS * c <= 2 * TILE_K:
        n = r
        tm, tn, tk, split_k = _tile(m, 512), n, kdim, True
        b_spec = pl.BlockSpec((N_CHIPS, tn, c), lambda i, j, k: (0, j, 0))
    elif col_sharded:
        n = r
        tn = _tile(n, 1024)
        tk = _tile(c, TILE_K)
        per = c // tk
        b_spec = pl.BlockSpec((None, tn, tk), lambda i, j, k: (k // per, j, k % per))
    elif N_CHIPS * r <= TILE_K:
        n = N_CHIPS * r
        tm, tn, tk = _tile(m, 512), n, _tile(c, TILE_K)
        b_spec = pl.BlockSpec((N_CHIPS, r, tk), lambda i, j, k: (0, 0, k))
    else:
        n = N_CHIPS * r
        tn = _tile(r, 1024)
        tk = _tile(c, TILE_K)
        per = r // tn
        b_spec = pl.BlockSpec((None, tn, tk), lambda i, j, k: (j // per, j % per, k))
    grid = (m // tm, n // tn, kdim // tk)
    o_spec = pl.BlockSpec((tm, tn), lambda i, j, k: (i, j))
    in_specs = [pl.BlockSpec((tm, tk), lambda i, j, k: (i, k)), b_spec] + [o_spec] * len(extra)
    return _matmul(name, (a, wg) + tuple(extra), in_specs,
                   tuple(jax.ShapeDtypeStruct((m, n), d) for d in out_dtypes),
                   tuple(o_spec for _ in out_dtypes), grid, NT, (tm, tn), epilogue, split_k)


def _mm_wgrad(name, a, g, col_sharded, c):
    s, kdim = a.shape
    _, n = g.shape
    ts = _tile(s, TILE_K)
    if col_sharded:
        r = kdim
        tm = _tile(kdim, 1024)
        tn = _tile(c, 1024)
        per = c // tn
        o_spec = pl.BlockSpec((None, tm, tn), lambda i, j, k: (j // per, i, j % per))
    else:
        r = kdim // N_CHIPS
        tm = _tile(r, 512)
        tn = _tile(c, 2048)
        per = r // tm
        o_spec = pl.BlockSpec((None, tm, tn), lambda i, j, k: (i // per, i % per, j))
    grid = (kdim // tm, n // tn, s // ts)
    in_specs = [pl.BlockSpec((ts, tm), lambda i, j, k: (k, i)), pl.BlockSpec((ts, tn), lambda i, j, k: (k, j))]
    return _matmul(name, (a, g), in_specs, (jax.ShapeDtypeStruct((N_CHIPS, r, c), BF16),), (o_spec,),
                   grid, TN, (tm, tn), _ep_store)[0]


def _rms_fwd(name, x, g):
    s, d = x.shape
    tb = _tile(s, 256)

    def body(x_ref, g_ref, o_ref):
        xv = x_ref[...]
        r = lax.rsqrt(jnp.mean(xv * xv, axis=-1, keepdims=True) + EPS)
        o_ref[...] = (xv * r * g_ref[...]).astype(BF16)

    return _ordered_call(
        body, name=name, grid=(s // tb,),
        in_specs=[pl.BlockSpec((tb, d), lambda i: (i, 0)), pl.BlockSpec((1, d), lambda i: (0, 0))],
        out_specs=pl.BlockSpec((tb, d), lambda i: (i, 0)),
        out_shape=jax.ShapeDtypeStruct((s, d), BF16), compiler_params=_cp("parallel"),
    )(x, g.reshape(1, d))


def _rms_bwd(name, dh, x, g, dres):
    s, d = x.shape
    tb = _tile(s, 256)

    def body(dh_ref, x_ref, g_ref, dres_ref, dx_ref, dxb_ref, dg_ref):
        i = pl.program_id(0)
        xv = x_ref[...]
        r = lax.rsqrt(jnp.mean(xv * xv, axis=-1, keepdims=True) + EPS)
        xhat = xv * r
        dhv = dh_ref[...]
        dxhat = dhv * g_ref[...]
        dx = dres_ref[...] + r * (dxhat - xhat * jnp.mean(dxhat * xhat, axis=-1, keepdims=True))
        dx_ref[...] = dx
        dxb_ref[...] = dx.astype(BF16)

        @pl.when(i == 0)
        def _():
            dg_ref[...] = jnp.zeros_like(dg_ref)

        dg_ref[...] += _colsum(dhv * xhat)

    row = pl.BlockSpec((tb, d), lambda i: (i, 0))
    vec = pl.BlockSpec((1, d), lambda i: (0, 0))
    return _ordered_call(
        body, name=name, grid=(s // tb,), in_specs=[row, row, vec, row], out_specs=(row, row, vec),
        out_shape=(jax.ShapeDtypeStruct((s, d), F32), jax.ShapeDtypeStruct((s, d), BF16),
                   jax.ShapeDtypeStruct((1, d), F32)),
        compiler_params=_cp("arbitrary"),
    )(dh, x, g.reshape(1, d), dres)


def _loss_head(y, t):
    s, d = y.shape
    tb = _tile(s, 256)

    def body(y_ref, t_ref, dy_ref, dyb_ref, loss_ref, acc):
        i = pl.program_id(0)
        e = y_ref[...] - t_ref[...]
        dy = e * (1.0 / d)
        dy_ref[...] = dy
        dyb_ref[...] = dy.astype(BF16)

        @pl.when(i == 0)
        def _():
            acc[...] = jnp.zeros_like(acc)

        acc[...] += _colsum(e * e)

        @pl.when(i == pl.num_programs(0) - 1)
        def _():
            loss_ref[...] = jnp.sum(acc[...], axis=-1, keepdims=True) * (0.5 / d)

    row = pl.BlockSpec((tb, d), lambda i: (i, 0))
    return _ordered_call(
        body, name="loss_head", grid=(s // tb,), in_specs=[row, row],
        out_specs=(row, row, pl.BlockSpec((1, 1), lambda i: (0, 0))),
        out_shape=(jax.ShapeDtypeStruct((s, d), F32), jax.ShapeDtypeStruct((s, d), BF16),
                   jax.ShapeDtypeStruct((1, 1), F32)),
        scratch_shapes=[pltpu.VMEM((1, d), F32)], compiler_params=_cp("arbitrary"),
    )(y, t)


def _mixnorm_fwd(name, ya, yc, ys, g):
    s, aw = ya.shape
    cw, sw = yc.shape[1], ys.shape[1]
    d = aw + cw + sw
    tb = _tile(s, 256)

    def body(ya_ref, yc_ref, ys_ref, g_ref, o_ref):
        off = 0
        for ref, w in ((ya_ref, aw), (yc_ref, cw), (ys_ref, sw)):
            v = ref[...]
            r = lax.rsqrt(jnp.mean(v * v, axis=-1, keepdims=True) + EPS)
            o_ref[:, off:off + w] = (v * r * g_ref[:, off:off + w]).astype(BF16)
            off += w

    def row(w):
        return pl.BlockSpec((tb, w), lambda i: (i, 0))

    return _ordered_call(
        body, name=name, grid=(s // tb,),
        in_specs=[row(aw), row(cw), row(sw), pl.BlockSpec((1, d), lambda i: (0, 0))], out_specs=row(d),
        out_shape=jax.ShapeDtypeStruct((s, d), BF16), compiler_params=_cp("parallel"),
    )(ya, yc, ys, g.reshape(1, d))


def _mixnorm_bwd(name, dmix, ya, yc, ys, g):
    s, aw = ya.shape
    cw, sw = yc.shape[1], ys.shape[1]
    d = aw + cw + sw
    tb = _tile(s, 256)

    def body(dm_ref, ya_ref, yc_ref, ys_ref, g_ref, dya_ref, dyc_ref, dys_ref, dg_ref):
        i = pl.program_id(0)

        @pl.when(i == 0)
        def _():
            dg_ref[...] = jnp.zeros_like(dg_ref)

        off = 0
        for ref, dref, w in ((ya_ref, dya_ref, aw), (yc_ref, dyc_ref, cw), (ys_ref, dys_ref, sw)):
            v = ref[...]
            r = lax.rsqrt(jnp.mean(v * v, axis=-1, keepdims=True) + EPS)
            vhat = v * r
            dm = dm_ref[:, off:off + w]
            dvhat = dm * g_ref[:, off:off + w]
            dref[...] = r * (dvhat - vhat * jnp.mean(dvhat * vhat, axis=-1, keepdims=True))
            dg_ref[:, off:off + w] += _colsum(dm * vhat)
            off += w

    def row(w):
        return pl.BlockSpec((tb, w), lambda i: (i, 0))

    vec = pl.BlockSpec((1, d), lambda i: (0, 0))
    return _ordered_call(
        body, name=name, grid=(s // tb,), in_specs=[row(d), row(aw), row(cw), row(sw), vec],
        out_specs=(row(aw), row(cw), row(sw), vec),
        out_shape=(jax.ShapeDtypeStruct((s, aw), F32), jax.ShapeDtypeStruct((s, cw), F32),
                   jax.ShapeDtypeStruct((s, sw), F32), jax.ShapeDtypeStruct((1, d), F32)),
        compiler_params=_cp("arbitrary"),
    )(dmix, ya, yc, ys, g.reshape(1, d))


def _head_rms(x):
    r = lax.rsqrt(jnp.mean(x * x, axis=-1, keepdims=True) + EPS)
    return x * r, r


def _attn_mask(n):
    qi = lax.broadcasted_iota(jnp.int32, (GQA * WINDOW, 2 * WINDOW), 0) & (WINDOW - 1)
    sj = lax.broadcasted_iota(jnp.int32, (GQA * WINDOW, 2 * WINDOW), 1)
    rel = qi + WINDOW - sj
    return (rel >= 0) & (rel < WINDOW) & ((sj >= WINDOW) | (n > 0))


def _attn_specs(nq, nkv, nb):
    qspec = pl.BlockSpec((nq, WINDOW, HEAD_DIM), lambda n: (0, n, 0))
    cur = pl.BlockSpec((nkv, WINDOW, HEAD_DIM), lambda n: (0, n, 0))
    prev = pl.BlockSpec((nkv, WINDOW, HEAD_DIM), lambda n: (0, jnp.maximum(n - 1, 0), 0))
    nxt = pl.BlockSpec((nkv, WINDOW, HEAD_DIM), lambda n: (0, jnp.minimum(n + 1, nb - 1), 0))
    gain = pl.BlockSpec((1, HEAD_DIM), lambda n: (0, 0))
    sink = pl.BlockSpec((nq, 1, 128), lambda n: (0, 0, 0))
    return qspec, cur, prev, nxt, gain, sink


def _group_sinks(s_ref, g):
    return jnp.concatenate([jnp.broadcast_to(s_ref[g * GQA + i][:, :1], (WINDOW, 1)) for i in range(GQA)], axis=0)


def _attn_probs(qn_b, kn_b, valid, sink):
    logits = _dot(qn_b, kn_b, NT) * (HEAD_DIM ** -0.5)
    logits = jnp.where(valid, logits, NEG_INF)
    m = jnp.maximum(jnp.max(logits, axis=-1, keepdims=True), sink)
    p = jnp.exp(logits - m)
    es = jnp.exp(sink - m)
    denom = jnp.sum(p, axis=-1, keepdims=True) + es
    return p / denom, es / denom


def _attn_fwd(name, q, k, v, gq, gk, sinks_b):
    nq, s, _ = q.shape
    nkv = k.shape[0]
    nb = s // WINDOW
    qspec, cur, prev, _, gain, sink = _attn_specs(nq, nkv, nb)

    def body(q_ref, kc_ref, kp_ref, vc_ref, vp_ref, gq_ref, gk_ref, s_ref, o_ref):
        gkv = gk_ref[...]
        valid = _attn_mask(pl.program_id(0))
        for g in range(nkv):
            kn_b = jnp.concatenate([_head_rms(kp_ref[g])[0] * gkv, _head_rms(kc_ref[g])[0] * gkv],
                                   axis=0).astype(BF16)
            vv_b = jnp.concatenate([vp_ref[g], vc_ref[g]], axis=0).astype(BF16)
            heads = pl.ds(g * GQA, GQA)
            q4 = q_ref[heads].reshape(GQA * WINDOW, HEAD_DIM)
            qn_b = (_head_rms(q4)[0] * gq_ref[...]).astype(BF16)
            probs, _ = _attn_probs(qn_b, kn_b, valid, _group_sinks(s_ref, g))
            o_ref[heads] = _dot(probs.astype(BF16), vv_b).reshape(GQA, WINDOW, HEAD_DIM)

    return _ordered_call(
        body, name=name, grid=(nb,), in_specs=[qspec, cur, prev, cur, prev, gain, gain, sink], out_specs=qspec,
        out_shape=jax.ShapeDtypeStruct((nq, s, HEAD_DIM), F32), compiler_params=_cp("parallel"),
    )(q, k, k, v, v, gq.reshape(1, HEAD_DIM), gk.reshape(1, HEAD_DIM), sinks_b)


def _attn_bwd(name, q, k, v, gq, gk, sinks_b, do):
    nq, s, _ = q.shape
    nkv = k.shape[0]
    nb = s // WINDOW
    qspec, cur, prev, _, gain, sink = _attn_specs(nq, nkv, nb)

    def body(q_ref, kc_ref, kp_ref, vc_ref, vp_ref, gq_ref, gk_ref, s_ref, do_ref,
             dq_ref, dkc_ref, dkp_ref, dvc_ref, dvp_ref, dgq_ref, ds_ref):
        n = pl.program_id(0)

        @pl.when(n == 0)
        def _():
            dgq_ref[...] = jnp.zeros_like(dgq_ref)
            ds_ref[...] = jnp.zeros_like(ds_ref)

        gkv = gk_ref[...]
        gqv = gq_ref[...]
        valid = _attn_mask(n)
        dgq = jnp.zeros((1, HEAD_DIM), F32)
        for g in range(nkv):
            kn_b = jnp.concatenate([_head_rms(kp_ref[g])[0] * gkv, _head_rms(kc_ref[g])[0] * gkv],
                                   axis=0).astype(BF16)
            vv_b = jnp.concatenate([vp_ref[g], vc_ref[g]], axis=0).astype(BF16)
            heads = pl.ds(g * GQA, GQA)
            qhat, r = _head_rms(q_ref[heads].reshape(GQA * WINDOW, HEAD_DIM))
            qn_b = (qhat * gqv).astype(BF16)
            probs, psink = _attn_probs(qn_b, kn_b, valid, _group_sinks(s_ref, g))
            do_b = do_ref[heads].reshape(GQA * WINDOW, HEAD_DIM).astype(BF16)
            dp = _dot(do_b, vv_b, NT)
            delta = jnp.sum(probs * dp, axis=-1, keepdims=True)
            dl_b = (probs * (dp - delta) * (HEAD_DIM ** -0.5)).astype(BF16)
            sink_term = psink * delta
            for i in range(GQA):
                ds_ref[g * GQA + i] += jnp.broadcast_to(
                    -jnp.sum(sink_term[i * WINDOW:(i + 1) * WINDOW], axis=0, keepdims=True), (1, 128))
            dqn = _dot(dl_b, kn_b)
            dkn = _dot(dl_b, qn_b, TN)
            dvv = _dot(probs.astype(BF16), do_b, TN)
            dgq += _colsum(dqn * qhat)
            dqhat = dqn * gqv
            dq_ref[heads] = (r * (dqhat - qhat * jnp.mean(dqhat * qhat, axis=-1, keepdims=True))).astype(
                BF16).reshape(GQA, WINDOW, HEAD_DIM)
            dkp_ref[g] = dkn[:WINDOW]
            dkc_ref[g] = dkn[WINDOW:]
            dvp_ref[g] = dvv[:WINDOW]
            dvc_ref[g] = dvv[WINDOW:]
        dgq_ref[...] += dgq

    kv_shape = jax.ShapeDtypeStruct((nkv, s, HEAD_DIM), F32)
    return _ordered_call(
        body, name=name, grid=(nb,), in_specs=[qspec, cur, prev, cur, prev, gain, gain, sink, qspec],
        out_specs=(qspec, cur, cur, cur, cur, gain, sink),
        out_shape=(jax.ShapeDtypeStruct((nq, s, HEAD_DIM), BF16), kv_shape, kv_shape, kv_shape, kv_shape,
                   jax.ShapeDtypeStruct((1, HEAD_DIM), F32), jax.ShapeDtypeStruct((nq, 1, 128), F32)),
        compiler_params=_cp("arbitrary"),
    )(q, k, k, v, v, gq.reshape(1, HEAD_DIM), gk.reshape(1, HEAD_DIM), sinks_b, do)


def _attn_bwd_kv(name, k, gk, dkc, dkp, dvc, dvp):
    nkv, s, _ = k.shape
    nb = s // WINDOW
    _, cur, _, nxt, gain, _ = _attn_specs(GQA * nkv, nkv, nb)

    def body(k_ref, gk_ref, dkc_ref, dkp_ref, dvc_ref, dvp_ref, dk_ref, dv_ref, dgk_ref):
        n = pl.program_id(0)

        @pl.when(n == 0)
        def _():
            dgk_ref[...] = jnp.zeros_like(dgk_ref)

        has_next = n < nb - 1
        dgk = jnp.zeros((1, HEAD_DIM), F32)
        for g in range(nkv):
            dkn = dkc_ref[g] + jnp.where(has_next, dkp_ref[g], 0.0)
            dv_ref[g] = (dvc_ref[g] + jnp.where(has_next, dvp_ref[g], 0.0)).astype(BF16)
            khat, r = _head_rms(k_ref[g])
            dgk += _colsum(dkn * khat)
            dkhat = dkn * gk_ref[...]
            dk_ref[g] = (r * (dkhat - khat * jnp.mean(dkhat * khat, axis=-1, keepdims=True))).astype(BF16)
        dgk_ref[...] += dgk

    kv_shape = jax.ShapeDtypeStruct((nkv, s, HEAD_DIM), BF16)
    return _ordered_call(
        body, name=name, grid=(nb,), in_specs=[cur, gain, cur, nxt, cur, nxt], out_specs=(cur, cur, gain),
        out_shape=(kv_shape, kv_shape, jax.ShapeDtypeStruct((1, HEAD_DIM), F32)),
        compiler_params=_cp("arbitrary"),
    )(k, gk.reshape(1, HEAD_DIM), dkc, dkp, dvc, dvp)


SUBLANES = 8


def _fill_shifts(buf, shifts, tb):
    rows = tb + HALO - SUBLANES
    for b in range(1, SUBLANES):
        shifts[b - 1, pl.ds(0, rows), :] = buf[pl.ds(b, rows), :]


def _window(buf, shifts, off, tb):
    b = off % SUBLANES
    return buf[pl.ds(off, tb), :] if b == 0 else shifts[b - 1, pl.ds(off - b, tb), :]


def _conv_recompute(i, a_ref, gt_ref, ap_ref, gp_ref, w_ref, b_ref, hbuf, shifts, tb):
    hbuf[pl.ds(HALO, tb), :] = a_ref[...] * _sigmoid(gt_ref[...])
    tail = ap_ref[pl.ds(tb - HALO, HALO), :] * _sigmoid(gp_ref[pl.ds(tb - HALO, HALO), :])
    hbuf[pl.ds(0, HALO), :] = jnp.where(i > 0, tail, 0.0)
    _fill_shifts(hbuf, shifts, tb)
    acc = jnp.broadcast_to(b_ref[...], a_ref.shape)
    for kk in range(CONV_KERNEL):
        acc = acc + w_ref[pl.ds(kk, 1), :] * _window(hbuf, shifts, HALO - (CONV_KERNEL - 1) + kk, tb)
    return acc


def _layer_norm_stats(c):
    mu = jnp.mean(c, axis=-1, keepdims=True)
    xc = c - mu
    r = lax.rsqrt(jnp.mean(xc * xc, axis=-1, keepdims=True) + EPS)
    return xc * r, r


def _conv_specs(s, cw, tb, a_blk):
    cur = lambda off: pl.BlockSpec((tb, cw), lambda i: (i, a_blk + off))
    prev = lambda off: pl.BlockSpec((tb, cw), lambda i: (jnp.maximum(i - 1, 0), a_blk + off))
    wspec = pl.BlockSpec((HALO, cw), lambda i: (0, 0))
    vec = pl.BlockSpec((1, cw), lambda i: (0, 0))
    row = pl.BlockSpec((tb, cw), lambda i: (i, 0))
    return cur, prev, wspec, vec, row


def _conv_fwd(name, proj, a_blk, w, b, lg, lb):
    s = proj.shape[0]
    cw = w.shape[1]
    tb = _tile(s, 256)
    cur, prev, wspec, vec, row = _conv_specs(s, cw, tb, a_blk)

    def body(a_ref, gt_ref, ap_ref, gp_ref, w_ref, b_ref, lg_ref, lb_ref, y_ref, hbuf, shifts):
        c = _conv_recompute(pl.program_id(0), a_ref, gt_ref, ap_ref, gp_ref, w_ref, b_ref, hbuf, shifts, tb)
        chat, _ = _layer_norm_stats(c)
        z = chat * lg_ref[...] + lb_ref[...]
        y_ref[...] = z * _sigmoid(z)

    return _ordered_call(
        body, name=name, grid=(s // tb,), in_specs=[cur(0), cur(1), prev(0), prev(1), wspec, vec, vec, vec],
        out_specs=row, out_shape=jax.ShapeDtypeStruct((s, cw), F32),
        scratch_shapes=[pltpu.VMEM((tb + HALO, cw), F32), pltpu.VMEM((SUBLANES - 1, tb + HALO, cw), F32)],
        compiler_params=_cp("arbitrary"),
    )(proj, proj, proj, proj, w, b, lg, lb)


def _conv_bwd1(name, proj, a_blk, w, b, lg, lb, dy):
    s = proj.shape[0]
    cw = w.shape[1]
    tb = _tile(s, 256)
    cur, prev, wspec, vec, row = _conv_specs(s, cw, tb, a_blk)

    def body(a_ref, gt_ref, ap_ref, gp_ref, w_ref, b_ref, lg_ref, lb_ref, dy_ref,
             dc_ref, dw_ref, db_ref, dlg_ref, dlb_ref, hbuf, shifts):
        i = pl.program_id(0)

        @pl.when(i == 0)
        def _():
            dw_ref[...] = jnp.zeros_like(dw_ref)
            db_ref[...] = jnp.zeros_like(db_ref)
            dlg_ref[...] = jnp.zeros_like(dlg_ref)
            dlb_ref[...] = jnp.zeros_like(dlb_ref)

        c = _conv_recompute(i, a_ref, gt_ref, ap_ref, gp_ref, w_ref, b_ref, hbuf, shifts, tb)
        chat, r = _layer_norm_stats(c)
        z = chat * lg_ref[...] + lb_ref[...]
        sg = _sigmoid(z)
        dz = dy_ref[...] * (sg + z * sg * (1.0 - sg))
        dlg_ref[...] += _colsum(dz * chat)
        dlb_ref[...] += _colsum(dz)
        dzg = dz * lg_ref[...]
        dc = r * (dzg - jnp.mean(dzg, axis=-1, keepdims=True) - chat * jnp.mean(dzg * chat, axis=-1, keepdims=True))
        dc_ref[...] = dc
        db_ref[...] += _colsum(dc)
        for kk in range(CONV_KERNEL):
            dw_ref[pl.ds(kk, 1), :] += _colsum(dc * _window(hbuf, shifts, HALO - (CONV_KERNEL - 1) + kk, tb))

    return _ordered_call(
        body, name=name, grid=(s // tb,), in_specs=[cur(0), cur(1), prev(0), prev(1), wspec, vec, vec, vec, row],
        out_specs=(row, wspec, vec, vec, vec),
        out_shape=(jax.ShapeDtypeStruct((s, cw), F32), jax.ShapeDtypeStruct((HALO, cw), F32),
                   jax.ShapeDtypeStruct((1, cw), F32), jax.ShapeDtypeStruct((1, cw), F32),
                   jax.ShapeDtypeStruct((1, cw), F32)),
        scratch_shapes=[pltpu.VMEM((tb + HALO, cw), F32), pltpu.VMEM((SUBLANES - 1, tb + HALO, cw), F32)],
        compiler_params=_cp("arbitrary"),
    )(proj, proj, proj, proj, w, b, lg, lb, dy)


def _conv_bwd2(name, proj, a_blk, w, dc):
    s = proj.shape[0]
    cw = w.shape[1]
    tb = _tile(s, 256)
    nblk = s // tb
    cur, _, wspec, _, row = _conv_specs(s, cw, tb, a_blk)
    nxt = pl.BlockSpec((tb, cw), lambda i: (jnp.minimum(i + 1, nblk - 1), 0))

    def body(a_ref, gt_ref, w_ref, dc_ref, dn_ref, o_ref, dbuf, shifts):
        i = pl.program_id(0)
        dbuf[pl.ds(0, tb), :] = dc_ref[...]
        dbuf[pl.ds(tb, HALO), :] = jnp.where(i < nblk - 1, dn_ref[pl.ds(0, HALO), :], 0.0)
        _fill_shifts(dbuf, shifts, tb)
        dh = jnp.zeros((tb, cw), F32)
        for kk in range(CONV_KERNEL):
            dh = dh + w_ref[pl.ds(kk, 1), :] * _window(dbuf, shifts, CONV_KERNEL - 1 - kk, tb)
        sg = _sigmoid(gt_ref[...])
        o_ref[:, 0:cw] = (dh * sg).astype(BF16)
        o_ref[:, cw:2 * cw] = (dh * a_ref[...] * sg * (1.0 - sg)).astype(BF16)

    return _ordered_call(
        body, name=name, grid=(nblk,), in_specs=[cur(0), cur(1), wspec, row, nxt],
        out_specs=pl.BlockSpec((tb, 2 * cw), lambda i: (i, 0)), out_shape=jax.ShapeDtypeStruct((s, 2 * cw), BF16),
        scratch_shapes=[pltpu.VMEM((tb + HALO, cw), F32), pltpu.VMEM((SUBLANES - 1, tb + HALO, cw), F32)],
        compiler_params=_cp("arbitrary"),
    )(proj, proj, w, dc, dc)


def _sgu_common(v_ref, lg_ref, lb_ref, w_ref, bexp_ref, sw):
    vhat, r = _layer_norm_stats(v_ref[...])
    vn_b = (vhat * lg_ref[...] + lb_ref[...]).astype(BF16)
    ii = lax.broadcasted_iota(jnp.int32, (WINDOW, WINDOW), 0)
    jj = lax.broadcasted_iota(jnp.int32, (WINDOW, WINDOW), 1)
    tril = jj <= ii
    head_of = lax.broadcasted_iota(jnp.int32, (WINDOW, sw), 1) // HEAD_DIM
    wts = [jnp.where(tril, w_ref[h], 0.0).astype(BF16) for h in range(sw // HEAD_DIM)]
    sv = bexp_ref[...]
    for h, wt in enumerate(wts):
        sv = sv + jnp.where(head_of == h, _dot(wt, vn_b), 0.0)
    return vhat, r, vn_b, tril, head_of, wts, sv


def _sgu_specs(sw, u_blk):
    nh = sw // HEAD_DIM
    u = pl.BlockSpec((WINDOW, sw), lambda n: (n, u_blk))
    v = pl.BlockSpec((WINDOW, sw), lambda n: (n, u_blk + 1))
    vec = pl.BlockSpec((1, sw), lambda n: (0, 0))
    wspec = pl.BlockSpec((nh, WINDOW, WINDOW), lambda n: (0, 0, 0))
    bspec = pl.BlockSpec((WINDOW, sw), lambda n: (0, 0))
    row = pl.BlockSpec((WINDOW, sw), lambda n: (n, 0))
    return u, v, vec, wspec, bspec, row


def _sgu_fwd(name, proj, u_blk, lg, lb, w, bexp):
    s = proj.shape[0]
    sw = lg.shape[1]
    u, v, vec, wspec, bspec, row = _sgu_specs(sw, u_blk)

    def body(u_ref, v_ref, lg_ref, lb_ref, w_ref, bexp_ref, y_ref):
        sv = _sgu_common(v_ref, lg_ref, lb_ref, w_ref, bexp_ref, sw)[-1]
        y_ref[...] = u_ref[...] * sv

    return _ordered_call(
        body, name=name, grid=(s // WINDOW,), in_specs=[u, v, vec, vec, wspec, bspec], out_specs=row,
        out_shape=jax.ShapeDtypeStruct((s, sw), F32), compiler_params=_cp("parallel"),
    )(proj, proj, lg, lb, w, bexp)


def _sgu_bwd(name, proj, u_blk, lg, lb, w, bexp, dy):
    s = proj.shape[0]
    sw = lg.shape[1]
    nh = sw // HEAD_DIM
    u, v, vec, wspec, bspec, row = _sgu_specs(sw, u_blk)
    dbspec = pl.BlockSpec((nh, WINDOW), lambda n: (0, 0))

    def body(u_ref, v_ref, lg_ref, lb_ref, w_ref, bexp_ref, dy_ref, o_ref, dw_ref, db_ref, dlg_ref, dlb_ref):
        n = pl.program_id(0)

        @pl.when(n == 0)
        def _():
            dw_ref[...] = jnp.zeros_like(dw_ref)
            db_ref[...] = jnp.zeros_like(db_ref)
            dlg_ref[...] = jnp.zeros_like(dlg_ref)
            dlb_ref[...] = jnp.zeros_like(dlb_ref)

        vhat, r, vn_b, tril, head_of, wts, sv = _sgu_common(v_ref, lg_ref, lb_ref, w_ref, bexp_ref, sw)
        dyv = dy_ref[...]
        o_ref[:, 0:sw] = (dyv * sv).astype(BF16)
        ds = dyv * u_ref[...]
        dvn = jnp.zeros((WINDOW, sw), F32)
        for h, wt in enumerate(wts):
            dsm_b = jnp.where(head_of == h, ds, 0.0).astype(BF16)
            dvn = dvn + _dot(wt, dsm_b, TN)
            dw_ref[h] += jnp.where(tril, _dot(dsm_b, vn_b, NT), 0.0)
        hmask = (lax.broadcasted_iota(jnp.int32, (nh, sw), 1) // HEAD_DIM
                 == lax.broadcasted_iota(jnp.int32, (nh, sw), 0)).astype(F32)
        db_ref[...] += lax.dot_general(hmask, ds, NT, precision=lax.Precision.HIGHEST, preferred_element_type=F32)
        dlg_ref[...] += _colsum(dvn * vhat)
        dlb_ref[...] += _colsum(dvn)
        dvg = dvn * lg_ref[...]
        dv = r * (dvg - jnp.mean(dvg, axis=-1, keepdims=True) - vhat * jnp.mean(dvg * vhat, axis=-1, keepdims=True))
        o_ref[:, sw:2 * sw] = dv.astype(BF16)

    return _ordered_call(
        body, name=name, grid=(s // WINDOW,), in_specs=[u, v, vec, vec, wspec, bspec, row],
        out_specs=(pl.BlockSpec((WINDOW, 2 * sw), lambda n: (n, 0)), wspec, dbspec, vec, vec),
        out_shape=(jax.ShapeDtypeStruct((s, 2 * sw), BF16), jax.ShapeDtypeStruct((nh, WINDOW, WINDOW), F32),
                   jax.ShapeDtypeStruct((nh, WINDOW), F32), jax.ShapeDtypeStruct((1, sw), F32),
                   jax.ShapeDtypeStruct((1, sw), F32)),
        compiler_params=_cp("arbitrary"),
    )(proj, proj, lg, lb, w, bexp, dy)


def _adamw(name, w, g, m, v):
    rows, cols = w.shape
    tr = _tile(rows, 256)

    def body(w_ref, g_ref, m_ref, v_ref, d_ref, nm_ref, nv_ref):
        gv = g_ref[...]
        mv = ADAM_B1 * m_ref[...] + (1.0 - ADAM_B1) * gv
        vv = ADAM_B2 * v_ref[...] + (1.0 - ADAM_B2) * (gv * gv)
        m_hat = mv / (1.0 - ADAM_B1 ** ADAM_STEP)
        v_hat = vv / (1.0 - ADAM_B2 ** ADAM_STEP)
        d_ref[...] = -ADAM_LR * (m_hat / (jnp.sqrt(v_hat) + ADAM_EPS) + ADAM_WD * w_ref[...])
        nm_ref[...] = mv
        nv_ref[...] = vv

    spec = pl.BlockSpec((tr, cols), lambda i: (i, 0))
    shape = jax.ShapeDtypeStruct((rows, cols), F32)
    return _ordered_call(
        body, name=name, grid=(rows // tr,), in_specs=[spec] * 4, out_specs=(spec,) * 3, out_shape=(shape,) * 3,
        compiler_params=_cp("parallel"),
    )(w, g, m, v)


def _route():
    x, y, c = lax.axis_index("x"), lax.axis_index("y"), lax.axis_index("c")
    n1 = (jnp.where(c == 0, 1 - x, x), jnp.where(c == 0, y, 1 - y))
    n2 = (jnp.where(c == 0, x, 1 - x), jnp.where(c == 0, 1 - y, y))
    return x, y, c, n1, n2, (1 - x, 1 - y)


def _cidx(chip):
    return 2 * chip[0] + chip[1]


def _remote(src, dst, sems, k, device):
    send_sems, recv_sems = sems
    return pltpu.make_async_remote_copy(src_ref=src, dst_ref=dst, send_sem=send_sems.at[k], recv_sem=recv_sems.at[k],
                                        device_id=device, device_id_type=MESH)


def _exchange(name, bufs, n_sems, build):
    n = len(bufs)

    def body(*refs):
        cps = build(refs[n:2 * n], (refs[2 * n], refs[2 * n + 1]))
        for cp in cps:
            cp.start()
        for cp in cps:
            cp.wait()

    return _ordered_call(
        body, name=name, in_specs=[ANY] * n, out_specs=tuple(ANY for _ in range(n)),
        out_shape=tuple(jax.ShapeDtypeStruct(b.shape, b.dtype) for b in bufs),
        input_output_aliases={i: i for i in range(n)},
        scratch_shapes=[pltpu.SemaphoreType.DMA((n_sems,)), pltpu.SemaphoreType.DMA((n_sems,))],
        compiler_params=pltpu.CompilerParams(has_side_effects=True),
    )(*bufs)


HBM_SPEC = pl.BlockSpec(memory_space=pltpu.HBM)
SEM_SPEC = pl.BlockSpec(memory_space=pltpu.SEMAPHORE)
DATAFLOW = pltpu.SideEffectType.DATAFLOW_SIDE_EFFECTING


def _exchange_start(name, bufs, n_sems, build):
    n = len(bufs)

    def body(*refs):
        for cp in build(refs[:n], (refs[n], refs[n + 1])):
            cp.start()
        refs[-1][...] = jnp.zeros_like(refs[-1])

    out = _ordered_call(
        body, name=name,
        out_shape=(pltpu.SemaphoreType.DMA((n_sems,)), pltpu.SemaphoreType.DMA((n_sems,)))
        + tuple(pltpu.HBM(b.shape, b.dtype) for b in bufs) + (jax.ShapeDtypeStruct((8, 128), F32),),
        in_specs=[HBM_SPEC] * n,
        out_specs=(SEM_SPEC, SEM_SPEC) + (HBM_SPEC,) * n + (pl.BlockSpec(memory_space=pltpu.VMEM),),
        input_output_aliases={i: 2 + i for i in range(n)},
        compiler_params=pltpu.CompilerParams(has_side_effects=DATAFLOW),
    )(*[pltpu.with_memory_space_constraint(b, pltpu.HBM) for b in bufs])
    return dict(name=name, send=out[0], recv=out[1], bufs=list(out[2:2 + n]), token=out[-1], build=build)


def _exchange_wait(handle):
    n = len(handle["bufs"])

    def body(*refs):
        for cp in handle["build"](refs[:n], (refs[n], refs[n + 1])):
            cp.wait_send()
            cp.wait_recv()

    return list(_ordered_call(
        body, name=handle["name"] + "_wait", out_shape=tuple(pltpu.HBM(b.shape, b.dtype) for b in handle["bufs"]),
        in_specs=[HBM_SPEC] * n + [SEM_SPEC, SEM_SPEC], out_specs=(HBM_SPEC,) * n,
        input_output_aliases={i: i for i in range(n)},
        compiler_params=pltpu.CompilerParams(has_side_effects=DATAFLOW),
    )(*handle["bufs"], handle["send"], handle["recv"]))


def _cast_place(name, w, l, me_idx, dtype):
    _, r, c = w.shape
    tr = _tile(r, 512)

    def body(me_ref, w_ref, o_ref):
        o_ref[...] = w_ref[...].astype(dtype)

    grid_spec = pltpu.PrefetchScalarGridSpec(
        num_scalar_prefetch=1, grid=(r // tr,),
        in_specs=[pl.BlockSpec((None, tr, c), lambda i, me_ref: (l, i, 0))],
        out_specs=pl.BlockSpec((None, tr, c), lambda i, me_ref: (me_ref[0], i, 0)))
    return _ordered_call(
        body, name=name, grid_spec=grid_spec, out_shape=jax.ShapeDtypeStruct((N_CHIPS, r, c), dtype),
        compiler_params=_cp("arbitrary"),
    )(me_idx, w)


def _my_half(ref, blk, c):
    hr = ref.shape[1] // 2
    return ref.at[blk, pl.ds(c * hr, hr), :]


def _gather_step(entering):
    lens = [len(e) for e in entering]
    flat = [b for e in entering for b in e]

    def build(refs, sems):
        x, y, c, n1, n2, dg = _route()
        me = _cidx((x, y))
        plan = ([(r, (me,), (*n1, c)) for r in refs[:lens[0]]]
                + [(r, (me, _cidx(n1)), (*n2, c)) for r in refs[lens[0]:lens[0] + lens[1]]]
                + [(r, (_cidx(n1), _cidx(n2), _cidx(dg)), (x, y, 1 - c)) for r in refs[lens[0] + lens[1]:]])
        cps = []
        for ref, blocks, peer in plan:
            for blk in blocks:
                cps.append(_remote(_my_half(ref, blk, c), _my_half(ref, blk, c), sems, len(cps), peer))
        return cps

    return flat, lens[0] + 2 * lens[1] + 3 * lens[2], build


RI_C, RI_ME, RI_N2, RI_N1 = 0, 1, 2, 3


def _pair_sum(name, g, sib, route_idx):
    _, rows, cols = g.shape
    hr = rows // 2
    tr = _tile(hr, 512)
    per = hr // tr

    def body(ri, g_ref, s_ref, o_ref):
        o_ref[...] = (g_ref[...].astype(F32) + s_ref[...].astype(F32)).astype(BF16)

    blk = (None, tr, cols)
    grid_spec = pltpu.PrefetchScalarGridSpec(
        num_scalar_prefetch=1, grid=(2, per),
        in_specs=[pl.BlockSpec(blk, lambda j, i, ri: (ri[RI_N1 + j], ri[RI_C] * per + i, 0)),
                  pl.BlockSpec(blk, lambda j, i, ri: (ri[RI_N1 + j], i, 0))],
        out_specs=pl.BlockSpec(blk, lambda j, i, ri: (j, i, 0)))
    return _ordered_call(
        body, name=name, grid_spec=grid_spec, out_shape=jax.ShapeDtypeStruct((2, hr, cols), BF16),
        compiler_params=_cp("parallel", "parallel"),
    )(route_idx, g, sib)


def _sum_stage1(name, g, sib, got, route_idx):
    _, hr, cols = sib.shape
    tr = _tile(hr, 512)
    per = hr // tr

    def body(ri, gm_ref, sm_ref, gn_ref, sn_ref, g0_ref, g1_ref, keep_ref, send_ref):
        keep_ref[...] = (gm_ref[...].astype(F32) + sm_ref[...].astype(F32)) + g0_ref[...].astype(F32)
        send_ref[...] = ((gn_ref[...].astype(F32) + sn_ref[...].astype(F32)) + g1_ref[...].astype(F32)).astype(BF16)

    blk = (None, tr, cols)
    row = pl.BlockSpec((tr, cols), lambda i, ri: (i, 0))

    def mine(which):
        return pl.BlockSpec(blk, lambda i, ri: (ri[which], ri[RI_C] * per + i, 0))

    def theirs(which):
        return pl.BlockSpec(blk, lambda i, ri: (ri[which], i, 0))

    grid_spec = pltpu.PrefetchScalarGridSpec(
        num_scalar_prefetch=1, grid=(per,),
        in_specs=[mine(RI_ME), theirs(RI_ME), mine(RI_N2), theirs(RI_N2),
                  pl.BlockSpec(blk, lambda i, ri: (0, i, 0)), pl.BlockSpec(blk, lambda i, ri: (1, i, 0))],
        out_specs=(row, row))
    return _ordered_call(
        body, name=name, grid_spec=grid_spec,
        out_shape=(jax.ShapeDtypeStruct((hr, cols), F32), jax.ShapeDtypeStruct((hr, cols), BF16)),
        compiler_params=_cp("parallel"),
    )(route_idx, g, sib, g, sib, got, got)


def _sum_stage2(name, keep, got):
    hr, cols = keep.shape
    tr = _tile(hr, 512)

    def body(k_ref, g_ref, o_ref):
        o_ref[...] = k_ref[...] + g_ref[...].astype(F32)

    row = pl.BlockSpec((tr, cols), lambda i: (i, 0))
    return _ordered_call(
        body, name=name, grid=(hr // tr,), in_specs=[row, row], out_specs=row,
        out_shape=jax.ShapeDtypeStruct((hr, cols), F32), compiler_params=_cp("parallel"),
    )(keep, got)


def _reduce_scatter(tag, names, grads, route_idx):
    n = len(grads)
    hrs = [g.shape[1] // 2 for g in grads]

    def empty(t, lead, dtype):
        return lax.empty(lead + (hrs[t], grads[t].shape[2]), dtype)

    def pair_stage(refs, sems):
        x, y, c, n1, n2, dg = _route()
        return [_remote(refs[t].at[:, pl.ds((1 - c) * hrs[t], hrs[t]), :], refs[n + t], sems, t, (x, y, 1 - c))
                for t in range(n)]

    def stage1(refs, sems):
        x, y, c, n1, n2, dg = _route()
        return [_remote(refs[t].at[slot], refs[n + t].at[slot], sems, 2 * t + slot, (*n1, c))
                for t in range(n) for slot in range(2)]

    def stage2(refs, sems):
        x, y, c, n1, n2, dg = _route()
        return [_remote(refs[t], refs[n + t], sems, t, (*n2, c)) for t in range(n)]

    def stage3(refs, sems):
        x, y, c, n1, n2, dg = _route()
        return [_remote(refs[t], refs[n + t], sems, t, (x, y, 1 - c)) for t in range(n)]

    state = {}

    def start():
        state["h"] = _exchange_start(f"rs_pair_{tag}", list(grads) + [empty(t, (N_CHIPS,), BF16) for t in range(n)],
                                     n, pair_stage)

    def pair_done():
        state["pair"] = _exchange_wait(state["h"])
        psum = [_pair_sum(f"rs_psum_{names[t]}", state["pair"][t], state["pair"][n + t], route_idx) for t in range(n)]
        state["h"] = _exchange_start(f"rs_x1_{tag}", psum + [empty(t, (2,), BF16) for t in range(n)], 2 * n, stage1)

    def x1_done():
        out = _exchange_wait(state["h"])
        state["keep"], send = zip(*[_sum_stage1(f"rs_sum1_{names[t]}", state["pair"][t], state["pair"][n + t],
                                                out[n + t], route_idx) for t in range(n)])
        state["h"] = _exchange_start(f"rs_x2_{tag}", list(send) + [empty(t, (), BF16) for t in range(n)], n, stage2)

    def x2_done():
        out = _exchange_wait(state["h"])
        mine = [_sum_stage2(f"rs_sum2_{names[t]}", state["keep"][t], out[n + t]) for t in range(n)]
        state["h"] = _exchange_start(f"rs_half_{tag}", mine + [empty(t, (), F32) for t in range(n)], n, stage3)

    def finish():
        out = _exchange_wait(state["h"])
        return list(zip(out[:n], out[n:]))

    return start, pair_done, x1_done, x2_done, finish


def _adamw_big(name, w, m, v, f, h, l, c_idx, prev):
    n_l, r, cols = w.shape
    hr = r // 2
    tr = _tile(hr, 256)
    per = hr // tr

    def body(c_ref, w_ref, m_ref, v_ref, f_ref, h_ref, *rest):
        g_ref, d_ref, nm_ref, nv_ref = rest[-4:]
        gv = jnp.where(pl.program_id(0) == c_ref[0], f_ref[...], h_ref[...])
        mv = ADAM_B1 * m_ref[...] + (1.0 - ADAM_B1) * gv
        vv = ADAM_B2 * v_ref[...] + (1.0 - ADAM_B2) * (gv * gv)
        m_hat = mv / (1.0 - ADAM_B1 ** ADAM_STEP)
        v_hat = vv / (1.0 - ADAM_B2 ** ADAM_STEP)
        g_ref[...] = gv
        d_ref[...] = -ADAM_LR * (m_hat / (jnp.sqrt(v_hat) + ADAM_EPS) + ADAM_WD * w_ref[...])
        nm_ref[...] = mv
        nv_ref[...] = vv

    big = pl.BlockSpec((None, tr, cols), lambda hf, i, c_ref: (l, hf * per + i, 0))
    fspec = pl.BlockSpec((tr, cols), lambda hf, i, c_ref: (jnp.where(hf == c_ref[0], i, 0), 0))
    hspec = pl.BlockSpec((tr, cols), lambda hf, i, c_ref: (jnp.where(hf == c_ref[0], 0, i), 0))
    grid_spec = pltpu.PrefetchScalarGridSpec(
        num_scalar_prefetch=1, grid=(2, per), in_specs=[big] * 3 + [fspec, hspec] + [ANY] * len(prev),
        out_specs=(big,) * 4)
    return _ordered_call(
        body, name=name, grid_spec=grid_spec, out_shape=(jax.ShapeDtypeStruct(w.shape, F32),) * 4,
        input_output_aliases={6 + k: k for k in range(len(prev))}, compiler_params=_cp("arbitrary", "arbitrary"),
    )(c_idx, w, m, v, f, h, *prev)


def _small_allreduce(buf):
    rows = buf.shape[0]
    hr = rows // 2

    def body(in_ref, out_ref, pair, acc, got1, got2, send_sems, recv_sems):
        x, y, c, n1, n2, dg = _route()
        sems = (send_sems, recv_sems)
        sibling = (x, y, 1 - c)
        mine = pl.ds(pl.multiple_of(c * hr, 8), hr)
        pair[c] = in_ref[...]
        cp = _remote(in_ref, pair.at[c], sems, 0, sibling)
        cp.start()
        cp.wait()
        acc[...] = pair[0, mine, :] + pair[1, mine, :]
        cp = _remote(acc, got1, sems, 1, (*n1, c))
        cp.start()
        cp.wait()
        acc[...] = acc[...] + got1[...]
        cp = _remote(acc, got2, sems, 2, (*n2, c))
        cp.start()
        cp.wait()
        out_ref[mine, :] = acc[...] + got2[...]
        cp = _remote(out_ref.at[mine, :], out_ref.at[mine, :], sems, 3, sibling)
        cp.start()
        cp.wait()

    half = pltpu.VMEM((hr, 128), F32)
    return _ordered_call(
        body, name="small_allreduce", in_specs=[pl.BlockSpec(memory_space=pltpu.VMEM)],
        out_specs=pl.BlockSpec(memory_space=pltpu.VMEM), out_shape=jax.ShapeDtypeStruct((rows, 128), F32),
        scratch_shapes=[pltpu.VMEM((2, rows, 128), F32), half, half, half,
                        pltpu.SemaphoreType.DMA((4,)), pltpu.SemaphoreType.DMA((4,))],
        compiler_params=pltpu.CompilerParams(has_side_effects=True, vmem_limit_bytes=VMEM_LIMIT),
    )(buf)


BIG = ("w_in", "w_out", "w_up", "w_down")
COL_SHARDED = {"w_in": True, "w_out": False, "w_up": True, "w_down": False}
SMALL = ("ln1_g", "q_norm_g", "k_norm_g", "sinks", "conv_w", "conv_b", "conv_ln_g", "conv_ln_b", "sgu_ln_g",
         "sgu_ln_b", "sgu_w", "sgu_b", "out_norm_g", "ln2_g")
WEIGHTS = ("ln1_g", "w_in", "q_norm_g", "k_norm_g", "sinks", "conv_w", "conv_b", "conv_ln_g", "conv_ln_b",
           "sgu_ln_g", "sgu_ln_b", "sgu_w", "sgu_b", "out_norm_g", "w_out", "ln2_g", "w_up", "w_down")
PACK_QUANTUM = 8 * 128
PACK_ROWS = 512


def _pack(arrs):
    parts = []
    for a in arrs:
        f = a.reshape(-1)
        parts.append(jnp.pad(f, (0, -f.shape[0] % PACK_QUANTUM)).reshape(-1, 128))
    rows = sum(p.shape[0] for p in parts)
    parts.append(jnp.zeros((-rows % PACK_ROWS, 128), F32))
    return jnp.concatenate(parts, axis=0)


def _unpack(buf, shapes):
    out, off = [], 0
    for shp in shapes:
        n = 1
        for dd in shp:
            n *= dd
        rows = (n + PACK_QUANTUM - 1) // PACK_QUANTUM * 8
        out.append(buf[off:off + rows].reshape(-1)[:n].reshape(shp))
        off += rows
    return out


def _to_heads(t, nh):
    return t.reshape(t.shape[0], nh, HEAD_DIM).transpose(1, 0, 2)


def _from_heads(t):
    return t.transpose(1, 0, 2).reshape(t.shape[1], t.shape[0] * HEAD_DIM)


def _no_hook(point, carry):
    return carry


def _layer_fwd(l, x, p, wg, hook=_no_hook):
    d = x.shape[1]
    aw, cw = d // 2, d // 4
    nq = aw // HEAD_DIM
    nkv = nq // GQA
    kvw = nkv * HEAD_DIM
    x = hook("fwd_start", x)
    h1 = _rms_fwd(f"ln1_fwd_{l}", x, p["ln1_g"])
    proj = _mm_act_w(f"proj_{l}", h1, wg["w_in"], True, _ep_store)[0]
    proj = hook("fwd_proj", proj)
    q = _to_heads(proj[:, :aw], nq)
    k = _to_heads(proj[:, aw:aw + kvw], nkv)
    v = _to_heads(proj[:, aw + kvw:aw + 2 * kvw], nkv)
    sinks_b = jnp.broadcast_to(p["sinks"][:, None, None], (nq, 1, 128))
    ya = _from_heads(_attn_fwd(f"attn_fwd_{l}", q, k, v, p["q_norm_g"], p["k_norm_g"], sinks_b))
    ya = hook("fwd_attn", ya)
    yc = _conv_fwd(f"conv_fwd_{l}", proj, 3, p["conv_w"], p["conv_b"], p["conv_ln_g"], p["conv_ln_b"])
    ys = _sgu_fwd(f"sgu_fwd_{l}", proj, 5, p["sgu_ln_g"], p["sgu_ln_b"], p["sgu_w"], p["sgu_bexp"])
    mix = _mixnorm_fwd(f"mixnorm_fwd_{l}", ya, yc, ys, p["out_norm_g"])
    mix = hook("fwd_mid", mix)
    xm = _mm_act_w(f"out_{l}", mix, wg["w_out"], False, _ep_residual, extra=(x,))[0]
    h2 = _rms_fwd(f"ln2_fwd_{l}", xm, p["ln2_g"])
    h2 = hook("fwd_ln2", h2)
    up_b, act_b = _mm_act_w(f"up_{l}", h2, wg["w_up"], True, _ep_up, out_dtypes=(BF16, BF16))
    act_b = hook("fwd_up", act_b)
    xo = _mm_act_w(f"down_{l}", act_b, wg["w_down"], False, _ep_residual, extra=(xm,))[0]
    xo = hook("fwd_end", xo)
    saved = dict(x=x, h1=h1, proj=proj, q=q, k=k, v=v, sinks_b=sinks_b, ya=ya, yc=yc, ys=ys, mix=mix, xm=xm, h2=h2,
                 up_b=up_b, act_b=act_b)
    return xo, saved


def _layer_bwd(l, dxo, dxo_b, p, wg, sv, big, hook=_no_hook):
    d = dxo.shape[1]
    nq = (d // 2) // HEAD_DIM
    small = {}
    dxo_b = hook("bwd_start", dxo_b)
    big["w_down"] = _mm_wgrad(f"dw_down_{l}", sv["act_b"], dxo_b, False, d)
    dup_b = _mm_act_wt(f"dup_{l}", dxo_b, wg["w_down"], False, _ep_dup, extra=(sv["up_b"],), out_dtypes=(BF16,))[0]
    dup_b = hook("bwd_dup", dup_b)
    big["w_up"] = _mm_wgrad(f"dw_up_{l}", sv["h2"], dup_b, True, wg["w_up"].shape[2])
    dh2 = _mm_act_wt(f"dh2_{l}", dup_b, wg["w_up"], True, _ep_store)[0]
    dh2 = hook("bwd_dh2", dh2)
    dxm, dxm_b, small["ln2_g"] = _rms_bwd(f"ln2_bwd_{l}", dh2, sv["xm"], p["ln2_g"], dxo)
    big["w_out"] = _mm_wgrad(f"dw_out_{l}", sv["mix"], dxm_b, False, d)
    dmix = _mm_act_wt(f"dmix_{l}", dxm_b, wg["w_out"], False, _ep_store)[0]
    dya, dyc, dys, small["out_norm_g"] = _mixnorm_bwd(f"mixnorm_bwd_{l}", dmix, sv["ya"], sv["yc"], sv["ys"],
                                                      p["out_norm_g"])
    dya = hook("bwd_mix", dya)
    dq, dkc, dkp, dvc, dvp, small["q_norm_g"], dsink = _attn_bwd(
        f"attn_bwd_{l}", sv["q"], sv["k"], sv["v"], p["q_norm_g"], p["k_norm_g"], sv["sinks_b"], _to_heads(dya, nq))
    dkc = hook("bwd_attn", dkc)
    small["sinks"] = dsink[:, 0, 0]
    dk, dv, small["k_norm_g"] = _attn_bwd_kv(f"attn_bwd_kv_{l}", sv["k"], p["k_norm_g"], dkc, dkp, dvc, dvp)
    dc, dcw, small["conv_b"], small["conv_ln_g"], small["conv_ln_b"] = _conv_bwd1(
        f"conv_bwd1_{l}", sv["proj"], 3, p["conv_w"], p["conv_b"], p["conv_ln_g"], p["conv_ln_b"], dyc)
    small["conv_w"] = dcw[:CONV_KERNEL]
    dxc_b = _conv_bwd2(f"conv_bwd2_{l}", sv["proj"], 3, p["conv_w"], dc)
    dxs_b, small["sgu_w"], small["sgu_b"], small["sgu_ln_g"], small["sgu_ln_b"] = _sgu_bwd(
        f"sgu_bwd_{l}", sv["proj"], 5, p["sgu_ln_g"], p["sgu_ln_b"], p["sgu_w"], p["sgu_bexp"], dys)
    dxs_b = hook("bwd_sgu", dxs_b)
    dproj_b = jnp.concatenate([_from_heads(dq), _from_heads(dk), _from_heads(dv), dxc_b, dxs_b], axis=1)
    big["w_in"] = _mm_wgrad(f"dw_in_{l}", sv["h1"], dproj_b, True, wg["w_in"].shape[2])
    dh1 = _mm_act_wt(f"dh1_{l}", dproj_b, wg["w_in"], True, _ep_store)[0]
    dx, dx_b, small["ln1_g"] = _rms_bwd(f"ln1_bwd_{l}", dh1, sv["x"], p["ln1_g"], dxm)
    dx_b = hook("bwd_end", dx_b)
    return dx, dx_b, small


def kernel(x, ln1_g, w_in, q_norm_g, k_norm_g, sinks, conv_w, conv_b, conv_ln_g, conv_ln_b, sgu_ln_g, sgu_ln_b, sgu_w, sgu_b, out_norm_g, w_out, ln2_g, w_up, w_down, loss_target, m_ln1_g, m_w_in, m_q_norm_g, m_k_norm_g, m_sinks, m_conv_w, m_conv_b, m_conv_ln_g, m_conv_ln_b, m_sgu_ln_g, m_sgu_ln_b, m_sgu_w, m_sgu_b, m_out_norm_g, m_w_out, m_ln2_g, m_w_up, m_w_down, v_ln1_g, v_w_in, v_q_norm_g, v_k_norm_g, v_sinks, v_conv_w, v_conv_b, v_conv_ln_g, v_conv_ln_b, v_sgu_ln_g, v_sgu_ln_b, v_sgu_w, v_sgu_b, v_out_norm_g, v_w_out, v_ln2_g, v_w_up, v_w_down):
    given = dict(locals())
    _LAST[0] = None
    n_layers = ln1_g.shape[0]
    s, d = x.shape[1], x.shape[2]
    cw = d // 4
    xi, yi, core = lax.axis_index("x"), lax.axis_index("y"), lax.axis_index("c")
    chip = 2 * xi + yi
    first_partner = jnp.where(core == 0, 2 * (1 - xi) + yi, 2 * xi + (1 - yi))
    second_partner = jnp.where(core == 0, 2 * xi + (1 - yi), 2 * (1 - xi) + yi)
    route_idx = jnp.stack([core, chip, second_partner, first_partner, 3 - chip]).astype(jnp.int32)

    conv_w_pad = jnp.pad(conv_w, ((0, 0), (0, HALO - CONV_KERNEL), (0, 0))).reshape(1, n_layers * HALO, -1)
    cwl = conv_w_pad.shape[2]
    buf = {}

    def place(key):
        if key == "conv_w":
            buf[key] = _cast_place("place_conv_w", conv_w_pad, 0, route_idx[1:2], F32)
        else:
            buf[key] = _cast_place(f"place_{key[0]}_{key[1]}", given[key[0]], key[1], route_idx[1:2], BF16)

    groups = [["conv_w", ("w_in", 0)]] + [[(nm, l)] for l in range(n_layers) for nm in BIG if (nm, l) != ("w_in", 0)]
    n_steps = len(groups) + 2
    pending = {}

    def start_step(st):
        keys = [groups[st - j] if 0 <= st - j < len(groups) else [] for j in range(3)]
        flat, n_sems, build = _gather_step([[buf[k] for k in ks] for ks in keys])
        pending["keys"] = [k for ks in keys for k in ks]
        pending["h"] = _exchange_start(f"gather_step{st}", flat, n_sems, build)

    def wait_step():
        for k, b in zip(pending["keys"], _exchange_wait(pending["h"])):
            buf[k] = b

    later = [k for grp in groups[1:] for k in grp]
    for k in groups[0]:
        place(k)
    for st, upto in enumerate((2, 5, len(later))):
        start_step(st)
        for k in later[:upto]:
            if k not in buf:
                place(k)
        wait_step()
    conv_w_full = buf["conv_w"].reshape(N_CHIPS, n_layers, HALO, cwl).transpose(1, 2, 0, 3).reshape(
        n_layers, HALO, cw)

    class LayerWeights:
        def __init__(self, l):
            self.l = l

        def __getitem__(self, nm):
            return buf[(nm, self.l)]

    wgs = [LayerWeights(l) for l in range(n_layers)]
    fwd_points = [(l, pt) for l in range(n_layers) for pt in ("fwd_start", "fwd_attn", "fwd_ln2", "fwd_up", "fwd_end")
                  if (pt != "fwd_start" or l == 0) and (pt != "fwd_end" or l + 1 < n_layers)]
    assert len(fwd_points) == n_steps - 3 + 1, "one hook point per pipeline step, and one to wait for the last"
    fwd_tables = [{} for _ in range(n_layers)]
    for i, (l, pt) in enumerate(fwd_points):
        if i > 0:
            fwd_tables[l].setdefault(pt, []).append(wait_step)
        if 3 + i < n_steps:
            fwd_tables[l].setdefault(pt, []).append(functools.partial(start_step, 3 + i))
    params = []
    for l in range(n_layers):
        p = {nm: given[nm][l] for nm in SMALL if nm != "conv_w"}
        for nm in ("conv_b", "conv_ln_g", "conv_ln_b", "sgu_ln_g", "sgu_ln_b"):
            p[nm] = p[nm].reshape(1, -1)
        p["conv_w"] = conv_w_full[l]
        p["sgu_bexp"] = jnp.repeat(sgu_b[l].T, HEAD_DIM, axis=1)
        params.append(p)

    def make_hook(table):
        def hook(point, carry):
            for fn in table.get(point, ()):
                fn()
            return carry
        return hook

    h = x.reshape(s, d)
    saved = []
    for l in range(n_layers):
        h, sv = _layer_fwd(l, h, params[l], wgs[l], make_hook(fwd_tables[l]))
        saved.append(sv)
    dh, dh_b, loss_part = _loss_head(h, loss_target.reshape(s, d))
    loss = lax.psum(loss_part[0, 0], ("x", "y", "c"))

    big_grads = [{} for _ in range(n_layers)]
    small_grads = [None] * n_layers
    halves = {}

    def rs_group(tag, l, names):
        phases = {}

        def start():
            phases["p"] = _reduce_scatter(tag, [f"{nm}_{l}" for nm in names], [big_grads[l][nm] for nm in names],
                                          route_idx)
            phases["p"][0]()

        def step(k):
            return lambda: phases["p"][k]()

        def finish():
            for nm, fh in zip(names, phases["p"][4]()):
                halves[(nm, l)] = fh

        return [start, step(1), step(2), step(3), finish]

    early = rs_group("l0a", 0, ("w_down", "w_up", "w_out"))
    for l in reversed(range(n_layers)):
        table = {}
        if l + 1 < n_layers:
            above = rs_group(f"l{l + 1}", l + 1, BIG)
            for point, fn in zip(("bwd_start", "bwd_dup", "bwd_mix", "bwd_attn", "bwd_sgu"), above):
                table.setdefault(point, []).append(fn)
        if l == 0:
            for point, fn in zip(("bwd_mix", "bwd_attn", "bwd_end"), early[:3]):
                table.setdefault(point, []).append(fn)
        dh, dh_b, small_grads[l] = _layer_bwd(l, dh, dh_b, params[l], wgs[l], saved[l], big_grads[l],
                                              make_hook(table))
    grad_x = dh.reshape(x.shape)

    grads, delta, new_m, new_v = {}, {}, {}, {}
    adam_state = {nm: () for nm in BIG}

    def adam(nm, l):
        f, h = halves[(nm, l)]
        adam_state[nm] = _adamw_big(f"adamw_{nm}_{l}", given[nm], given["m_" + nm], given["v_" + nm], f, h, l,
                                    route_idx[0:1], adam_state[nm])

    def small_update():
        small_shapes = [(n_layers,) + small_grads[0][nm].shape for nm in SMALL]
        small_sum = _small_allreduce(_pack([jnp.stack([small_grads[l][nm] for l in range(n_layers)])
                                            for nm in SMALL]))
        for nm, g in zip(SMALL, _unpack(small_sum, small_shapes)):
            grads[nm] = g.reshape((n_layers,) + given[nm].shape[1:]) if nm != "conv_w" else g
        grads["conv_w"] = lax.dynamic_slice_in_dim(grads["conv_w"], chip * cwl, cwl, axis=2)
        packed = [_pack([src[nm] for nm in SMALL]) for src in
                  ({nm: given[nm] for nm in SMALL}, grads, {nm: given["m_" + nm] for nm in SMALL},
                   {nm: given["v_" + nm] for nm in SMALL})]
        local_shapes = [given[nm].shape for nm in SMALL]
        for dst, buf in zip((delta, new_m, new_v), _adamw("adamw_small", *packed)):
            for nm, a in zip(SMALL, _unpack(buf, local_shapes)):
                dst[nm] = a

    upper = [(nm, l) for l in reversed(range(1, n_layers)) for nm in reversed(BIG)]
    late = rs_group("l0b", 0, ("w_in",))
    late[0]()
    for task in upper[:1]:
        adam(*task)
    late[1]()
    for task in upper[1:]:
        adam(*task)
    early[3]()
    late[2]()
    small_update()
    early[4]()
    for nm in ("w_down", "w_up"):
        adam(nm, 0)
    late[3]()
    adam("w_out", 0)
    late[4]()
    adam("w_in", 0)
    for nm in BIG:
        grads[nm], delta[nm], new_m[nm], new_v[nm] = adam_state[nm]
    return (loss, grad_x, *[grads[nm] for nm in WEIGHTS], *[delta[nm] for nm in WEIGHTS],
            *[new_m[nm] for nm in WEIGHTS], *[new_v[nm] for nm in WEIGHTS])
```

```python
import functools

import jax
import jax.numpy as jnp
from jax import lax
from jax.experimental import pallas as pl
from jax.experimental.pallas import tpu as pltpu

F32 = jnp.float32
BF16 = jnp.bfloat16
EPS = 1e-6
NEG_INF = -1e30
HEAD_DIM = 64
WINDOW = 128
CONV_KERNEL = 31
HALO = 32
GQA = 4
N_CHIPS = 4
ADAM_LR, ADAM_B1, ADAM_B2, ADAM_EPS, ADAM_WD, ADAM_STEP = 0.001, 0.9, 0.999, 1e-08, 0.01, 10
VMEM_LIMIT = 56 * 1024 * 1024
TILE_K = 2048
MESH = pl.DeviceIdType.MESH
ANY = pl.BlockSpec(memory_space=pl.ANY)

NN = (((1,), (0,)), ((), ()))
NT = (((1,), (1,)), ((), ()))
TN = (((0,), (0,)), ((), ()))


def _cp(*sem):
    return pltpu.CompilerParams(dimension_semantics=sem, vmem_limit_bytes=VMEM_LIMIT)


_LAST = [None]
TOKEN = jax.ShapeDtypeStruct((8, 128), F32)


def _ordered_call(body, *, out_shape, out_specs=None, in_specs=None, grid_spec=None, grid=None, **kw):
    single = not isinstance(out_shape, (tuple, list))
    shapes = (out_shape,) if single else tuple(out_shape)

    def run(*operands):
        dep = _LAST[0]
        n = len(operands)
        n_dep = 0 if dep is None else 1

        def fn(*refs):
            outs = refs[n + n_dep:n + n_dep + len(shapes)]
            token = refs[n + n_dep + len(shapes)]
            body(*refs[:n], *outs, *refs[n + n_dep + len(shapes) + 1:])
            token[...] = jnp.zeros_like(token)

        specs_in = list(grid_spec.in_specs if grid_spec is not None else in_specs) + [ANY] * n_dep
        specs_out = grid_spec.out_specs if grid_spec is not None else out_specs
        specs_out = tuple(specs_out) if isinstance(specs_out, (tuple, list)) else (specs_out,)
        if grid_spec is not None or grid:
            specs_out += (pl.BlockSpec(TOKEN.shape, lambda *_: (0, 0)),)
        else:
            specs_out += (pl.BlockSpec(memory_space=pltpu.VMEM),)
        args = operands + ((dep,) if n_dep else ())
        if grid_spec is not None:
            spec = pltpu.PrefetchScalarGridSpec(num_scalar_prefetch=grid_spec.num_scalar_prefetch, grid=grid_spec.grid,
                                                in_specs=specs_in, out_specs=specs_out)
            out = pl.pallas_call(fn, grid_spec=spec, out_shape=shapes + (TOKEN,), **kw)(*args)
        else:
            if grid:
                kw["grid"] = grid
            out = pl.pallas_call(fn, in_specs=specs_in, out_specs=specs_out, out_shape=shapes + (TOKEN,), **kw)(*args)
        _LAST[0] = out[-1]
        return out[0] if single else tuple(out[:-1])

    return run


def _tile(dim, pref):
    if dim <= pref:
        return dim
    for t in range(pref, 0, -128):
        if dim % t == 0:
            return t
    while dim % pref:
        pref //= 2
    return pref


def _dot(a, b, dims=NN):
    return lax.dot_general(a, b, dims, preferred_element_type=F32)


def _colsum(v):
    return jnp.sum(v, axis=0, keepdims=True)


def _sigmoid(x):
    return 1.0 / (1.0 + jnp.exp(-x))


def _matmul(name, operands, in_specs, out_shape, out_specs, grid, dims, acc_shape, epilogue, split_k=False):
    nk = grid[2]
    n_in = len(operands)

    def product(a_ref, b_ref):
        if split_k:
            ck = b_ref.shape[2]
            out = _dot(a_ref[:, 0:ck], b_ref[0], dims)
            for j in range(1, N_CHIPS):
                out = out + _dot(a_ref[:, j * ck:(j + 1) * ck], b_ref[j], dims)
            return out
        bv = b_ref[...]
        return _dot(a_ref[...], bv.reshape(-1, bv.shape[-1]) if bv.ndim == 3 else bv, dims)

    def body(*refs):
        a_ref, b_ref = refs[0], refs[1]
        extra = refs[2:n_in]
        if nk == 1:
            epilogue(product(a_ref, b_ref), extra, refs[n_in:])
            return
        outs = refs[n_in:-1]
        acc = refs[-1]
        k = pl.program_id(2)

        @pl.when(k == 0)
        def _():
            acc[...] = product(a_ref, b_ref)

        @pl.when((k > 0) & (k < nk - 1))
        def _():
            acc[...] += product(a_ref, b_ref)

        @pl.when(k == nk - 1)
        def _():
            epilogue(acc[...] + product(a_ref, b_ref), extra, outs)

    return _ordered_call(
        body, name=name, grid=grid, in_specs=in_specs, out_specs=out_specs, out_shape=out_shape,
        scratch_shapes=[pltpu.VMEM(acc_shape, F32)] if nk > 1 else [],
        compiler_params=_cp("parallel", "parallel", "arbitrary"),
    )(*operands)


def _ep_store(acc, extra, outs):
    outs[0][...] = acc.astype(outs[0].dtype)


def _ep_residual(acc, extra, outs):
    outs[0][...] = extra[0][...] + acc


def _ep_up(acc, extra, outs):
    outs[0][...] = acc.astype(BF16)
    r = jnp.maximum(acc, 0.0)
    outs[1][...] = (r * r).astype(BF16)


def _ep_dup(acc, extra, outs):
    outs[0][...] = (acc * (2.0 * jnp.maximum(extra[0][...].astype(F32), 0.0))).astype(BF16)


def _mm_act_w(name, a, wg, col_sharded, epilogue, extra=(), out_dtypes=(F32,)):
    m, kdim = a.shape
    _, r, c = wg.shape
    tm = _tile(m, 1024)
    if col_sharded:
        n = N_CHIPS * c
        tn = _tile(c, 1024)
        tk = _tile(kdim, TILE_K)
        per = c // tn
        b_spec = pl.BlockSpec((None, tk, tn), lambda i, j, k: (j // per, k, j % per))
    elif N_CHIPS * r <= TILE_K:
        n = c
        tm, tn, tk = _tile(m, 512), n, kdim
        b_spec = pl.BlockSpec((N_CHIPS, r, tn), lambda i, j, k: (0, 0, j))
    else:
        n = c
        tn = _tile(n, 1024)
        tk = _tile(r, TILE_K)
        per = r // tk
        b_spec = pl.BlockSpec((None, tk, tn), lambda i, j, k: (k // per, k % per, j))
    grid = (m // tm, n // tn, kdim // tk)
    o_spec = pl.BlockSpec((tm, tn), lambda i, j, k: (i, j))
    in_specs = [pl.BlockSpec((tm, tk), lambda i, j, k: (i, k)), b_spec] + [o_spec] * len(extra)
    return _matmul(name, (a, wg) + tuple(extra), in_specs,
                   tuple(jax.ShapeDtypeStruct((m, n), d) for d in out_dtypes),
                   tuple(o_spec for _ in out_dtypes), grid, NN, (tm, tn), epilogue)


def _mm_act_wt(name, a, wg, col_sharded, epilogue, extra=(), out_dtypes=(F32,)):
    m, kdim = a.shape
    _, r, c = wg.shape
    tm = _tile(m, 1024)
    split_k = False
    if col_sharded and N_CHIPS * c <= 2 * TILE_K:
        n = r
        tm, tn, tk, split_k = _tile(m, 512), n, kdim, True
        b_spec = pl.BlockSpec((N_CHIPS, tn, c), lambda i, j, k: (0, j, 0))
    elif col_sharded:
        n = r
        tn = _tile(n, 1024)
        tk = _tile(c, TILE_K)
        per = c // tk
        b_spec = pl.BlockSpec((None, tn, tk), lambda i, j, k: (k // per, j, k % per))
    elif N_CHIPS * r <= TILE_K:
        n = N_CHIPS * r
        tm, tn, tk = _tile(m, 512), n, _tile(c, TILE_K)
        b_spec = pl.BlockSpec((N_CHIPS, r, tk), lambda i, j, k: (0, 0, k))
    else:
        n = N_CHIPS * r
        tn = _tile(r, 1024)
        tk = _tile(c, TILE_K)
        per = r // tn
        b_spec = pl.BlockSpec((None, tn, tk), lambda i, j, k: (j // per, j % per, k))
    grid = (m // tm, n // tn, kdim // tk)
    o_spec = pl.BlockSpec((tm, tn), lambda i, j, k: (i, j))
    in_specs = [pl.BlockSpec((tm, tk), lambda i, j, k: (i, k)), b_spec] + [o_spec] * len(extra)
    return _matmul(name, (a, wg) + tuple(extra), in_specs,
                   tuple(jax.ShapeDtypeStruct((m, n), d) for d in out_dtypes),
                   tuple(o_spec for _ in out_dtypes), grid, NT, (tm, tn), epilogue, split_k)


def _mm_wgrad(name, a, g, col_sharded, c):
    s, kdim = a.shape
    _, n = g.shape
    ts = _tile(s, TILE_K)
    if col_sharded:
        r = kdim
        tm = _tile(kdim, 1024)
        tn = _tile(c, 1024)
        per = c // tn
        o_spec = pl.BlockSpec((None, tm, tn), lambda i, j, k: (j // per, i, j % per))
    else:
        r = kdim // N_CHIPS
        tm = _tile(r, 512)
        tn = _tile(c, 2048)
        per = r // tm
        o_spec = pl.BlockSpec((None, tm, tn), lambda i, j, k: (i // per, i % per, j))
    grid = (kdim // tm, n // tn, s // ts)
    in_specs = [pl.BlockSpec((ts, tm), lambda i, j, k: (k, i)), pl.BlockSpec((ts, tn), lambda i, j, k: (k, j))]
    return _matmul(name, (a, g), in_specs, (jax.ShapeDtypeStruct((N_CHIPS, r, c), BF16),), (o_spec,),
                   grid, TN, (tm, tn), _ep_store)[0]


def _rms_fwd(name, x, g):
    s, d = x.shape
    tb = _tile(s, 256)

    def body(x_ref, g_ref, o_ref):
        xv = x_ref[...]
        r = lax.rsqrt(jnp.mean(xv * xv, axis=-1, keepdims=True) + EPS)
        o_ref[...] = (xv * r * g_ref[...]).astype(BF16)

    return _ordered_call(
        body, name=name, grid=(s // tb,),
        in_specs=[pl.BlockSpec((tb, d), lambda i: (i, 0)), pl.BlockSpec((1, d), lambda i: (0, 0))],
        out_specs=pl.BlockSpec((tb, d), lambda i: (i, 0)),
        out_shape=jax.ShapeDtypeStruct((s, d), BF16), compiler_params=_cp("parallel"),
    )(x, g.reshape(1, d))


def _rms_bwd(name, dh, x, g, dres):
    s, d = x.shape
    tb = _tile(s, 256)

    def body(dh_ref, x_ref, g_ref, dres_ref, dx_ref, dxb_ref, dg_ref):
        i = pl.program_id(0)
        xv = x_ref[...]
        r = lax.rsqrt(jnp.mean(xv * xv, axis=-1, keepdims=True) + EPS)
        xhat = xv * r
        dhv = dh_ref[...]
        dxhat = dhv * g_ref[...]
        dx = dres_ref[...] + r * (dxhat - xhat * jnp.mean(dxhat * xhat, axis=-1, keepdims=True))
        dx_ref[...] = dx
        dxb_ref[...] = dx.astype(BF16)

        @pl.when(i == 0)
        def _():
            dg_ref[...] = jnp.zeros_like(dg_ref)

        dg_ref[...] += _colsum(dhv * xhat)

    row = pl.BlockSpec((tb, d), lambda i: (i, 0))
    vec = pl.BlockSpec((1, d), lambda i: (0, 0))
    return _ordered_call(
        body, name=name, grid=(s // tb,), in_specs=[row, row, vec, row], out_specs=(row, row, vec),
        out_shape=(jax.ShapeDtypeStruct((s, d), F32), jax.ShapeDtypeStruct((s, d), BF16),
                   jax.ShapeDtypeStruct((1, d), F32)),
        compiler_params=_cp("arbitrary"),
    )(dh, x, g.reshape(1, d), dres)


def _loss_head(y, t):
    s, d = y.shape
    tb = _tile(s, 256)

    def body(y_ref, t_ref, dy_ref, dyb_ref, loss_ref, acc):
        i = pl.program_id(0)
        e = y_ref[...] - t_ref[...]
        dy = e * (1.0 / d)
        dy_ref[...] = dy
        dyb_ref[...] = dy.astype(BF16)

        @pl.when(i == 0)
        def _():
            acc[...] = jnp.zeros_like(acc)

        acc[...] += _colsum(e * e)

        @pl.when(i == pl.num_programs(0) - 1)
        def _():
            loss_ref[...] = jnp.sum(acc[...], axis=-1, keepdims=True) * (0.5 / d)

    row = pl.BlockSpec((tb, d), lambda i: (i, 0))
    return _ordered_call(
        body, name="loss_head", grid=(s // tb,), in_specs=[row, row],
        out_specs=(row, row, pl.BlockSpec((1, 1), lambda i: (0, 0))),
        out_shape=(jax.ShapeDtypeStruct((s, d), F32), jax.ShapeDtypeStruct((s, d), BF16),
                   jax.ShapeDtypeStruct((1, 1), F32)),
        scratch_shapes=[pltpu.VMEM((1, d), F32)], compiler_params=_cp("arbitrary"),
    )(y, t)


def _mixnorm_fwd(name, ya, yc, ys, g):
    s, aw = ya.shape
    cw, sw = yc.shape[1], ys.shape[1]
    d = aw + cw + sw
    tb = _tile(s, 256)

    def body(ya_ref, yc_ref, ys_ref, g_ref, o_ref):
        off = 0
        for ref, w in ((ya_ref, aw), (yc_ref, cw), (ys_ref, sw)):
            v = ref[...]
            r = lax.rsqrt(jnp.mean(v * v, axis=-1, keepdims=True) + EPS)
            o_ref[:, off:off + w] = (v * r * g_ref[:, off:off + w]).astype(BF16)
            off += w

    def row(w):
        return pl.BlockSpec((tb, w), lambda i: (i, 0))

    return _ordered_call(
        body, name=name, grid=(s // tb,),
        in_specs=[row(aw), row(cw), row(sw), pl.BlockSpec((1, d), lambda i: (0, 0))], out_specs=row(d),
        out_shape=jax.ShapeDtypeStruct((s, d), BF16), compiler_params=_cp("parallel"),
    )(ya, yc, ys, g.reshape(1, d))


def _mixnorm_bwd(name, dmix, ya, yc, ys, g):
    s, aw = ya.shape
    cw, sw = yc.shape[1], ys.shape[1]
    d = aw + cw + sw
    tb = _tile(s, 256)

    def body(dm_ref, ya_ref, yc_ref, ys_ref, g_ref, dya_ref, dyc_ref, dys_ref, dg_ref):
        i = pl.program_id(0)

        @pl.when(i == 0)
        def _():
            dg_ref[...] = jnp.zeros_like(dg_ref)

        off = 0
        for ref, dref, w in ((ya_ref, dya_ref, aw), (yc_ref, dyc_ref, cw), (ys_ref, dys_ref, sw)):
            v = ref[...]
            r = lax.rsqrt(jnp.mean(v * v, axis=-1, keepdims=True) + EPS)
            vhat = v * r
            dm = dm_ref[:, off:off + w]
            dvhat = dm * g_ref[:, off:off + w]
            dref[...] = r * (dvhat - vhat * jnp.mean(dvhat * vhat, axis=-1, keepdims=True))
            dg_ref[:, off:off + w] += _colsum(dm * vhat)
            off += w

    def row(w):
        return pl.BlockSpec((tb, w), lambda i: (i, 0))

    vec = pl.BlockSpec((1, d), lambda i: (0, 0))
    return _ordered_call(
        body, name=name, grid=(s // tb,), in_specs=[row(d), row(aw), row(cw), row(sw), vec],
        out_specs=(row(aw), row(cw), row(sw), vec),
        out_shape=(jax.ShapeDtypeStruct((s, aw), F32), jax.ShapeDtypeStruct((s, cw), F32),
                   jax.ShapeDtypeStruct((s, sw), F32), jax.ShapeDtypeStruct((1, d), F32)),
        compiler_params=_cp("arbitrary"),
    )(dmix, ya, yc, ys, g.reshape(1, d))


def _head_rms(x):
    r = lax.rsqrt(jnp.mean(x * x, axis=-1, keepdims=True) + EPS)
    return x * r, r


def _attn_mask(n):
    qi = lax.broadcasted_iota(jnp.int32, (GQA * WINDOW, 2 * WINDOW), 0) & (WINDOW - 1)
    sj = lax.broadcasted_iota(jnp.int32, (GQA * WINDOW, 2 * WINDOW), 1)
    rel = qi + WINDOW - sj
    return (rel >= 0) & (rel < WINDOW) & ((sj >= WINDOW) | (n > 0))


def _attn_specs(nq, nkv, nb):
    qspec = pl.BlockSpec((nq, WINDOW, HEAD_DIM), lambda n: (0, n, 0))
    cur = pl.BlockSpec((nkv, WINDOW, HEAD_DIM), lambda n: (0, n, 0))
    prev = pl.BlockSpec((nkv, WINDOW, HEAD_DIM), lambda n: (0, jnp.maximum(n - 1, 0), 0))
    nxt = pl.BlockSpec((nkv, WINDOW, HEAD_DIM), lambda n: (0, jnp.minimum(n + 1, nb - 1), 0))
    gain = pl.BlockSpec((1, HEAD_DIM), lambda n: (0, 0))
    sink = pl.BlockSpec((nq, 1, 128), lambda n: (0, 0, 0))
    return qspec, cur, prev, nxt, gain, sink


def _group_sinks(s_ref, g):
    return jnp.concatenate([jnp.broadcast_to(s_ref[g * GQA + i][:, :1], (WINDOW, 1)) for i in range(GQA)], axis=0)


def _attn_probs(qn_b, kn_b, valid, sink):
    logits = _dot(qn_b, kn_b, NT) * (HEAD_DIM ** -0.5)
    logits = jnp.where(valid, logits, NEG_INF)
    m = jnp.maximum(jnp.max(logits, axis=-1, keepdims=True), sink)
    p = jnp.exp(logits - m)
    es = jnp.exp(sink - m)
    denom = jnp.sum(p, axis=-1, keepdims=True) + es
    return p / denom, es / denom


def _attn_fwd(name, q, k, v, gq, gk, sinks_b):
    nq, s, _ = q.shape
    nkv = k.shape[0]
    nb = s // WINDOW
    qspec, cur, prev, _, gain, sink = _attn_specs(nq, nkv, nb)

    def body(q_ref, kc_ref, kp_ref, vc_ref, vp_ref, gq_ref, gk_ref, s_ref, o_ref):
        gkv = gk_ref[...]
        valid = _attn_mask(pl.program_id(0))
        for g in range(nkv):
            kn_b = jnp.concatenate([_head_rms(kp_ref[g])[0] * gkv, _head_rms(kc_ref[g])[0] * gkv],
                                   axis=0).astype(BF16)
            vv_b = jnp.concatenate([vp_ref[g], vc_ref[g]], axis=0).astype(BF16)
            heads = pl.ds(g * GQA, GQA)
            q4 = q_ref[heads].reshape(GQA * WINDOW, HEAD_DIM)
            qn_b = (_head_rms(q4)[0] * gq_ref[...]).astype(BF16)
            probs, _ = _attn_probs(qn_b, kn_b, valid, _group_sinks(s_ref, g))
            o_ref[heads] = _dot(probs.astype(BF16), vv_b).reshape(GQA, WINDOW, HEAD_DIM)

    return _ordered_call(
        body, name=name, grid=(nb,), in_specs=[qspec, cur, prev, cur, prev, gain, gain, sink], out_specs=qspec,
        out_shape=jax.ShapeDtypeStruct((nq, s, HEAD_DIM), F32), compiler_params=_cp("parallel"),
    )(q, k, k, v, v, gq.reshape(1, HEAD_DIM), gk.reshape(1, HEAD_DIM), sinks_b)


def _attn_bwd(name, q, k, v, gq, gk, sinks_b, do):
    nq, s, _ = q.shape
    nkv = k.shape[0]
    nb = s // WINDOW
    qspec, cur, prev, _, gain, sink = _attn_specs(nq, nkv, nb)

    def body(q_ref, kc_ref, kp_ref, vc_ref, vp_ref, gq_ref, gk_ref, s_ref, do_ref,
             dq_ref, dkc_ref, dkp_ref, dvc_ref, dvp_ref, dgq_ref, ds_ref):
        n = pl.program_id(0)

        @pl.when(n == 0)
        def _():
            dgq_ref[...] = jnp.zeros_like(dgq_ref)
            ds_ref[...] = jnp.zeros_like(ds_ref)

        gkv = gk_ref[...]
        gqv = gq_ref[...]
        valid = _attn_mask(n)
        dgq = jnp.zeros((1, HEAD_DIM), F32)
        for g in range(nkv):
            kn_b = jnp.concatenate([_head_rms(kp_ref[g])[0] * gkv, _head_rms(kc_ref[g])[0] * gkv],
                                   axis=0).astype(BF16)
            vv_b = jnp.concatenate([vp_ref[g], vc_ref[g]], axis=0).astype(BF16)
            heads = pl.ds(g * GQA, GQA)
            qhat, r = _head_rms(q_ref[heads].reshape(GQA * WINDOW, HEAD_DIM))
            qn_b = (qhat * gqv).astype(BF16)
            probs, psink = _attn_probs(qn_b, kn_b, valid, _group_sinks(s_ref, g))
            do_b = do_ref[heads].reshape(GQA * WINDOW, HEAD_DIM).astype(BF16)
            dp = _dot(do_b, vv_b, NT)
            delta = jnp.sum(probs * dp, axis=-1, keepdims=True)
            dl_b = (probs * (dp - delta) * (HEAD_DIM ** -0.5)).astype(BF16)
            sink_term = psink * delta
            for i in range(GQA):
                ds_ref[g * GQA + i] += jnp.broadcast_to(
                    -jnp.sum(sink_term[i * WINDOW:(i + 1) * WINDOW], axis=0, keepdims=True), (1, 128))
            dqn = _dot(dl_b, kn_b)
            dkn = _dot(dl_b, qn_b, TN)
            dvv = _dot(probs.astype(BF16), do_b, TN)
            dgq += _colsum(dqn * qhat)
            dqhat = dqn * gqv
            dq_ref[heads] = (r * (dqhat - qhat * jnp.mean(dqhat * qhat, axis=-1, keepdims=True))).astype(
                BF16).reshape(GQA, WINDOW, HEAD_DIM)
            dkp_ref[g] = dkn[:WINDOW]
            dkc_ref[g] = dkn[WINDOW:]
            dvp_ref[g] = dvv[:WINDOW]
            dvc_ref[g] = dvv[WINDOW:]
        dgq_ref[...] += dgq

    kv_shape = jax.ShapeDtypeStruct((nkv, s, HEAD_DIM), F32)
    return _ordered_call(
        body, name=name, grid=(nb,), in_specs=[qspec, cur, prev, cur, prev, gain, gain, sink, qspec],
        out_specs=(qspec, cur, cur, cur, cur, gain, sink),
        out_shape=(jax.ShapeDtypeStruct((nq, s, HEAD_DIM), BF16), kv_shape, kv_shape, kv_shape, kv_shape,
                   jax.ShapeDtypeStruct((1, HEAD_DIM), F32), jax.ShapeDtypeStruct((nq, 1, 128), F32)),
        compiler_params=_cp("arbitrary"),
    )(q, k, k, v, v, gq.reshape(1, HEAD_DIM), gk.reshape(1, HEAD_DIM), sinks_b, do)


def _attn_bwd_kv(name, k, gk, dkc, dkp, dvc, dvp):
    nkv, s, _ = k.shape
    nb = s // WINDOW
    _, cur, _, nxt, gain, _ = _attn_specs(GQA * nkv, nkv, nb)

    def body(k_ref, gk_ref, dkc_ref, dkp_ref, dvc_ref, dvp_ref, dk_ref, dv_ref, dgk_ref):
        n = pl.program_id(0)

        @pl.when(n == 0)
        def _():
            dgk_ref[...] = jnp.zeros_like(dgk_ref)

        has_next = n < nb - 1
        dgk = jnp.zeros((1, HEAD_DIM), F32)
        for g in range(nkv):
            dkn = dkc_ref[g] + jnp.where(has_next, dkp_ref[g], 0.0)
            dv_ref[g] = (dvc_ref[g] + jnp.where(has_next, dvp_ref[g], 0.0)).astype(BF16)
            khat, r = _head_rms(k_ref[g])
            dgk += _colsum(dkn * khat)
            dkhat = dkn * gk_ref[...]
            dk_ref[g] = (r * (dkhat - khat * jnp.mean(dkhat * khat, axis=-1, keepdims=True))).astype(BF16)
        dgk_ref[...] += dgk

    kv_shape = jax.ShapeDtypeStruct((nkv, s, HEAD_DIM), BF16)
    return _ordered_call(
        body, name=name, grid=(nb,), in_specs=[cur, gain, cur, nxt, cur, nxt], out_specs=(cur, cur, gain),
        out_shape=(kv_shape, kv_shape, jax.ShapeDtypeStruct((1, HEAD_DIM), F32)),
        compiler_params=_cp("arbitrary"),
    )(k, gk.reshape(1, HEAD_DIM), dkc, dkp, dvc, dvp)


SUBLANES = 8


def _fill_shifts(buf, shifts, tb):
    rows = tb + HALO - SUBLANES
    for b in range(1, SUBLANES):
        shifts[b - 1, pl.ds(0, rows), :] = buf[pl.ds(b, rows), :]


def _window(buf, shifts, off, tb):
    b = off % SUBLANES
    return buf[pl.ds(off, tb), :] if b == 0 else shifts[b - 1, pl.ds(off - b, tb), :]


def _conv_recompute(i, a_ref, gt_ref, ap_ref, gp_ref, w_ref, b_ref, hbuf, shifts, tb):
    hbuf[pl.ds(HALO, tb), :] = a_ref[...] * _sigmoid(gt_ref[...])
    tail = ap_ref[pl.ds(tb - HALO, HALO), :] * _sigmoid(gp_ref[pl.ds(tb - HALO, HALO), :])
    hbuf[pl.ds(0, HALO), :] = jnp.where(i > 0, tail, 0.0)
    _fill_shifts(hbuf, shifts, tb)
    acc = jnp.broadcast_to(b_ref[...], a_ref.shape)
    for kk in range(CONV_KERNEL):
        acc = acc + w_ref[pl.ds(kk, 1), :] * _window(hbuf, shifts, HALO - (CONV_KERNEL - 1) + kk, tb)
    return acc


def _layer_norm_stats(c):
    mu = jnp.mean(c, axis=-1, keepdims=True)
    xc = c - mu
    r = lax.rsqrt(jnp.mean(xc * xc, axis=-1, keepdims=True) + EPS)
    return xc * r, r


def _conv_specs(s, cw, tb, a_blk):
    cur = lambda off: pl.BlockSpec((tb, cw), lambda i: (i, a_blk + off))
    prev = lambda off: pl.BlockSpec((tb, cw), lambda i: (jnp.maximum(i - 1, 0), a_blk + off))
    wspec = pl.BlockSpec((HALO, cw), lambda i: (0, 0))
    vec = pl.BlockSpec((1, cw), lambda i: (0, 0))
    row = pl.BlockSpec((tb, cw), lambda i: (i, 0))
    return cur, prev, wspec, vec, row


def _conv_fwd(name, proj, a_blk, w, b, lg, lb):
    s = proj.shape[0]
    cw = w.shape[1]
    tb = _tile(s, 256)
    cur, prev, wspec, vec, row = _conv_specs(s, cw, tb, a_blk)

    def body(a_ref, gt_ref, ap_ref, gp_ref, w_ref, b_ref, lg_ref, lb_ref, y_ref, hbuf, shifts):
        c = _conv_recompute(pl.program_id(0), a_ref, gt_ref, ap_ref, gp_ref, w_ref, b_ref, hbuf, shifts, tb)
        chat, _ = _layer_norm_stats(c)
        z = chat * lg_ref[...] + lb_ref[...]
        y_ref[...] = z * _sigmoid(z)

    return _ordered_call(
        body, name=name, grid=(s // tb,), in_specs=[cur(0), cur(1), prev(0), prev(1), wspec, vec, vec, vec],
        out_specs=row, out_shape=jax.ShapeDtypeStruct((s, cw), F32),
        scratch_shapes=[pltpu.VMEM((tb + HALO, cw), F32), pltpu.VMEM((SUBLANES - 1, tb + HALO, cw), F32)],
        compiler_params=_cp("arbitrary"),
    )(proj, proj, proj, proj, w, b, lg, lb)


def _conv_bwd1(name, proj, a_blk, w, b, lg, lb, dy):
    s = proj.shape[0]
    cw = w.shape[1]
    tb = _tile(s, 256)
    cur, prev, wspec, vec, row = _conv_specs(s, cw, tb, a_blk)

    def body(a_ref, gt_ref, ap_ref, gp_ref, w_ref, b_ref, lg_ref, lb_ref, dy_ref,
             dc_ref, dw_ref, db_ref, dlg_ref, dlb_ref, hbuf, shifts):
        i = pl.program_id(0)

        @pl.when(i == 0)
        def _():
            dw_ref[...] = jnp.zeros_like(dw_ref)
            db_ref[...] = jnp.zeros_like(db_ref)
            dlg_ref[...] = jnp.zeros_like(dlg_ref)
            dlb_ref[...] = jnp.zeros_like(dlb_ref)

        c = _conv_recompute(i, a_ref, gt_ref, ap_ref, gp_ref, w_ref, b_ref, hbuf, shifts, tb)
        chat, r = _layer_norm_stats(c)
        z = chat * lg_ref[...] + lb_ref[...]
        sg = _sigmoid(z)
        dz = dy_ref[...] * (sg + z * sg * (1.0 - sg))
        dlg_ref[...] += _colsum(dz * chat)
        dlb_ref[...] += _colsum(dz)
        dzg = dz * lg_ref[...]
        dc = r * (dzg - jnp.mean(dzg, axis=-1, keepdims=True) - chat * jnp.mean(dzg * chat, axis=-1, keepdims=True))
        dc_ref[...] = dc
        db_ref[...] += _colsum(dc)
        for kk in range(CONV_KERNEL):
            dw_ref[pl.ds(kk, 1), :] += _colsum(dc * _window(hbuf, shifts, HALO - (CONV_KERNEL - 1) + kk, tb))

    return _ordered_call(
        body, name=name, grid=(s // tb,), in_specs=[cur(0), cur(1), prev(0), prev(1), wspec, vec, vec, vec, row],
        out_specs=(row, wspec, vec, vec, vec),
        out_shape=(jax.ShapeDtypeStruct((s, cw), F32), jax.ShapeDtypeStruct((HALO, cw), F32),
                   jax.ShapeDtypeStruct((1, cw), F32), jax.ShapeDtypeStruct((1, cw), F32),
                   jax.ShapeDtypeStruct((1, cw), F32)),
        scratch_shapes=[pltpu.VMEM((tb + HALO, cw), F32), pltpu.VMEM((SUBLANES - 1, tb + HALO, cw), F32)],
        compiler_params=_cp("arbitrary"),
    )(proj, proj, proj, proj, w, b, lg, lb, dy)


def _conv_bwd2(name, proj, a_blk, w, dc):
    s = proj.shape[0]
    cw = w.shape[1]
    tb = _tile(s, 256)
    nblk = s // tb
    cur, _, wspec, _, row = _conv_specs(s, cw, tb, a_blk)
    nxt = pl.BlockSpec((tb, cw), lambda i: (jnp.minimum(i + 1, nblk - 1), 0))

    def body(a_ref, gt_ref, w_ref, dc_ref, dn_ref, o_ref, dbuf, shifts):
        i = pl.program_id(0)
        dbuf[pl.ds(0, tb), :] = dc_ref[...]
        dbuf[pl.ds(tb, HALO), :] = jnp.where(i < nblk - 1, dn_ref[pl.ds(0, HALO), :], 0.0)
        _fill_shifts(dbuf, shifts, tb)
        dh = jnp.zeros((tb, cw), F32)
        for kk in range(CONV_KERNEL):
            dh = dh + w_ref[pl.ds(kk, 1), :] * _window(dbuf, shifts, CONV_KERNEL - 1 - kk, tb)
        sg = _sigmoid(gt_ref[...])
        o_ref[:, 0:cw] = (dh * sg).astype(BF16)
        o_ref[:, cw:2 * cw] = (dh * a_ref[...] * sg * (1.0 - sg)).astype(BF16)

    return _ordered_call(
        body, name=name, grid=(nblk,), in_specs=[cur(0), cur(1), wspec, row, nxt],
        out_specs=pl.BlockSpec((tb, 2 * cw), lambda i: (i, 0)), out_shape=jax.ShapeDtypeStruct((s, 2 * cw), BF16),
        scratch_shapes=[pltpu.VMEM((tb + HALO, cw), F32), pltpu.VMEM((SUBLANES - 1, tb + HALO, cw), F32)],
        compiler_params=_cp("arbitrary"),
    )(proj, proj, w, dc, dc)


def _sgu_common(v_ref, lg_ref, lb_ref, w_ref, bexp_ref, sw):
    vhat, r = _layer_norm_stats(v_ref[...])
    vn_b = (vhat * lg_ref[...] + lb_ref[...]).astype(BF16)
    ii = lax.broadcasted_iota(jnp.int32, (WINDOW, WINDOW), 0)
    jj = lax.broadcasted_iota(jnp.int32, (WINDOW, WINDOW), 1)
    tril = jj <= ii
    head_of = lax.broadcasted_iota(jnp.int32, (WINDOW, sw), 1) // HEAD_DIM
    wts = [jnp.where(tril, w_ref[h], 0.0).astype(BF16) for h in range(sw // HEAD_DIM)]
    sv = bexp_ref[...]
    for h, wt in enumerate(wts):
        sv = sv + jnp.where(head_of == h, _dot(wt, vn_b), 0.0)
    return vhat, r, vn_b, tril, head_of, wts, sv


def _sgu_specs(sw, u_blk):
    nh = sw // HEAD_DIM
    u = pl.BlockSpec((WINDOW, sw), lambda n: (n, u_blk))
    v = pl.BlockSpec((WINDOW, sw), lambda n: (n, u_blk + 1))
    vec = pl.BlockSpec((1, sw), lambda n: (0, 0))
    wspec = pl.BlockSpec((nh, WINDOW, WINDOW), lambda n: (0, 0, 0))
    bspec = pl.BlockSpec((WINDOW, sw), lambda n: (0, 0))
    row = pl.BlockSpec((WINDOW, sw), lambda n: (n, 0))
    return u, v, vec, wspec, bspec, row


def _sgu_fwd(name, proj, u_blk, lg, lb, w, bexp):
    s = proj.shape[0]
    sw = lg.shape[1]
    u, v, vec, wspec, bspec, row = _sgu_specs(sw, u_blk)

    def body(u_ref, v_ref, lg_ref, lb_ref, w_ref, bexp_ref, y_ref):
        sv = _sgu_common(v_ref, lg_ref, lb_ref, w_ref, bexp_ref, sw)[-1]
        y_ref[...] = u_ref[...] * sv

    return _ordered_call(
        body, name=name, grid=(s // WINDOW,), in_specs=[u, v, vec, vec, wspec, bspec], out_specs=row,
        out_shape=jax.ShapeDtypeStruct((s, sw), F32), compiler_params=_cp("parallel"),
    )(proj, proj, lg, lb, w, bexp)


def _sgu_bwd(name, proj, u_blk, lg, lb, w, bexp, dy):
    s = proj.shape[0]
    sw = lg.shape[1]
    nh = sw // HEAD_DIM
    u, v, vec, wspec, bspec, row = _sgu_specs(sw, u_blk)
    dbspec = pl.BlockSpec((nh, WINDOW), lambda n: (0, 0))

    def body(u_ref, v_ref, lg_ref, lb_ref, w_ref, bexp_ref, dy_ref, o_ref, dw_ref, db_ref, dlg_ref, dlb_ref):
        n = pl.program_id(0)

        @pl.when(n == 0)
        def _():
            dw_ref[...] = jnp.zeros_like(dw_ref)
            db_ref[...] = jnp.zeros_like(db_ref)
            dlg_ref[...] = jnp.zeros_like(dlg_ref)
            dlb_ref[...] = jnp.zeros_like(dlb_ref)

        vhat, r, vn_b, tril, head_of, wts, sv = _sgu_common(v_ref, lg_ref, lb_ref, w_ref, bexp_ref, sw)
        dyv = dy_ref[...]
        o_ref[:, 0:sw] = (dyv * sv).astype(BF16)
        ds = dyv * u_ref[...]
        dvn = jnp.zeros((WINDOW, sw), F32)
        for h, wt in enumerate(wts):
            dsm_b = jnp.where(head_of == h, ds, 0.0).astype(BF16)
            dvn = dvn + _dot(wt, dsm_b, TN)
            dw_ref[h] += jnp.where(tril, _dot(dsm_b, vn_b, NT), 0.0)
        hmask = (lax.broadcasted_iota(jnp.int32, (nh, sw), 1) // HEAD_DIM
                 == lax.broadcasted_iota(jnp.int32, (nh, sw), 0)).astype(F32)
        db_ref[...] += lax.dot_general(hmask, ds, NT, precision=lax.Precision.HIGHEST, preferred_element_type=F32)
        dlg_ref[...] += _colsum(dvn * vhat)
        dlb_ref[...] += _colsum(dvn)
        dvg = dvn * lg_ref[...]
        dv = r * (dvg - jnp.mean(dvg, axis=-1, keepdims=True) - vhat * jnp.mean(dvg * vhat, axis=-1, keepdims=True))
        o_ref[:, sw:2 * sw] = dv.astype(BF16)

    return _ordered_call(
        body, name=name, grid=(s // WINDOW,), in_specs=[u, v, vec, vec, wspec, bspec, row],
        out_specs=(pl.BlockSpec((WINDOW, 2 * sw), lambda n: (n, 0)), wspec, dbspec, vec, vec),
        out_shape=(jax.ShapeDtypeStruct((s, 2 * sw), BF16), jax.ShapeDtypeStruct((nh, WINDOW, WINDOW), F32),
                   jax.ShapeDtypeStruct((nh, WINDOW), F32), jax.ShapeDtypeStruct((1, sw), F32),
                   jax.ShapeDtypeStruct((1, sw), F32)),
        compiler_params=_cp("arbitrary"),
    )(proj, proj, lg, lb, w, bexp, dy)


def _adamw(name, w, g, m, v):
    rows, cols = w.shape
    tr = _tile(rows, 256)

    def body(w_ref, g_ref, m_ref, v_ref, d_ref, nm_ref, nv_ref):
        gv = g_ref[...]
        mv = ADAM_B1 * m_ref[...] + (1.0 - ADAM_B1) * gv
        vv = ADAM_B2 * v_ref[...] + (1.0 - ADAM_B2) * (gv * gv)
        m_hat = mv / (1.0 - ADAM_B1 ** ADAM_STEP)
        v_hat = vv / (1.0 - ADAM_B2 ** ADAM_STEP)
        d_ref[...] = -ADAM_LR * (m_hat / (jnp.sqrt(v_hat) + ADAM_EPS) + ADAM_WD * w_ref[...])
        nm_ref[...] = mv
        nv_ref[...] = vv

    spec = pl.BlockSpec((tr, cols), lambda i: (i, 0))
    shape = jax.ShapeDtypeStruct((rows, cols), F32)
    return _ordered_call(
        body, name=name, grid=(rows // tr,), in_specs=[spec] * 4, out_specs=(spec,) * 3, out_shape=(shape,) * 3,
        compiler_params=_cp("parallel"),
    )(w, g, m, v)


def _route():
    x, y, c = lax.axis_index("x"), lax.axis_index("y"), lax.axis_index("c")
    n1 = (jnp.where(c == 0, 1 - x, x), jnp.where(c == 0, y, 1 - y))
    n2 = (jnp.where(c == 0, x, 1 - x), jnp.where(c == 0, 1 - y, y))
    return x, y, c, n1, n2, (1 - x, 1 - y)


def _cidx(chip):
    return 2 * chip[0] + chip[1]


def _remote(src, dst, sems, k, device):
    send_sems, recv_sems = sems
    return pltpu.make_async_remote_copy(src_ref=src, dst_ref=dst, send_sem=send_sems.at[k], recv_sem=recv_sems.at[k],
                                        device_id=device, device_id_type=MESH)


def _exchange(name, bufs, n_sems, build):
    n = len(bufs)

    def body(*refs):
        cps = build(refs[n:2 * n], (refs[2 * n], refs[2 * n + 1]))
        for cp in cps:
            cp.start()
        for cp in cps:
            cp.wait()

    return _ordered_call(
        body, name=name, in_specs=[ANY] * n, out_specs=tuple(ANY for _ in range(n)),
        out_shape=tuple(jax.ShapeDtypeStruct(b.shape, b.dtype) for b in bufs),
        input_output_aliases={i: i for i in range(n)},
        scratch_shapes=[pltpu.SemaphoreType.DMA((n_sems,)), pltpu.SemaphoreType.DMA((n_sems,))],
        compiler_params=pltpu.CompilerParams(has_side_effects=True),
    )(*bufs)


HBM_SPEC = pl.BlockSpec(memory_space=pltpu.HBM)
SEM_SPEC = pl.BlockSpec(memory_space=pltpu.SEMAPHORE)
DATAFLOW = pltpu.SideEffectType.DATAFLOW_SIDE_EFFECTING


def _exchange_start(name, bufs, n_sems, build):
    n = len(bufs)

    def body(*refs):
        for cp in build(refs[:n], (refs[n], refs[n + 1])):
            cp.start()

    out = _ordered_call(
        body, name=name,
        out_shape=(pltpu.SemaphoreType.DMA((n_sems,)), pltpu.SemaphoreType.DMA((n_sems,)))
        + tuple(pltpu.HBM(b.shape, b.dtype) for b in bufs),
        in_specs=[HBM_SPEC] * n, out_specs=(SEM_SPEC, SEM_SPEC) + (HBM_SPEC,) * n,
        input_output_aliases={i: 2 + i for i in range(n)},
        compiler_params=pltpu.CompilerParams(has_side_effects=DATAFLOW),
    )(*[pltpu.with_memory_space_constraint(b, pltpu.HBM) for b in bufs])
    return dict(name=name, send=out[0], recv=out[1], bufs=list(out[2:2 + n]), build=build)


def _exchange_wait(handle):
    n = len(handle["bufs"])

    def body(*refs):
        for cp in handle["build"](refs[:n], (refs[n], refs[n + 1])):
            cp.wait_send()
            cp.wait_recv()

    return list(_ordered_call(
        body, name=handle["name"] + "_wait", out_shape=tuple(pltpu.HBM(b.shape, b.dtype) for b in handle["bufs"]),
        in_specs=[HBM_SPEC] * n + [SEM_SPEC, SEM_SPEC], out_specs=(HBM_SPEC,) * n,
        input_output_aliases={i: i for i in range(n)},
        compiler_params=pltpu.CompilerParams(has_side_effects=DATAFLOW),
    )(*handle["bufs"], handle["send"], handle["recv"]))


def _cast_place(name, w, l, me_idx, dtype):
    _, r, c = w.shape
    tr = _tile(r, 512)

    def body(me_ref, w_ref, o_ref):
        o_ref[...] = w_ref[...].astype(dtype)

    grid_spec = pltpu.PrefetchScalarGridSpec(
        num_scalar_prefetch=1, grid=(r // tr,),
        in_specs=[pl.BlockSpec((None, tr, c), lambda i, me_ref: (l, i, 0))],
        out_specs=pl.BlockSpec((None, tr, c), lambda i, me_ref: (me_ref[0], i, 0)))
    return _ordered_call(
        body, name=name, grid_spec=grid_spec, out_shape=jax.ShapeDtypeStruct((N_CHIPS, r, c), dtype),
        compiler_params=_cp("arbitrary"),
    )(me_idx, w)


def _my_half(ref, blk, c):
    hr = ref.shape[1] // 2
    return ref.at[blk, pl.ds(c * hr, hr), :]


def _gather_step(entering):
    lens = [len(e) for e in entering]
    flat = [b for e in entering for b in e]

    def build(refs, sems):
        x, y, c, n1, n2, dg = _route()
        me = _cidx((x, y))
        plan = ([(r, (me,), (*n1, c)) for r in refs[:lens[0]]]
                + [(r, (me, _cidx(n1)), (*n2, c)) for r in refs[lens[0]:lens[0] + lens[1]]]
                + [(r, (_cidx(n1), _cidx(n2), _cidx(dg)), (x, y, 1 - c)) for r in refs[lens[0] + lens[1]:]])
        cps = []
        for ref, blocks, peer in plan:
            for blk in blocks:
                cps.append(_remote(_my_half(ref, blk, c), _my_half(ref, blk, c), sems, len(cps), peer))
        return cps

    return flat, lens[0] + 2 * lens[1] + 3 * lens[2], build


RI_C, RI_ME, RI_N2, RI_N1 = 0, 1, 2, 3


def _pair_sum(name, g, sib, route_idx):
    _, rows, cols = g.shape
    hr = rows // 2
    tr = _tile(hr, 512)
    per = hr // tr

    def body(ri, g_ref, s_ref, o_ref):
        o_ref[...] = (g_ref[...].astype(F32) + s_ref[...].astype(F32)).astype(BF16)

    blk = (None, tr, cols)
    grid_spec = pltpu.PrefetchScalarGridSpec(
        num_scalar_prefetch=1, grid=(2, per),
        in_specs=[pl.BlockSpec(blk, lambda j, i, ri: (ri[RI_N1 + j], ri[RI_C] * per + i, 0)),
                  pl.BlockSpec(blk, lambda j, i, ri: (ri[RI_N1 + j], i, 0))],
        out_specs=pl.BlockSpec(blk, lambda j, i, ri: (j, i, 0)))
    return _ordered_call(
        body, name=name, grid_spec=grid_spec, out_shape=jax.ShapeDtypeStruct((2, hr, cols), BF16),
        compiler_params=_cp("parallel", "parallel"),
    )(route_idx, g, sib)


def _sum_stage1(name, g, sib, got, route_idx):
    _, hr, cols = sib.shape
    tr = _tile(hr, 512)
    per = hr // tr

    def body(ri, gm_ref, sm_ref, gn_ref, sn_ref, g0_ref, g1_ref, keep_ref, send_ref):
        keep_ref[...] = (gm_ref[...].astype(F32) + sm_ref[...].astype(F32)) + g0_ref[...].astype(F32)
        send_ref[...] = ((gn_ref[...].astype(F32) + sn_ref[...].astype(F32)) + g1_ref[...].astype(F32)).astype(BF16)

    blk = (None, tr, cols)
    row = pl.BlockSpec((tr, cols), lambda i, ri: (i, 0))

    def mine(which):
        return pl.BlockSpec(blk, lambda i, ri: (ri[which], ri[RI_C] * per + i, 0))

    def theirs(which):
        return pl.BlockSpec(blk, lambda i, ri: (ri[which], i, 0))

    grid_spec = pltpu.PrefetchScalarGridSpec(
        num_scalar_prefetch=1, grid=(per,),
        in_specs=[mine(RI_ME), theirs(RI_ME), mine(RI_N2), theirs(RI_N2),
                  pl.BlockSpec(blk, lambda i, ri: (0, i, 0)), pl.BlockSpec(blk, lambda i, ri: (1, i, 0))],
        out_specs=(row, row))
    return _ordered_call(
        body, name=name, grid_spec=grid_spec,
        out_shape=(jax.ShapeDtypeStruct((hr, cols), F32), jax.ShapeDtypeStruct((hr, cols), BF16)),
        compiler_params=_cp("parallel"),
    )(route_idx, g, sib, g, sib, got, got)


def _sum_stage2(name, keep, got):
    hr, cols = keep.shape
    tr = _tile(hr, 512)

    def body(k_ref, g_ref, o_ref):
        o_ref[...] = k_ref[...] + g_ref[...].astype(F32)

    row = pl.BlockSpec((tr, cols), lambda i: (i, 0))
    return _ordered_call(
        body, name=name, grid=(hr // tr,), in_specs=[row, row], out_specs=row,
        out_shape=jax.ShapeDtypeStruct((hr, cols), F32), compiler_params=_cp("parallel"),
    )(keep, got)


def _reduce_scatter(tag, names, grads, route_idx):
    n = len(grads)
    hrs = [g.shape[1] // 2 for g in grads]

    def empty(t, lead, dtype):
        return lax.empty(lead + (hrs[t], grads[t].shape[2]), dtype)

    def pair_stage(refs, sems):
        x, y, c, n1, n2, dg = _route()
        return [_remote(refs[t].at[:, pl.ds((1 - c) * hrs[t], hrs[t]), :], refs[n + t], sems, t, (x, y, 1 - c))
                for t in range(n)]

    def stage1(refs, sems):
        x, y, c, n1, n2, dg = _route()
        return [_remote(refs[t].at[slot], refs[n + t].at[slot], sems, 2 * t + slot, (*n1, c))
                for t in range(n) for slot in range(2)]

    def stage2(refs, sems):
        x, y, c, n1, n2, dg = _route()
        return [_remote(refs[t], refs[n + t], sems, t, (*n2, c)) for t in range(n)]

    def stage3(refs, sems):
        x, y, c, n1, n2, dg = _route()
        return [_remote(refs[t], refs[n + t], sems, t, (x, y, 1 - c)) for t in range(n)]

    state = {}

    def start():
        state["h"] = _exchange_start(f"rs_pair_{tag}", list(grads) + [empty(t, (N_CHIPS,), BF16) for t in range(n)],
                                     n, pair_stage)

    def pair_done():
        state["pair"] = _exchange_wait(state["h"])
        psum = [_pair_sum(f"rs_psum_{names[t]}", state["pair"][t], state["pair"][n + t], route_idx) for t in range(n)]
        state["h"] = _exchange_start(f"rs_x1_{tag}", psum + [empty(t, (2,), BF16) for t in range(n)], 2 * n, stage1)

    def x1_done():
        out = _exchange_wait(state["h"])
        state["keep"], send = zip(*[_sum_stage1(f"rs_sum1_{names[t]}", state["pair"][t], state["pair"][n + t],
                                                out[n + t], route_idx) for t in range(n)])
        state["h"] = _exchange_start(f"rs_x2_{tag}", list(send) + [empty(t, (), BF16) for t in range(n)], n, stage2)

    def x2_done():
        out = _exchange_wait(state["h"])
        mine = [_sum_stage2(f"rs_sum2_{names[t]}", state["keep"][t], out[n + t]) for t in range(n)]
        state["h"] = _exchange_start(f"rs_half_{tag}", mine + [empty(t, (), F32) for t in range(n)], n, stage3)

    def finish():
        out = _exchange_wait(state["h"])
        return list(zip(out[:n], out[n:]))

    return start, pair_done, x1_done, x2_done, finish


def _adamw_big(name, w, m, v, f, h, l, c_idx, prev):
    n_l, r, cols = w.shape
    hr = r // 2
    tr = _tile(hr, 256)
    per = hr // tr

    def body(c_ref, w_ref, m_ref, v_ref, f_ref, h_ref, *rest):
        g_ref, d_ref, nm_ref, nv_ref = rest[-4:]
        gv = jnp.where(pl.program_id(0) == c_ref[0], f_ref[...], h_ref[...])
        mv = ADAM_B1 * m_ref[...] + (1.0 - ADAM_B1) * gv
        vv = ADAM_B2 * v_ref[...] + (1.0 - ADAM_B2) * (gv * gv)
        m_hat = mv / (1.0 - ADAM_B1 ** ADAM_STEP)
        v_hat = vv / (1.0 - ADAM_B2 ** ADAM_STEP)
        g_ref[...] = gv
        d_ref[...] = -ADAM_LR * (m_hat / (jnp.sqrt(v_hat) + ADAM_EPS) + ADAM_WD * w_ref[...])
        nm_ref[...] = mv
        nv_ref[...] = vv

    big = pl.BlockSpec((None, tr, cols), lambda hf, i, c_ref: (l, hf * per + i, 0))
    fspec = pl.BlockSpec((tr, cols), lambda hf, i, c_ref: (jnp.where(hf == c_ref[0], i, 0), 0))
    hspec = pl.BlockSpec((tr, cols), lambda hf, i, c_ref: (jnp.where(hf == c_ref[0], 0, i), 0))
    grid_spec = pltpu.PrefetchScalarGridSpec(
        num_scalar_prefetch=1, grid=(2, per), in_specs=[big] * 3 + [fspec, hspec] + [ANY] * len(prev),
        out_specs=(big,) * 4)
    return _ordered_call(
        body, name=name, grid_spec=grid_spec, out_shape=(jax.ShapeDtypeStruct(w.shape, F32),) * 4,
        input_output_aliases={6 + k: k for k in range(len(prev))}, compiler_params=_cp("arbitrary", "arbitrary"),
    )(c_idx, w, m, v, f, h, *prev)


def _small_allreduce(buf):
    rows = buf.shape[0]
    hr = rows // 2

    def body(in_ref, out_ref, pair, acc, got1, got2, send_sems, recv_sems):
        x, y, c, n1, n2, dg = _route()
        sems = (send_sems, recv_sems)
        sibling = (x, y, 1 - c)
        mine = pl.ds(pl.multiple_of(c * hr, 8), hr)
        pair[c] = in_ref[...]
        cp = _remote(in_ref, pair.at[c], sems, 0, sibling)
        cp.start()
        cp.wait()
        acc[...] = pair[0, mine, :] + pair[1, mine, :]
        cp = _remote(acc, got1, sems, 1, (*n1, c))
        cp.start()
        cp.wait()
        acc[...] = acc[...] + got1[...]
        cp = _remote(acc, got2, sems, 2, (*n2, c))
        cp.start()
        cp.wait()
        out_ref[mine, :] = acc[...] + got2[...]
        cp = _remote(out_ref.at[mine, :], out_ref.at[mine, :], sems, 3, sibling)
        cp.start()
        cp.wait()

    half = pltpu.VMEM((hr, 128), F32)
    return _ordered_call(
        body, name="small_allreduce", in_specs=[pl.BlockSpec(memory_space=pltpu.VMEM)],
        out_specs=pl.BlockSpec(memory_space=pltpu.VMEM), out_shape=jax.ShapeDtypeStruct((rows, 128), F32),
        scratch_shapes=[pltpu.VMEM((2, rows, 128), F32), half, half, half,
                        pltpu.SemaphoreType.DMA((4,)), pltpu.SemaphoreType.DMA((4,))],
        compiler_params=pltpu.CompilerParams(has_side_effects=True, vmem_limit_bytes=VMEM_LIMIT),
    )(buf)


BIG = ("w_in", "w_out", "w_up", "w_down")
COL_SHARDED = {"w_in": True, "w_out": False, "w_up": True, "w_down": False}
SMALL = ("ln1_g", "q_norm_g", "k_norm_g", "sinks", "conv_w", "conv_b", "conv_ln_g", "conv_ln_b", "sgu_ln_g",
         "sgu_ln_b", "sgu_w", "sgu_b", "out_norm_g", "ln2_g")
WEIGHTS = ("ln1_g", "w_in", "q_norm_g", "k_norm_g", "sinks", "conv_w", "conv_b", "conv_ln_g", "conv_ln_b",
           "sgu_ln_g", "sgu_ln_b", "sgu_w", "sgu_b", "out_norm_g", "w_out", "ln2_g", "w_up", "w_down")
PACK_QUANTUM = 8 * 128
PACK_ROWS = 512


def _pack(arrs):
    parts = []
    for a in arrs:
        f = a.reshape(-1)
        parts.append(jnp.pad(f, (0, -f.shape[0] % PACK_QUANTUM)).reshape(-1, 128))
    rows = sum(p.shape[0] for p in parts)
    parts.append(jnp.zeros((-rows % PACK_ROWS, 128), F32))
    return jnp.concatenate(parts, axis=0)


def _unpack(buf, shapes):
    out, off = [], 0
    for shp in shapes:
        n = 1
        for dd in shp:
            n *= dd
        rows = (n + PACK_QUANTUM - 1) // PACK_QUANTUM * 8
        out.append(buf[off:off + rows].reshape(-1)[:n].reshape(shp))
        off += rows
    return out


def _to_heads(t, nh):
    return t.reshape(t.shape[0], nh, HEAD_DIM).transpose(1, 0, 2)


def _from_heads(t):
    return t.transpose(1, 0, 2).reshape(t.shape[1], t.shape[0] * HEAD_DIM)


def _no_hook(point, carry):
    return carry


def _layer_fwd(l, x, p, wg, hook=_no_hook):
    d = x.shape[1]
    aw, cw = d // 2, d // 4
    nq = aw // HEAD_DIM
    nkv = nq // GQA
    kvw = nkv * HEAD_DIM
    x = hook("fwd_start", x)
    h1 = _rms_fwd(f"ln1_fwd_{l}", x, p["ln1_g"])
    proj = _mm_act_w(f"proj_{l}", h1, wg["w_in"], True, _ep_store)[0]
    proj = hook("fwd_proj", proj)
    q = _to_heads(proj[:, :aw], nq)
    k = _to_heads(proj[:, aw:aw + kvw], nkv)
    v = _to_heads(proj[:, aw + kvw:aw + 2 * kvw], nkv)
    sinks_b = jnp.broadcast_to(p["sinks"][:, None, None], (nq, 1, 128))
    ya = _from_heads(_attn_fwd(f"attn_fwd_{l}", q, k, v, p["q_norm_g"], p["k_norm_g"], sinks_b))
    ya = hook("fwd_attn", ya)
    yc = _conv_fwd(f"conv_fwd_{l}", proj, 3, p["conv_w"], p["conv_b"], p["conv_ln_g"], p["conv_ln_b"])
    ys = _sgu_fwd(f"sgu_fwd_{l}", proj, 5, p["sgu_ln_g"], p["sgu_ln_b"], p["sgu_w"], p["sgu_bexp"])
    mix = _mixnorm_fwd(f"mixnorm_fwd_{l}", ya, yc, ys, p["out_norm_g"])
    mix = hook("fwd_mid", mix)
    xm = _mm_act_w(f"out_{l}", mix, wg["w_out"], False, _ep_residual, extra=(x,))[0]
    h2 = _rms_fwd(f"ln2_fwd_{l}", xm, p["ln2_g"])
    h2 = hook("fwd_ln2", h2)
    up_b, act_b = _mm_act_w(f"up_{l}", h2, wg["w_up"], True, _ep_up, out_dtypes=(BF16, BF16))
    act_b = hook("fwd_up", act_b)
    xo = _mm_act_w(f"down_{l}", act_b, wg["w_down"], False, _ep_residual, extra=(xm,))[0]
    xo = hook("fwd_end", xo)
    saved = dict(x=x, h1=h1, proj=proj, q=q, k=k, v=v, sinks_b=sinks_b, ya=ya, yc=yc, ys=ys, mix=mix, xm=xm, h2=h2,
                 up_b=up_b, act_b=act_b)
    return xo, saved


def _layer_bwd(l, dxo, dxo_b, p, wg, sv, big, hook=_no_hook):
    d = dxo.shape[1]
    nq = (d // 2) // HEAD_DIM
    small = {}
    dxo_b = hook("bwd_start", dxo_b)
    big["w_down"] = _mm_wgrad(f"dw_down_{l}", sv["act_b"], dxo_b, False, d)
    dup_b = _mm_act_wt(f"dup_{l}", dxo_b, wg["w_down"], False, _ep_dup, extra=(sv["up_b"],), out_dtypes=(BF16,))[0]
    dup_b = hook("bwd_dup", dup_b)
    big["w_up"] = _mm_wgrad(f"dw_up_{l}", sv["h2"], dup_b, True, wg["w_up"].shape[2])
    dh2 = _mm_act_wt(f"dh2_{l}", dup_b, wg["w_up"], True, _ep_store)[0]
    dh2 = hook("bwd_dh2", dh2)
    dxm, dxm_b, small["ln2_g"] = _rms_bwd(f"ln2_bwd_{l}", dh2, sv["xm"], p["ln2_g"], dxo)
    big["w_out"] = _mm_wgrad(f"dw_out_{l}", sv["mix"], dxm_b, False, d)
    dmix = _mm_act_wt(f"dmix_{l}", dxm_b, wg["w_out"], False, _ep_store)[0]
    dya, dyc, dys, small["out_norm_g"] = _mixnorm_bwd(f"mixnorm_bwd_{l}", dmix, sv["ya"], sv["yc"], sv["ys"],
                                                      p["out_norm_g"])
    dya = hook("bwd_mix", dya)
    dq, dkc, dkp, dvc, dvp, small["q_norm_g"], dsink = _attn_bwd(
        f"attn_bwd_{l}", sv["q"], sv["k"], sv["v"], p["q_norm_g"], p["k_norm_g"], sv["sinks_b"], _to_heads(dya, nq))
    dkc = hook("bwd_attn", dkc)
    small["sinks"] = dsink[:, 0, 0]
    dk, dv, small["k_norm_g"] = _attn_bwd_kv(f"attn_bwd_kv_{l}", sv["k"], p["k_norm_g"], dkc, dkp, dvc, dvp)
    dc, dcw, small["conv_b"], small["conv_ln_g"], small["conv_ln_b"] = _conv_bwd1(
        f"conv_bwd1_{l}", sv["proj"], 3, p["conv_w"], p["conv_b"], p["conv_ln_g"], p["conv_ln_b"], dyc)
    small["conv_w"] = dcw[:CONV_KERNEL]
    dxc_b = _conv_bwd2(f"conv_bwd2_{l}", sv["proj"], 3, p["conv_w"], dc)
    dxs_b, small["sgu_w"], small["sgu_b"], small["sgu_ln_g"], small["sgu_ln_b"] = _sgu_bwd(
        f"sgu_bwd_{l}", sv["proj"], 5, p["sgu_ln_g"], p["sgu_ln_b"], p["sgu_w"], p["sgu_bexp"], dys)
    dxs_b = hook("bwd_sgu", dxs_b)
    dproj_b = jnp.concatenate([_from_heads(dq), _from_heads(dk), _from_heads(dv), dxc_b, dxs_b], axis=1)
    big["w_in"] = _mm_wgrad(f"dw_in_{l}", sv["h1"], dproj_b, True, wg["w_in"].shape[2])
    dh1 = _mm_act_wt(f"dh1_{l}", dproj_b, wg["w_in"], True, _ep_store)[0]
    dx, dx_b, small["ln1_g"] = _rms_bwd(f"ln1_bwd_{l}", dh1, sv["x"], p["ln1_g"], dxm)
    dx_b = hook("bwd_end", dx_b)
    return dx, dx_b, small


def kernel(x, ln1_g, w_in, q_norm_g, k_norm_g, sinks, conv_w, conv_b, conv_ln_g, conv_ln_b, sgu_ln_g, sgu_ln_b, sgu_w, sgu_b, out_norm_g, w_out, ln2_g, w_up, w_down, loss_target, m_ln1_g, m_w_in, m_q_norm_g, m_k_norm_g, m_sinks, m_conv_w, m_conv_b, m_conv_ln_g, m_conv_ln_b, m_sgu_ln_g, m_sgu_ln_b, m_sgu_w, m_sgu_b, m_out_norm_g, m_w_out, m_ln2_g, m_w_up, m_w_down, v_ln1_g, v_w_in, v_q_norm_g, v_k_norm_g, v_sinks, v_conv_w, v_conv_b, v_conv_ln_g, v_conv_ln_b, v_sgu_ln_g, v_sgu_ln_b, v_sgu_w, v_sgu_b, v_out_norm_g, v_w_out, v_ln2_g, v_w_up, v_w_down):
    given = dict(locals())
    _LAST[0] = None
    n_layers = ln1_g.shape[0]
    s, d = x.shape[1], x.shape[2]
    cw = d // 4
    xi, yi, core = lax.axis_index("x"), lax.axis_index("y"), lax.axis_index("c")
    chip = 2 * xi + yi
    first_partner = jnp.where(core == 0, 2 * (1 - xi) + yi, 2 * xi + (1 - yi))
    second_partner = jnp.where(core == 0, 2 * xi + (1 - yi), 2 * (1 - xi) + yi)
    route_idx = jnp.stack([core, chip, second_partner, first_partner, 3 - chip]).astype(jnp.int32)

    conv_w_pad = jnp.pad(conv_w, ((0, 0), (0, HALO - CONV_KERNEL), (0, 0))).reshape(1, n_layers * HALO, -1)
    cwl = conv_w_pad.shape[2]
    buf = {}

    def place(key):
        if key == "conv_w":
            buf[key] = _cast_place("place_conv_w", conv_w_pad, 0, route_idx[1:2], F32)
        else:
            buf[key] = _cast_place(f"place_{key[0]}_{key[1]}", given[key[0]], key[1], route_idx[1:2], BF16)

    groups = [["conv_w", ("w_in", 0)]] + [[(nm, l)] for l in range(n_layers) for nm in BIG if (nm, l) != ("w_in", 0)]
    n_steps = len(groups) + 2
    pending = {}

    def start_step(st):
        keys = [groups[st - j] if 0 <= st - j < len(groups) else [] for j in range(3)]
        flat, n_sems, build = _gather_step([[buf[k] for k in ks] for ks in keys])
        pending["keys"] = [k for ks in keys for k in ks]
        pending["h"] = _exchange_start(f"gather_step{st}", flat, n_sems, build)

    def wait_step():
        for k, b in zip(pending["keys"], _exchange_wait(pending["h"])):
            buf[k] = b

    later = [k for grp in groups[1:] for k in grp]
    for k in groups[0]:
        place(k)
    for st, upto in enumerate((2, 5, len(later))):
        start_step(st)
        for k in later[:upto]:
            if k not in buf:
                place(k)
        wait_step()
    conv_w_full = buf["conv_w"].reshape(N_CHIPS, n_layers, HALO, cwl).transpose(1, 2, 0, 3).reshape(
        n_layers, HALO, cw)

    class LayerWeights:
        def __init__(self, l):
            self.l = l

        def __getitem__(self, nm):
            return buf[(nm, self.l)]

    wgs = [LayerWeights(l) for l in range(n_layers)]
    fwd_points = [(l, pt) for l in range(n_layers) for pt in ("fwd_start", "fwd_attn", "fwd_ln2", "fwd_up", "fwd_end")
                  if (pt != "fwd_start" or l == 0) and (pt != "fwd_end" or l + 1 < n_layers)]
    assert len(fwd_points) == n_steps - 3 + 1, "one hook point per pipeline step, and one to wait for the last"
    fwd_tables = [{} for _ in range(n_layers)]
    for i, (l, pt) in enumerate(fwd_points):
        if i > 0:
            fwd_tables[l].setdefault(pt, []).append(wait_step)
        if 3 + i < n_steps:
            fwd_tables[l].setdefault(pt, []).append(functools.partial(start_step, 3 + i))
    params = []
    for l in range(n_layers):
        p = {nm: given[nm][l] for nm in SMALL if nm != "conv_w"}
        for nm in ("conv_b", "conv_ln_g", "conv_ln_b", "sgu_ln_g", "sgu_ln_b"):
            p[nm] = p[nm].reshape(1, -1)
        p["conv_w"] = conv_w_full[l]
        p["sgu_bexp"] = jnp.repeat(sgu_b[l].T, HEAD_DIM, axis=1)
        params.append(p)

    def make_hook(table):
        def hook(point, carry):
            for fn in table.get(point, ()):
                fn()
            return carry
        return hook

    h = x.reshape(s, d)
    saved = []
    for l in range(n_layers):
        h, sv = _layer_fwd(l, h, params[l], wgs[l], make_hook(fwd_tables[l]))
        saved.append(sv)
    dh, dh_b, loss_part = _loss_head(h, loss_target.reshape(s, d))
    loss = lax.psum(loss_part[0, 0], ("x", "y", "c"))

    big_grads = [{} for _ in range(n_layers)]
    small_grads = [None] * n_layers
    halves = {}

    def rs_group(tag, l, names):
        phases = {}

        def start():
            phases["p"] = _reduce_scatter(tag, [f"{nm}_{l}" for nm in names], [big_grads[l][nm] for nm in names],
                                          route_idx)
            phases["p"][0]()

        def step(k):
            return lambda: phases["p"][k]()

        def finish():
            for nm, fh in zip(names, phases["p"][4]()):
                halves[(nm, l)] = fh

        return [start, step(1), step(2), step(3), finish]

    early = rs_group("l0a", 0, ("w_down", "w_up", "w_out"))
    for l in reversed(range(n_layers)):
        table = {}
        if l + 1 < n_layers:
            above = rs_group(f"l{l + 1}", l + 1, BIG)
            for point, fn in zip(("bwd_start", "bwd_dup", "bwd_mix", "bwd_attn", "bwd_sgu"), above):
                table.setdefault(point, []).append(fn)
        if l == 0:
            for point, fn in zip(("bwd_mix", "bwd_attn", "bwd_end"), early[:3]):
                table.setdefault(point, []).append(fn)
        dh, dh_b, small_grads[l] = _layer_bwd(l, dh, dh_b, params[l], wgs[l], saved[l], big_grads[l],
                                              make_hook(table))
    grad_x = dh.reshape(x.shape)

    grads, delta, new_m, new_v = {}, {}, {}, {}
    adam_state = {nm: () for nm in BIG}

    def adam(nm, l):
        f, h = halves[(nm, l)]
        adam_state[nm] = _adamw_big(f"adamw_{nm}_{l}", given[nm], given["m_" + nm], given["v_" + nm], f, h, l,
                                    route_idx[0:1], adam_state[nm])

    def small_update():
        small_shapes = [(n_layers,) + small_grads[0][nm].shape for nm in SMALL]
        small_sum = _small_allreduce(_pack([jnp.stack([small_grads[l][nm] for l in range(n_layers)])
                                            for nm in SMALL]))
        for nm, g in zip(SMALL, _unpack(small_sum, small_shapes)):
            grads[nm] = g.reshape((n_layers,) + given[nm].shape[1:]) if nm != "conv_w" else g
        grads["conv_w"] = lax.dynamic_slice_in_dim(grads["conv_w"], chip * cwl, cwl, axis=2)
        packed = [_pack([src[nm] for nm in SMALL]) for src in
                  ({nm: given[nm] for nm in SMALL}, grads, {nm: given["m_" + nm] for nm in SMALL},
                   {nm: given["v_" + nm] for nm in SMALL})]
        local_shapes = [given[nm].shape for nm in SMALL]
        for dst, buf in zip((delta, new_m, new_v), _adamw("adamw_small", *packed)):
            for nm, a in zip(SMALL, _unpack(buf, local_shapes)):
                dst[nm] = a

    upper = [(nm, l) for l in reversed(range(1, n_layers)) for nm in reversed(BIG)]
    late = rs_group("l0b", 0, ("w_in",))
    late[0]()
    for task in upper[:1]:
        adam(*task)
    late[1]()
    for task in upper[1:]:
        adam(*task)
    early[3]()
    late[2]()
    small_update()
    early[4]()
    for nm in ("w_down", "w_up"):
        adam(nm, 0)
    late[3]()
    adam("w_out", 0)
    late[4]()
    adam("w_in", 0)
    for nm in BIG:
        grads[nm], delta[nm], new_m[nm], new_v[nm] = adam_state[nm]
    return (loss, grad_x, *[grads[nm] for nm in WEIGHTS], *[delta[nm] for nm in WEIGHTS],
            *[new_m[nm] for nm in WEIGHTS], *[new_v[nm] for nm in WEIGHTS])
```

```python
import functools

import jax
import jax.numpy as jnp
from jax import lax
from jax.experimental import pallas as pl
from jax.experimental.pallas import tpu as pltpu

F32 = jnp.float32
BF16 = jnp.bfloat16
EPS = 1e-6
NEG_INF = -1e30
HEAD_DIM = 64
WINDOW = 128
CONV_KERNEL = 31
HALO = 32
GQA = 4
N_CHIPS = 4
ADAM_LR, ADAM_B1, ADAM_B2, ADAM_EPS, ADAM_WD, ADAM_STEP = 0.001, 0.9, 0.999, 1e-08, 0.01, 10
VMEM_LIMIT = 56 * 1024 * 1024
TILE_K = 2048
MESH = pl.DeviceIdType.MESH
ANY = pl.BlockSpec(memory_space=pl.ANY)

NN = (((1,), (0,)), ((), ()))
NT = (((1,), (1,)), ((), ()))
TN = (((0,), (0,)), ((), ()))


def _cp(*sem):
    return pltpu.CompilerParams(dimension_semantics=sem, vmem_limit_bytes=VMEM_LIMIT)


_LAST = [None]
TOKEN = jax.ShapeDtypeStruct((8, 128), F32)


def _ordered_call(body, *, out_shape, out_specs=None, in_specs=None, grid_spec=None, grid=None, **kw):
    single = not isinstance(out_shape, (tuple, list))
    shapes = (out_shape,) if single else tuple(out_shape)

    def run(*operands):
        dep = _LAST[0]
        n = len(operands)
        n_dep = 0 if dep is None else 1

        def fn(*refs):
            outs = refs[n + n_dep:n + n_dep + len(shapes)]
            token = refs[n + n_dep + len(shapes)]
            body(*refs[:n], *outs, *refs[n + n_dep + len(shapes) + 1:])
            token[...] = jnp.zeros_like(token)

        specs_in = list(grid_spec.in_specs if grid_spec is not None else in_specs) + [ANY] * n_dep
        specs_out = grid_spec.out_specs if grid_spec is not None else out_specs
        specs_out = tuple(specs_out) if isinstance(specs_out, (tuple, list)) else (specs_out,)
        if grid_spec is not None or grid:
            specs_out += (pl.BlockSpec(TOKEN.shape, lambda *_: (0, 0)),)
        else:
            specs_out += (pl.BlockSpec(memory_space=pltpu.VMEM),)
        args = operands + ((dep,) if n_dep else ())
        if grid_spec is not None:
            spec = pltpu.PrefetchScalarGridSpec(num_scalar_prefetch=grid_spec.num_scalar_prefetch, grid=grid_spec.grid,
                                                in_specs=specs_in, out_specs=specs_out)
            out = pl.pallas_call(fn, grid_spec=spec, out_shape=shapes + (TOKEN,), **kw)(*args)
        else:
            if grid:
                kw["grid"] = grid
            out = pl.pallas_call(fn, in_specs=specs_in, out_specs=specs_out, out_shape=shapes + (TOKEN,), **kw)(*args)
        _LAST[0] = out[-1]
        return out[0] if single else tuple(out[:-1])

    return run


def _tile(dim, pref):
    if dim <= pref:
        return dim
    for t in range(pref, 0, -128):
        if dim % t == 0:
            return t
    while dim % pref:
        pref //= 2
    return pref


def _dot(a, b, dims=NN):
    return lax.dot_general(a, b, dims, preferred_element_type=F32)


def _colsum(v):
    return jnp.sum(v, axis=0, keepdims=True)


def _sigmoid(x):
    return 1.0 / (1.0 + jnp.exp(-x))


def _matmul(name, operands, in_specs, out_shape, out_specs, grid, dims, acc_shape, epilogue, split_k=False):
    nk = grid[2]
    n_in = len(operands)

    def product(a_ref, b_ref):
        if split_k:
            ck = b_ref.shape[2]
            out = _dot(a_ref[:, 0:ck], b_ref[0], dims)
            for j in range(1, N_CHIPS):
                out = out + _dot(a_ref[:, j * ck:(j + 1) * ck], b_ref[j], dims)
            return out
        bv = b_ref[...]
        return _dot(a_ref[...], bv.reshape(-1, bv.shape[-1]) if bv.ndim == 3 else bv, dims)

    def body(*refs):
        a_ref, b_ref = refs[0], refs[1]
        extra = refs[2:n_in]
        if nk == 1:
            epilogue(product(a_ref, b_ref), extra, refs[n_in:])
            return
        outs = refs[n_in:-1]
        acc = refs[-1]
        k = pl.program_id(2)

        @pl.when(k == 0)
        def _():
            acc[...] = product(a_ref, b_ref)

        @pl.when((k > 0) & (k < nk - 1))
        def _():
            acc[...] += product(a_ref, b_ref)

        @pl.when(k == nk - 1)
        def _():
            epilogue(acc[...] + product(a_ref, b_ref), extra, outs)

    return _ordered_call(
        body, name=name, grid=grid, in_specs=in_specs, out_specs=out_specs, out_shape=out_shape,
        scratch_shapes=[pltpu.VMEM(acc_shape, F32)] if nk > 1 else [],
        compiler_params=_cp("parallel", "parallel", "arbitrary"),
    )(*operands)


def _ep_store(acc, extra, outs):
    outs[0][...] = acc.astype(outs[0].dtype)


def _ep_residual(acc, extra, outs):
    outs[0][...] = extra[0][...] + acc


def _ep_residual_norm(acc, extra, outs):
    xm = extra[0][...] + acc
    outs[0][...] = xm
    r = lax.rsqrt(jnp.mean(xm * xm, axis=-1, keepdims=True) + EPS)
    outs[1][...] = (xm * r * extra[1][...]).astype(BF16)


def _ep_up(acc, extra, outs):
    outs[0][...] = acc.astype(BF16)
    r = jnp.maximum(acc, 0.0)
    outs[1][...] = (r * r).astype(BF16)


def _ep_dup(acc, extra, outs):
    outs[0][...] = (acc * (2.0 * jnp.maximum(extra[0][...].astype(F32), 0.0))).astype(BF16)


def _whole_rows(wg):
    return N_CHIPS * wg.shape[1] <= TILE_K


def _mm_act_w(name, a, wg, col_sharded, epilogue, extra=(), out_dtypes=(F32,), row_vectors=()):
    m, kdim = a.shape
    _, r, c = wg.shape
    tm = _tile(m, 1024)
    if col_sharded:
        n = N_CHIPS * c
        tn = _tile(c, 1024)
        tk = _tile(kdim, TILE_K)
        per = c // tn
        b_spec = pl.BlockSpec((None, tk, tn), lambda i, j, k: (j // per, k, j % per))
    elif _whole_rows(wg):
        n = c
        tm, tn, tk = _tile(m, 512), n, kdim
        b_spec = pl.BlockSpec((N_CHIPS, r, tn), lambda i, j, k: (0, 0, j))
    else:
        n = c
        tn = _tile(n, 1024)
        tk = _tile(r, TILE_K)
        per = r // tk
        b_spec = pl.BlockSpec((None, tk, tn), lambda i, j, k: (k // per, k % per, j))
    grid = (m // tm, n // tn, kdim // tk)
    o_spec = pl.BlockSpec((tm, tn), lambda i, j, k: (i, j))
    in_specs = ([pl.BlockSpec((tm, tk), lambda i, j, k: (i, k)), b_spec] + [o_spec] * len(extra)
                + [pl.BlockSpec((1, tn), lambda i, j, k: (0, j))] * len(row_vectors))
    return _matmul(name, (a, wg) + tuple(extra) + tuple(row_vectors), in_specs,
                   tuple(jax.ShapeDtypeStruct((m, n), d) for d in out_dtypes),
                   tuple(o_spec for _ in out_dtypes), grid, NN, (tm, tn), epilogue)


def _mm_act_wt(name, a, wg, col_sharded, epilogue, extra=(), out_dtypes=(F32,)):
    m, kdim = a.shape
    _, r, c = wg.shape
    tm = _tile(m, 1024)
    split_k = False
    if col_sharded and N_CHIPS * c <= 2 * TILE_K:
        n = r
        tm, tn, tk, split_k = _tile(m, 512), n, kdim, True
        b_spec = pl.BlockSpec((N_CHIPS, tn, c), lambda i, j, k: (0, j, 0))
    elif col_sharded:
        n = r
        tn = _tile(n, 1024)
        tk = _tile(c, TILE_K)
        per = c // tk
        b_spec = pl.BlockSpec((None, tn, tk), lambda i, j, k: (k // per, j, k % per))
    elif N_CHIPS * r <= TILE_K:
        n = N_CHIPS * r
        tm, tn, tk = _tile(m, 512), n, _tile(c, TILE_K)
        b_spec = pl.BlockSpec((N_CHIPS, r, tk), lambda i, j, k: (0, 0, k))
    else:
        n = N_CHIPS * r
        tn = _tile(r, 1024)
        tk = _tile(c, TILE_K)
        per = r // tn
        b_spec = pl.BlockSpec((None, tn, tk), lambda i, j, k: (j // per, j % per, k))
    grid = (m // tm, n // tn, kdim // tk)
    o_spec = pl.BlockSpec((tm, tn), lambda i, j, k: (i, j))
    in_specs = [pl.BlockSpec((tm, tk), lambda i, j, k: (i, k)), b_spec] + [o_spec] * len(extra)
    return _matmul(name, (a, wg) + tuple(extra), in_specs,
                   tuple(jax.ShapeDtypeStruct((m, n), d) for d in out_dtypes),
                   tuple(o_spec for _ in out_dtypes), grid, NT, (tm, tn), epilogue, split_k)


def _mm_wgrad(name, a, g, col_sharded, c):
    s, kdim = a.shape
    _, n = g.shape
    ts = _tile(s, TILE_K)
    if col_sharded:
        r = kdim
        tm = _tile(kdim, 1024)
        tn = _tile(c, 1024)
        per = c // tn
        o_spec = pl.BlockSpec((None, tm, tn), lambda i, j, k: (j // per, i, j % per))
    else:
        r = kdim // N_CHIPS
        tm = _tile(r, 512)
        tn = _tile(c, 2048)
        per = r // tm
        o_spec = pl.BlockSpec((None, tm, tn), lambda i, j, k: (i // per, i % per, j))
    grid = (kdim // tm, n // tn, s // ts)
    in_specs = [pl.BlockSpec((ts, tm), lambda i, j, k: (k, i)), pl.BlockSpec((ts, tn), lambda i, j, k: (k, j))]
    return _matmul(name, (a, g), in_specs, (jax.ShapeDtypeStruct((N_CHIPS, r, c), BF16),), (o_spec,),
                   grid, TN, (tm, tn), _ep_store)[0]


def _rms_fwd(name, x, g):
    s, d = x.shape
    tb = _tile(s, 256)

    def body(x_ref, g_ref, o_ref):
        xv = x_ref[...]
        r = lax.rsqrt(jnp.mean(xv * xv, axis=-1, keepdims=True) + EPS)
        o_ref[...] = (xv * r * g_ref[...]).astype(BF16)

    return _ordered_call(
        body, name=name, grid=(s // tb,),
        in_specs=[pl.BlockSpec((tb, d), lambda i: (i, 0)), pl.BlockSpec((1, d), lambda i: (0, 0))],
        out_specs=pl.BlockSpec((tb, d), lambda i: (i, 0)),
        out_shape=jax.ShapeDtypeStruct((s, d), BF16), compiler_params=_cp("parallel"),
    )(x, g.reshape(1, d))


def _rms_bwd(name, dh, x, g, dres):
    s, d = x.shape
    tb = _tile(s, 256)

    def body(dh_ref, x_ref, g_ref, dres_ref, dx_ref, dxb_ref, dg_ref):
        i = pl.program_id(0)
        xv = x_ref[...]
        r = lax.rsqrt(jnp.mean(xv * xv, axis=-1, keepdims=True) + EPS)
        xhat = xv * r
        dhv = dh_ref[...]
        dxhat = dhv * g_ref[...]
        dx = dres_ref[...] + r * (dxhat - xhat * jnp.mean(dxhat * xhat, axis=-1, keepdims=True))
        dx_ref[...] = dx
        dxb_ref[...] = dx.astype(BF16)

        @pl.when(i == 0)
        def _():
            dg_ref[...] = jnp.zeros_like(dg_ref)

        dg_ref[...] += _colsum(dhv * xhat)

    row = pl.BlockSpec((tb, d), lambda i: (i, 0))
    vec = pl.BlockSpec((1, d), lambda i: (0, 0))
    return _ordered_call(
        body, name=name, grid=(s // tb,), in_specs=[row, row, vec, row], out_specs=(row, row, vec),
        out_shape=(jax.ShapeDtypeStruct((s, d), F32), jax.ShapeDtypeStruct((s, d), BF16),
                   jax.ShapeDtypeStruct((1, d), F32)),
        compiler_params=_cp("arbitrary"),
    )(dh, x, g.reshape(1, d), dres)


def _loss_head(y, t):
    s, d = y.shape
    tb = _tile(s, 256)

    def body(y_ref, t_ref, dy_ref, dyb_ref, loss_ref, acc):
        i = pl.program_id(0)
        e = y_ref[...] - t_ref[...]
        dy = e * (1.0 / d)
        dy_ref[...] = dy
        dyb_ref[...] = dy.astype(BF16)

        @pl.when(i == 0)
        def _():
            acc[...] = jnp.zeros_like(acc)

        acc[...] += _colsum(e * e)

        @pl.when(i == pl.num_programs(0) - 1)
        def _():
            loss_ref[...] = jnp.sum(acc[...], axis=-1, keepdims=True) * (0.5 / d)

    row = pl.BlockSpec((tb, d), lambda i: (i, 0))
    return _ordered_call(
        body, name="loss_head", grid=(s // tb,), in_specs=[row, row],
        out_specs=(row, row, pl.BlockSpec((1, 1), lambda i: (0, 0))),
        out_shape=(jax.ShapeDtypeStruct((s, d), F32), jax.ShapeDtypeStruct((s, d), BF16),
                   jax.ShapeDtypeStruct((1, 1), F32)),
        scratch_shapes=[pltpu.VMEM((1, d), F32)], compiler_params=_cp("arbitrary"),
    )(y, t)


def _mixnorm_fwd(name, ya, yc, ys, g):
    s, aw = ya.shape
    cw, sw = yc.shape[1], ys.shape[1]
    d = aw + cw + sw
    tb = _tile(s, 256)

    def body(ya_ref, yc_ref, ys_ref, g_ref, o_ref):
        off = 0
        for ref, w in ((ya_ref, aw), (yc_ref, cw), (ys_ref, sw)):
            v = ref[...]
            r = lax.rsqrt(jnp.mean(v * v, axis=-1, keepdims=True) + EPS)
            o_ref[:, off:off + w] = (v * r * g_ref[:, off:off + w]).astype(BF16)
            off += w

    def row(w):
        return pl.BlockSpec((tb, w), lambda i: (i, 0))

    return _ordered_call(
        body, name=name, grid=(s // tb,),
        in_specs=[row(aw), row(cw), row(sw), pl.BlockSpec((1, d), lambda i: (0, 0))], out_specs=row(d),
        out_shape=jax.ShapeDtypeStruct((s, d), BF16), compiler_params=_cp("parallel"),
    )(ya, yc, ys, g.reshape(1, d))


def _mixnorm_bwd(name, dmix, ya, yc, ys, g):
    s, aw = ya.shape
    cw, sw = yc.shape[1], ys.shape[1]
    d = aw + cw + sw
    tb = _tile(s, 256)

    def body(dm_ref, ya_ref, yc_ref, ys_ref, g_ref, dya_ref, dyc_ref, dys_ref, dg_ref):
        i = pl.program_id(0)

        @pl.when(i == 0)
        def _():
            dg_ref[...] = jnp.zeros_like(dg_ref)

        off = 0
        for ref, dref, w in ((ya_ref, dya_ref, aw), (yc_ref, dyc_ref, cw), (ys_ref, dys_ref, sw)):
            v = ref[...]
            r = lax.rsqrt(jnp.mean(v * v, axis=-1, keepdims=True) + EPS)
            vhat = v * r
            dm = dm_ref[:, off:off + w]
            dvhat = dm * g_ref[:, off:off + w]
            dref[...] = r * (dvhat - vhat * jnp.mean(dvhat * vhat, axis=-1, keepdims=True))
            dg_ref[:, off:off + w] += _colsum(dm * vhat)
            off += w

    def row(w):
        return pl.BlockSpec((tb, w), lambda i: (i, 0))

    vec = pl.BlockSpec((1, d), lambda i: (0, 0))
    return _ordered_call(
        body, name=name, grid=(s // tb,), in_specs=[row(d), row(aw), row(cw), row(sw), vec],
        out_specs=(row(aw), row(cw), row(sw), vec),
        out_shape=(jax.ShapeDtypeStruct((s, aw), F32), jax.ShapeDtypeStruct((s, cw), F32),
                   jax.ShapeDtypeStruct((s, sw), F32), jax.ShapeDtypeStruct((1, d), F32)),
        compiler_params=_cp("arbitrary"),
    )(dmix, ya, yc, ys, g.reshape(1, d))


def _head_rms(x):
    r = lax.rsqrt(jnp.mean(x * x, axis=-1, keepdims=True) + EPS)
    return x * r, r


def _attn_mask(n):
    qi = lax.broadcasted_iota(jnp.int32, (GQA * WINDOW, 2 * WINDOW), 0) & (WINDOW - 1)
    sj = lax.broadcasted_iota(jnp.int32, (GQA * WINDOW, 2 * WINDOW), 1)
    rel = qi + WINDOW - sj
    return (rel >= 0) & (rel < WINDOW) & ((sj >= WINDOW) | (n > 0))


def _attn_specs(nq, nkv, nb):
    qspec = pl.BlockSpec((nq, WINDOW, HEAD_DIM), lambda n: (0, n, 0))
    cur = pl.BlockSpec((nkv, WINDOW, HEAD_DIM), lambda n: (0, n, 0))
    prev = pl.BlockSpec((nkv, WINDOW, HEAD_DIM), lambda n: (0, jnp.maximum(n - 1, 0), 0))
    nxt = pl.BlockSpec((nkv, WINDOW, HEAD_DIM), lambda n: (0, jnp.minimum(n + 1, nb - 1), 0))
    gain = pl.BlockSpec((1, HEAD_DIM), lambda n: (0, 0))
    sink = pl.BlockSpec((nq, 1, 128), lambda n: (0, 0, 0))
    return qspec, cur, prev, nxt, gain, sink


def _group_sinks(s_ref, g):
    return jnp.concatenate([jnp.broadcast_to(s_ref[g * GQA + i][:, :1], (WINDOW, 1)) for i in range(GQA)], axis=0)


ATTN_SCALE = HEAD_DIM ** -0.5
assert ATTN_SCALE == 2.0 ** -3


def _attn_probs(qs_b, kn_b, valid, sink):
    logits = jnp.where(valid, _dot(qs_b, kn_b, NT), NEG_INF)
    m = jnp.maximum(jnp.max(logits, axis=-1, keepdims=True), sink)
    p = jnp.exp(logits - m)
    es = jnp.exp(sink - m)
    inv = 1.0 / (jnp.sum(p, axis=-1, keepdims=True) + es)
    return p * inv, es * inv


def _attn_fwd(name, q, k, v, gq, gk, sinks_b):
    nq, s, _ = q.shape
    nkv = k.shape[0]
    nb = s // WINDOW
    qspec, cur, prev, _, gain, sink = _attn_specs(nq, nkv, nb)

    def body(q_ref, kc_ref, kp_ref, vc_ref, vp_ref, gq_ref, gk_ref, s_ref, o_ref):
        gkv = gk_ref[...]
        valid = _attn_mask(pl.program_id(0))
        for g in range(nkv):
            kn_b = jnp.concatenate([_head_rms(kp_ref[g])[0] * gkv, _head_rms(kc_ref[g])[0] * gkv],
                                   axis=0).astype(BF16)
            vv_b = jnp.concatenate([vp_ref[g], vc_ref[g]], axis=0).astype(BF16)
            heads = pl.ds(g * GQA, GQA)
            q4 = q_ref[heads].reshape(GQA * WINDOW, HEAD_DIM)
            qs_b = (_head_rms(q4)[0] * gq_ref[...] * ATTN_SCALE).astype(BF16)
            probs, _ = _attn_probs(qs_b, kn_b, valid, _group_sinks(s_ref, g))
            o_ref[heads] = _dot(probs.astype(BF16), vv_b).reshape(GQA, WINDOW, HEAD_DIM)

    return _ordered_call(
        body, name=name, grid=(nb,), in_specs=[qspec, cur, prev, cur, prev, gain, gain, sink], out_specs=qspec,
        out_shape=jax.ShapeDtypeStruct((nq, s, HEAD_DIM), F32), compiler_params=_cp("parallel"),
    )(q, k, k, v, v, gq.reshape(1, HEAD_DIM), gk.reshape(1, HEAD_DIM), sinks_b)


def _attn_bwd(name, q, k, v, gq, gk, sinks_b, do):
    nq, s, _ = q.shape
    nkv = k.shape[0]
    nb = s // WINDOW
    qspec, cur, prev, _, gain, sink = _attn_specs(nq, nkv, nb)

    def body(q_ref, kc_ref, kp_ref, vc_ref, vp_ref, gq_ref, gk_ref, s_ref, do_ref,
             dq_ref, dkc_ref, dkp_ref, dvc_ref, dvp_ref, dgq_ref, ds_ref):
        n = pl.program_id(0)

        @pl.when(n == 0)
        def _():
            dgq_ref[...] = jnp.zeros_like(dgq_ref)
            ds_ref[...] = jnp.zeros_like(ds_ref)

        gkv = gk_ref[...]
        gqv = gq_ref[...]
        valid = _attn_mask(n)
        dgq = jnp.zeros((1, HEAD_DIM), F32)
        for g in range(nkv):
            kn_b = jnp.concatenate([_head_rms(kp_ref[g])[0] * gkv, _head_rms(kc_ref[g])[0] * gkv],
                                   axis=0).astype(BF16)
            vv_b = jnp.concatenate([vp_ref[g], vc_ref[g]], axis=0).astype(BF16)
            heads = pl.ds(g * GQA, GQA)
            qhat, r = _head_rms(q_ref[heads].reshape(GQA * WINDOW, HEAD_DIM))
            qs_b = (qhat * gqv * ATTN_SCALE).astype(BF16)
            probs, psink = _attn_probs(qs_b, kn_b, valid, _group_sinks(s_ref, g))
            do_b = do_ref[heads].reshape(GQA * WINDOW, HEAD_DIM).astype(BF16)
            dp = _dot(do_b, vv_b, NT)
            delta = jnp.sum(probs * dp, axis=-1, keepdims=True)
            dl_b = (probs * (dp - delta)).astype(BF16)
            sink_term = psink * delta
            for i in range(GQA):
                ds_ref[g * GQA + i] += jnp.broadcast_to(
                    -jnp.sum(sink_term[i * WINDOW:(i + 1) * WINDOW], axis=0, keepdims=True), (1, 128))
            dqn = _dot(dl_b, kn_b) * ATTN_SCALE
            dkn = _dot(dl_b, qs_b, TN)
            dvv = _dot(probs.astype(BF16), do_b, TN)
            dgq += _colsum(dqn * qhat)
            dqhat = dqn * gqv
            dq_ref[heads] = (r * (dqhat - qhat * jnp.mean(dqhat * qhat, axis=-1, keepdims=True))).astype(
                BF16).reshape(GQA, WINDOW, HEAD_DIM)
            dkp_ref[g] = dkn[:WINDOW]
            dkc_ref[g] = dkn[WINDOW:]
            dvp_ref[g] = dvv[:WINDOW]
            dvc_ref[g] = dvv[WINDOW:]
        dgq_ref[...] += dgq

    kv_shape = jax.ShapeDtypeStruct((nkv, s, HEAD_DIM), F32)
    return _ordered_call(
        body, name=name, grid=(nb,), in_specs=[qspec, cur, prev, cur, prev, gain, gain, sink, qspec],
        out_specs=(qspec, cur, cur, cur, cur, gain, sink),
        out_shape=(jax.ShapeDtypeStruct((nq, s, HEAD_DIM), BF16), kv_shape, kv_shape, kv_shape, kv_shape,
                   jax.ShapeDtypeStruct((1, HEAD_DIM), F32), jax.ShapeDtypeStruct((nq, 1, 128), F32)),
        compiler_params=_cp("arbitrary"),
    )(q, k, k, v, v, gq.reshape(1, HEAD_DIM), gk.reshape(1, HEAD_DIM), sinks_b, do)


def _attn_bwd_kv(name, k, gk, dkc, dkp, dvc, dvp):
    nkv, s, _ = k.shape
    nb = s // WINDOW
    _, cur, _, nxt, gain, _ = _attn_specs(GQA * nkv, nkv, nb)

    def body(k_ref, gk_ref, dkc_ref, dkp_ref, dvc_ref, dvp_ref, dk_ref, dv_ref, dgk_ref):
        n = pl.program_id(0)

        @pl.when(n == 0)
        def _():
            dgk_ref[...] = jnp.zeros_like(dgk_ref)

        has_next = n < nb - 1
        dgk = jnp.zeros((1, HEAD_DIM), F32)
        for g in range(nkv):
            dkn = dkc_ref[g] + jnp.where(has_next, dkp_ref[g], 0.0)
            dv_ref[g] = (dvc_ref[g] + jnp.where(has_next, dvp_ref[g], 0.0)).astype(BF16)
            khat, r = _head_rms(k_ref[g])
            dgk += _colsum(dkn * khat)
            dkhat = dkn * gk_ref[...]
            dk_ref[g] = (r * (dkhat - khat * jnp.mean(dkhat * khat, axis=-1, keepdims=True))).astype(BF16)
        dgk_ref[...] += dgk

    kv_shape = jax.ShapeDtypeStruct((nkv, s, HEAD_DIM), BF16)
    return _ordered_call(
        body, name=name, grid=(nb,), in_specs=[cur, gain, cur, nxt, cur, nxt], out_specs=(cur, cur, gain),
        out_shape=(kv_shape, kv_shape, jax.ShapeDtypeStruct((1, HEAD_DIM), F32)),
        compiler_params=_cp("arbitrary"),
    )(k, gk.reshape(1, HEAD_DIM), dkc, dkp, dvc, dvp)


SUBLANES = 8


def _fill_shifts(buf, shifts, tb):
    rows = tb + HALO - SUBLANES
    for b in range(1, SUBLANES):
        shifts[b - 1, pl.ds(0, rows), :] = buf[pl.ds(b, rows), :]


def _window(buf, shifts, off, tb):
    b = off % SUBLANES
    return buf[pl.ds(off, tb), :] if b == 0 else shifts[b - 1, pl.ds(off - b, tb), :]


def _conv_recompute(i, a_ref, gt_ref, ap_ref, gp_ref, w_ref, b_ref, hbuf, shifts, tb):
    hbuf[pl.ds(HALO, tb), :] = a_ref[...] * _sigmoid(gt_ref[...])
    tail = ap_ref[pl.ds(tb - HALO, HALO), :] * _sigmoid(gp_ref[pl.ds(tb - HALO, HALO), :])
    hbuf[pl.ds(0, HALO), :] = jnp.where(i > 0, tail, 0.0)
    _fill_shifts(hbuf, shifts, tb)
    acc = jnp.broadcast_to(b_ref[...], a_ref.shape)
    for kk in range(CONV_KERNEL):
        acc = acc + w_ref[pl.ds(kk, 1), :] * _window(hbuf, shifts, HALO - (CONV_KERNEL - 1) + kk, tb)
    return acc


def _layer_norm_stats(c):
    mu = jnp.mean(c, axis=-1, keepdims=True)
    xc = c - mu
    r = lax.rsqrt(jnp.mean(xc * xc, axis=-1, keepdims=True) + EPS)
    return xc * r, r


def _conv_specs(s, cw, tb, a_blk):
    cur = lambda off: pl.BlockSpec((tb, cw), lambda i: (i, a_blk + off))
    prev = lambda off: pl.BlockSpec((tb, cw), lambda i: (jnp.maximum(i - 1, 0), a_blk + off))
    wspec = pl.BlockSpec((HALO, cw), lambda i: (0, 0))
    vec = pl.BlockSpec((1, cw), lambda i: (0, 0))
    row = pl.BlockSpec((tb, cw), lambda i: (i, 0))
    return cur, prev, wspec, vec, row


def _conv_fwd(name, proj, a_blk, w, b, lg, lb):
    s = proj.shape[0]
    cw = w.shape[1]
    tb = _tile(s, 256)
    cur, prev, wspec, vec, row = _conv_specs(s, cw, tb, a_blk)

    def body(a_ref, gt_ref, ap_ref, gp_ref, w_ref, b_ref, lg_ref, lb_ref, y_ref, hbuf, shifts):
        c = _conv_recompute(pl.program_id(0), a_ref, gt_ref, ap_ref, gp_ref, w_ref, b_ref, hbuf, shifts, tb)
        chat, _ = _layer_norm_stats(c)
        z = chat * lg_ref[...] + lb_ref[...]
        y_ref[...] = z * _sigmoid(z)

    return _ordered_call(
        body, name=name, grid=(s // tb,), in_specs=[cur(0), cur(1), prev(0), prev(1), wspec, vec, vec, vec],
        out_specs=row, out_shape=jax.ShapeDtypeStruct((s, cw), F32),
        scratch_shapes=[pltpu.VMEM((tb + HALO, cw), F32), pltpu.VMEM((SUBLANES - 1, tb + HALO, cw), F32)],
        compiler_params=_cp("arbitrary"),
    )(proj, proj, proj, proj, w, b, lg, lb)


def _conv_bwd1(name, proj, a_blk, w, b, lg, lb, dy):
    s = proj.shape[0]
    cw = w.shape[1]
    tb = _tile(s, 256)
    cur, prev, wspec, vec, row = _conv_specs(s, cw, tb, a_blk)

    def body(a_ref, gt_ref, ap_ref, gp_ref, w_ref, b_ref, lg_ref, lb_ref, dy_ref,
             dc_ref, dw_ref, db_ref, dlg_ref, dlb_ref, hbuf, shifts):
        i = pl.program_id(0)

        @pl.when(i == 0)
        def _():
            dw_ref[...] = jnp.zeros_like(dw_ref)
            db_ref[...] = jnp.zeros_like(db_ref)
            dlg_ref[...] = jnp.zeros_like(dlg_ref)
            dlb_ref[...] = jnp.zeros_like(dlb_ref)

        c = _conv_recompute(i, a_ref, gt_ref, ap_ref, gp_ref, w_ref, b_ref, hbuf, shifts, tb)
        chat, r = _layer_norm_stats(c)
        z = chat * lg_ref[...] + lb_ref[...]
        sg = _sigmoid(z)
        dz = dy_ref[...] * (sg + z * sg * (1.0 - sg))
        dlg_ref[...] += _colsum(dz * chat)
        dlb_ref[...] += _colsum(dz)
        dzg = dz * lg_ref[...]
        dc = r * (dzg - jnp.mean(dzg, axis=-1, keepdims=True) - chat * jnp.mean(dzg * chat, axis=-1, keepdims=True))
        dc_ref[...] = dc
        db_ref[...] += _colsum(dc)
        for kk in range(CONV_KERNEL):
            dw_ref[pl.ds(kk, 1), :] += _colsum(dc * _window(hbuf, shifts, HALO - (CONV_KERNEL - 1) + kk, tb))

    return _ordered_call(
        body, name=name, grid=(s // tb,), in_specs=[cur(0), cur(1), prev(0), prev(1), wspec, vec, vec, vec, row],
        out_specs=(row, wspec, vec, vec, vec),
        out_shape=(jax.ShapeDtypeStruct((s, cw), F32), jax.ShapeDtypeStruct((HALO, cw), F32),
                   jax.ShapeDtypeStruct((1, cw), F32), jax.ShapeDtypeStruct((1, cw), F32),
                   jax.ShapeDtypeStruct((1, cw), F32)),
        scratch_shapes=[pltpu.VMEM((tb + HALO, cw), F32), pltpu.VMEM((SUBLANES - 1, tb + HALO, cw), F32)],
        compiler_params=_cp("arbitrary"),
    )(proj, proj, proj, proj, w, b, lg, lb, dy)


def _conv_bwd2(name, proj, a_blk, w, dc):
    s = proj.shape[0]
    cw = w.shape[1]
    tb = _tile(s, 256)
    nblk = s // tb
    cur, _, wspec, _, row = _conv_specs(s, cw, tb, a_blk)
    nxt = pl.BlockSpec((tb, cw), lambda i: (jnp.minimum(i + 1, nblk - 1), 0))

    def body(a_ref, gt_ref, w_ref, dc_ref, dn_ref, o_ref, dbuf, shifts):
        i = pl.program_id(0)
        dbuf[pl.ds(0, tb), :] = dc_ref[...]
        dbuf[pl.ds(tb, HALO), :] = jnp.where(i < nblk - 1, dn_ref[pl.ds(0, HALO), :], 0.0)
        _fill_shifts(dbuf, shifts, tb)
        dh = jnp.zeros((tb, cw), F32)
        for kk in range(CONV_KERNEL):
            dh = dh + w_ref[pl.ds(kk, 1), :] * _window(dbuf, shifts, CONV_KERNEL - 1 - kk, tb)
        sg = _sigmoid(gt_ref[...])
        o_ref[:, 0:cw] = (dh * sg).astype(BF16)
        o_ref[:, cw:2 * cw] = (dh * a_ref[...] * sg * (1.0 - sg)).astype(BF16)

    return _ordered_call(
        body, name=name, grid=(nblk,), in_specs=[cur(0), cur(1), wspec, row, nxt],
        out_specs=pl.BlockSpec((tb, 2 * cw), lambda i: (i, 0)), out_shape=jax.ShapeDtypeStruct((s, 2 * cw), BF16),
        scratch_shapes=[pltpu.VMEM((tb + HALO, cw), F32), pltpu.VMEM((SUBLANES - 1, tb + HALO, cw), F32)],
        compiler_params=_cp("arbitrary"),
    )(proj, proj, w, dc, dc)


def _sgu_common(v_ref, lg_ref, lb_ref, w_ref, bexp_ref, sw):
    vhat, r = _layer_norm_stats(v_ref[...])
    vn_b = (vhat * lg_ref[...] + lb_ref[...]).astype(BF16)
    ii = lax.broadcasted_iota(jnp.int32, (WINDOW, WINDOW), 0)
    jj = lax.broadcasted_iota(jnp.int32, (WINDOW, WINDOW), 1)
    tril = jj <= ii
    head_of = lax.broadcasted_iota(jnp.int32, (WINDOW, sw), 1) // HEAD_DIM
    wts = [jnp.where(tril, w_ref[h], 0.0).astype(BF16) for h in range(sw // HEAD_DIM)]
    sv = bexp_ref[...]
    for h, wt in enumerate(wts):
        sv = sv + jnp.where(head_of == h, _dot(wt, vn_b), 0.0)
    return vhat, r, vn_b, tril, head_of, wts, sv


def _sgu_specs(sw, u_blk):
    nh = sw // HEAD_DIM
    u = pl.BlockSpec((WINDOW, sw), lambda n: (n, u_blk))
    v = pl.BlockSpec((WINDOW, sw), lambda n: (n, u_blk + 1))
    vec = pl.BlockSpec((1, sw), lambda n: (0, 0))
    wspec = pl.BlockSpec((nh, WINDOW, WINDOW), lambda n: (0, 0, 0))
    bspec = pl.BlockSpec((WINDOW, sw), lambda n: (0, 0))
    row = pl.BlockSpec((WINDOW, sw), lambda n: (n, 0))
    return u, v, vec, wspec, bspec, row


def _sgu_fwd(name, proj, u_blk, lg, lb, w, bexp):
    s = proj.shape[0]
    sw = lg.shape[1]
    u, v, vec, wspec, bspec, row = _sgu_specs(sw, u_blk)

    def body(u_ref, v_ref, lg_ref, lb_ref, w_ref, bexp_ref, y_ref):
        sv = _sgu_common(v_ref, lg_ref, lb_ref, w_ref, bexp_ref, sw)[-1]
        y_ref[...] = u_ref[...] * sv

    return _ordered_call(
        body, name=name, grid=(s // WINDOW,), in_specs=[u, v, vec, vec, wspec, bspec], out_specs=row,
        out_shape=jax.ShapeDtypeStruct((s, sw), F32), compiler_params=_cp("parallel"),
    )(proj, proj, lg, lb, w, bexp)


def _sgu_bwd(name, proj, u_blk, lg, lb, w, bexp, dy):
    s = proj.shape[0]
    sw = lg.shape[1]
    nh = sw // HEAD_DIM
    u, v, vec, wspec, bspec, row = _sgu_specs(sw, u_blk)
    dbspec = pl.BlockSpec((nh, WINDOW), lambda n: (0, 0))

    def body(u_ref, v_ref, lg_ref, lb_ref, w_ref, bexp_ref, dy_ref, o_ref, dw_ref, db_ref, dlg_ref, dlb_ref):
        n = pl.program_id(0)

        @pl.when(n == 0)
        def _():
            dw_ref[...] = jnp.zeros_like(dw_ref)
            db_ref[...] = jnp.zeros_like(db_ref)
            dlg_ref[...] = jnp.zeros_like(dlg_ref)
            dlb_ref[...] = jnp.zeros_like(dlb_ref)

        vhat, r, vn_b, tril, head_of, wts, sv = _sgu_common(v_ref, lg_ref, lb_ref, w_ref, bexp_ref, sw)
        dyv = dy_ref[...]
        o_ref[:, 0:sw] = (dyv * sv).astype(BF16)
        ds = dyv * u_ref[...]
        dvn = jnp.zeros((WINDOW, sw), F32)
        for h, wt in enumerate(wts):
            dsm_b = jnp.where(head_of == h, ds, 0.0).astype(BF16)
            dvn = dvn + _dot(wt, dsm_b, TN)
            dw_ref[h] += jnp.where(tril, _dot(dsm_b, vn_b, NT), 0.0)
        hmask = (lax.broadcasted_iota(jnp.int32, (nh, sw), 1) // HEAD_DIM
                 == lax.broadcasted_iota(jnp.int32, (nh, sw), 0)).astype(F32)
        db_ref[...] += lax.dot_general(hmask, ds, NT, precision=lax.Precision.HIGHEST, preferred_element_type=F32)
        dlg_ref[...] += _colsum(dvn * vhat)
        dlb_ref[...] += _colsum(dvn)
        dvg = dvn * lg_ref[...]
        dv = r * (dvg - jnp.mean(dvg, axis=-1, keepdims=True) - vhat * jnp.mean(dvg * vhat, axis=-1, keepdims=True))
        o_ref[:, sw:2 * sw] = dv.astype(BF16)

    return _ordered_call(
        body, name=name, grid=(s // WINDOW,), in_specs=[u, v, vec, vec, wspec, bspec, row],
        out_specs=(pl.BlockSpec((WINDOW, 2 * sw), lambda n: (n, 0)), wspec, dbspec, vec, vec),
        out_shape=(jax.ShapeDtypeStruct((s, 2 * sw), BF16), jax.ShapeDtypeStruct((nh, WINDOW, WINDOW), F32),
                   jax.ShapeDtypeStruct((nh, WINDOW), F32), jax.ShapeDtypeStruct((1, sw), F32),
                   jax.ShapeDtypeStruct((1, sw), F32)),
        compiler_params=_cp("arbitrary"),
    )(proj, proj, lg, lb, w, bexp, dy)


def _adamw_many(name, ws, gs, ms, vs):
    n = len(ws)

    def body(*refs):
        for t in range(n):
            w_ref, g_ref, m_ref, v_ref = (refs[k * n + t] for k in range(4))
            d_ref, nm_ref, nv_ref = (refs[(4 + k) * n + t] for k in range(3))
            gv = g_ref[...]
            mv = ADAM_B1 * m_ref[...] + (1.0 - ADAM_B1) * gv
            vv = ADAM_B2 * v_ref[...] + (1.0 - ADAM_B2) * (gv * gv)
            m_hat = mv / (1.0 - ADAM_B1 ** ADAM_STEP)
            v_hat = vv / (1.0 - ADAM_B2 ** ADAM_STEP)
            d_ref[...] = -ADAM_LR * (m_hat / (jnp.sqrt(v_hat) + ADAM_EPS) + ADAM_WD * w_ref[...])
            nm_ref[...] = mv
            nv_ref[...] = vv

    whole = pl.BlockSpec(memory_space=pltpu.VMEM)
    out = _ordered_call(
        body, name=name, in_specs=[whole] * (4 * n), out_specs=(whole,) * (3 * n),
        out_shape=tuple(jax.ShapeDtypeStruct(w.shape, F32) for w in ws) * 3,
        compiler_params=pltpu.CompilerParams(vmem_limit_bytes=VMEM_LIMIT),
    )(*ws, *gs, *ms, *vs)
    return out[:n], out[n:2 * n], out[2 * n:]


def _route():
    x, y, c = lax.axis_index("x"), lax.axis_index("y"), lax.axis_index("c")
    n1 = (jnp.where(c == 0, 1 - x, x), jnp.where(c == 0, y, 1 - y))
    n2 = (jnp.where(c == 0, x, 1 - x), jnp.where(c == 0, 1 - y, y))
    return x, y, c, n1, n2, (1 - x, 1 - y)


def _cidx(chip):
    return 2 * chip[0] + chip[1]


def _remote(src, dst, sems, k, device):
    send_sems, recv_sems = sems
    return pltpu.make_async_remote_copy(src_ref=src, dst_ref=dst, send_sem=send_sems.at[k], recv_sem=recv_sems.at[k],
                                        device_id=device, device_id_type=MESH)


def _exchange(name, bufs, n_sems, build):
    n = len(bufs)

    def body(*refs):
        cps = build(refs[n:2 * n], (refs[2 * n], refs[2 * n + 1]))
        for cp in cps:
            cp.start()
        for cp in cps:
            cp.wait()

    return _ordered_call(
        body, name=name, in_specs=[ANY] * n, out_specs=tuple(ANY for _ in range(n)),
        out_shape=tuple(jax.ShapeDtypeStruct(b.shape, b.dtype) for b in bufs),
        input_output_aliases={i: i for i in range(n)},
        scratch_shapes=[pltpu.SemaphoreType.DMA((n_sems,)), pltpu.SemaphoreType.DMA((n_sems,))],
        compiler_params=pltpu.CompilerParams(has_side_effects=True),
    )(*bufs)


HBM_SPEC = pl.BlockSpec(memory_space=pltpu.HBM)
SEM_SPEC = pl.BlockSpec(memory_space=pltpu.SEMAPHORE)
DATAFLOW = pltpu.SideEffectType.DATAFLOW_SIDE_EFFECTING


def _exchange_start(name, bufs, n_sems, build):
    n = len(bufs)

    def body(*refs):
        for cp in build(refs[:n], (refs[n], refs[n + 1])):
            cp.start()

    out = _ordered_call(
        body, name=name,
        out_shape=(pltpu.SemaphoreType.DMA((n_sems,)), pltpu.SemaphoreType.DMA((n_sems,)))
        + tuple(pltpu.HBM(b.shape, b.dtype) for b in bufs),
        in_specs=[HBM_SPEC] * n, out_specs=(SEM_SPEC, SEM_SPEC) + (HBM_SPEC,) * n,
        input_output_aliases={i: 2 + i for i in range(n)},
        compiler_params=pltpu.CompilerParams(has_side_effects=DATAFLOW),
    )(*[pltpu.with_memory_space_constraint(b, pltpu.HBM) for b in bufs])
    return dict(name=name, send=out[0], recv=out[1], bufs=list(out[2:2 + n]), build=build)


def _exchange_wait(handle):
    n = len(handle["bufs"])

    def body(*refs):
        for cp in handle["build"](refs[:n], (refs[n], refs[n + 1])):
            cp.wait_send()
            cp.wait_recv()

    return list(_ordered_call(
        body, name=handle["name"] + "_wait", out_shape=tuple(pltpu.HBM(b.shape, b.dtype) for b in handle["bufs"]),
        in_specs=[HBM_SPEC] * n + [SEM_SPEC, SEM_SPEC], out_specs=(HBM_SPEC,) * n,
        input_output_aliases={i: i for i in range(n)},
        compiler_params=pltpu.CompilerParams(has_side_effects=DATAFLOW),
    )(*handle["bufs"], handle["send"], handle["recv"]))


def _cast_place(name, w, l, me_idx, dtype):
    _, r, c = w.shape
    tr = _tile(r, 512)

    def body(me_ref, w_ref, o_ref):
        o_ref[...] = w_ref[...].astype(dtype)

    grid_spec = pltpu.PrefetchScalarGridSpec(
        num_scalar_prefetch=1, grid=(r // tr,),
        in_specs=[pl.BlockSpec((None, tr, c), lambda i, me_ref: (l, i, 0))],
        out_specs=pl.BlockSpec((None, tr, c), lambda i, me_ref: (me_ref[0], i, 0)))
    return _ordered_call(
        body, name=name, grid_spec=grid_spec, out_shape=jax.ShapeDtypeStruct((N_CHIPS, r, c), dtype),
        compiler_params=_cp("arbitrary"),
    )(me_idx, w)


def _my_half(ref, blk, c):
    hr = ref.shape[1] // 2
    return ref.at[blk, pl.ds(c * hr, hr), :]


def _gather_step(entering):
    lens = [len(e) for e in entering]
    flat = [b for e in entering for b in e]

    def build(refs, sems):
        x, y, c, n1, n2, dg = _route()
        me = _cidx((x, y))
        plan = ([(r, (me,), (*n1, c)) for r in refs[:lens[0]]]
                + [(r, (me, _cidx(n1)), (*n2, c)) for r in refs[lens[0]:lens[0] + lens[1]]]
                + [(r, (_cidx(n1), _cidx(n2), _cidx(dg)), (x, y, 1 - c)) for r in refs[lens[0] + lens[1]:]])
        cps = []
        for ref, blocks, peer in plan:
            for blk in blocks:
                cps.append(_remote(_my_half(ref, blk, c), _my_half(ref, blk, c), sems, len(cps), peer))
        return cps

    return flat, lens[0] + 2 * lens[1] + 3 * lens[2], build


RI_C, RI_ME, RI_N2, RI_N1 = 0, 1, 2, 3


def _pair_sum(name, g, sib, route_idx):
    _, rows, cols = g.shape
    hr = rows // 2
    tr = _tile(hr, 512)
    per = hr // tr

    def body(ri, g_ref, s_ref, o_ref):
        o_ref[...] = (g_ref[...].astype(F32) + s_ref[...].astype(F32)).astype(BF16)

    blk = (None, tr, cols)
    grid_spec = pltpu.PrefetchScalarGridSpec(
        num_scalar_prefetch=1, grid=(2, per),
        in_specs=[pl.BlockSpec(blk, lambda j, i, ri: (ri[RI_N1 + j], ri[RI_C] * per + i, 0)),
                  pl.BlockSpec(blk, lambda j, i, ri: (ri[RI_N1 + j], i, 0))],
        out_specs=pl.BlockSpec(blk, lambda j, i, ri: (j, i, 0)))
    return _ordered_call(
        body, name=name, grid_spec=grid_spec, out_shape=jax.ShapeDtypeStruct((2, hr, cols), BF16),
        compiler_params=_cp("parallel", "parallel"),
    )(route_idx, g, sib)


def _sum_stage1(name, g, sib, got, route_idx):
    _, hr, cols = sib.shape
    tr = _tile(hr, 512)
    per = hr // tr

    def body(ri, gm_ref, sm_ref, gn_ref, sn_ref, g0_ref, g1_ref, keep_ref, send_ref):
        keep_ref[...] = (gm_ref[...].astype(F32) + sm_ref[...].astype(F32)) + g0_ref[...].astype(F32)
        send_ref[...] = ((gn_ref[...].astype(F32) + sn_ref[...].astype(F32)) + g1_ref[...].astype(F32)).astype(BF16)

    blk = (None, tr, cols)
    row = pl.BlockSpec((tr, cols), lambda i, ri: (i, 0))

    def mine(which):
        return pl.BlockSpec(blk, lambda i, ri: (ri[which], ri[RI_C] * per + i, 0))

    def theirs(which):
        return pl.BlockSpec(blk, lambda i, ri: (ri[which], i, 0))

    grid_spec = pltpu.PrefetchScalarGridSpec(
        num_scalar_prefetch=1, grid=(per,),
        in_specs=[mine(RI_ME), theirs(RI_ME), mine(RI_N2), theirs(RI_N2),
                  pl.BlockSpec(blk, lambda i, ri: (0, i, 0)), pl.BlockSpec(blk, lambda i, ri: (1, i, 0))],
        out_specs=(row, row))
    return _ordered_call(
        body, name=name, grid_spec=grid_spec,
        out_shape=(jax.ShapeDtypeStruct((hr, cols), F32), jax.ShapeDtypeStruct((hr, cols), BF16)),
        compiler_params=_cp("parallel"),
    )(route_idx, g, sib, g, sib, got, got)


def _sum_stage2(name, keep, got):
    hr, cols = keep.shape
    tr = _tile(hr, 512)

    def body(k_ref, g_ref, o_ref):
        o_ref[...] = k_ref[...] + g_ref[...].astype(F32)

    row = pl.BlockSpec((tr, cols), lambda i: (i, 0))
    return _ordered_call(
        body, name=name, grid=(hr // tr,), in_specs=[row, row], out_specs=row,
        out_shape=jax.ShapeDtypeStruct((hr, cols), F32), compiler_params=_cp("parallel"),
    )(keep, got)


def _reduce_scatter(tag, names, grads, route_idx):
    n = len(grads)
    hrs = [g.shape[1] // 2 for g in grads]

    def empty(t, lead, dtype):
        return lax.empty(lead + (hrs[t], grads[t].shape[2]), dtype)

    def pair_stage(refs, sems):
        x, y, c, n1, n2, dg = _route()
        return [_remote(refs[t].at[:, pl.ds((1 - c) * hrs[t], hrs[t]), :], refs[n + t], sems, t, (x, y, 1 - c))
                for t in range(n)]

    def stage1(refs, sems):
        x, y, c, n1, n2, dg = _route()
        return [_remote(refs[t].at[slot], refs[n + t].at[slot], sems, 2 * t + slot, (*n1, c))
                for t in range(n) for slot in range(2)]

    def stage2(refs, sems):
        x, y, c, n1, n2, dg = _route()
        return [_remote(refs[t], refs[n + t], sems, t, (*n2, c)) for t in range(n)]

    def stage3(refs, sems):
        x, y, c, n1, n2, dg = _route()
        return [_remote(refs[t], refs[n + t], sems, t, (x, y, 1 - c)) for t in range(n)]

    state = {}

    def start():
        state["h"] = _exchange_start(f"rs_pair_{tag}", list(grads) + [empty(t, (N_CHIPS,), BF16) for t in range(n)],
                                     n, pair_stage)

    def pair_done():
        state["pair"] = _exchange_wait(state["h"])
        psum = [_pair_sum(f"rs_psum_{names[t]}", state["pair"][t], state["pair"][n + t], route_idx) for t in range(n)]
        state["h"] = _exchange_start(f"rs_x1_{tag}", psum + [empty(t, (2,), BF16) for t in range(n)], 2 * n, stage1)

    def x1_done():
        out = _exchange_wait(state["h"])
        state["keep"], send = zip(*[_sum_stage1(f"rs_sum1_{names[t]}", state["pair"][t], state["pair"][n + t],
                                                out[n + t], route_idx) for t in range(n)])
        state["h"] = _exchange_start(f"rs_x2_{tag}", list(send) + [empty(t, (), BF16) for t in range(n)], n, stage2)

    def x2_done():
        out = _exchange_wait(state["h"])
        mine = [_sum_stage2(f"rs_sum2_{names[t]}", state["keep"][t], out[n + t]) for t in range(n)]
        state["h"] = _exchange_start(f"rs_half_{tag}", mine + [empty(t, (), F32) for t in range(n)], n, stage3)

    def finish():
        out = _exchange_wait(state["h"])
        return list(zip(out[:n], out[n:]))

    return start, pair_done, x1_done, x2_done, finish


def _adamw_big(name, w, m, v, f, h, l, c_idx, prev):
    n_l, r, cols = w.shape
    hr = r // 2
    tr = _tile(hr, 256)
    per = hr // tr

    def body(c_ref, w_ref, m_ref, v_ref, f_ref, h_ref, *rest):
        g_ref, d_ref, nm_ref, nv_ref = rest[-4:]
        gv = jnp.where(pl.program_id(0) == c_ref[0], f_ref[...], h_ref[...])
        mv = ADAM_B1 * m_ref[...] + (1.0 - ADAM_B1) * gv
        vv = ADAM_B2 * v_ref[...] + (1.0 - ADAM_B2) * (gv * gv)
        m_hat = mv / (1.0 - ADAM_B1 ** ADAM_STEP)
        v_hat = vv / (1.0 - ADAM_B2 ** ADAM_STEP)
        g_ref[...] = gv
        d_ref[...] = -ADAM_LR * (m_hat / (jnp.sqrt(v_hat) + ADAM_EPS) + ADAM_WD * w_ref[...])
        nm_ref[...] = mv
        nv_ref[...] = vv

    big = pl.BlockSpec((None, tr, cols), lambda hf, i, c_ref: (l, hf * per + i, 0))
    fspec = pl.BlockSpec((tr, cols), lambda hf, i, c_ref: (jnp.where(hf == c_ref[0], i, 0), 0))
    hspec = pl.BlockSpec((tr, cols), lambda hf, i, c_ref: (jnp.where(hf == c_ref[0], 0, i), 0))
    grid_spec = pltpu.PrefetchScalarGridSpec(
        num_scalar_prefetch=1, grid=(2, per), in_specs=[big] * 3 + [fspec, hspec] + [ANY] * len(prev),
        out_specs=(big,) * 4)
    return _ordered_call(
        body, name=name, grid_spec=grid_spec, out_shape=(jax.ShapeDtypeStruct(w.shape, F32),) * 4,
        input_output_aliases={6 + k: k for k in range(len(prev))}, compiler_params=_cp("arbitrary", "arbitrary"),
    )(c_idx, w, m, v, f, h, *prev)


def _small_allreduce(buf):
    rows = buf.shape[0]
    hr = rows // 2

    def body(in_ref, out_ref, pair, acc, got1, got2, send_sems, recv_sems):
        x, y, c, n1, n2, dg = _route()
        sems = (send_sems, recv_sems)
        sibling = (x, y, 1 - c)
        mine = pl.ds(pl.multiple_of(c * hr, 8), hr)
        pair[c] = in_ref[...]
        cp = _remote(in_ref, pair.at[c], sems, 0, sibling)
        cp.start()
        cp.wait()
        acc[...] = pair[0, mine, :] + pair[1, mine, :]
        cp = _remote(acc, got1, sems, 1, (*n1, c))
        cp.start()
        cp.wait()
        acc[...] = acc[...] + got1[...]
        cp = _remote(acc, got2, sems, 2, (*n2, c))
        cp.start()
        cp.wait()
        out_ref[mine, :] = acc[...] + got2[...]
        cp = _remote(out_ref.at[mine, :], out_ref.at[mine, :], sems, 3, sibling)
        cp.start()
        cp.wait()

    half = pltpu.VMEM((hr, 128), F32)
    return _ordered_call(
        body, name="small_allreduce", in_specs=[pl.BlockSpec(memory_space=pltpu.VMEM)],
        out_specs=pl.BlockSpec(memory_space=pltpu.VMEM), out_shape=jax.ShapeDtypeStruct((rows, 128), F32),
        scratch_shapes=[pltpu.VMEM((2, rows, 128), F32), half, half, half,
                        pltpu.SemaphoreType.DMA((4,)), pltpu.SemaphoreType.DMA((4,))],
        compiler_params=pltpu.CompilerParams(has_side_effects=True, vmem_limit_bytes=VMEM_LIMIT),
    )(buf)


BIG = ("w_in", "w_out", "w_up", "w_down")
COL_SHARDED = {"w_in": True, "w_out": False, "w_up": True, "w_down": False}
SMALL = ("ln1_g", "q_norm_g", "k_norm_g", "sinks", "conv_w", "conv_b", "conv_ln_g", "conv_ln_b", "sgu_ln_g",
         "sgu_ln_b", "sgu_w", "sgu_b", "out_norm_g", "ln2_g")
WEIGHTS = ("ln1_g", "w_in", "q_norm_g", "k_norm_g", "sinks", "conv_w", "conv_b", "conv_ln_g", "conv_ln_b",
           "sgu_ln_g", "sgu_ln_b", "sgu_w", "sgu_b", "out_norm_g", "w_out", "ln2_g", "w_up", "w_down")
PACK_QUANTUM = 8 * 128
PACK_ROWS = 512


def _pack(arrs):
    parts = []
    for a in arrs:
        f = a.reshape(-1)
        parts.append(jnp.pad(f, (0, -f.shape[0] % PACK_QUANTUM)).reshape(-1, 128))
    rows = sum(p.shape[0] for p in parts)
    parts.append(jnp.zeros((-rows % PACK_ROWS, 128), F32))
    return jnp.concatenate(parts, axis=0)


def _unpack(buf, shapes):
    out, off = [], 0
    for shp in shapes:
        n = 1
        for dd in shp:
            n *= dd
        rows = (n + PACK_QUANTUM - 1) // PACK_QUANTUM * 8
        out.append(buf[off:off + rows].reshape(-1)[:n].reshape(shp))
        off += rows
    return out


def _to_heads(t, nh):
    return t.reshape(t.shape[0], nh, HEAD_DIM).transpose(1, 0, 2)


def _from_heads(t):
    return t.transpose(1, 0, 2).reshape(t.shape[1], t.shape[0] * HEAD_DIM)


def _no_hook(point, carry):
    return carry


def _layer_fwd(l, x, p, wg, hook=_no_hook):
    d = x.shape[1]
    aw, cw = d // 2, d // 4
    nq = aw // HEAD_DIM
    nkv = nq // GQA
    kvw = nkv * HEAD_DIM
    x = hook("fwd_start", x)
    h1 = _rms_fwd(f"ln1_fwd_{l}", x, p["ln1_g"])
    proj = _mm_act_w(f"proj_{l}", h1, wg["w_in"], True, _ep_store)[0]
    proj = hook("fwd_proj", proj)
    q = _to_heads(proj[:, :aw], nq)
    k = _to_heads(proj[:, aw:aw + kvw], nkv)
    v = _to_heads(proj[:, aw + kvw:aw + 2 * kvw], nkv)
    sinks_b = jnp.broadcast_to(p["sinks"][:, None, None], (nq, 1, 128))
    ya = _from_heads(_attn_fwd(f"attn_fwd_{l}", q, k, v, p["q_norm_g"], p["k_norm_g"], sinks_b))
    ya = hook("fwd_attn", ya)
    yc = _conv_fwd(f"conv_fwd_{l}", proj, 3, p["conv_w"], p["conv_b"], p["conv_ln_g"], p["conv_ln_b"])
    ys = _sgu_fwd(f"sgu_fwd_{l}", proj, 5, p["sgu_ln_g"], p["sgu_ln_b"], p["sgu_w"], p["sgu_bexp"])
    mix = _mixnorm_fwd(f"mixnorm_fwd_{l}", ya, yc, ys, p["out_norm_g"])
    mix = hook("fwd_mid", mix)
    if _whole_rows(wg["w_out"]):
        xm, h2 = _mm_act_w(f"out_{l}", mix, wg["w_out"], False, _ep_residual_norm, extra=(x,),
                           out_dtypes=(F32, BF16), row_vectors=(p["ln2_g"].reshape(1, d),))
    else:
        xm = _mm_act_w(f"out_{l}", mix, wg["w_out"], False, _ep_residual, extra=(x,))[0]
        h2 = _rms_fwd(f"ln2_fwd_{l}", xm, p["ln2_g"])
    h2 = hook("fwd_ln2", h2)
    up_b, act_b = _mm_act_w(f"up_{l}", h2, wg["w_up"], True, _ep_up, out_dtypes=(BF16, BF16))
    act_b = hook("fwd_up", act_b)
    xo = _mm_act_w(f"down_{l}", act_b, wg["w_down"], False, _ep_residual, extra=(xm,))[0]
    xo = hook("fwd_end", xo)
    saved = dict(x=x, h1=h1, proj=proj, q=q, k=k, v=v, sinks_b=sinks_b, ya=ya, yc=yc, ys=ys, mix=mix, xm=xm, h2=h2,
                 up_b=up_b, act_b=act_b)
    return xo, saved


def _layer_bwd(l, dxo, dxo_b, p, wg, sv, big, hook=_no_hook):
    d = dxo.shape[1]
    nq = (d // 2) // HEAD_DIM
    small = {}
    dxo_b = hook("bwd_start", dxo_b)
    big["w_down"] = _mm_wgrad(f"dw_down_{l}", sv["act_b"], dxo_b, False, d)
    dup_b = _mm_act_wt(f"dup_{l}", dxo_b, wg["w_down"], False, _ep_dup, extra=(sv["up_b"],), out_dtypes=(BF16,))[0]
    dup_b = hook("bwd_dup", dup_b)
    big["w_up"] = _mm_wgrad(f"dw_up_{l}", sv["h2"], dup_b, True, wg["w_up"].shape[2])
    dh2 = _mm_act_wt(f"dh2_{l}", dup_b, wg["w_up"], True, _ep_store)[0]
    dh2 = hook("bwd_dh2", dh2)
    dxm, dxm_b, small["ln2_g"] = _rms_bwd(f"ln2_bwd_{l}", dh2, sv["xm"], p["ln2_g"], dxo)
    big["w_out"] = _mm_wgrad(f"dw_out_{l}", sv["mix"], dxm_b, False, d)
    dmix = _mm_act_wt(f"dmix_{l}", dxm_b, wg["w_out"], False, _ep_store)[0]
    dya, dyc, dys, small["out_norm_g"] = _mixnorm_bwd(f"mixnorm_bwd_{l}", dmix, sv["ya"], sv["yc"], sv["ys"],
                                                      p["out_norm_g"])
    dya = hook("bwd_mix", dya)
    dq, dkc, dkp, dvc, dvp, small["q_norm_g"], dsink = _attn_bwd(
        f"attn_bwd_{l}", sv["q"], sv["k"], sv["v"], p["q_norm_g"], p["k_norm_g"], sv["sinks_b"], _to_heads(dya, nq))
    dkc = hook("bwd_attn", dkc)
    small["sinks"] = dsink[:, 0, 0]
    dk, dv, small["k_norm_g"] = _attn_bwd_kv(f"attn_bwd_kv_{l}", sv["k"], p["k_norm_g"], dkc, dkp, dvc, dvp)
    dc, dcw, small["conv_b"], small["conv_ln_g"], small["conv_ln_b"] = _conv_bwd1(
        f"conv_bwd1_{l}", sv["proj"], 3, p["conv_w"], p["conv_b"], p["conv_ln_g"], p["conv_ln_b"], dyc)
    small["conv_w"] = dcw[:CONV_KERNEL]
    dxc_b = _conv_bwd2(f"conv_bwd2_{l}", sv["proj"], 3, p["conv_w"], dc)
    dxs_b, small["sgu_w"], small["sgu_b"], small["sgu_ln_g"], small["sgu_ln_b"] = _sgu_bwd(
        f"sgu_bwd_{l}", sv["proj"], 5, p["sgu_ln_g"], p["sgu_ln_b"], p["sgu_w"], p["sgu_bexp"], dys)
    dxs_b = hook("bwd_sgu", dxs_b)
    dproj_b = jnp.concatenate([_from_heads(dq), _from_heads(dk), _from_heads(dv), dxc_b, dxs_b], axis=1)
    big["w_in"] = _mm_wgrad(f"dw_in_{l}", sv["h1"], dproj_b, True, wg["w_in"].shape[2])
    dh1 = _mm_act_wt(f"dh1_{l}", dproj_b, wg["w_in"], True, _ep_store)[0]
    dx, dx_b, small["ln1_g"] = _rms_bwd(f"ln1_bwd_{l}", dh1, sv["x"], p["ln1_g"], dxm)
    dx_b = hook("bwd_end", dx_b)
    return dx, dx_b, small


def kernel(x, ln1_g, w_in, q_norm_g, k_norm_g, sinks, conv_w, conv_b, conv_ln_g, conv_ln_b, sgu_ln_g, sgu_ln_b, sgu_w, sgu_b, out_norm_g, w_out, ln2_g, w_up, w_down, loss_target, m_ln1_g, m_w_in, m_q_norm_g, m_k_norm_g, m_sinks, m_conv_w, m_conv_b, m_conv_ln_g, m_conv_ln_b, m_sgu_ln_g, m_sgu_ln_b, m_sgu_w, m_sgu_b, m_out_norm_g, m_w_out, m_ln2_g, m_w_up, m_w_down, v_ln1_g, v_w_in, v_q_norm_g, v_k_norm_g, v_sinks, v_conv_w, v_conv_b, v_conv_ln_g, v_conv_ln_b, v_sgu_ln_g, v_sgu_ln_b, v_sgu_w, v_sgu_b, v_out_norm_g, v_w_out, v_ln2_g, v_w_up, v_w_down):
    given = dict(locals())
    _LAST[0] = None
    n_layers = ln1_g.shape[0]
    s, d = x.shape[1], x.shape[2]
    cw = d // 4
    xi, yi, core = lax.axis_index("x"), lax.axis_index("y"), lax.axis_index("c")
    chip = 2 * xi + yi
    first_partner = jnp.where(core == 0, 2 * (1 - xi) + yi, 2 * xi + (1 - yi))
    second_partner = jnp.where(core == 0, 2 * xi + (1 - yi), 2 * (1 - xi) + yi)
    route_idx = jnp.stack([core, chip, second_partner, first_partner, 3 - chip]).astype(jnp.int32)

    conv_w_pad = jnp.pad(conv_w, ((0, 0), (0, HALO - CONV_KERNEL), (0, 0))).reshape(1, n_layers * HALO, -1)
    cwl = conv_w_pad.shape[2]
    buf = {}

    def place(key):
        if key == "conv_w":
            buf[key] = _cast_place("place_conv_w", conv_w_pad, 0, route_idx[1:2], F32)
        else:
            buf[key] = _cast_place(f"place_{key[0]}_{key[1]}", given[key[0]], key[1], route_idx[1:2], BF16)

    groups = [["conv_w", ("w_in", 0)]] + [[(nm, l)] for l in range(n_layers) for nm in BIG if (nm, l) != ("w_in", 0)]
    n_steps = len(groups) + 2
    pending = {}

    def start_step(st):
        keys = [groups[st - j] if 0 <= st - j < len(groups) else [] for j in range(3)]
        flat, n_sems, build = _gather_step([[buf[k] for k in ks] for ks in keys])
        pending["keys"] = [k for ks in keys for k in ks]
        pending["h"] = _exchange_start(f"gather_step{st}", flat, n_sems, build)

    def wait_step():
        for k, b in zip(pending["keys"], _exchange_wait(pending["h"])):
            buf[k] = b

    later = [k for grp in groups[1:] for k in grp]
    for k in groups[0]:
        place(k)
    for st, upto in enumerate((2, 5, len(later))):
        start_step(st)
        for k in later[:upto]:
            if k not in buf:
                place(k)
        wait_step()
    conv_w_full = buf["conv_w"].reshape(N_CHIPS, n_layers, HALO, cwl).transpose(1, 2, 0, 3).reshape(
        n_layers, HALO, cw)

    class LayerWeights:
        def __init__(self, l):
            self.l = l

        def __getitem__(self, nm):
            return buf[(nm, self.l)]

    wgs = [LayerWeights(l) for l in range(n_layers)]
    fwd_points = [(l, pt) for l in range(n_layers) for pt in ("fwd_start", "fwd_attn", "fwd_ln2", "fwd_up", "fwd_end")
                  if (pt != "fwd_start" or l == 0) and (pt != "fwd_end" or l + 1 < n_layers)]
    assert len(fwd_points) == n_steps - 3 + 1, "one hook point per pipeline step, and one to wait for the last"
    fwd_tables = [{} for _ in range(n_layers)]
    for i, (l, pt) in enumerate(fwd_points):
        if i > 0:
            fwd_tables[l].setdefault(pt, []).append(wait_step)
        if 3 + i < n_steps:
            fwd_tables[l].setdefault(pt, []).append(functools.partial(start_step, 3 + i))
    params = []
    for l in range(n_layers):
        p = {nm: given[nm][l] for nm in SMALL if nm != "conv_w"}
        for nm in ("conv_b", "conv_ln_g", "conv_ln_b", "sgu_ln_g", "sgu_ln_b"):
            p[nm] = p[nm].reshape(1, -1)
        p["conv_w"] = conv_w_full[l]
        p["sgu_bexp"] = jnp.repeat(sgu_b[l].T, HEAD_DIM, axis=1)
        params.append(p)

    def make_hook(table):
        def hook(point, carry):
            for fn in table.get(point, ()):
                fn()
            return carry
        return hook

    h = x.reshape(s, d)
    saved = []
    for l in range(n_layers):
        h, sv = _layer_fwd(l, h, params[l], wgs[l], make_hook(fwd_tables[l]))
        saved.append(sv)
    dh, dh_b, loss_part = _loss_head(h, loss_target.reshape(s, d))
    loss = lax.psum(loss_part[0, 0], ("x", "y", "c"))

    big_grads = [{} for _ in range(n_layers)]
    small_grads = [None] * n_layers
    halves = {}

    def rs_group(tag, l, names):
        phases = {}

        def start():
            phases["p"] = _reduce_scatter(tag, [f"{nm}_{l}" for nm in names], [big_grads[l][nm] for nm in names],
                                          route_idx)
            phases["p"][0]()

        def step(k):
            return lambda: phases["p"][k]()

        def finish():
            for nm, fh in zip(names, phases["p"][4]()):
                halves[(nm, l)] = fh

        return [start, step(1), step(2), step(3), finish]

    early = rs_group("l0a", 0, ("w_down", "w_up", "w_out"))
    for l in reversed(range(n_layers)):
        table = {}
        if l + 1 < n_layers:
            above = rs_group(f"l{l + 1}", l + 1, BIG)
            for point, fn in zip(("bwd_start", "bwd_dup", "bwd_mix", "bwd_attn", "bwd_sgu"), above):
                table.setdefault(point, []).append(fn)
        if l == 0:
            for point, fn in zip(("bwd_mix", "bwd_attn", "bwd_end"), early[:3]):
                table.setdefault(point, []).append(fn)
        dh, dh_b, small_grads[l] = _layer_bwd(l, dh, dh_b, params[l], wgs[l], saved[l], big_grads[l],
                                              make_hook(table))
    grad_x = dh.reshape(x.shape)

    grads, delta, new_m, new_v = {}, {}, {}, {}
    adam_state = {nm: () for nm in BIG}

    def adam(nm, l):
        f, h = halves[(nm, l)]
        adam_state[nm] = _adamw_big(f"adamw_{nm}_{l}", given[nm], given["m_" + nm], given["v_" + nm], f, h, l,
                                    route_idx[0:1], adam_state[nm])

    def small_update():
        small_shapes = [(n_layers,) + small_grads[0][nm].shape for nm in SMALL]
        small_sum = _small_allreduce(_pack([jnp.stack([small_grads[l][nm] for l in range(n_layers)])
                                            for nm in SMALL]))
        for nm, g in zip(SMALL, _unpack(small_sum, small_shapes)):
            grads[nm] = g.reshape((n_layers,) + given[nm].shape[1:]) if nm != "conv_w" else g
        grads["conv_w"] = lax.dynamic_slice_in_dim(grads["conv_w"], chip * cwl, cwl, axis=2)
        results = _adamw_many("adamw_small", [given[nm] for nm in SMALL], [grads[nm] for nm in SMALL],
                              [given["m_" + nm] for nm in SMALL], [given["v_" + nm] for nm in SMALL])
        for dst, arrs in zip((delta, new_m, new_v), results):
            dst.update(zip(SMALL, arrs))

    upper = [(nm, l) for l in reversed(range(1, n_layers)) for nm in reversed(BIG)]
    late = rs_group("l0b", 0, ("w_in",))
    late[0]()
    for task in upper[:1]:
        adam(*task)
    late[1]()
    for task in upper[1:]:
        adam(*task)
    early[3]()
    late[2]()
    small_update()
    early[4]()
    for nm in ("w_down", "w_up"):
        adam(nm, 0)
    late[3]()
    adam("w_out", 0)
    late[4]()
    adam("w_in", 0)
    for nm in BIG:
        grads[nm], delta[nm], new_m[nm], new_v[nm] = adam_state[nm]
    return (loss, grad_x, *[grads[nm] for nm in WEIGHTS], *[delta[nm] for nm in WEIGHTS],
            *[new_m[nm] for nm in WEIGHTS], *[new_v[nm] for nm in WEIGHTS])
```

```python
import functools

import jax
import jax.numpy as jnp
from jax import lax
from jax.experimental import pallas as pl
from jax.experimental.pallas import tpu as pltpu

F32 = jnp.float32
BF16 = jnp.bfloat16
EPS = 1e-6
NEG_INF = -1e30
HEAD_DIM = 64
WINDOW = 128
CONV_KERNEL = 31
HALO = 32
GQA = 4
N_CHIPS = 4
ADAM_LR, ADAM_B1, ADAM_B2, ADAM_EPS, ADAM_WD, ADAM_STEP = 0.001, 0.9, 0.999, 1e-08, 0.01, 10
VMEM_LIMIT = 56 * 1024 * 1024
TILE_K = 2048
MESH = pl.DeviceIdType.MESH
ANY = pl.BlockSpec(memory_space=pl.ANY)

NN = (((1,), (0,)), ((), ()))
NT = (((1,), (1,)), ((), ()))
TN = (((0,), (0,)), ((), ()))


def _cp(*sem):
    return pltpu.CompilerParams(dimension_semantics=sem, vmem_limit_bytes=VMEM_LIMIT)


_LAST = [None]
TOKEN = jax.ShapeDtypeStruct((8, 128), F32)


def _ordered_call(body, *, out_shape, out_specs=None, in_specs=None, grid_spec=None, grid=None, **kw):
    single = not isinstance(out_shape, (tuple, list))
    shapes = (out_shape,) if single else tuple(out_shape)

    def run(*operands):
        dep = _LAST[0]
        n = len(operands)
        n_dep = 0 if dep is None else 1

        def fn(*refs):
            outs = refs[n + n_dep:n + n_dep + len(shapes)]
            token = refs[n + n_dep + len(shapes)]
            body(*refs[:n], *outs, *refs[n + n_dep + len(shapes) + 1:])
            token[...] = jnp.zeros_like(token)

        specs_in = list(grid_spec.in_specs if grid_spec is not None else in_specs) + [ANY] * n_dep
        specs_out = grid_spec.out_specs if grid_spec is not None else out_specs
        specs_out = tuple(specs_out) if isinstance(specs_out, (tuple, list)) else (specs_out,)
        if grid_spec is not None or grid:
            specs_out += (pl.BlockSpec(TOKEN.shape, lambda *_: (0, 0)),)
        else:
            specs_out += (pl.BlockSpec(memory_space=pltpu.VMEM),)
        args = operands + ((dep,) if n_dep else ())
        if grid_spec is not None:
            spec = pltpu.PrefetchScalarGridSpec(num_scalar_prefetch=grid_spec.num_scalar_prefetch, grid=grid_spec.grid,
                                                in_specs=specs_in, out_specs=specs_out)
            out = pl.pallas_call(fn, grid_spec=spec, out_shape=shapes + (TOKEN,), **kw)(*args)
        else:
            if grid:
                kw["grid"] = grid
            out = pl.pallas_call(fn, in_specs=specs_in, out_specs=specs_out, out_shape=shapes + (TOKEN,), **kw)(*args)
        _LAST[0] = out[-1]
        return out[0] if single else tuple(out[:-1])

    return run


def _tile(dim, pref):
    if dim <= pref:
        return dim
    for t in range(pref, 0, -128):
        if dim % t == 0:
            return t
    while dim % pref:
        pref //= 2
    return pref


def _dot(a, b, dims=NN):
    return lax.dot_general(a, b, dims, preferred_element_type=F32)


def _colsum(v):
    return jnp.sum(v, axis=0, keepdims=True)


def _sigmoid(x):
    return 1.0 / (1.0 + jnp.exp(-x))


def _matmul(name, operands, in_specs, out_shape, out_specs, grid, dims, acc_shape, epilogue, split_k=False):
    nk = grid[2]
    n_in = len(operands)

    def product(a_ref, b_ref):
        if split_k:
            ck = b_ref.shape[2]
            out = _dot(a_ref[:, 0:ck], b_ref[0], dims)
            for j in range(1, N_CHIPS):
                out = out + _dot(a_ref[:, j * ck:(j + 1) * ck], b_ref[j], dims)
            return out
        bv = b_ref[...]
        return _dot(a_ref[...], bv.reshape(-1, bv.shape[-1]) if bv.ndim == 3 else bv, dims)

    def body(*refs):
        a_ref, b_ref = refs[0], refs[1]
        extra = refs[2:n_in]
        if nk == 1:
            epilogue(product(a_ref, b_ref), extra, refs[n_in:])
            return
        outs = refs[n_in:-1]
        acc = refs[-1]
        k = pl.program_id(2)

        @pl.when(k == 0)
        def _():
            acc[...] = product(a_ref, b_ref)

        @pl.when((k > 0) & (k < nk - 1))
        def _():
            acc[...] += product(a_ref, b_ref)

        @pl.when(k == nk - 1)
        def _():
            epilogue(acc[...] + product(a_ref, b_ref), extra, outs)

    return _ordered_call(
        body, name=name, grid=grid, in_specs=in_specs, out_specs=out_specs, out_shape=out_shape,
        scratch_shapes=[pltpu.VMEM(acc_shape, F32)] if nk > 1 else [],
        compiler_params=_cp("parallel", "parallel", "arbitrary"),
    )(*operands)


def _ep_store(acc, extra, outs):
    outs[0][...] = acc.astype(outs[0].dtype)


def _ep_residual(acc, extra, outs):
    outs[0][...] = extra[0][...] + acc


def _ep_residual_norm(acc, extra, outs):
    xm = extra[0][...] + acc
    outs[0][...] = xm
    r = lax.rsqrt(jnp.mean(xm * xm, axis=-1, keepdims=True) + EPS)
    outs[1][...] = (xm * r * extra[1][...]).astype(BF16)


def _ep_up(acc, extra, outs):
    outs[0][...] = acc.astype(BF16)
    r = jnp.maximum(acc, 0.0)
    outs[1][...] = (r * r).astype(BF16)


def _ep_dup(acc, extra, outs):
    outs[0][...] = (acc * (2.0 * jnp.maximum(extra[0][...].astype(F32), 0.0))).astype(BF16)


def _whole_rows(wg):
    return N_CHIPS * wg.shape[1] <= TILE_K


def _mm_act_w(name, a, wg, col_sharded, epilogue, extra=(), out_dtypes=(F32,), row_vectors=()):
    m, kdim = a.shape
    _, r, c = wg.shape
    tm = _tile(m, 1024)
    if col_sharded:
        n = N_CHIPS * c
        tn = _tile(c, 1024)
        tk = _tile(kdim, TILE_K)
        per = c // tn
        b_spec = pl.BlockSpec((None, tk, tn), lambda i, j, k: (j // per, k, j % per))
    elif _whole_rows(wg):
        n = c
        tm, tn, tk = _tile(m, 512), n, kdim
        b_spec = pl.BlockSpec((N_CHIPS, r, tn), lambda i, j, k: (0, 0, j))
    else:
        n = c
        tn = _tile(n, 1024)
        tk = _tile(r, TILE_K)
        per = r // tk
        b_spec = pl.BlockSpec((None, tk, tn), lambda i, j, k: (k // per, k % per, j))
    grid = (m // tm, n // tn, kdim // tk)
    o_spec = pl.BlockSpec((tm, tn), lambda i, j, k: (i, j))
    in_specs = ([pl.BlockSpec((tm, tk), lambda i, j, k: (i, k)), b_spec] + [o_spec] * len(extra)
                + [pl.BlockSpec((1, tn), lambda i, j, k: (0, j))] * len(row_vectors))
    return _matmul(name, (a, wg) + tuple(extra) + tuple(row_vectors), in_specs,
                   tuple(jax.ShapeDtypeStruct((m, n), d) for d in out_dtypes),
                   tuple(o_spec for _ in out_dtypes), grid, NN, (tm, tn), epilogue)


def _mm_act_wt(name, a, wg, col_sharded, epilogue, extra=(), out_dtypes=(F32,)):
    m, kdim = a.shape
    _, r, c = wg.shape
    tm = _tile(m, 1024)
    split_k = False
    if col_sharded and N_CHIPS * c <= 2 * TILE_K:
        n = r
        tm, tn, tk, split_k = _tile(m, 512), n, kdim, True
        b_spec = pl.BlockSpec((N_CHIPS, tn, c), lambda i, j, k: (0, j, 0))
    elif col_sharded:
        n = r
        tn = _tile(n, 1024)
        tk = _tile(c, TILE_K)
        per = c // tk
        b_spec = pl.BlockSpec((None, tn, tk), lambda i, j, k: (k // per, j, k % per))
    elif N_CHIPS * r <= TILE_K:
        n = N_CHIPS * r
        tm, tn, tk = _tile(m, 512), n, _tile(c, TILE_K)
        b_spec = pl.BlockSpec((N_CHIPS, r, tk), lambda i, j, k: (0, 0, k))
    else:
        n = N_CHIPS * r
        tn = _tile(r, 1024)
        tk = _tile(c, TILE_K)
        per = r // tn
        b_spec = pl.BlockSpec((None, tn, tk), lambda i, j, k: (j // per, j % per, k))
    grid = (m // tm, n // tn, kdim // tk)
    o_spec = pl.BlockSpec((tm, tn), lambda i, j, k: (i, j))
    in_specs = [pl.BlockSpec((tm, tk), lambda i, j, k: (i, k)), b_spec] + [o_spec] * len(extra)
    return _matmul(name, (a, wg) + tuple(extra), in_specs,
                   tuple(jax.ShapeDtypeStruct((m, n), d) for d in out_dtypes),
                   tuple(o_spec for _ in out_dtypes), grid, NT, (tm, tn), epilogue, split_k)


def _mm_wgrad(name, a, g, col_sharded, c):
    s, kdim = a.shape
    _, n = g.shape
    ts = _tile(s, TILE_K)
    if col_sharded:
        r = kdim
        tm = _tile(kdim, 1024)
        tn = _tile(c, 1024)
        per = c // tn
        o_spec = pl.BlockSpec((None, tm, tn), lambda i, j, k: (j // per, i, j % per))
    else:
        r = kdim // N_CHIPS
        tm = _tile(r, 512)
        tn = _tile(c, 2048)
        per = r // tm
        o_spec = pl.BlockSpec((None, tm, tn), lambda i, j, k: (i // per, i % per, j))
    grid = (kdim // tm, n // tn, s // ts)
    in_specs = [pl.BlockSpec((ts, tm), lambda i, j, k: (k, i)), pl.BlockSpec((ts, tn), lambda i, j, k: (k, j))]
    return _matmul(name, (a, g), in_specs, (jax.ShapeDtypeStruct((N_CHIPS, r, c), BF16),), (o_spec,),
                   grid, TN, (tm, tn), _ep_store)[0]


def _rms_fwd(name, x, g):
    s, d = x.shape
    tb = _tile(s, 256)

    def body(x_ref, g_ref, o_ref):
        xv = x_ref[...]
        r = lax.rsqrt(jnp.mean(xv * xv, axis=-1, keepdims=True) + EPS)
        o_ref[...] = (xv * r * g_ref[...]).astype(BF16)

    return _ordered_call(
        body, name=name, grid=(s // tb,),
        in_specs=[pl.BlockSpec((tb, d), lambda i: (i, 0)), pl.BlockSpec((1, d), lambda i: (0, 0))],
        out_specs=pl.BlockSpec((tb, d), lambda i: (i, 0)),
        out_shape=jax.ShapeDtypeStruct((s, d), BF16), compiler_params=_cp("parallel"),
    )(x, g.reshape(1, d))


def _rms_bwd(name, dh, x, g, dres):
    s, d = x.shape
    tb = _tile(s, 256)

    def body(dh_ref, x_ref, g_ref, dres_ref, dx_ref, dxb_ref, dg_ref):
        i = pl.program_id(0)
        xv = x_ref[...]
        r = lax.rsqrt(jnp.mean(xv * xv, axis=-1, keepdims=True) + EPS)
        xhat = xv * r
        dhv = dh_ref[...]
        dxhat = dhv * g_ref[...]
        dx = dres_ref[...] + r * (dxhat - xhat * jnp.mean(dxhat * xhat, axis=-1, keepdims=True))
        dx_ref[...] = dx
        dxb_ref[...] = dx.astype(BF16)

        @pl.when(i == 0)
        def _():
            dg_ref[...] = jnp.zeros_like(dg_ref)

        dg_ref[...] += _colsum(dhv * xhat)

    row = pl.BlockSpec((tb, d), lambda i: (i, 0))
    vec = pl.BlockSpec((1, d), lambda i: (0, 0))
    return _ordered_call(
        body, name=name, grid=(s // tb,), in_specs=[row, row, vec, row], out_specs=(row, row, vec),
        out_shape=(jax.ShapeDtypeStruct((s, d), F32), jax.ShapeDtypeStruct((s, d), BF16),
                   jax.ShapeDtypeStruct((1, d), F32)),
        compiler_params=_cp("arbitrary"),
    )(dh, x, g.reshape(1, d), dres)


def _loss_head(y, t):
    s, d = y.shape
    tb = _tile(s, 256)

    def body(y_ref, t_ref, dy_ref, dyb_ref, loss_ref, acc):
        i = pl.program_id(0)
        e = y_ref[...] - t_ref[...]
        dy = e * (1.0 / d)
        dy_ref[...] = dy
        dyb_ref[...] = dy.astype(BF16)

        @pl.when(i == 0)
        def _():
            acc[...] = jnp.zeros_like(acc)

        acc[...] += _colsum(e * e)

        @pl.when(i == pl.num_programs(0) - 1)
        def _():
            loss_ref[...] = jnp.sum(acc[...], axis=-1, keepdims=True) * (0.5 / d)

    row = pl.BlockSpec((tb, d), lambda i: (i, 0))
    return _ordered_call(
        body, name="loss_head", grid=(s // tb,), in_specs=[row, row],
        out_specs=(row, row, pl.BlockSpec((1, 1), lambda i: (0, 0))),
        out_shape=(jax.ShapeDtypeStruct((s, d), F32), jax.ShapeDtypeStruct((s, d), BF16),
                   jax.ShapeDtypeStruct((1, 1), F32)),
        scratch_shapes=[pltpu.VMEM((1, d), F32)], compiler_params=_cp("arbitrary"),
    )(y, t)


def _mixnorm_fwd(name, ya, yc, ys, g):
    s, aw = ya.shape
    cw, sw = yc.shape[1], ys.shape[1]
    d = aw + cw + sw
    tb = _tile(s, 256)

    def body(ya_ref, yc_ref, ys_ref, g_ref, o_ref):
        off = 0
        for ref, w in ((ya_ref, aw), (yc_ref, cw), (ys_ref, sw)):
            v = ref[...]
            r = lax.rsqrt(jnp.mean(v * v, axis=-1, keepdims=True) + EPS)
            o_ref[:, off:off + w] = (v * r * g_ref[:, off:off + w]).astype(BF16)
            off += w

    def row(w):
        return pl.BlockSpec((tb, w), lambda i: (i, 0))

    return _ordered_call(
        body, name=name, grid=(s // tb,),
        in_specs=[row(aw), row(cw), row(sw), pl.BlockSpec((1, d), lambda i: (0, 0))], out_specs=row(d),
        out_shape=jax.ShapeDtypeStruct((s, d), BF16), compiler_params=_cp("parallel"),
    )(ya, yc, ys, g.reshape(1, d))


def _mixnorm_bwd(name, dmix, ya, yc, ys, g):
    s, aw = ya.shape
    cw, sw = yc.shape[1], ys.shape[1]
    d = aw + cw + sw
    tb = _tile(s, 256)

    def body(dm_ref, ya_ref, yc_ref, ys_ref, g_ref, dya_ref, dyc_ref, dys_ref, dg_ref):
        i = pl.program_id(0)

        @pl.when(i == 0)
        def _():
            dg_ref[...] = jnp.zeros_like(dg_ref)

        off = 0
        for ref, dref, w in ((ya_ref, dya_ref, aw), (yc_ref, dyc_ref, cw), (ys_ref, dys_ref, sw)):
            v = ref[...]
            r = lax.rsqrt(jnp.mean(v * v, axis=-1, keepdims=True) + EPS)
            vhat = v * r
            dm = dm_ref[:, off:off + w]
            dvhat = dm * g_ref[:, off:off + w]
            dref[...] = r * (dvhat - vhat * jnp.mean(dvhat * vhat, axis=-1, keepdims=True))
            dg_ref[:, off:off + w] += _colsum(dm * vhat)
            off += w

    def row(w):
        return pl.BlockSpec((tb, w), lambda i: (i, 0))

    vec = pl.BlockSpec((1, d), lambda i: (0, 0))
    return _ordered_call(
        body, name=name, grid=(s // tb,), in_specs=[row(d), row(aw), row(cw), row(sw), vec],
        out_specs=(row(aw), row(cw), row(sw), vec),
        out_shape=(jax.ShapeDtypeStruct((s, aw), F32), jax.ShapeDtypeStruct((s, cw), F32),
                   jax.ShapeDtypeStruct((s, sw), F32), jax.ShapeDtypeStruct((1, d), F32)),
        compiler_params=_cp("arbitrary"),
    )(dmix, ya, yc, ys, g.reshape(1, d))


def _head_rms(x):
    r = lax.rsqrt(jnp.mean(x * x, axis=-1, keepdims=True) + EPS)
    return x * r, r


def _attn_mask(n):
    qi = lax.broadcasted_iota(jnp.int32, (GQA * WINDOW, 2 * WINDOW), 0) & (WINDOW - 1)
    sj = lax.broadcasted_iota(jnp.int32, (GQA * WINDOW, 2 * WINDOW), 1)
    rel = qi + WINDOW - sj
    return (rel >= 0) & (rel < WINDOW) & ((sj >= WINDOW) | (n > 0))


def _attn_specs(nq, nkv, nb):
    qspec = pl.BlockSpec((nq, WINDOW, HEAD_DIM), lambda n: (0, n, 0))
    cur = pl.BlockSpec((nkv, WINDOW, HEAD_DIM), lambda n: (0, n, 0))
    prev = pl.BlockSpec((nkv, WINDOW, HEAD_DIM), lambda n: (0, jnp.maximum(n - 1, 0), 0))
    nxt = pl.BlockSpec((nkv, WINDOW, HEAD_DIM), lambda n: (0, jnp.minimum(n + 1, nb - 1), 0))
    gain = pl.BlockSpec((1, HEAD_DIM), lambda n: (0, 0))
    sink = pl.BlockSpec((nq, 1, 128), lambda n: (0, 0, 0))
    return qspec, cur, prev, nxt, gain, sink


def _group_sinks(s_ref, g):
    return jnp.concatenate([jnp.broadcast_to(s_ref[g * GQA + i][:, :1], (WINDOW, 1)) for i in range(GQA)], axis=0)


ATTN_SCALE = HEAD_DIM ** -0.5
assert ATTN_SCALE == 2.0 ** -3


def _attn_probs(qs_b, kn_b, valid, sink):
    logits = jnp.where(valid, _dot(qs_b, kn_b, NT), NEG_INF)
    m = jnp.maximum(jnp.max(logits, axis=-1, keepdims=True), sink)
    p = jnp.exp(logits - m)
    es = jnp.exp(sink - m)
    inv = 1.0 / (jnp.sum(p, axis=-1, keepdims=True) + es)
    return p * inv, es * inv


def _attn_fwd(name, q, k, v, gq, gk, sinks_b):
    nq, s, _ = q.shape
    nkv = k.shape[0]
    nb = s // WINDOW
    qspec, cur, prev, _, gain, sink = _attn_specs(nq, nkv, nb)

    def body(q_ref, kc_ref, kp_ref, vc_ref, vp_ref, gq_ref, gk_ref, s_ref, o_ref):
        gkv = gk_ref[...]
        valid = _attn_mask(pl.program_id(0))
        for g in range(nkv):
            kn_b = jnp.concatenate([_head_rms(kp_ref[g])[0] * gkv, _head_rms(kc_ref[g])[0] * gkv],
                                   axis=0).astype(BF16)
            vv_b = jnp.concatenate([vp_ref[g], vc_ref[g]], axis=0).astype(BF16)
            heads = pl.ds(g * GQA, GQA)
            q4 = q_ref[heads].reshape(GQA * WINDOW, HEAD_DIM)
            qs_b = (_head_rms(q4)[0] * gq_ref[...] * ATTN_SCALE).astype(BF16)
            probs, _ = _attn_probs(qs_b, kn_b, valid, _group_sinks(s_ref, g))
            o_ref[heads] = _dot(probs.astype(BF16), vv_b).reshape(GQA, WINDOW, HEAD_DIM)

    return _ordered_call(
        body, name=name, grid=(nb,), in_specs=[qspec, cur, prev, cur, prev, gain, gain, sink], out_specs=qspec,
        out_shape=jax.ShapeDtypeStruct((nq, s, HEAD_DIM), F32), compiler_params=_cp("parallel"),
    )(q, k, k, v, v, gq.reshape(1, HEAD_DIM), gk.reshape(1, HEAD_DIM), sinks_b)


def _attn_bwd(name, q, k, v, gq, gk, sinks_b, do):
    nq, s, _ = q.shape
    nkv = k.shape[0]
    nb = s // WINDOW
    qspec, cur, prev, _, gain, sink = _attn_specs(nq, nkv, nb)

    def body(q_ref, kc_ref, kp_ref, vc_ref, vp_ref, gq_ref, gk_ref, s_ref, do_ref,
             dq_ref, dkc_ref, dkp_ref, dvc_ref, dvp_ref, dgq_ref, ds_ref):
        n = pl.program_id(0)

        @pl.when(n == 0)
        def _():
            dgq_ref[...] = jnp.zeros_like(dgq_ref)
            ds_ref[...] = jnp.zeros_like(ds_ref)

        gkv = gk_ref[...]
        gqv = gq_ref[...]
        valid = _attn_mask(n)
        dgq = jnp.zeros((1, HEAD_DIM), F32)
        for g in range(nkv):
            kn_b = jnp.concatenate([_head_rms(kp_ref[g])[0] * gkv, _head_rms(kc_ref[g])[0] * gkv],
                                   axis=0).astype(BF16)
            vv_b = jnp.concatenate([vp_ref[g], vc_ref[g]], axis=0).astype(BF16)
            heads = pl.ds(g * GQA, GQA)
            qhat, r = _head_rms(q_ref[heads].reshape(GQA * WINDOW, HEAD_DIM))
            qs = qhat * gqv * ATTN_SCALE
            qs_b = qs.astype(BF16)
            probs, psink = _attn_probs(qs_b, kn_b, valid, _group_sinks(s_ref, g))
            do = do_ref[heads].reshape(GQA * WINDOW, HEAD_DIM)
            do_b = do.astype(BF16)
            dp = _dot(do_b, vv_b, NT)
            delta = jnp.sum(probs * dp, axis=-1, keepdims=True)
            dl_b = (probs * (dp - delta)).astype(BF16)
            sink_term = psink * delta
            for i in range(GQA):
                ds_ref[g * GQA + i] += jnp.broadcast_to(
                    -jnp.sum(sink_term[i * WINDOW:(i + 1) * WINDOW], axis=0, keepdims=True), (1, 128))
            dqn = _dot(dl_b, kn_b) * ATTN_SCALE
            dkn = _dot(qs.T.astype(BF16), dl_b).T
            dvv = _dot(do.T.astype(BF16), probs.astype(BF16)).T
            dgq += _colsum(dqn * qhat)
            dqhat = dqn * gqv
            dq_ref[heads] = (r * (dqhat - qhat * jnp.mean(dqhat * qhat, axis=-1, keepdims=True))).astype(
                BF16).reshape(GQA, WINDOW, HEAD_DIM)
            dkp_ref[g] = dkn[:WINDOW]
            dkc_ref[g] = dkn[WINDOW:]
            dvp_ref[g] = dvv[:WINDOW]
            dvc_ref[g] = dvv[WINDOW:]
        dgq_ref[...] += dgq

    kv_shape = jax.ShapeDtypeStruct((nkv, s, HEAD_DIM), F32)
    return _ordered_call(
        body, name=name, grid=(nb,), in_specs=[qspec, cur, prev, cur, prev, gain, gain, sink, qspec],
        out_specs=(qspec, cur, cur, cur, cur, gain, sink),
        out_shape=(jax.ShapeDtypeStruct((nq, s, HEAD_DIM), BF16), kv_shape, kv_shape, kv_shape, kv_shape,
                   jax.ShapeDtypeStruct((1, HEAD_DIM), F32), jax.ShapeDtypeStruct((nq, 1, 128), F32)),
        compiler_params=_cp("arbitrary"),
    )(q, k, k, v, v, gq.reshape(1, HEAD_DIM), gk.reshape(1, HEAD_DIM), sinks_b, do)


def _attn_bwd_kv(name, k, gk, dkc, dkp, dvc, dvp):
    nkv, s, _ = k.shape
    nb = s // WINDOW
    _, cur, _, nxt, gain, _ = _attn_specs(GQA * nkv, nkv, nb)

    def body(k_ref, gk_ref, dkc_ref, dkp_ref, dvc_ref, dvp_ref, dk_ref, dv_ref, dgk_ref):
        n = pl.program_id(0)

        @pl.when(n == 0)
        def _():
            dgk_ref[...] = jnp.zeros_like(dgk_ref)

        has_next = n < nb - 1
        dgk = jnp.zeros((1, HEAD_DIM), F32)
        for g in range(nkv):
            dkn = dkc_ref[g] + jnp.where(has_next, dkp_ref[g], 0.0)
            dv_ref[g] = (dvc_ref[g] + jnp.where(has_next, dvp_ref[g], 0.0)).astype(BF16)
            khat, r = _head_rms(k_ref[g])
            dgk += _colsum(dkn * khat)
            dkhat = dkn * gk_ref[...]
            dk_ref[g] = (r * (dkhat - khat * jnp.mean(dkhat * khat, axis=-1, keepdims=True))).astype(BF16)
        dgk_ref[...] += dgk

    kv_shape = jax.ShapeDtypeStruct((nkv, s, HEAD_DIM), BF16)
    return _ordered_call(
        body, name=name, grid=(nb,), in_specs=[cur, gain, cur, nxt, cur, nxt], out_specs=(cur, cur, gain),
        out_shape=(kv_shape, kv_shape, jax.ShapeDtypeStruct((1, HEAD_DIM), F32)),
        compiler_params=_cp("arbitrary"),
    )(k, gk.reshape(1, HEAD_DIM), dkc, dkp, dvc, dvp)


SUBLANES = 8


def _fill_shifts(buf, shifts, tb):
    rows = tb + HALO - SUBLANES
    for b in range(1, SUBLANES):
        shifts[b - 1, pl.ds(0, rows), :] = buf[pl.ds(b, rows), :]


def _window(buf, shifts, off, tb):
    b = off % SUBLANES
    return buf[pl.ds(off, tb), :] if b == 0 else shifts[b - 1, pl.ds(off - b, tb), :]


def _conv_recompute(i, a_ref, gt_ref, ap_ref, gp_ref, w_ref, b_ref, hbuf, shifts, tb):
    hbuf[pl.ds(HALO, tb), :] = a_ref[...] * _sigmoid(gt_ref[...])
    tail = ap_ref[pl.ds(tb - HALO, HALO), :] * _sigmoid(gp_ref[pl.ds(tb - HALO, HALO), :])
    hbuf[pl.ds(0, HALO), :] = jnp.where(i > 0, tail, 0.0)
    _fill_shifts(hbuf, shifts, tb)
    acc = jnp.broadcast_to(b_ref[...], a_ref.shape)
    for kk in range(CONV_KERNEL):
        acc = acc + w_ref[pl.ds(kk, 1), :] * _window(hbuf, shifts, HALO - (CONV_KERNEL - 1) + kk, tb)
    return acc


def _layer_norm_stats(c):
    mu = jnp.mean(c, axis=-1, keepdims=True)
    xc = c - mu
    r = lax.rsqrt(jnp.mean(xc * xc, axis=-1, keepdims=True) + EPS)
    return xc * r, r


def _conv_specs(s, cw, tb, a_blk):
    cur = lambda off: pl.BlockSpec((tb, cw), lambda i: (i, a_blk + off))
    prev = lambda off: pl.BlockSpec((tb, cw), lambda i: (jnp.maximum(i - 1, 0), a_blk + off))
    wspec = pl.BlockSpec((HALO, cw), lambda i: (0, 0))
    vec = pl.BlockSpec((1, cw), lambda i: (0, 0))
    row = pl.BlockSpec((tb, cw), lambda i: (i, 0))
    return cur, prev, wspec, vec, row


def _conv_fwd(name, proj, a_blk, w, b, lg, lb):
    s = proj.shape[0]
    cw = w.shape[1]
    tb = _tile(s, 256)
    cur, prev, wspec, vec, row = _conv_specs(s, cw, tb, a_blk)

    def body(a_ref, gt_ref, ap_ref, gp_ref, w_ref, b_ref, lg_ref, lb_ref, y_ref, hbuf, shifts):
        c = _conv_recompute(pl.program_id(0), a_ref, gt_ref, ap_ref, gp_ref, w_ref, b_ref, hbuf, shifts, tb)
        chat, _ = _layer_norm_stats(c)
        z = chat * lg_ref[...] + lb_ref[...]
        y_ref[...] = z * _sigmoid(z)

    return _ordered_call(
        body, name=name, grid=(s // tb,), in_specs=[cur(0), cur(1), prev(0), prev(1), wspec, vec, vec, vec],
        out_specs=row, out_shape=jax.ShapeDtypeStruct((s, cw), F32),
        scratch_shapes=[pltpu.VMEM((tb + HALO, cw), F32), pltpu.VMEM((SUBLANES - 1, tb + HALO, cw), F32)],
        compiler_params=_cp("arbitrary"),
    )(proj, proj, proj, proj, w, b, lg, lb)


def _conv_bwd1(name, proj, a_blk, w, b, lg, lb, dy):
    s = proj.shape[0]
    cw = w.shape[1]
    tb = _tile(s, 256)
    cur, prev, wspec, vec, row = _conv_specs(s, cw, tb, a_blk)

    def body(a_ref, gt_ref, ap_ref, gp_ref, w_ref, b_ref, lg_ref, lb_ref, dy_ref,
             dc_ref, dw_ref, db_ref, dlg_ref, dlb_ref, hbuf, shifts):
        i = pl.program_id(0)

        @pl.when(i == 0)
        def _():
            dw_ref[...] = jnp.zeros_like(dw_ref)
            db_ref[...] = jnp.zeros_like(db_ref)
            dlg_ref[...] = jnp.zeros_like(dlg_ref)
            dlb_ref[...] = jnp.zeros_like(dlb_ref)

        c = _conv_recompute(i, a_ref, gt_ref, ap_ref, gp_ref, w_ref, b_ref, hbuf, shifts, tb)
        chat, r = _layer_norm_stats(c)
        z = chat * lg_ref[...] + lb_ref[...]
        sg = _sigmoid(z)
        dz = dy_ref[...] * (sg + z * sg * (1.0 - sg))
        dlg_ref[...] += _colsum(dz * chat)
        dlb_ref[...] += _colsum(dz)
        dzg = dz * lg_ref[...]
        dc = r * (dzg - jnp.mean(dzg, axis=-1, keepdims=True) - chat * jnp.mean(dzg * chat, axis=-1, keepdims=True))
        dc_ref[...] = dc
        db_ref[...] += _colsum(dc)
        for kk in range(CONV_KERNEL):
            dw_ref[pl.ds(kk, 1), :] += _colsum(dc * _window(hbuf, shifts, HALO - (CONV_KERNEL - 1) + kk, tb))

    return _ordered_call(
        body, name=name, grid=(s // tb,), in_specs=[cur(0), cur(1), prev(0), prev(1), wspec, vec, vec, vec, row],
        out_specs=(row, wspec, vec, vec, vec),
        out_shape=(jax.ShapeDtypeStruct((s, cw), F32), jax.ShapeDtypeStruct((HALO, cw), F32),
                   jax.ShapeDtypeStruct((1, cw), F32), jax.ShapeDtypeStruct((1, cw), F32),
                   jax.ShapeDtypeStruct((1, cw), F32)),
        scratch_shapes=[pltpu.VMEM((tb + HALO, cw), F32), pltpu.VMEM((SUBLANES - 1, tb + HALO, cw), F32)],
        compiler_params=_cp("arbitrary"),
    )(proj, proj, proj, proj, w, b, lg, lb, dy)


def _conv_bwd2(name, proj, a_blk, w, dc):
    s = proj.shape[0]
    cw = w.shape[1]
    tb = _tile(s, 256)
    nblk = s // tb
    cur, _, wspec, _, row = _conv_specs(s, cw, tb, a_blk)
    nxt = pl.BlockSpec((tb, cw), lambda i: (jnp.minimum(i + 1, nblk - 1), 0))

    def body(a_ref, gt_ref, w_ref, dc_ref, dn_ref, o_ref, dbuf, shifts):
        i = pl.program_id(0)
        dbuf[pl.ds(0, tb), :] = dc_ref[...]
        dbuf[pl.ds(tb, HALO), :] = jnp.where(i < nblk - 1, dn_ref[pl.ds(0, HALO), :], 0.0)
        _fill_shifts(dbuf, shifts, tb)
        dh = jnp.zeros((tb, cw), F32)
        for kk in range(CONV_KERNEL):
            dh = dh + w_ref[pl.ds(kk, 1), :] * _window(dbuf, shifts, CONV_KERNEL - 1 - kk, tb)
        sg = _sigmoid(gt_ref[...])
        o_ref[:, 0:cw] = (dh * sg).astype(BF16)
        o_ref[:, cw:2 * cw] = (dh * a_ref[...] * sg * (1.0 - sg)).astype(BF16)

    return _ordered_call(
        body, name=name, grid=(nblk,), in_specs=[cur(0), cur(1), wspec, row, nxt],
        out_specs=pl.BlockSpec((tb, 2 * cw), lambda i: (i, 0)), out_shape=jax.ShapeDtypeStruct((s, 2 * cw), BF16),
        scratch_shapes=[pltpu.VMEM((tb + HALO, cw), F32), pltpu.VMEM((SUBLANES - 1, tb + HALO, cw), F32)],
        compiler_params=_cp("arbitrary"),
    )(proj, proj, w, dc, dc)


def _sgu_common(v_ref, lg_ref, lb_ref, w_ref, bexp_ref, sw):
    vhat, r = _layer_norm_stats(v_ref[...])
    vn_b = (vhat * lg_ref[...] + lb_ref[...]).astype(BF16)
    ii = lax.broadcasted_iota(jnp.int32, (WINDOW, WINDOW), 0)
    jj = lax.broadcasted_iota(jnp.int32, (WINDOW, WINDOW), 1)
    tril = jj <= ii
    head_of = lax.broadcasted_iota(jnp.int32, (WINDOW, sw), 1) // HEAD_DIM
    wts = [jnp.where(tril, w_ref[h], 0.0).astype(BF16) for h in range(sw // HEAD_DIM)]
    sv = bexp_ref[...]
    for h, wt in enumerate(wts):
        sv = sv + jnp.where(head_of == h, _dot(wt, vn_b), 0.0)
    return vhat, r, vn_b, tril, head_of, wts, sv


def _sgu_specs(sw, u_blk):
    nh = sw // HEAD_DIM
    u = pl.BlockSpec((WINDOW, sw), lambda n: (n, u_blk))
    v = pl.BlockSpec((WINDOW, sw), lambda n: (n, u_blk + 1))
    vec = pl.BlockSpec((1, sw), lambda n: (0, 0))
    wspec = pl.BlockSpec((nh, WINDOW, WINDOW), lambda n: (0, 0, 0))
    bspec = pl.BlockSpec((WINDOW, sw), lambda n: (0, 0))
    row = pl.BlockSpec((WINDOW, sw), lambda n: (n, 0))
    return u, v, vec, wspec, bspec, row


def _sgu_fwd(name, proj, u_blk, lg, lb, w, bexp):
    s = proj.shape[0]
    sw = lg.shape[1]
    u, v, vec, wspec, bspec, row = _sgu_specs(sw, u_blk)

    def body(u_ref, v_ref, lg_ref, lb_ref, w_ref, bexp_ref, y_ref):
        sv = _sgu_common(v_ref, lg_ref, lb_ref, w_ref, bexp_ref, sw)[-1]
        y_ref[...] = u_ref[...] * sv

    return _ordered_call(
        body, name=name, grid=(s // WINDOW,), in_specs=[u, v, vec, vec, wspec, bspec], out_specs=row,
        out_shape=jax.ShapeDtypeStruct((s, sw), F32), compiler_params=_cp("parallel"),
    )(proj, proj, lg, lb, w, bexp)


def _sgu_bwd(name, proj, u_blk, lg, lb, w, bexp, dy):
    s = proj.shape[0]
    sw = lg.shape[1]
    nh = sw // HEAD_DIM
    u, v, vec, wspec, bspec, row = _sgu_specs(sw, u_blk)
    dbspec = pl.BlockSpec((nh, WINDOW), lambda n: (0, 0))

    def body(u_ref, v_ref, lg_ref, lb_ref, w_ref, bexp_ref, dy_ref, o_ref, dw_ref, db_ref, dlg_ref, dlb_ref):
        n = pl.program_id(0)

        @pl.when(n == 0)
        def _():
            dw_ref[...] = jnp.zeros_like(dw_ref)
            db_ref[...] = jnp.zeros_like(db_ref)
            dlg_ref[...] = jnp.zeros_like(dlg_ref)
            dlb_ref[...] = jnp.zeros_like(dlb_ref)

        vhat, r, vn_b, tril, head_of, wts, sv = _sgu_common(v_ref, lg_ref, lb_ref, w_ref, bexp_ref, sw)
        dyv = dy_ref[...]
        o_ref[:, 0:sw] = (dyv * sv).astype(BF16)
        ds = dyv * u_ref[...]
        dvn = jnp.zeros((WINDOW, sw), F32)
        for h, wt in enumerate(wts):
            dsm_b = jnp.where(head_of == h, ds, 0.0).astype(BF16)
            dvn = dvn + _dot(wt, dsm_b, TN)
            dw_ref[h] += jnp.where(tril, _dot(dsm_b, vn_b, NT), 0.0)
        hmask = (lax.broadcasted_iota(jnp.int32, (nh, sw), 1) // HEAD_DIM
                 == lax.broadcasted_iota(jnp.int32, (nh, sw), 0)).astype(F32)
        db_ref[...] += lax.dot_general(hmask, ds, NT, precision=lax.Precision.HIGHEST, preferred_element_type=F32)
        dlg_ref[...] += _colsum(dvn * vhat)
        dlb_ref[...] += _colsum(dvn)
        dvg = dvn * lg_ref[...]
        dv = r * (dvg - jnp.mean(dvg, axis=-1, keepdims=True) - vhat * jnp.mean(dvg * vhat, axis=-1, keepdims=True))
        o_ref[:, sw:2 * sw] = dv.astype(BF16)

    return _ordered_call(
        body, name=name, grid=(s // WINDOW,), in_specs=[u, v, vec, vec, wspec, bspec, row],
        out_specs=(pl.BlockSpec((WINDOW, 2 * sw), lambda n: (n, 0)), wspec, dbspec, vec, vec),
        out_shape=(jax.ShapeDtypeStruct((s, 2 * sw), BF16), jax.ShapeDtypeStruct((nh, WINDOW, WINDOW), F32),
                   jax.ShapeDtypeStruct((nh, WINDOW), F32), jax.ShapeDtypeStruct((1, sw), F32),
                   jax.ShapeDtypeStruct((1, sw), F32)),
        compiler_params=_cp("arbitrary"),
    )(proj, proj, lg, lb, w, bexp, dy)


def _adamw_many(name, ws, gs, ms, vs):
    n = len(ws)

    def body(*refs):
        for t in range(n):
            w_ref, g_ref, m_ref, v_ref = (refs[k * n + t] for k in range(4))
            d_ref, nm_ref, nv_ref = (refs[(4 + k) * n + t] for k in range(3))
            gv = g_ref[...]
            mv = ADAM_B1 * m_ref[...] + (1.0 - ADAM_B1) * gv
            vv = ADAM_B2 * v_ref[...] + (1.0 - ADAM_B2) * (gv * gv)
            m_hat = mv / (1.0 - ADAM_B1 ** ADAM_STEP)
            v_hat = vv / (1.0 - ADAM_B2 ** ADAM_STEP)
            d_ref[...] = -ADAM_LR * (m_hat / (jnp.sqrt(v_hat) + ADAM_EPS) + ADAM_WD * w_ref[...])
            nm_ref[...] = mv
            nv_ref[...] = vv

    whole = pl.BlockSpec(memory_space=pltpu.VMEM)
    out = _ordered_call(
        body, name=name, in_specs=[whole] * (4 * n), out_specs=(whole,) * (3 * n),
        out_shape=tuple(jax.ShapeDtypeStruct(w.shape, F32) for w in ws) * 3,
        compiler_params=pltpu.CompilerParams(vmem_limit_bytes=VMEM_LIMIT),
    )(*ws, *gs, *ms, *vs)
    return out[:n], out[n:2 * n], out[2 * n:]


def _route():
    x, y, c = lax.axis_index("x"), lax.axis_index("y"), lax.axis_index("c")
    n1 = (jnp.where(c == 0, 1 - x, x), jnp.where(c == 0, y, 1 - y))
    n2 = (jnp.where(c == 0, x, 1 - x), jnp.where(c == 0, 1 - y, y))
    return x, y, c, n1, n2, (1 - x, 1 - y)


def _cidx(chip):
    return 2 * chip[0] + chip[1]


def _remote(src, dst, sems, k, device):
    send_sems, recv_sems = sems
    return pltpu.make_async_remote_copy(src_ref=src, dst_ref=dst, send_sem=send_sems.at[k], recv_sem=recv_sems.at[k],
                                        device_id=device, device_id_type=MESH)


def _exchange(name, bufs, n_sems, build):
    n = len(bufs)

    def body(*refs):
        cps = build(refs[n:2 * n], (refs[2 * n], refs[2 * n + 1]))
        for cp in cps:
            cp.start()
        for cp in cps:
            cp.wait()

    return _ordered_call(
        body, name=name, in_specs=[ANY] * n, out_specs=tuple(ANY for _ in range(n)),
        out_shape=tuple(jax.ShapeDtypeStruct(b.shape, b.dtype) for b in bufs),
        input_output_aliases={i: i for i in range(n)},
        scratch_shapes=[pltpu.SemaphoreType.DMA((n_sems,)), pltpu.SemaphoreType.DMA((n_sems,))],
        compiler_params=pltpu.CompilerParams(has_side_effects=True),
    )(*bufs)


HBM_SPEC = pl.BlockSpec(memory_space=pltpu.HBM)
SEM_SPEC = pl.BlockSpec(memory_space=pltpu.SEMAPHORE)
DATAFLOW = pltpu.SideEffectType.DATAFLOW_SIDE_EFFECTING


def _exchange_start(name, bufs, n_sems, build):
    n = len(bufs)

    def body(*refs):
        for cp in build(refs[:n], (refs[n], refs[n + 1])):
            cp.start()

    out = _ordered_call(
        body, name=name,
        out_shape=(pltpu.SemaphoreType.DMA((n_sems,)), pltpu.SemaphoreType.DMA((n_sems,)))
        + tuple(pltpu.HBM(b.shape, b.dtype) for b in bufs),
        in_specs=[HBM_SPEC] * n, out_specs=(SEM_SPEC, SEM_SPEC) + (HBM_SPEC,) * n,
        input_output_aliases={i: 2 + i for i in range(n)},
        compiler_params=pltpu.CompilerParams(has_side_effects=DATAFLOW),
    )(*[pltpu.with_memory_space_constraint(b, pltpu.HBM) for b in bufs])
    return dict(name=name, send=out[0], recv=out[1], bufs=list(out[2:2 + n]), build=build)


def _exchange_wait(handle):
    n = len(handle["bufs"])

    def body(*refs):
        for cp in handle["build"](refs[:n], (refs[n], refs[n + 1])):
            cp.wait_send()
            cp.wait_recv()

    return list(_ordered_call(
        body, name=handle["name"] + "_wait", out_shape=tuple(pltpu.HBM(b.shape, b.dtype) for b in handle["bufs"]),
        in_specs=[HBM_SPEC] * n + [SEM_SPEC, SEM_SPEC], out_specs=(HBM_SPEC,) * n,
        input_output_aliases={i: i for i in range(n)},
        compiler_params=pltpu.CompilerParams(has_side_effects=DATAFLOW),
    )(*handle["bufs"], handle["send"], handle["recv"]))


def _cast_place(name, w, l, me_idx, dtype):
    _, r, c = w.shape
    tr = _tile(r, 512)

    def body(me_ref, w_ref, o_ref):
        o_ref[...] = w_ref[...].astype(dtype)

    grid_spec = pltpu.PrefetchScalarGridSpec(
        num_scalar_prefetch=1, grid=(r // tr,),
        in_specs=[pl.BlockSpec((None, tr, c), lambda i, me_ref: (l, i, 0))],
        out_specs=pl.BlockSpec((None, tr, c), lambda i, me_ref: (me_ref[0], i, 0)))
    return _ordered_call(
        body, name=name, grid_spec=grid_spec, out_shape=jax.ShapeDtypeStruct((N_CHIPS, r, c), dtype),
        compiler_params=_cp("arbitrary"),
    )(me_idx, w)


def _my_half(ref, blk, c):
    hr = ref.shape[1] // 2
    return ref.at[blk, pl.ds(c * hr, hr), :]


def _gather_step(entering):
    lens = [len(e) for e in entering]
    flat = [b for e in entering for b in e]

    def build(refs, sems):
        x, y, c, n1, n2, dg = _route()
        me = _cidx((x, y))
        plan = ([(r, (me,), (*n1, c)) for r in refs[:lens[0]]]
                + [(r, (me, _cidx(n1)), (*n2, c)) for r in refs[lens[0]:lens[0] + lens[1]]]
                + [(r, (_cidx(n1), _cidx(n2), _cidx(dg)), (x, y, 1 - c)) for r in refs[lens[0] + lens[1]:]])
        cps = []
        for ref, blocks, peer in plan:
            for blk in blocks:
                cps.append(_remote(_my_half(ref, blk, c), _my_half(ref, blk, c), sems, len(cps), peer))
        return cps

    return flat, lens[0] + 2 * lens[1] + 3 * lens[2], build


RI_C, RI_ME, RI_N2, RI_N1 = 0, 1, 2, 3


def _pair_sum(name, g, sib, route_idx):
    _, rows, cols = g.shape
    hr = rows // 2
    tr = _tile(hr, 512)
    per = hr // tr

    def body(ri, g_ref, s_ref, o_ref):
        o_ref[...] = (g_ref[...].astype(F32) + s_ref[...].astype(F32)).astype(BF16)

    blk = (None, tr, cols)
    grid_spec = pltpu.PrefetchScalarGridSpec(
        num_scalar_prefetch=1, grid=(2, per),
        in_specs=[pl.BlockSpec(blk, lambda j, i, ri: (ri[RI_N1 + j], ri[RI_C] * per + i, 0)),
                  pl.BlockSpec(blk, lambda j, i, ri: (ri[RI_N1 + j], i, 0))],
        out_specs=pl.BlockSpec(blk, lambda j, i, ri: (j, i, 0)))
    return _ordered_call(
        body, name=name, grid_spec=grid_spec, out_shape=jax.ShapeDtypeStruct((2, hr, cols), BF16),
        compiler_params=_cp("parallel", "parallel"),
    )(route_idx, g, sib)


def _sum_stage1(name, g, sib, got, route_idx):
    _, hr, cols = sib.shape
    tr = _tile(hr, 512)
    per = hr // tr

    def body(ri, gm_ref, sm_ref, gn_ref, sn_ref, g0_ref, g1_ref, keep_ref, send_ref):
        keep_ref[...] = (gm_ref[...].astype(F32) + sm_ref[...].astype(F32)) + g0_ref[...].astype(F32)
        send_ref[...] = ((gn_ref[...].astype(F32) + sn_ref[...].astype(F32)) + g1_ref[...].astype(F32)).astype(BF16)

    blk = (None, tr, cols)
    row = pl.BlockSpec((tr, cols), lambda i, ri: (i, 0))

    def mine(which):
        return pl.BlockSpec(blk, lambda i, ri: (ri[which], ri[RI_C] * per + i, 0))

    def theirs(which):
        return pl.BlockSpec(blk, lambda i, ri: (ri[which], i, 0))

    grid_spec = pltpu.PrefetchScalarGridSpec(
        num_scalar_prefetch=1, grid=(per,),
        in_specs=[mine(RI_ME), theirs(RI_ME), mine(RI_N2), theirs(RI_N2),
                  pl.BlockSpec(blk, lambda i, ri: (0, i, 0)), pl.BlockSpec(blk, lambda i, ri: (1, i, 0))],
        out_specs=(row, row))
    return _ordered_call(
        body, name=name, grid_spec=grid_spec,
        out_shape=(jax.ShapeDtypeStruct((hr, cols), F32), jax.ShapeDtypeStruct((hr, cols), BF16)),
        compiler_params=_cp("parallel"),
    )(route_idx, g, sib, g, sib, got, got)


def _sum_stage2(name, keep, got):
    hr, cols = keep.shape
    tr = _tile(hr, 512)

    def body(k_ref, g_ref, o_ref):
        o_ref[...] = k_ref[...] + g_ref[...].astype(F32)

    row = pl.BlockSpec((tr, cols), lambda i: (i, 0))
    return _ordered_call(
        body, name=name, grid=(hr // tr,), in_specs=[row, row], out_specs=row,
        out_shape=jax.ShapeDtypeStruct((hr, cols), F32), compiler_params=_cp("parallel"),
    )(keep, got)


def _reduce_scatter(tag, names, grads, route_idx):
    n = len(grads)
    hrs = [g.shape[1] // 2 for g in grads]

    def empty(t, lead, dtype):
        return lax.empty(lead + (hrs[t], grads[t].shape[2]), dtype)

    def pair_stage(refs, sems):
        x, y, c, n1, n2, dg = _route()
        return [_remote(refs[t].at[:, pl.ds((1 - c) * hrs[t], hrs[t]), :], refs[n + t], sems, t, (x, y, 1 - c))
                for t in range(n)]

    def stage1(refs, sems):
        x, y, c, n1, n2, dg = _route()
        return [_remote(refs[t].at[slot], refs[n + t].at[slot], sems, 2 * t + slot, (*n1, c))
                for t in range(n) for slot in range(2)]

    def stage2(refs, sems):
        x, y, c, n1, n2, dg = _route()
        return [_remote(refs[t], refs[n + t], sems, t, (*n2, c)) for t in range(n)]

    def stage3(refs, sems):
        x, y, c, n1, n2, dg = _route()
        return [_remote(refs[t], refs[n + t], sems, t, (x, y, 1 - c)) for t in range(n)]

    state = {}

    def start():
        state["h"] = _exchange_start(f"rs_pair_{tag}", list(grads) + [empty(t, (N_CHIPS,), BF16) for t in range(n)],
                                     n, pair_stage)

    def pair_done():
        state["pair"] = _exchange_wait(state["h"])
        psum = [_pair_sum(f"rs_psum_{names[t]}", state["pair"][t], state["pair"][n + t], route_idx) for t in range(n)]
        state["h"] = _exchange_start(f"rs_x1_{tag}", psum + [empty(t, (2,), BF16) for t in range(n)], 2 * n, stage1)

    def x1_done():
        out = _exchange_wait(state["h"])
        state["keep"], send = zip(*[_sum_stage1(f"rs_sum1_{names[t]}", state["pair"][t], state["pair"][n + t],
                                                out[n + t], route_idx) for t in range(n)])
        state["h"] = _exchange_start(f"rs_x2_{tag}", list(send) + [empty(t, (), BF16) for t in range(n)], n, stage2)

    def x2_done():
        out = _exchange_wait(state["h"])
        mine = [_sum_stage2(f"rs_sum2_{names[t]}", state["keep"][t], out[n + t]) for t in range(n)]
        state["h"] = _exchange_start(f"rs_half_{tag}", mine + [empty(t, (), F32) for t in range(n)], n, stage3)

    def finish():
        out = _exchange_wait(state["h"])
        return list(zip(out[:n], out[n:]))

    return start, pair_done, x1_done, x2_done, finish


def _adamw_big(name, w, m, v, f, h, l, c_idx, prev):
    n_l, r, cols = w.shape
    hr = r // 2
    tr = _tile(hr, 256)
    per = hr // tr

    def body(c_ref, w_ref, m_ref, v_ref, f_ref, h_ref, *rest):
        g_ref, d_ref, nm_ref, nv_ref = rest[-4:]
        gv = jnp.where(pl.program_id(0) == c_ref[0], f_ref[...], h_ref[...])
        mv = ADAM_B1 * m_ref[...] + (1.0 - ADAM_B1) * gv
        vv = ADAM_B2 * v_ref[...] + (1.0 - ADAM_B2) * (gv * gv)
        m_hat = mv / (1.0 - ADAM_B1 ** ADAM_STEP)
        v_hat = vv / (1.0 - ADAM_B2 ** ADAM_STEP)
        g_ref[...] = gv
        d_ref[...] = -ADAM_LR * (m_hat / (jnp.sqrt(v_hat) + ADAM_EPS) + ADAM_WD * w_ref[...])
        nm_ref[...] = mv
        nv_ref[...] = vv

    big = pl.BlockSpec((None, tr, cols), lambda hf, i, c_ref: (l, hf * per + i, 0))
    fspec = pl.BlockSpec((tr, cols), lambda hf, i, c_ref: (jnp.where(hf == c_ref[0], i, 0), 0))
    hspec = pl.BlockSpec((tr, cols), lambda hf, i, c_ref: (jnp.where(hf == c_ref[0], 0, i), 0))
    grid_spec = pltpu.PrefetchScalarGridSpec(
        num_scalar_prefetch=1, grid=(2, per), in_specs=[big] * 3 + [fspec, hspec] + [ANY] * len(prev),
        out_specs=(big,) * 4)
    return _ordered_call(
        body, name=name, grid_spec=grid_spec, out_shape=(jax.ShapeDtypeStruct(w.shape, F32),) * 4,
        input_output_aliases={6 + k: k for k in range(len(prev))}, compiler_params=_cp("arbitrary", "arbitrary"),
    )(c_idx, w, m, v, f, h, *prev)


def _small_allreduce(buf):
    rows = buf.shape[0]
    hr = rows // 2

    def body(in_ref, out_ref, pair, acc, got1, got2, send_sems, recv_sems):
        x, y, c, n1, n2, dg = _route()
        sems = (send_sems, recv_sems)
        sibling = (x, y, 1 - c)
        mine = pl.ds(pl.multiple_of(c * hr, 8), hr)
        pair[c] = in_ref[...]
        cp = _remote(in_ref, pair.at[c], sems, 0, sibling)
        cp.start()
        cp.wait()
        acc[...] = pair[0, mine, :] + pair[1, mine, :]
        cp = _remote(acc, got1, sems, 1, (*n1, c))
        cp.start()
        cp.wait()
        acc[...] = acc[...] + got1[...]
        cp = _remote(acc, got2, sems, 2, (*n2, c))
        cp.start()
        cp.wait()
        out_ref[mine, :] = acc[...] + got2[...]
        cp = _remote(out_ref.at[mine, :], out_ref.at[mine, :], sems, 3, sibling)
        cp.start()
        cp.wait()

    half = pltpu.VMEM((hr, 128), F32)
    return _ordered_call(
        body, name="small_allreduce", in_specs=[pl.BlockSpec(memory_space=pltpu.VMEM)],
        out_specs=pl.BlockSpec(memory_space=pltpu.VMEM), out_shape=jax.ShapeDtypeStruct((rows, 128), F32),
        scratch_shapes=[pltpu.VMEM((2, rows, 128), F32), half, half, half,
                        pltpu.SemaphoreType.DMA((4,)), pltpu.SemaphoreType.DMA((4,))],
        compiler_params=pltpu.CompilerParams(has_side_effects=True, vmem_limit_bytes=VMEM_LIMIT),
    )(buf)


BIG = ("w_in", "w_out", "w_up", "w_down")
COL_SHARDED = {"w_in": True, "w_out": False, "w_up": True, "w_down": False}
SMALL = ("ln1_g", "q_norm_g", "k_norm_g", "sinks", "conv_w", "conv_b", "conv_ln_g", "conv_ln_b", "sgu_ln_g",
         "sgu_ln_b", "sgu_w", "sgu_b", "out_norm_g", "ln2_g")
WEIGHTS = ("ln1_g", "w_in", "q_norm_g", "k_norm_g", "sinks", "conv_w", "conv_b", "conv_ln_g", "conv_ln_b",
           "sgu_ln_g", "sgu_ln_b", "sgu_w", "sgu_b", "out_norm_g", "w_out", "ln2_g", "w_up", "w_down")
PACK_QUANTUM = 8 * 128
PACK_ROWS = 512


def _pack(arrs):
    parts = []
    for a in arrs:
        f = a.reshape(-1)
        parts.append(jnp.pad(f, (0, -f.shape[0] % PACK_QUANTUM)).reshape(-1, 128))
    rows = sum(p.shape[0] for p in parts)
    parts.append(jnp.zeros((-rows % PACK_ROWS, 128), F32))
    return jnp.concatenate(parts, axis=0)


def _unpack(buf, shapes):
    out, off = [], 0
    for shp in shapes:
        n = 1
        for dd in shp:
            n *= dd
        rows = (n + PACK_QUANTUM - 1) // PACK_QUANTUM * 8
        out.append(buf[off:off + rows].reshape(-1)[:n].reshape(shp))
        off += rows
    return out


def _to_heads(t, nh):
    return t.reshape(t.shape[0], nh, HEAD_DIM).transpose(1, 0, 2)


def _from_heads(t):
    return t.transpose(1, 0, 2).reshape(t.shape[1], t.shape[0] * HEAD_DIM)


def _no_hook(point, carry):
    return carry


def _layer_fwd(l, x, p, wg, hook=_no_hook):
    d = x.shape[1]
    aw, cw = d // 2, d // 4
    nq = aw // HEAD_DIM
    nkv = nq // GQA
    kvw = nkv * HEAD_DIM
    x = hook("fwd_start", x)
    h1 = _rms_fwd(f"ln1_fwd_{l}", x, p["ln1_g"])
    proj = _mm_act_w(f"proj_{l}", h1, wg["w_in"], True, _ep_store)[0]
    proj = hook("fwd_proj", proj)
    q = _to_heads(proj[:, :aw], nq)
    k = _to_heads(proj[:, aw:aw + kvw], nkv)
    v = _to_heads(proj[:, aw + kvw:aw + 2 * kvw], nkv)
    sinks_b = jnp.broadcast_to(p["sinks"][:, None, None], (nq, 1, 128))
    ya = _from_heads(_attn_fwd(f"attn_fwd_{l}", q, k, v, p["q_norm_g"], p["k_norm_g"], sinks_b))
    ya = hook("fwd_attn", ya)
    yc = _conv_fwd(f"conv_fwd_{l}", proj, 3, p["conv_w"], p["conv_b"], p["conv_ln_g"], p["conv_ln_b"])
    ys = _sgu_fwd(f"sgu_fwd_{l}", proj, 5, p["sgu_ln_g"], p["sgu_ln_b"], p["sgu_w"], p["sgu_bexp"])
    mix = _mixnorm_fwd(f"mixnorm_fwd_{l}", ya, yc, ys, p["out_norm_g"])
    mix = hook("fwd_mid", mix)
    if _whole_rows(wg["w_out"]):
        xm, h2 = _mm_act_w(f"out_{l}", mix, wg["w_out"], False, _ep_residual_norm, extra=(x,),
                           out_dtypes=(F32, BF16), row_vectors=(p["ln2_g"].reshape(1, d),))
    else:
        xm = _mm_act_w(f"out_{l}", mix, wg["w_out"], False, _ep_residual, extra=(x,))[0]
        h2 = _rms_fwd(f"ln2_fwd_{l}", xm, p["ln2_g"])
    h2 = hook("fwd_ln2", h2)
    up_b, act_b = _mm_act_w(f"up_{l}", h2, wg["w_up"], True, _ep_up, out_dtypes=(BF16, BF16))
    act_b = hook("fwd_up", act_b)
    xo = _mm_act_w(f"down_{l}", act_b, wg["w_down"], False, _ep_residual, extra=(xm,))[0]
    xo = hook("fwd_end", xo)
    saved = dict(x=x, h1=h1, proj=proj, q=q, k=k, v=v, sinks_b=sinks_b, ya=ya, yc=yc, ys=ys, mix=mix, xm=xm, h2=h2,
                 up_b=up_b, act_b=act_b)
    return xo, saved


def _layer_bwd(l, dxo, dxo_b, p, wg, sv, big, hook=_no_hook):
    d = dxo.shape[1]
    nq = (d // 2) // HEAD_DIM
    small = {}
    dxo_b = hook("bwd_start", dxo_b)
    big["w_down"] = _mm_wgrad(f"dw_down_{l}", sv["act_b"], dxo_b, False, d)
    dup_b = _mm_act_wt(f"dup_{l}", dxo_b, wg["w_down"], False, _ep_dup, extra=(sv["up_b"],), out_dtypes=(BF16,))[0]
    dup_b = hook("bwd_dup", dup_b)
    big["w_up"] = _mm_wgrad(f"dw_up_{l}", sv["h2"], dup_b, True, wg["w_up"].shape[2])
    dh2 = _mm_act_wt(f"dh2_{l}", dup_b, wg["w_up"], True, _ep_store)[0]
    dh2 = hook("bwd_dh2", dh2)
    dxm, dxm_b, small["ln2_g"] = _rms_bwd(f"ln2_bwd_{l}", dh2, sv["xm"], p["ln2_g"], dxo)
    big["w_out"] = _mm_wgrad(f"dw_out_{l}", sv["mix"], dxm_b, False, d)
    dmix = _mm_act_wt(f"dmix_{l}", dxm_b, wg["w_out"], False, _ep_store)[0]
    dya, dyc, dys, small["out_norm_g"] = _mixnorm_bwd(f"mixnorm_bwd_{l}", dmix, sv["ya"], sv["yc"], sv["ys"],
                                                      p["out_norm_g"])
    dya = hook("bwd_mix", dya)
    dq, dkc, dkp, dvc, dvp, small["q_norm_g"], dsink = _attn_bwd(
        f"attn_bwd_{l}", sv["q"], sv["k"], sv["v"], p["q_norm_g"], p["k_norm_g"], sv["sinks_b"], _to_heads(dya, nq))
    dkc = hook("bwd_attn", dkc)
    small["sinks"] = dsink[:, 0, 0]
    dk, dv, small["k_norm_g"] = _attn_bwd_kv(f"attn_bwd_kv_{l}", sv["k"], p["k_norm_g"], dkc, dkp, dvc, dvp)
    dc, dcw, small["conv_b"], small["conv_ln_g"], small["conv_ln_b"] = _conv_bwd1(
        f"conv_bwd1_{l}", sv["proj"], 3, p["conv_w"], p["conv_b"], p["conv_ln_g"], p["conv_ln_b"], dyc)
    small["conv_w"] = dcw[:CONV_KERNEL]
    dxc_b = _conv_bwd2(f"conv_bwd2_{l}", sv["proj"], 3, p["conv_w"], dc)
    dxs_b, small["sgu_w"], small["sgu_b"], small["sgu_ln_g"], small["sgu_ln_b"] = _sgu_bwd(
        f"sgu_bwd_{l}", sv["proj"], 5, p["sgu_ln_g"], p["sgu_ln_b"], p["sgu_w"], p["sgu_bexp"], dys)
    dxs_b = hook("bwd_sgu", dxs_b)
    dproj_b = jnp.concatenate([_from_heads(dq), _from_heads(dk), _from_heads(dv), dxc_b, dxs_b], axis=1)
    big["w_in"] = _mm_wgrad(f"dw_in_{l}", sv["h1"], dproj_b, True, wg["w_in"].shape[2])
    dh1 = _mm_act_wt(f"dh1_{l}", dproj_b, wg["w_in"], True, _ep_store)[0]
    dx, dx_b, small["ln1_g"] = _rms_bwd(f"ln1_bwd_{l}", dh1, sv["x"], p["ln1_g"], dxm)
    dx_b = hook("bwd_end", dx_b)
    return dx, dx_b, small


def kernel(x, ln1_g, w_in, q_norm_g, k_norm_g, sinks, conv_w, conv_b, conv_ln_g, conv_ln_b, sgu_ln_g, sgu_ln_b, sgu_w, sgu_b, out_norm_g, w_out, ln2_g, w_up, w_down, loss_target, m_ln1_g, m_w_in, m_q_norm_g, m_k_norm_g, m_sinks, m_conv_w, m_conv_b, m_conv_ln_g, m_conv_ln_b, m_sgu_ln_g, m_sgu_ln_b, m_sgu_w, m_sgu_b, m_out_norm_g, m_w_out, m_ln2_g, m_w_up, m_w_down, v_ln1_g, v_w_in, v_q_norm_g, v_k_norm_g, v_sinks, v_conv_w, v_conv_b, v_conv_ln_g, v_conv_ln_b, v_sgu_ln_g, v_sgu_ln_b, v_sgu_w, v_sgu_b, v_out_norm_g, v_w_out, v_ln2_g, v_w_up, v_w_down):
    given = dict(locals())
    _LAST[0] = None
    n_layers = ln1_g.shape[0]
    s, d = x.shape[1], x.shape[2]
    cw = d // 4
    xi, yi, core = lax.axis_index("x"), lax.axis_index("y"), lax.axis_index("c")
    chip = 2 * xi + yi
    first_partner = jnp.where(core == 0, 2 * (1 - xi) + yi, 2 * xi + (1 - yi))
    second_partner = jnp.where(core == 0, 2 * xi + (1 - yi), 2 * (1 - xi) + yi)
    route_idx = jnp.stack([core, chip, second_partner, first_partner, 3 - chip]).astype(jnp.int32)

    conv_w_pad = jnp.pad(conv_w, ((0, 0), (0, HALO - CONV_KERNEL), (0, 0))).reshape(1, n_layers * HALO, -1)
    cwl = conv_w_pad.shape[2]
    buf = {}

    def place(key):
        if key == "conv_w":
            buf[key] = _cast_place("place_conv_w", conv_w_pad, 0, route_idx[1:2], F32)
        else:
            buf[key] = _cast_place(f"place_{key[0]}_{key[1]}", given[key[0]], key[1], route_idx[1:2], BF16)

    groups = [["conv_w", ("w_in", 0)]] + [[(nm, l)] for l in range(n_layers) for nm in BIG if (nm, l) != ("w_in", 0)]
    n_steps = len(groups) + 2
    pending = {}

    def start_step(st):
        keys = [groups[st - j] if 0 <= st - j < len(groups) else [] for j in range(3)]
        flat, n_sems, build = _gather_step([[buf[k] for k in ks] for ks in keys])
        pending["keys"] = [k for ks in keys for k in ks]
        pending["h"] = _exchange_start(f"gather_step{st}", flat, n_sems, build)

    def wait_step():
        for k, b in zip(pending["keys"], _exchange_wait(pending["h"])):
            buf[k] = b

    later = [k for grp in groups[1:] for k in grp]
    for k in groups[0]:
        place(k)
    for st, upto in enumerate((2, 5, len(later))):
        start_step(st)
        for k in later[:upto]:
            if k not in buf:
                place(k)
        wait_step()
    conv_w_full = buf["conv_w"].reshape(N_CHIPS, n_layers, HALO, cwl).transpose(1, 2, 0, 3).reshape(
        n_layers, HALO, cw)

    class LayerWeights:
        def __init__(self, l):
            self.l = l

        def __getitem__(self, nm):
            return buf[(nm, self.l)]

    wgs = [LayerWeights(l) for l in range(n_layers)]
    fwd_points = [(l, pt) for l in range(n_layers) for pt in ("fwd_start", "fwd_attn", "fwd_ln2", "fwd_up", "fwd_end")
                  if (pt != "fwd_start" or l == 0) and (pt != "fwd_end" or l + 1 < n_layers)]
    assert len(fwd_points) == n_steps - 3 + 1, "one hook point per pipeline step, and one to wait for the last"
    fwd_tables = [{} for _ in range(n_layers)]
    for i, (l, pt) in enumerate(fwd_points):
        if i > 0:
            fwd_tables[l].setdefault(pt, []).append(wait_step)
        if 3 + i < n_steps:
            fwd_tables[l].setdefault(pt, []).append(functools.partial(start_step, 3 + i))
    params = []
    for l in range(n_layers):
        p = {nm: given[nm][l] for nm in SMALL if nm != "conv_w"}
        for nm in ("conv_b", "conv_ln_g", "conv_ln_b", "sgu_ln_g", "sgu_ln_b"):
            p[nm] = p[nm].reshape(1, -1)
        p["conv_w"] = conv_w_full[l]
        p["sgu_bexp"] = jnp.repeat(sgu_b[l].T, HEAD_DIM, axis=1)
        params.append(p)

    def make_hook(table):
        def hook(point, carry):
            for fn in table.get(point, ()):
                fn()
            return carry
        return hook

    h = x.reshape(s, d)
    saved = []
    for l in range(n_layers):
        h, sv = _layer_fwd(l, h, params[l], wgs[l], make_hook(fwd_tables[l]))
        saved.append(sv)
    dh, dh_b, loss_part = _loss_head(h, loss_target.reshape(s, d))
    loss = lax.psum(loss_part[0, 0], ("x", "y", "c"))

    big_grads = [{} for _ in range(n_layers)]
    small_grads = [None] * n_layers
    halves = {}

    def rs_group(tag, l, names):
        phases = {}

        def start():
            phases["p"] = _reduce_scatter(tag, [f"{nm}_{l}" for nm in names], [big_grads[l][nm] for nm in names],
                                          route_idx)
            phases["p"][0]()

        def step(k):
            return lambda: phases["p"][k]()

        def finish():
            for nm, fh in zip(names, phases["p"][4]()):
                halves[(nm, l)] = fh

        return [start, step(1), step(2), step(3), finish]

    early = rs_group("l0a", 0, ("w_down", "w_up", "w_out"))
    for l in reversed(range(n_layers)):
        table = {}
        if l + 1 < n_layers:
            above = rs_group(f"l{l + 1}", l + 1, BIG)
            for point, fn in zip(("bwd_start", "bwd_dup", "bwd_mix", "bwd_attn", "bwd_sgu"), above):
                table.setdefault(point, []).append(fn)
        if l == 0:
            for point, fn in zip(("bwd_mix", "bwd_attn", "bwd_end"), early[:3]):
                table.setdefault(point, []).append(fn)
        dh, dh_b, small_grads[l] = _layer_bwd(l, dh, dh_b, params[l], wgs[l], saved[l], big_grads[l],
                                              make_hook(table))
    grad_x = dh.reshape(x.shape)

    grads, delta, new_m, new_v = {}, {}, {}, {}
    adam_state = {nm: () for nm in BIG}

    def adam(nm, l):
        f, h = halves[(nm, l)]
        adam_state[nm] = _adamw_big(f"adamw_{nm}_{l}", given[nm], given["m_" + nm], given["v_" + nm], f, h, l,
                                    route_idx[0:1], adam_state[nm])

    def small_update():
        small_shapes = [(n_layers,) + small_grads[0][nm].shape for nm in SMALL]
        small_sum = _small_allreduce(_pack([jnp.stack([small_grads[l][nm] for l in range(n_layers)])
                                            for nm in SMALL]))
        for nm, g in zip(SMALL, _unpack(small_sum, small_shapes)):
            grads[nm] = g.reshape((n_layers,) + given[nm].shape[1:]) if nm != "conv_w" else g
        grads["conv_w"] = lax.dynamic_slice_in_dim(grads["conv_w"], chip * cwl, cwl, axis=2)
        results = _adamw_many("adamw_small", [given[nm] for nm in SMALL], [grads[nm] for nm in SMALL],
                              [given["m_" + nm] for nm in SMALL], [given["v_" + nm] for nm in SMALL])
        for dst, arrs in zip((delta, new_m, new_v), results):
            dst.update(zip(SMALL, arrs))

    upper = [(nm, l) for l in reversed(range(1, n_layers)) for nm in reversed(BIG)]
    late = rs_group("l0b", 0, ("w_in",))
    late[0]()
    for task in upper[:1]:
        adam(*task)
    late[1]()
    for task in upper[1:]:
        adam(*task)
    early[3]()
    late[2]()
    small_update()
    early[4]()
    for nm in ("w_down", "w_up"):
        adam(nm, 0)
    late[3]()
    adam("w_out", 0)
    late[4]()
    adam("w_in", 0)
    for nm in BIG:
        grads[nm], delta[nm], new_m[nm], new_v[nm] = adam_state[nm]
    return (loss, grad_x, *[grads[nm] for nm in WEIGHTS], *[delta[nm] for nm in WEIGHTS],
            *[new_m[nm] for nm in WEIGHTS], *[new_v[nm] for nm in WEIGHTS])
```

```python
import functools

import jax
import jax.numpy as jnp
from jax import lax
from jax.experimental import pallas as pl
from jax.experimental.pallas import tpu as pltpu

F32 = jnp.float32
BF16 = jnp.bfloat16
EPS = 1e-6
NEG_INF = -1e30
HEAD_DIM = 64
WINDOW = 128
CONV_KERNEL = 31
HALO = 32
GQA = 4
N_CHIPS = 4
ADAM_LR, ADAM_B1, ADAM_B2, ADAM_EPS, ADAM_WD, ADAM_STEP = 0.001, 0.9, 0.999, 1e-08, 0.01, 10
VMEM_LIMIT = 56 * 1024 * 1024
TILE_K = 2048
MESH = pl.DeviceIdType.MESH
ANY = pl.BlockSpec(memory_space=pl.ANY)

NN = (((1,), (0,)), ((), ()))
NT = (((1,), (1,)), ((), ()))
TN = (((0,), (0,)), ((), ()))


def _cp(*sem):
    return pltpu.CompilerParams(dimension_semantics=sem, vmem_limit_bytes=VMEM_LIMIT)


_LAST = [None]
TOKEN = jax.ShapeDtypeStruct((8, 128), F32)


def _ordered_call(body, *, out_shape, out_specs=None, in_specs=None, grid_spec=None, grid=None, **kw):
    single = not isinstance(out_shape, (tuple, list))
    shapes = (out_shape,) if single else tuple(out_shape)

    def run(*operands):
        dep = _LAST[0]
        n = len(operands)
        n_dep = 0 if dep is None else 1

        def fn(*refs):
            outs = refs[n + n_dep:n + n_dep + len(shapes)]
            token = refs[n + n_dep + len(shapes)]
            body(*refs[:n], *outs, *refs[n + n_dep + len(shapes) + 1:])
            token[...] = jnp.zeros_like(token)

        specs_in = list(grid_spec.in_specs if grid_spec is not None else in_specs) + [ANY] * n_dep
        specs_out = grid_spec.out_specs if grid_spec is not None else out_specs
        specs_out = tuple(specs_out) if isinstance(specs_out, (tuple, list)) else (specs_out,)
        if grid_spec is not None or grid:
            specs_out += (pl.BlockSpec(TOKEN.shape, lambda *_: (0, 0)),)
        else:
            specs_out += (pl.BlockSpec(memory_space=pltpu.VMEM),)
        args = operands + ((dep,) if n_dep else ())
        if grid_spec is not None:
            spec = pltpu.PrefetchScalarGridSpec(num_scalar_prefetch=grid_spec.num_scalar_prefetch, grid=grid_spec.grid,
                                                in_specs=specs_in, out_specs=specs_out)
            out = pl.pallas_call(fn, grid_spec=spec, out_shape=shapes + (TOKEN,), **kw)(*args)
        else:
            if grid:
                kw["grid"] = grid
            out = pl.pallas_call(fn, in_specs=specs_in, out_specs=specs_out, out_shape=shapes + (TOKEN,), **kw)(*args)
        _LAST[0] = out[-1]
        return out[0] if single else tuple(out[:-1])

    return run


def _tile(dim, pref):
    if dim <= pref:
        return dim
    for t in range(pref, 0, -128):
        if dim % t == 0:
            return t
    while dim % pref:
        pref //= 2
    return pref


def _dot(a, b, dims=NN):
    return lax.dot_general(a, b, dims, preferred_element_type=F32)


def _colsum(v):
    return jnp.sum(v, axis=0, keepdims=True)


def _sigmoid(x):
    return 1.0 / (1.0 + jnp.exp(-x))


def _matmul(name, operands, in_specs, out_shape, out_specs, grid, dims, acc_shape, epilogue, split_k=False):
    nk = grid[2]
    n_in = len(operands)

    def product(a_ref, b_ref):
        if split_k:
            ck = b_ref.shape[2]
            out = _dot(a_ref[:, 0:ck], b_ref[0], dims)
            for j in range(1, N_CHIPS):
                out = out + _dot(a_ref[:, j * ck:(j + 1) * ck], b_ref[j], dims)
            return out
        bv = b_ref[...]
        return _dot(a_ref[...], bv.reshape(-1, bv.shape[-1]) if bv.ndim == 3 else bv, dims)

    def body(*refs):
        a_ref, b_ref = refs[0], refs[1]
        extra = refs[2:n_in]
        if nk == 1:
            epilogue(product(a_ref, b_ref), extra, refs[n_in:])
            return
        outs = refs[n_in:-1]
        acc = refs[-1]
        k = pl.program_id(2)

        @pl.when(k == 0)
        def _():
            acc[...] = product(a_ref, b_ref)

        @pl.when((k > 0) & (k < nk - 1))
        def _():
            acc[...] += product(a_ref, b_ref)

        @pl.when(k == nk - 1)
        def _():
            epilogue(acc[...] + product(a_ref, b_ref), extra, outs)

    return _ordered_call(
        body, name=name, grid=grid, in_specs=in_specs, out_specs=out_specs, out_shape=out_shape,
        scratch_shapes=[pltpu.VMEM(acc_shape, F32)] if nk > 1 else [],
        compiler_params=_cp("parallel", "parallel", "arbitrary"),
    )(*operands)


def _ep_store(acc, extra, outs):
    outs[0][...] = acc.astype(outs[0].dtype)


def _ep_residual(acc, extra, outs):
    outs[0][...] = extra[0][...] + acc


def _ep_residual_norm(acc, extra, outs):
    xm = extra[0][...] + acc
    outs[0][...] = xm
    r = lax.rsqrt(jnp.mean(xm * xm, axis=-1, keepdims=True) + EPS)
    outs[1][...] = (xm * r * extra[1][...]).astype(BF16)


def _ep_up(acc, extra, outs):
    outs[0][...] = acc.astype(BF16)
    r = jnp.maximum(acc, 0.0)
    outs[1][...] = (r * r).astype(BF16)


def _ep_dup(acc, extra, outs):
    outs[0][...] = (acc * (2.0 * jnp.maximum(extra[0][...].astype(F32), 0.0))).astype(BF16)


def _whole_rows(wg):
    return N_CHIPS * wg.shape[1] <= TILE_K


def _mm_act_w(name, a, wg, col_sharded, epilogue, extra=(), out_dtypes=(F32,), row_vectors=()):
    m, kdim = a.shape
    _, r, c = wg.shape
    tm = _tile(m, 1024)
    if col_sharded:
        n = N_CHIPS * c
        tn = _tile(c, 1024)
        tk = _tile(kdim, TILE_K)
        per = c // tn
        b_spec = pl.BlockSpec((None, tk, tn), lambda i, j, k: (j // per, k, j % per))
    elif _whole_rows(wg):
        n = c
        tm, tn, tk = _tile(m, 512), n, kdim
        b_spec = pl.BlockSpec((N_CHIPS, r, tn), lambda i, j, k: (0, 0, j))
    else:
        n = c
        tn = _tile(n, 1024)
        tk = _tile(r, TILE_K)
        per = r // tk
        b_spec = pl.BlockSpec((None, tk, tn), lambda i, j, k: (k // per, k % per, j))
    grid = (m // tm, n // tn, kdim // tk)
    o_spec = pl.BlockSpec((tm, tn), lambda i, j, k: (i, j))
    in_specs = ([pl.BlockSpec((tm, tk), lambda i, j, k: (i, k)), b_spec] + [o_spec] * len(extra)
                + [pl.BlockSpec((1, tn), lambda i, j, k: (0, j))] * len(row_vectors))
    return _matmul(name, (a, wg) + tuple(extra) + tuple(row_vectors), in_specs,
                   tuple(jax.ShapeDtypeStruct((m, n), d) for d in out_dtypes),
                   tuple(o_spec for _ in out_dtypes), grid, NN, (tm, tn), epilogue)


def _mm_act_wt(name, a, wg, col_sharded, epilogue, extra=(), out_dtypes=(F32,)):
    m, kdim = a.shape
    _, r, c = wg.shape
    tm = _tile(m, 1024)
    split_k = False
    if col_sharded and N_CHIPS * c <= 2 * TILE_K:
        n = r
        tm, tn, tk, split_k = _tile(m, 512), n, kdim, True
        b_spec = pl.BlockSpec((N_CHIPS, tn, c), lambda i, j, k: (0, j, 0))
    elif col_sharded:
        n = r
        tn = _tile(n, 1024)
        tk = _tile(c, TILE_K)
        per = c // tk
        b_spec = pl.BlockSpec((None, tn, tk), lambda i, j, k: (k // per, j, k % per))
    elif N_CHIPS * r <= TILE_K:
        n = N_CHIPS * r
        tm, tn, tk = _tile(m, 512), n, _tile(c, TILE_K)
        b_spec = pl.BlockSpec((N_CHIPS, r, tk), lambda i, j, k: (0, 0, k))
    else:
        n = N_CHIPS * r
        tn = _tile(r, 1024)
        tk = _tile(c, TILE_K)
        per = r // tn
        b_spec = pl.BlockSpec((None, tn, tk), lambda i, j, k: (j // per, j % per, k))
    grid = (m // tm, n // tn, kdim // tk)
    o_spec = pl.BlockSpec((tm, tn), lambda i, j, k: (i, j))
    in_specs = [pl.BlockSpec((tm, tk), lambda i, j, k: (i, k)), b_spec] + [o_spec] * len(extra)
    return _matmul(name, (a, wg) + tuple(extra), in_specs,
                   tuple(jax.ShapeDtypeStruct((m, n), d) for d in out_dtypes),
                   tuple(o_spec for _ in out_dtypes), grid, NT, (tm, tn), epilogue, split_k)


def _mm_wgrad(name, a, g, col_sharded, c):
    s, kdim = a.shape
    _, n = g.shape
    ts = _tile(s, TILE_K)
    if col_sharded:
        r = kdim
        tm = _tile(kdim, 1024)
        tn = _tile(c, 1024)
        per = c // tn
        o_spec = pl.BlockSpec((None, tm, tn), lambda i, j, k: (j // per, i, j % per))
    else:
        r = kdim // N_CHIPS
        tm = _tile(r, 512)
        tn = _tile(c, 2048)
        per = r // tm
        o_spec = pl.BlockSpec((None, tm, tn), lambda i, j, k: (i // per, i % per, j))
    grid = (kdim // tm, n // tn, s // ts)
    in_specs = [pl.BlockSpec((ts, tm), lambda i, j, k: (k, i)), pl.BlockSpec((ts, tn), lambda i, j, k: (k, j))]
    return _matmul(name, (a, g), in_specs, (jax.ShapeDtypeStruct((N_CHIPS, r, c), BF16),), (o_spec,),
                   grid, TN, (tm, tn), _ep_store)[0]


def _rms_fwd(name, x, g):
    s, d = x.shape
    tb = _tile(s, 256)

    def body(x_ref, g_ref, o_ref):
        xv = x_ref[...]
        r = lax.rsqrt(jnp.mean(xv * xv, axis=-1, keepdims=True) + EPS)
        o_ref[...] = (xv * r * g_ref[...]).astype(BF16)

    return _ordered_call(
        body, name=name, grid=(s // tb,),
        in_specs=[pl.BlockSpec((tb, d), lambda i: (i, 0)), pl.BlockSpec((1, d), lambda i: (0, 0))],
        out_specs=pl.BlockSpec((tb, d), lambda i: (i, 0)),
        out_shape=jax.ShapeDtypeStruct((s, d), BF16), compiler_params=_cp("parallel"),
    )(x, g.reshape(1, d))


def _rms_bwd(name, dh, x, g, dres):
    s, d = x.shape
    tb = _tile(s, 256)

    def body(dh_ref, x_ref, g_ref, dres_ref, dx_ref, dxb_ref, dg_ref):
        i = pl.program_id(0)
        xv = x_ref[...]
        r = lax.rsqrt(jnp.mean(xv * xv, axis=-1, keepdims=True) + EPS)
        xhat = xv * r
        dhv = dh_ref[...]
        dxhat = dhv * g_ref[...]
        dx = dres_ref[...] + r * (dxhat - xhat * jnp.mean(dxhat * xhat, axis=-1, keepdims=True))
        dx_ref[...] = dx
        dxb_ref[...] = dx.astype(BF16)

        @pl.when(i == 0)
        def _():
            dg_ref[...] = jnp.zeros_like(dg_ref)

        dg_ref[...] += _colsum(dhv * xhat)

    row = pl.BlockSpec((tb, d), lambda i: (i, 0))
    vec = pl.BlockSpec((1, d), lambda i: (0, 0))
    return _ordered_call(
        body, name=name, grid=(s // tb,), in_specs=[row, row, vec, row], out_specs=(row, row, vec),
        out_shape=(jax.ShapeDtypeStruct((s, d), F32), jax.ShapeDtypeStruct((s, d), BF16),
                   jax.ShapeDtypeStruct((1, d), F32)),
        compiler_params=_cp("arbitrary"),
    )(dh, x, g.reshape(1, d), dres)


def _loss_head(y, t):
    s, d = y.shape
    tb = _tile(s, 256)

    def body(y_ref, t_ref, dy_ref, dyb_ref, loss_ref, acc):
        i = pl.program_id(0)
        e = y_ref[...] - t_ref[...]
        dy = e * (1.0 / d)
        dy_ref[...] = dy
        dyb_ref[...] = dy.astype(BF16)

        @pl.when(i == 0)
        def _():
            acc[...] = jnp.zeros_like(acc)

        acc[...] += _colsum(e * e)

        @pl.when(i == pl.num_programs(0) - 1)
        def _():
            loss_ref[...] = jnp.sum(acc[...], axis=-1, keepdims=True) * (0.5 / d)

    row = pl.BlockSpec((tb, d), lambda i: (i, 0))
    return _ordered_call(
        body, name="loss_head", grid=(s // tb,), in_specs=[row, row],
        out_specs=(row, row, pl.BlockSpec((1, 1), lambda i: (0, 0))),
        out_shape=(jax.ShapeDtypeStruct((s, d), F32), jax.ShapeDtypeStruct((s, d), BF16),
                   jax.ShapeDtypeStruct((1, 1), F32)),
        scratch_shapes=[pltpu.VMEM((1, d), F32)], compiler_params=_cp("arbitrary"),
    )(y, t)


def _mixnorm_fwd(name, ya, yc, ys, g):
    s, aw = ya.shape
    cw, sw = yc.shape[1], ys.shape[1]
    d = aw + cw + sw
    tb = _tile(s, 256)

    def body(ya_ref, yc_ref, ys_ref, g_ref, o_ref):
        off = 0
        for ref, w in ((ya_ref, aw), (yc_ref, cw), (ys_ref, sw)):
            v = ref[...]
            r = lax.rsqrt(jnp.mean(v * v, axis=-1, keepdims=True) + EPS)
            o_ref[:, off:off + w] = (v * r * g_ref[:, off:off + w]).astype(BF16)
            off += w

    def row(w):
        return pl.BlockSpec((tb, w), lambda i: (i, 0))

    return _ordered_call(
        body, name=name, grid=(s // tb,),
        in_specs=[row(aw), row(cw), row(sw), pl.BlockSpec((1, d), lambda i: (0, 0))], out_specs=row(d),
        out_shape=jax.ShapeDtypeStruct((s, d), BF16), compiler_params=_cp("parallel"),
    )(ya, yc, ys, g.reshape(1, d))


def _mixnorm_bwd(name, dmix, ya, yc, ys, g):
    s, aw = ya.shape
    cw, sw = yc.shape[1], ys.shape[1]
    d = aw + cw + sw
    tb = _tile(s, 256)

    def body(dm_ref, ya_ref, yc_ref, ys_ref, g_ref, dya_ref, dyc_ref, dys_ref, dg_ref):
        i = pl.program_id(0)

        @pl.when(i == 0)
        def _():
            dg_ref[...] = jnp.zeros_like(dg_ref)

        off = 0
        for ref, dref, w in ((ya_ref, dya_ref, aw), (yc_ref, dyc_ref, cw), (ys_ref, dys_ref, sw)):
            v = ref[...]
            r = lax.rsqrt(jnp.mean(v * v, axis=-1, keepdims=True) + EPS)
            vhat = v * r
            dm = dm_ref[:, off:off + w]
            dvhat = dm * g_ref[:, off:off + w]
            dref[...] = r * (dvhat - vhat * jnp.mean(dvhat * vhat, axis=-1, keepdims=True))
            dg_ref[:, off:off + w] += _colsum(dm * vhat)
            off += w

    def row(w):
        return pl.BlockSpec((tb, w), lambda i: (i, 0))

    vec = pl.BlockSpec((1, d), lambda i: (0, 0))
    return _ordered_call(
        body, name=name, grid=(s // tb,), in_specs=[row(d), row(aw), row(cw), row(sw), vec],
        out_specs=(row(aw), row(cw), row(sw), vec),
        out_shape=(jax.ShapeDtypeStruct((s, aw), F32), jax.ShapeDtypeStruct((s, cw), F32),
                   jax.ShapeDtypeStruct((s, sw), F32), jax.ShapeDtypeStruct((1, d), F32)),
        compiler_params=_cp("arbitrary"),
    )(dmix, ya, yc, ys, g.reshape(1, d))


def _head_rms(x):
    r = lax.rsqrt(jnp.mean(x * x, axis=-1, keepdims=True) + EPS)
    return x * r, r


def _attn_mask(n):
    qi = lax.broadcasted_iota(jnp.int32, (GQA * WINDOW, 2 * WINDOW), 0) & (WINDOW - 1)
    sj = lax.broadcasted_iota(jnp.int32, (GQA * WINDOW, 2 * WINDOW), 1)
    rel = qi + WINDOW - sj
    return (rel >= 0) & (rel < WINDOW) & ((sj >= WINDOW) | (n > 0))


def _attn_specs(nq, nkv, nb):
    qspec = pl.BlockSpec((nq, WINDOW, HEAD_DIM), lambda n: (0, n, 0))
    cur = pl.BlockSpec((nkv, WINDOW, HEAD_DIM), lambda n: (0, n, 0))
    prev = pl.BlockSpec((nkv, WINDOW, HEAD_DIM), lambda n: (0, jnp.maximum(n - 1, 0), 0))
    nxt = pl.BlockSpec((nkv, WINDOW, HEAD_DIM), lambda n: (0, jnp.minimum(n + 1, nb - 1), 0))
    gain = pl.BlockSpec((1, HEAD_DIM), lambda n: (0, 0))
    sink = pl.BlockSpec((nq, 1, 128), lambda n: (0, 0, 0))
    return qspec, cur, prev, nxt, gain, sink


def _group_sinks(s_ref, g):
    return jnp.concatenate([jnp.broadcast_to(s_ref[g * GQA + i][:, :1], (WINDOW, 1)) for i in range(GQA)], axis=0)


ATTN_SCALE = HEAD_DIM ** -0.5
assert ATTN_SCALE == 2.0 ** -3


def _attn_probs(qs_b, kn_b, valid, sink):
    logits = jnp.where(valid, _dot(qs_b, kn_b, NT), NEG_INF)
    m = jnp.maximum(jnp.max(logits, axis=-1, keepdims=True), sink)
    p = jnp.exp(logits - m)
    es = jnp.exp(sink - m)
    inv = 1.0 / (jnp.sum(p, axis=-1, keepdims=True) + es)
    return p * inv, es * inv


def _attn_fwd(name, q, k, v, gq, gk, sinks_b):
    nq, s, _ = q.shape
    nkv = k.shape[0]
    nb = s // WINDOW
    qspec, cur, prev, _, gain, sink = _attn_specs(nq, nkv, nb)

    def body(q_ref, kc_ref, kp_ref, vc_ref, vp_ref, gq_ref, gk_ref, s_ref, o_ref):
        gkv = gk_ref[...]
        valid = _attn_mask(pl.program_id(0))
        for g in range(nkv):
            kn_b = jnp.concatenate([_head_rms(kp_ref[g])[0] * gkv, _head_rms(kc_ref[g])[0] * gkv],
                                   axis=0).astype(BF16)
            vv_b = jnp.concatenate([vp_ref[g], vc_ref[g]], axis=0).astype(BF16)
            heads = pl.ds(g * GQA, GQA)
            q4 = q_ref[heads].reshape(GQA * WINDOW, HEAD_DIM)
            qs_b = (_head_rms(q4)[0] * gq_ref[...] * ATTN_SCALE).astype(BF16)
            probs, _ = _attn_probs(qs_b, kn_b, valid, _group_sinks(s_ref, g))
            o_ref[heads] = _dot(probs.astype(BF16), vv_b).reshape(GQA, WINDOW, HEAD_DIM)

    return _ordered_call(
        body, name=name, grid=(nb,), in_specs=[qspec, cur, prev, cur, prev, gain, gain, sink], out_specs=qspec,
        out_shape=jax.ShapeDtypeStruct((nq, s, HEAD_DIM), F32), compiler_params=_cp("parallel"),
    )(q, k, k, v, v, gq.reshape(1, HEAD_DIM), gk.reshape(1, HEAD_DIM), sinks_b)


def _attn_bwd(name, q, k, v, gq, gk, sinks_b, do):
    nq, s, _ = q.shape
    nkv = k.shape[0]
    nb = s // WINDOW
    qspec, cur, prev, _, gain, sink = _attn_specs(nq, nkv, nb)

    def body(q_ref, kc_ref, kp_ref, vc_ref, vp_ref, gq_ref, gk_ref, s_ref, do_ref,
             dq_ref, dkc_ref, dkp_ref, dvc_ref, dvp_ref, dgq_ref, ds_ref):
        n = pl.program_id(0)

        @pl.when(n == 0)
        def _():
            dgq_ref[...] = jnp.zeros_like(dgq_ref)
            ds_ref[...] = jnp.zeros_like(ds_ref)

        gkv = gk_ref[...]
        gqv = gq_ref[...]
        valid = _attn_mask(n)
        dgq = jnp.zeros((1, HEAD_DIM), F32)
        for g in range(nkv):
            kn_b = jnp.concatenate([_head_rms(kp_ref[g])[0] * gkv, _head_rms(kc_ref[g])[0] * gkv],
                                   axis=0).astype(BF16)
            vv_b = jnp.concatenate([vp_ref[g], vc_ref[g]], axis=0).astype(BF16)
            heads = pl.ds(g * GQA, GQA)
            qhat, r = _head_rms(q_ref[heads].reshape(GQA * WINDOW, HEAD_DIM))
            qs = qhat * gqv * ATTN_SCALE
            qs_b = qs.astype(BF16)
            probs, psink = _attn_probs(qs_b, kn_b, valid, _group_sinks(s_ref, g))
            do = do_ref[heads].reshape(GQA * WINDOW, HEAD_DIM)
            do_b = do.astype(BF16)
            dp = _dot(do_b, vv_b, NT)
            delta = jnp.sum(probs * dp, axis=-1, keepdims=True)
            dl_b = (probs * (dp - delta)).astype(BF16)
            sink_term = psink * delta
            for i in range(GQA):
                ds_ref[g * GQA + i] += jnp.broadcast_to(
                    -jnp.sum(sink_term[i * WINDOW:(i + 1) * WINDOW], axis=0, keepdims=True), (1, 128))
            dqn = _dot(dl_b, kn_b) * ATTN_SCALE
            dkn = _dot(qs.T.astype(BF16), dl_b).T
            dvv = _dot(do.T.astype(BF16), probs.astype(BF16)).T
            dgq += _colsum(dqn * qhat)
            dqhat = dqn * gqv
            dq_ref[heads] = (r * (dqhat - qhat * jnp.mean(dqhat * qhat, axis=-1, keepdims=True))).astype(
                BF16).reshape(GQA, WINDOW, HEAD_DIM)
            dkp_ref[g] = dkn[:WINDOW]
            dkc_ref[g] = dkn[WINDOW:]
            dvp_ref[g] = dvv[:WINDOW]
            dvc_ref[g] = dvv[WINDOW:]
        dgq_ref[...] += dgq

    kv_shape = jax.ShapeDtypeStruct((nkv, s, HEAD_DIM), F32)
    return _ordered_call(
        body, name=name, grid=(nb,), in_specs=[qspec, cur, prev, cur, prev, gain, gain, sink, qspec],
        out_specs=(qspec, cur, cur, cur, cur, gain, sink),
        out_shape=(jax.ShapeDtypeStruct((nq, s, HEAD_DIM), BF16), kv_shape, kv_shape, kv_shape, kv_shape,
                   jax.ShapeDtypeStruct((1, HEAD_DIM), F32), jax.ShapeDtypeStruct((nq, 1, 128), F32)),
        compiler_params=_cp("arbitrary"),
    )(q, k, k, v, v, gq.reshape(1, HEAD_DIM), gk.reshape(1, HEAD_DIM), sinks_b, do)


def _attn_bwd_kv(name, k, gk, dkc, dkp, dvc, dvp):
    nkv, s, _ = k.shape
    nb = s // WINDOW
    _, cur, _, nxt, gain, _ = _attn_specs(GQA * nkv, nkv, nb)

    def body(k_ref, gk_ref, dkc_ref, dkp_ref, dvc_ref, dvp_ref, dk_ref, dv_ref, dgk_ref):
        n = pl.program_id(0)

        @pl.when(n == 0)
        def _():
            dgk_ref[...] = jnp.zeros_like(dgk_ref)

        has_next = n < nb - 1
        dgk = jnp.zeros((1, HEAD_DIM), F32)
        for g in range(nkv):
            dkn = dkc_ref[g] + jnp.where(has_next, dkp_ref[g], 0.0)
            dv_ref[g] = (dvc_ref[g] + jnp.where(has_next, dvp_ref[g], 0.0)).astype(BF16)
            khat, r = _head_rms(k_ref[g])
            dgk += _colsum(dkn * khat)
            dkhat = dkn * gk_ref[...]
            dk_ref[g] = (r * (dkhat - khat * jnp.mean(dkhat * khat, axis=-1, keepdims=True))).astype(BF16)
        dgk_ref[...] += dgk

    kv_shape = jax.ShapeDtypeStruct((nkv, s, HEAD_DIM), BF16)
    return _ordered_call(
        body, name=name, grid=(nb,), in_specs=[cur, gain, cur, nxt, cur, nxt], out_specs=(cur, cur, gain),
        out_shape=(kv_shape, kv_shape, jax.ShapeDtypeStruct((1, HEAD_DIM), F32)),
        compiler_params=_cp("arbitrary"),
    )(k, gk.reshape(1, HEAD_DIM), dkc, dkp, dvc, dvp)


SUBLANES = 8


def _fill_shifts(buf, shifts, tb):
    rows = tb + HALO - SUBLANES
    for b in range(1, SUBLANES):
        shifts[b - 1, pl.ds(0, rows), :] = buf[pl.ds(b, rows), :]


def _window(buf, shifts, off, tb):
    b = off % SUBLANES
    return buf[pl.ds(off, tb), :] if b == 0 else shifts[b - 1, pl.ds(off - b, tb), :]


def _conv_recompute(i, a_ref, gt_ref, ap_ref, gp_ref, w_ref, b_ref, hbuf, shifts, tb):
    hbuf[pl.ds(HALO, tb), :] = a_ref[...] * _sigmoid(gt_ref[...])
    tail = ap_ref[pl.ds(tb - HALO, HALO), :] * _sigmoid(gp_ref[pl.ds(tb - HALO, HALO), :])
    hbuf[pl.ds(0, HALO), :] = jnp.where(i > 0, tail, 0.0)
    _fill_shifts(hbuf, shifts, tb)
    acc = jnp.broadcast_to(b_ref[...], a_ref.shape)
    for kk in range(CONV_KERNEL):
        acc = acc + w_ref[pl.ds(kk, 1), :] * _window(hbuf, shifts, HALO - (CONV_KERNEL - 1) + kk, tb)
    return acc


def _layer_norm_stats(c):
    mu = jnp.mean(c, axis=-1, keepdims=True)
    xc = c - mu
    r = lax.rsqrt(jnp.mean(xc * xc, axis=-1, keepdims=True) + EPS)
    return xc * r, r


def _conv_specs(s, cw, tb, a_blk):
    cur = lambda off: pl.BlockSpec((tb, cw), lambda i: (i, a_blk + off))
    prev = lambda off: pl.BlockSpec((tb, cw), lambda i: (jnp.maximum(i - 1, 0), a_blk + off))
    wspec = pl.BlockSpec((HALO, cw), lambda i: (0, 0))
    vec = pl.BlockSpec((1, cw), lambda i: (0, 0))
    row = pl.BlockSpec((tb, cw), lambda i: (i, 0))
    return cur, prev, wspec, vec, row


def _conv_fwd(name, proj, a_blk, w, b, lg, lb):
    s = proj.shape[0]
    cw = w.shape[1]
    tb = _tile(s, 256)
    cur, prev, wspec, vec, row = _conv_specs(s, cw, tb, a_blk)

    def body(a_ref, gt_ref, ap_ref, gp_ref, w_ref, b_ref, lg_ref, lb_ref, y_ref, hbuf, shifts):
        c = _conv_recompute(pl.program_id(0), a_ref, gt_ref, ap_ref, gp_ref, w_ref, b_ref, hbuf, shifts, tb)
        chat, _ = _layer_norm_stats(c)
        z = chat * lg_ref[...] + lb_ref[...]
        y_ref[...] = z * _sigmoid(z)

    return _ordered_call(
        body, name=name, grid=(s // tb,), in_specs=[cur(0), cur(1), prev(0), prev(1), wspec, vec, vec, vec],
        out_specs=row, out_shape=jax.ShapeDtypeStruct((s, cw), F32),
        scratch_shapes=[pltpu.VMEM((tb + HALO, cw), F32), pltpu.VMEM((SUBLANES - 1, tb + HALO, cw), F32)],
        compiler_params=_cp("arbitrary"),
    )(proj, proj, proj, proj, w, b, lg, lb)


def _conv_bwd1(name, proj, a_blk, w, b, lg, lb, dy):
    s = proj.shape[0]
    cw = w.shape[1]
    tb = _tile(s, 256)
    cur, prev, wspec, vec, row = _conv_specs(s, cw, tb, a_blk)

    def body(a_ref, gt_ref, ap_ref, gp_ref, w_ref, b_ref, lg_ref, lb_ref, dy_ref,
             dc_ref, dw_ref, db_ref, dlg_ref, dlb_ref, hbuf, shifts):
        i = pl.program_id(0)

        @pl.when(i == 0)
        def _():
            dw_ref[...] = jnp.zeros_like(dw_ref)
            db_ref[...] = jnp.zeros_like(db_ref)
            dlg_ref[...] = jnp.zeros_like(dlg_ref)
            dlb_ref[...] = jnp.zeros_like(dlb_ref)

        c = _conv_recompute(i, a_ref, gt_ref, ap_ref, gp_ref, w_ref, b_ref, hbuf, shifts, tb)
        chat, r = _layer_norm_stats(c)
        z = chat * lg_ref[...] + lb_ref[...]
        sg = _sigmoid(z)
        dz = dy_ref[...] * (sg + z * sg * (1.0 - sg))
        dlg_ref[...] += _colsum(dz * chat)
        dlb_ref[...] += _colsum(dz)
        dzg = dz * lg_ref[...]
        dc = r * (dzg - jnp.mean(dzg, axis=-1, keepdims=True) - chat * jnp.mean(dzg * chat, axis=-1, keepdims=True))
        dc_ref[...] = dc
        db_ref[...] += _colsum(dc)
        for kk in range(CONV_KERNEL):
            dw_ref[pl.ds(kk, 1), :] += _colsum(dc * _window(hbuf, shifts, HALO - (CONV_KERNEL - 1) + kk, tb))

    return _ordered_call(
        body, name=name, grid=(s // tb,), in_specs=[cur(0), cur(1), prev(0), prev(1), wspec, vec, vec, vec, row],
        out_specs=(row, wspec, vec, vec, vec),
        out_shape=(jax.ShapeDtypeStruct((s, cw), F32), jax.ShapeDtypeStruct((HALO, cw), F32),
                   jax.ShapeDtypeStruct((1, cw), F32), jax.ShapeDtypeStruct((1, cw), F32),
                   jax.ShapeDtypeStruct((1, cw), F32)),
        scratch_shapes=[pltpu.VMEM((tb + HALO, cw), F32), pltpu.VMEM((SUBLANES - 1, tb + HALO, cw), F32)],
        compiler_params=_cp("arbitrary"),
    )(proj, proj, proj, proj, w, b, lg, lb, dy)


def _conv_bwd2(name, proj, a_blk, w, dc):
    s = proj.shape[0]
    cw = w.shape[1]
    tb = _tile(s, 256)
    nblk = s // tb
    cur, _, wspec, _, row = _conv_specs(s, cw, tb, a_blk)
    nxt = pl.BlockSpec((tb, cw), lambda i: (jnp.minimum(i + 1, nblk - 1), 0))

    def body(a_ref, gt_ref, w_ref, dc_ref, dn_ref, o_ref, dbuf, shifts):
        i = pl.program_id(0)
        dbuf[pl.ds(0, tb), :] = dc_ref[...]
        dbuf[pl.ds(tb, HALO), :] = jnp.where(i < nblk - 1, dn_ref[pl.ds(0, HALO), :], 0.0)
        _fill_shifts(dbuf, shifts, tb)
        dh = jnp.zeros((tb, cw), F32)
        for kk in range(CONV_KERNEL):
            dh = dh + w_ref[pl.ds(kk, 1), :] * _window(dbuf, shifts, CONV_KERNEL - 1 - kk, tb)
        sg = _sigmoid(gt_ref[...])
        o_ref[:, 0:cw] = (dh * sg).astype(BF16)
        o_ref[:, cw:2 * cw] = (dh * a_ref[...] * sg * (1.0 - sg)).astype(BF16)

    return _ordered_call(
        body, name=name, grid=(nblk,), in_specs=[cur(0), cur(1), wspec, row, nxt],
        out_specs=pl.BlockSpec((tb, 2 * cw), lambda i: (i, 0)), out_shape=jax.ShapeDtypeStruct((s, 2 * cw), BF16),
        scratch_shapes=[pltpu.VMEM((tb + HALO, cw), F32), pltpu.VMEM((SUBLANES - 1, tb + HALO, cw), F32)],
        compiler_params=_cp("arbitrary"),
    )(proj, proj, w, dc, dc)


def _sgu_common(v_ref, lg_ref, lb_ref, w_ref, bexp_ref, sw):
    vhat, r = _layer_norm_stats(v_ref[...])
    vn_b = (vhat * lg_ref[...] + lb_ref[...]).astype(BF16)
    ii = lax.broadcasted_iota(jnp.int32, (WINDOW, WINDOW), 0)
    jj = lax.broadcasted_iota(jnp.int32, (WINDOW, WINDOW), 1)
    tril = jj <= ii
    head_of = lax.broadcasted_iota(jnp.int32, (WINDOW, sw), 1) // HEAD_DIM
    wts = [jnp.where(tril, w_ref[h], 0.0).astype(BF16) for h in range(sw // HEAD_DIM)]
    sv = bexp_ref[...]
    for h, wt in enumerate(wts):
        sv = sv + jnp.where(head_of == h, _dot(wt, vn_b), 0.0)
    return vhat, r, vn_b, tril, head_of, wts, sv


def _sgu_specs(sw, u_blk):
    nh = sw // HEAD_DIM
    u = pl.BlockSpec((WINDOW, sw), lambda n: (n, u_blk))
    v = pl.BlockSpec((WINDOW, sw), lambda n: (n, u_blk + 1))
    vec = pl.BlockSpec((1, sw), lambda n: (0, 0))
    wspec = pl.BlockSpec((nh, WINDOW, WINDOW), lambda n: (0, 0, 0))
    bspec = pl.BlockSpec((WINDOW, sw), lambda n: (0, 0))
    row = pl.BlockSpec((WINDOW, sw), lambda n: (n, 0))
    return u, v, vec, wspec, bspec, row


def _sgu_fwd(name, proj, u_blk, lg, lb, w, bexp):
    s = proj.shape[0]
    sw = lg.shape[1]
    u, v, vec, wspec, bspec, row = _sgu_specs(sw, u_blk)

    def body(u_ref, v_ref, lg_ref, lb_ref, w_ref, bexp_ref, y_ref):
        sv = _sgu_common(v_ref, lg_ref, lb_ref, w_ref, bexp_ref, sw)[-1]
        y_ref[...] = u_ref[...] * sv

    return _ordered_call(
        body, name=name, grid=(s // WINDOW,), in_specs=[u, v, vec, vec, wspec, bspec], out_specs=row,
        out_shape=jax.ShapeDtypeStruct((s, sw), F32), compiler_params=_cp("parallel"),
    )(proj, proj, lg, lb, w, bexp)


def _sgu_bwd(name, proj, u_blk, lg, lb, w, bexp, dy):
    s = proj.shape[0]
    sw = lg.shape[1]
    nh = sw // HEAD_DIM
    u, v, vec, wspec, bspec, row = _sgu_specs(sw, u_blk)
    dbspec = pl.BlockSpec((nh, WINDOW), lambda n: (0, 0))

    def body(u_ref, v_ref, lg_ref, lb_ref, w_ref, bexp_ref, dy_ref, o_ref, dw_ref, db_ref, dlg_ref, dlb_ref):
        n = pl.program_id(0)

        @pl.when(n == 0)
        def _():
            dw_ref[...] = jnp.zeros_like(dw_ref)
            db_ref[...] = jnp.zeros_like(db_ref)
            dlg_ref[...] = jnp.zeros_like(dlg_ref)
            dlb_ref[...] = jnp.zeros_like(dlb_ref)

        vhat, r, vn_b, tril, head_of, wts, sv = _sgu_common(v_ref, lg_ref, lb_ref, w_ref, bexp_ref, sw)
        dyv = dy_ref[...]
        o_ref[:, 0:sw] = (dyv * sv).astype(BF16)
        ds = dyv * u_ref[...]
        dvn = jnp.zeros((WINDOW, sw), F32)
        for h, wt in enumerate(wts):
            dsm_b = jnp.where(head_of == h, ds, 0.0).astype(BF16)
            dvn = dvn + _dot(wt, dsm_b, TN)
            dw_ref[h] += jnp.where(tril, _dot(dsm_b, vn_b, NT), 0.0)
        hmask = (lax.broadcasted_iota(jnp.int32, (nh, sw), 1) // HEAD_DIM
                 == lax.broadcasted_iota(jnp.int32, (nh, sw), 0)).astype(F32)
        db_ref[...] += lax.dot_general(hmask, ds, NT, precision=lax.Precision.HIGHEST, preferred_element_type=F32)
        dlg_ref[...] += _colsum(dvn * vhat)
        dlb_ref[...] += _colsum(dvn)
        dvg = dvn * lg_ref[...]
        dv = r * (dvg - jnp.mean(dvg, axis=-1, keepdims=True) - vhat * jnp.mean(dvg * vhat, axis=-1, keepdims=True))
        o_ref[:, sw:2 * sw] = dv.astype(BF16)

    return _ordered_call(
        body, name=name, grid=(s // WINDOW,), in_specs=[u, v, vec, vec, wspec, bspec, row],
        out_specs=(pl.BlockSpec((WINDOW, 2 * sw), lambda n: (n, 0)), wspec, dbspec, vec, vec),
        out_shape=(jax.ShapeDtypeStruct((s, 2 * sw), BF16), jax.ShapeDtypeStruct((nh, WINDOW, WINDOW), F32),
                   jax.ShapeDtypeStruct((nh, WINDOW), F32), jax.ShapeDtypeStruct((1, sw), F32),
                   jax.ShapeDtypeStruct((1, sw), F32)),
        compiler_params=_cp("arbitrary"),
    )(proj, proj, lg, lb, w, bexp, dy)


def _adamw_many(name, ws, gs, ms, vs):
    n = len(ws)

    def body(*refs):
        for t in range(n):
            w_ref, g_ref, m_ref, v_ref = (refs[k * n + t] for k in range(4))
            d_ref, nm_ref, nv_ref = (refs[(4 + k) * n + t] for k in range(3))
            gv = g_ref[...]
            mv = ADAM_B1 * m_ref[...] + (1.0 - ADAM_B1) * gv
            vv = ADAM_B2 * v_ref[...] + (1.0 - ADAM_B2) * (gv * gv)
            m_hat = mv / (1.0 - ADAM_B1 ** ADAM_STEP)
            v_hat = vv / (1.0 - ADAM_B2 ** ADAM_STEP)
            d_ref[...] = -ADAM_LR * (m_hat / (jnp.sqrt(v_hat) + ADAM_EPS) + ADAM_WD * w_ref[...])
            nm_ref[...] = mv
            nv_ref[...] = vv

    whole = pl.BlockSpec(memory_space=pltpu.VMEM)
    out = _ordered_call(
        body, name=name, in_specs=[whole] * (4 * n), out_specs=(whole,) * (3 * n),
        out_shape=tuple(jax.ShapeDtypeStruct(w.shape, F32) for w in ws) * 3,
        compiler_params=pltpu.CompilerParams(vmem_limit_bytes=VMEM_LIMIT),
    )(*ws, *gs, *ms, *vs)
    return out[:n], out[n:2 * n], out[2 * n:]


def _route():
    x, y, c = lax.axis_index("x"), lax.axis_index("y"), lax.axis_index("c")
    n1 = (jnp.where(c == 0, 1 - x, x), jnp.where(c == 0, y, 1 - y))
    n2 = (jnp.where(c == 0, x, 1 - x), jnp.where(c == 0, 1 - y, y))
    return x, y, c, n1, n2, (1 - x, 1 - y)


def _cidx(chip):
    return 2 * chip[0] + chip[1]


def _remote(src, dst, sems, k, device):
    send_sems, recv_sems = sems
    return pltpu.make_async_remote_copy(src_ref=src, dst_ref=dst, send_sem=send_sems.at[k], recv_sem=recv_sems.at[k],
                                        device_id=device, device_id_type=MESH)


HBM_SPEC = pl.BlockSpec(memory_space=pltpu.HBM)
SEM_SPEC = pl.BlockSpec(memory_space=pltpu.SEMAPHORE)
DATAFLOW = pltpu.SideEffectType.DATAFLOW_SIDE_EFFECTING


def _exchange_start(name, bufs, n_sems, build):
    n = len(bufs)

    def body(*refs):
        for cp in build(refs[:n], (refs[n], refs[n + 1])):
            cp.start()

    out = _ordered_call(
        body, name=name,
        out_shape=(pltpu.SemaphoreType.DMA((n_sems,)), pltpu.SemaphoreType.DMA((n_sems,)))
        + tuple(pltpu.HBM(b.shape, b.dtype) for b in bufs),
        in_specs=[HBM_SPEC] * n, out_specs=(SEM_SPEC, SEM_SPEC) + (HBM_SPEC,) * n,
        input_output_aliases={i: 2 + i for i in range(n)},
        compiler_params=pltpu.CompilerParams(has_side_effects=DATAFLOW),
    )(*[pltpu.with_memory_space_constraint(b, pltpu.HBM) for b in bufs])
    return dict(name=name, send=out[0], recv=out[1], bufs=list(out[2:2 + n]), build=build)


def _exchange_wait(handle):
    n = len(handle["bufs"])

    def body(*refs):
        for cp in handle["build"](refs[:n], (refs[n], refs[n + 1])):
            cp.wait_send()
            cp.wait_recv()

    return list(_ordered_call(
        body, name=handle["name"] + "_wait", out_shape=tuple(pltpu.HBM(b.shape, b.dtype) for b in handle["bufs"]),
        in_specs=[HBM_SPEC] * n + [SEM_SPEC, SEM_SPEC], out_specs=(HBM_SPEC,) * n,
        input_output_aliases={i: i for i in range(n)},
        compiler_params=pltpu.CompilerParams(has_side_effects=DATAFLOW),
    )(*handle["bufs"], handle["send"], handle["recv"]))


def _cast_place(name, w, l, me_idx, dtype):
    _, r, c = w.shape
    tr = _tile(r, 512)

    def body(me_ref, w_ref, o_ref):
        o_ref[...] = w_ref[...].astype(dtype)

    grid_spec = pltpu.PrefetchScalarGridSpec(
        num_scalar_prefetch=1, grid=(r // tr,),
        in_specs=[pl.BlockSpec((None, tr, c), lambda i, me_ref: (l, i, 0))],
        out_specs=pl.BlockSpec((None, tr, c), lambda i, me_ref: (me_ref[0], i, 0)))
    return _ordered_call(
        body, name=name, grid_spec=grid_spec, out_shape=jax.ShapeDtypeStruct((N_CHIPS, r, c), dtype),
        compiler_params=_cp("arbitrary"),
    )(me_idx, w)


def _my_half(ref, blk, c):
    hr = ref.shape[1] // 2
    return ref.at[blk, pl.ds(c * hr, hr), :]


def _gather_step(entering):
    lens = [len(e) for e in entering]
    flat = [b for e in entering for b in e]

    def build(refs, sems):
        x, y, c, n1, n2, dg = _route()
        me = _cidx((x, y))
        plan = ([(r, (me,), (*n1, c)) for r in refs[:lens[0]]]
                + [(r, (me, _cidx(n1)), (*n2, c)) for r in refs[lens[0]:lens[0] + lens[1]]]
                + [(r, (_cidx(n1), _cidx(n2), _cidx(dg)), (x, y, 1 - c)) for r in refs[lens[0] + lens[1]:]])
        cps = []
        for ref, blocks, peer in plan:
            for blk in blocks:
                cps.append(_remote(_my_half(ref, blk, c), _my_half(ref, blk, c), sems, len(cps), peer))
        return cps

    return flat, lens[0] + 2 * lens[1] + 3 * lens[2], build


RI_C, RI_ME, RI_N2, RI_N1 = 0, 1, 2, 3


def _pair_sum(name, g, sib, route_idx):
    _, rows, cols = g.shape
    hr = rows // 2
    tr = _tile(hr, 512)
    per = hr // tr

    def body(ri, g_ref, s_ref, o_ref):
        o_ref[...] = (g_ref[...].astype(F32) + s_ref[...].astype(F32)).astype(BF16)

    blk = (None, tr, cols)
    grid_spec = pltpu.PrefetchScalarGridSpec(
        num_scalar_prefetch=1, grid=(2, per),
        in_specs=[pl.BlockSpec(blk, lambda j, i, ri: (ri[RI_N1 + j], ri[RI_C] * per + i, 0)),
                  pl.BlockSpec(blk, lambda j, i, ri: (ri[RI_N1 + j], i, 0))],
        out_specs=pl.BlockSpec(blk, lambda j, i, ri: (j, i, 0)))
    return _ordered_call(
        body, name=name, grid_spec=grid_spec, out_shape=jax.ShapeDtypeStruct((2, hr, cols), BF16),
        compiler_params=_cp("parallel", "parallel"),
    )(route_idx, g, sib)


def _sum_stage1(name, g, sib, got, route_idx):
    _, hr, cols = sib.shape
    tr = _tile(hr, 512)
    per = hr // tr

    def body(ri, gm_ref, sm_ref, gn_ref, sn_ref, g0_ref, g1_ref, keep_ref, send_ref):
        keep_ref[...] = (gm_ref[...].astype(F32) + sm_ref[...].astype(F32)) + g0_ref[...].astype(F32)
        send_ref[...] = ((gn_ref[...].astype(F32) + sn_ref[...].astype(F32)) + g1_ref[...].astype(F32)).astype(BF16)

    blk = (None, tr, cols)
    row = pl.BlockSpec((tr, cols), lambda i, ri: (i, 0))

    def mine(which):
        return pl.BlockSpec(blk, lambda i, ri: (ri[which], ri[RI_C] * per + i, 0))

    def theirs(which):
        return pl.BlockSpec(blk, lambda i, ri: (ri[which], i, 0))

    grid_spec = pltpu.PrefetchScalarGridSpec(
        num_scalar_prefetch=1, grid=(per,),
        in_specs=[mine(RI_ME), theirs(RI_ME), mine(RI_N2), theirs(RI_N2),
                  pl.BlockSpec(blk, lambda i, ri: (0, i, 0)), pl.BlockSpec(blk, lambda i, ri: (1, i, 0))],
        out_specs=(row, row))
    return _ordered_call(
        body, name=name, grid_spec=grid_spec,
        out_shape=(jax.ShapeDtypeStruct((hr, cols), F32), jax.ShapeDtypeStruct((hr, cols), BF16)),
        compiler_params=_cp("parallel"),
    )(route_idx, g, sib, g, sib, got, got)


def _sum_stage2(name, keep, got):
    hr, cols = keep.shape
    tr = _tile(hr, 512)

    def body(k_ref, g_ref, o_ref):
        o_ref[...] = k_ref[...] + g_ref[...].astype(F32)

    row = pl.BlockSpec((tr, cols), lambda i: (i, 0))
    return _ordered_call(
        body, name=name, grid=(hr // tr,), in_specs=[row, row], out_specs=row,
        out_shape=jax.ShapeDtypeStruct((hr, cols), F32), compiler_params=_cp("parallel"),
    )(keep, got)


def _reduce_scatter(tag, names, grads, route_idx):
    n = len(grads)
    hrs = [g.shape[1] // 2 for g in grads]

    def empty(t, lead, dtype):
        return lax.empty(lead + (hrs[t], grads[t].shape[2]), dtype)

    def pair_stage(refs, sems):
        x, y, c, n1, n2, dg = _route()
        return [_remote(refs[t].at[:, pl.ds((1 - c) * hrs[t], hrs[t]), :], refs[n + t], sems, t, (x, y, 1 - c))
                for t in range(n)]

    def stage1(refs, sems):
        x, y, c, n1, n2, dg = _route()
        return [_remote(refs[t].at[slot], refs[n + t].at[slot], sems, 2 * t + slot, (*n1, c))
                for t in range(n) for slot in range(2)]

    def stage2(refs, sems):
        x, y, c, n1, n2, dg = _route()
        return [_remote(refs[t], refs[n + t], sems, t, (*n2, c)) for t in range(n)]

    def stage3(refs, sems):
        x, y, c, n1, n2, dg = _route()
        return [_remote(refs[t], refs[n + t], sems, t, (x, y, 1 - c)) for t in range(n)]

    state = {}

    def start():
        state["h"] = _exchange_start(f"rs_pair_{tag}", list(grads) + [empty(t, (N_CHIPS,), BF16) for t in range(n)],
                                     n, pair_stage)

    def pair_done():
        state["pair"] = _exchange_wait(state["h"])
        psum = [_pair_sum(f"rs_psum_{names[t]}", state["pair"][t], state["pair"][n + t], route_idx) for t in range(n)]
        state["h"] = _exchange_start(f"rs_x1_{tag}", psum + [empty(t, (2,), BF16) for t in range(n)], 2 * n, stage1)

    def x1_done():
        out = _exchange_wait(state["h"])
        state["keep"], send = zip(*[_sum_stage1(f"rs_sum1_{names[t]}", state["pair"][t], state["pair"][n + t],
                                                out[n + t], route_idx) for t in range(n)])
        state["h"] = _exchange_start(f"rs_x2_{tag}", list(send) + [empty(t, (), BF16) for t in range(n)], n, stage2)

    def x2_done():
        out = _exchange_wait(state["h"])
        mine = [_sum_stage2(f"rs_sum2_{names[t]}", state["keep"][t], out[n + t]) for t in range(n)]
        state["h"] = _exchange_start(f"rs_half_{tag}", mine + [empty(t, (), F32) for t in range(n)], n, stage3)

    def finish():
        out = _exchange_wait(state["h"])
        return list(zip(out[:n], out[n:]))

    return start, pair_done, x1_done, x2_done, finish


def _adamw_big(name, w, m, v, f, h, l, c_idx, prev):
    n_l, r, cols = w.shape
    hr = r // 2
    tr = _tile(hr, 256)
    per = hr // tr

    def body(c_ref, w_ref, m_ref, v_ref, f_ref, h_ref, *rest):
        g_ref, d_ref, nm_ref, nv_ref = rest[-4:]
        gv = jnp.where(pl.program_id(0) == c_ref[0], f_ref[...], h_ref[...])
        mv = ADAM_B1 * m_ref[...] + (1.0 - ADAM_B1) * gv
        vv = ADAM_B2 * v_ref[...] + (1.0 - ADAM_B2) * (gv * gv)
        m_hat = mv / (1.0 - ADAM_B1 ** ADAM_STEP)
        v_hat = vv / (1.0 - ADAM_B2 ** ADAM_STEP)
        g_ref[...] = gv
        d_ref[...] = -ADAM_LR * (m_hat / (jnp.sqrt(v_hat) + ADAM_EPS) + ADAM_WD * w_ref[...])
        nm_ref[...] = mv
        nv_ref[...] = vv

    big = pl.BlockSpec((None, tr, cols), lambda hf, i, c_ref: (l, hf * per + i, 0))
    fspec = pl.BlockSpec((tr, cols), lambda hf, i, c_ref: (jnp.where(hf == c_ref[0], i, 0), 0))
    hspec = pl.BlockSpec((tr, cols), lambda hf, i, c_ref: (jnp.where(hf == c_ref[0], 0, i), 0))
    grid_spec = pltpu.PrefetchScalarGridSpec(
        num_scalar_prefetch=1, grid=(2, per), in_specs=[big] * 3 + [fspec, hspec] + [ANY] * len(prev),
        out_specs=(big,) * 4)
    return _ordered_call(
        body, name=name, grid_spec=grid_spec, out_shape=(jax.ShapeDtypeStruct(w.shape, F32),) * 4,
        input_output_aliases={6 + k: k for k in range(len(prev))}, compiler_params=_cp("arbitrary", "arbitrary"),
    )(c_idx, w, m, v, f, h, *prev)


def _small_allreduce(buf):
    rows = buf.shape[0]
    hr = rows // 2

    def body(in_ref, out_ref, pair, acc, got1, got2, send_sems, recv_sems):
        x, y, c, n1, n2, dg = _route()
        sems = (send_sems, recv_sems)
        sibling = (x, y, 1 - c)
        mine = pl.ds(pl.multiple_of(c * hr, 8), hr)
        pair[c] = in_ref[...]
        cp = _remote(in_ref, pair.at[c], sems, 0, sibling)
        cp.start()
        cp.wait()
        acc[...] = pair[0, mine, :] + pair[1, mine, :]
        cp = _remote(acc, got1, sems, 1, (*n1, c))
        cp.start()
        cp.wait()
        acc[...] = acc[...] + got1[...]
        cp = _remote(acc, got2, sems, 2, (*n2, c))
        cp.start()
        cp.wait()
        out_ref[mine, :] = acc[...] + got2[...]
        cp = _remote(out_ref.at[mine, :], out_ref.at[mine, :], sems, 3, sibling)
        cp.start()
        cp.wait()

    half = pltpu.VMEM((hr, 128), F32)
    return _ordered_call(
        body, name="small_allreduce", in_specs=[pl.BlockSpec(memory_space=pltpu.VMEM)],
        out_specs=pl.BlockSpec(memory_space=pltpu.VMEM), out_shape=jax.ShapeDtypeStruct((rows, 128), F32),
        scratch_shapes=[pltpu.VMEM((2, rows, 128), F32), half, half, half,
                        pltpu.SemaphoreType.DMA((4,)), pltpu.SemaphoreType.DMA((4,))],
        compiler_params=pltpu.CompilerParams(has_side_effects=True, vmem_limit_bytes=VMEM_LIMIT),
    )(buf)


BIG = ("w_in", "w_out", "w_up", "w_down")
COL_SHARDED = {"w_in": True, "w_out": False, "w_up": True, "w_down": False}
SMALL = ("ln1_g", "q_norm_g", "k_norm_g", "sinks", "conv_w", "conv_b", "conv_ln_g", "conv_ln_b", "sgu_ln_g",
         "sgu_ln_b", "sgu_w", "sgu_b", "out_norm_g", "ln2_g")
WEIGHTS = ("ln1_g", "w_in", "q_norm_g", "k_norm_g", "sinks", "conv_w", "conv_b", "conv_ln_g", "conv_ln_b",
           "sgu_ln_g", "sgu_ln_b", "sgu_w", "sgu_b", "out_norm_g", "w_out", "ln2_g", "w_up", "w_down")
PACK_QUANTUM = 8 * 128
PACK_ROWS = 512


def _pack(arrs):
    parts = []
    for a in arrs:
        f = a.reshape(-1)
        parts.append(jnp.pad(f, (0, -f.shape[0] % PACK_QUANTUM)).reshape(-1, 128))
    rows = sum(p.shape[0] for p in parts)
    parts.append(jnp.zeros((-rows % PACK_ROWS, 128), F32))
    return jnp.concatenate(parts, axis=0)


def _unpack(buf, shapes):
    out, off = [], 0
    for shp in shapes:
        n = 1
        for dd in shp:
            n *= dd
        rows = (n + PACK_QUANTUM - 1) // PACK_QUANTUM * 8
        out.append(buf[off:off + rows].reshape(-1)[:n].reshape(shp))
        off += rows
    return out


def _to_heads(t, nh):
    return t.reshape(t.shape[0], nh, HEAD_DIM).transpose(1, 0, 2)


def _from_heads(t):
    return t.transpose(1, 0, 2).reshape(t.shape[1], t.shape[0] * HEAD_DIM)


def _no_hook(point, carry):
    return carry


def _layer_fwd(l, x, p, wg, hook=_no_hook):
    d = x.shape[1]
    aw, cw = d // 2, d // 4
    nq = aw // HEAD_DIM
    nkv = nq // GQA
    kvw = nkv * HEAD_DIM
    h1 = _rms_fwd(f"ln1_fwd_{l}", x, p["ln1_g"])
    h1 = hook("fwd_ln1", h1)
    proj = _mm_act_w(f"proj_{l}", h1, wg["w_in"], True, _ep_store)[0]
    proj = hook("fwd_proj", proj)
    q = _to_heads(proj[:, :aw], nq)
    k = _to_heads(proj[:, aw:aw + kvw], nkv)
    v = _to_heads(proj[:, aw + kvw:aw + 2 * kvw], nkv)
    sinks_b = jnp.broadcast_to(p["sinks"][:, None, None], (nq, 1, 128))
    ya = _from_heads(_attn_fwd(f"attn_fwd_{l}", q, k, v, p["q_norm_g"], p["k_norm_g"], sinks_b))
    ya = hook("fwd_attn", ya)
    yc = _conv_fwd(f"conv_fwd_{l}", proj, 3, p["conv_w"], p["conv_b"], p["conv_ln_g"], p["conv_ln_b"])
    ys = _sgu_fwd(f"sgu_fwd_{l}", proj, 5, p["sgu_ln_g"], p["sgu_ln_b"], p["sgu_w"], p["sgu_bexp"])
    mix = _mixnorm_fwd(f"mixnorm_fwd_{l}", ya, yc, ys, p["out_norm_g"])
    mix = hook("fwd_mid", mix)
    if _whole_rows(wg["w_out"]):
        xm, h2 = _mm_act_w(f"out_{l}", mix, wg["w_out"], False, _ep_residual_norm, extra=(x,),
                           out_dtypes=(F32, BF16), row_vectors=(p["ln2_g"].reshape(1, d),))
    else:
        xm = _mm_act_w(f"out_{l}", mix, wg["w_out"], False, _ep_residual, extra=(x,))[0]
        h2 = _rms_fwd(f"ln2_fwd_{l}", xm, p["ln2_g"])
    h2 = hook("fwd_ln2", h2)
    up_b, act_b = _mm_act_w(f"up_{l}", h2, wg["w_up"], True, _ep_up, out_dtypes=(BF16, BF16))
    act_b = hook("fwd_up", act_b)
    xo = _mm_act_w(f"down_{l}", act_b, wg["w_down"], False, _ep_residual, extra=(xm,))[0]
    xo = hook("fwd_end", xo)
    saved = dict(x=x, h1=h1, proj=proj, q=q, k=k, v=v, sinks_b=sinks_b, ya=ya, yc=yc, ys=ys, mix=mix, xm=xm, h2=h2,
                 up_b=up_b, act_b=act_b)
    return xo, saved


def _layer_bwd(l, dxo, dxo_b, p, wg, sv, big, hook=_no_hook):
    d = dxo.shape[1]
    nq = (d // 2) // HEAD_DIM
    small = {}
    dxo_b = hook("bwd_start", dxo_b)
    big["w_down"] = _mm_wgrad(f"dw_down_{l}", sv["act_b"], dxo_b, False, d)
    dup_b = _mm_act_wt(f"dup_{l}", dxo_b, wg["w_down"], False, _ep_dup, extra=(sv["up_b"],), out_dtypes=(BF16,))[0]
    dup_b = hook("bwd_dup", dup_b)
    big["w_up"] = _mm_wgrad(f"dw_up_{l}", sv["h2"], dup_b, True, wg["w_up"].shape[2])
    dh2 = _mm_act_wt(f"dh2_{l}", dup_b, wg["w_up"], True, _ep_store)[0]
    dh2 = hook("bwd_dh2", dh2)
    dxm, dxm_b, small["ln2_g"] = _rms_bwd(f"ln2_bwd_{l}", dh2, sv["xm"], p["ln2_g"], dxo)
    big["w_out"] = _mm_wgrad(f"dw_out_{l}", sv["mix"], dxm_b, False, d)
    dmix = _mm_act_wt(f"dmix_{l}", dxm_b, wg["w_out"], False, _ep_store)[0]
    dya, dyc, dys, small["out_norm_g"] = _mixnorm_bwd(f"mixnorm_bwd_{l}", dmix, sv["ya"], sv["yc"], sv["ys"],
                                                      p["out_norm_g"])
    dya = hook("bwd_mix", dya)
    dq, dkc, dkp, dvc, dvp, small["q_norm_g"], dsink = _attn_bwd(
        f"attn_bwd_{l}", sv["q"], sv["k"], sv["v"], p["q_norm_g"], p["k_norm_g"], sv["sinks_b"], _to_heads(dya, nq))
    dkc = hook("bwd_attn", dkc)
    small["sinks"] = dsink[:, 0, 0]
    dk, dv, small["k_norm_g"] = _attn_bwd_kv(f"attn_bwd_kv_{l}", sv["k"], p["k_norm_g"], dkc, dkp, dvc, dvp)
    dc, dcw, small["conv_b"], small["conv_ln_g"], small["conv_ln_b"] = _conv_bwd1(
        f"conv_bwd1_{l}", sv["proj"], 3, p["conv_w"], p["conv_b"], p["conv_ln_g"], p["conv_ln_b"], dyc)
    small["conv_w"] = dcw[:CONV_KERNEL]
    dxc_b = _conv_bwd2(f"conv_bwd2_{l}", sv["proj"], 3, p["conv_w"], dc)
    dxs_b, small["sgu_w"], small["sgu_b"], small["sgu_ln_g"], small["sgu_ln_b"] = _sgu_bwd(
        f"sgu_bwd_{l}", sv["proj"], 5, p["sgu_ln_g"], p["sgu_ln_b"], p["sgu_w"], p["sgu_bexp"], dys)
    dxs_b = hook("bwd_sgu", dxs_b)
    dproj_b = jnp.concatenate([_from_heads(dq), _from_heads(dk), _from_heads(dv), dxc_b, dxs_b], axis=1)
    big["w_in"] = _mm_wgrad(f"dw_in_{l}", sv["h1"], dproj_b, True, wg["w_in"].shape[2])
    dh1 = _mm_act_wt(f"dh1_{l}", dproj_b, wg["w_in"], True, _ep_store)[0]
    dx, dx_b, small["ln1_g"] = _rms_bwd(f"ln1_bwd_{l}", dh1, sv["x"], p["ln1_g"], dxm)
    dx_b = hook("bwd_end", dx_b)
    return dx, dx_b, small


def kernel(x, ln1_g, w_in, q_norm_g, k_norm_g, sinks, conv_w, conv_b, conv_ln_g, conv_ln_b, sgu_ln_g, sgu_ln_b, sgu_w, sgu_b, out_norm_g, w_out, ln2_g, w_up, w_down, loss_target, m_ln1_g, m_w_in, m_q_norm_g, m_k_norm_g, m_sinks, m_conv_w, m_conv_b, m_conv_ln_g, m_conv_ln_b, m_sgu_ln_g, m_sgu_ln_b, m_sgu_w, m_sgu_b, m_out_norm_g, m_w_out, m_ln2_g, m_w_up, m_w_down, v_ln1_g, v_w_in, v_q_norm_g, v_k_norm_g, v_sinks, v_conv_w, v_conv_b, v_conv_ln_g, v_conv_ln_b, v_sgu_ln_g, v_sgu_ln_b, v_sgu_w, v_sgu_b, v_out_norm_g, v_w_out, v_ln2_g, v_w_up, v_w_down):
    given = dict(locals())
    _LAST[0] = None
    n_layers = ln1_g.shape[0]
    s, d = x.shape[1], x.shape[2]
    cw = d // 4
    xi, yi, core = lax.axis_index("x"), lax.axis_index("y"), lax.axis_index("c")
    chip = 2 * xi + yi
    first_partner = jnp.where(core == 0, 2 * (1 - xi) + yi, 2 * xi + (1 - yi))
    second_partner = jnp.where(core == 0, 2 * xi + (1 - yi), 2 * (1 - xi) + yi)
    route_idx = jnp.stack([core, chip, second_partner, first_partner, 3 - chip]).astype(jnp.int32)

    conv_w_pad = jnp.pad(conv_w, ((0, 0), (0, HALO - CONV_KERNEL), (0, 0))).reshape(1, n_layers * HALO, -1)
    cwl = conv_w_pad.shape[2]
    buf = {}

    def place(key):
        if key == "conv_w":
            buf[key] = _cast_place("place_conv_w", conv_w_pad, 0, route_idx[1:2], F32)
        else:
            buf[key] = _cast_place(f"place_{key[0]}_{key[1]}", given[key[0]], key[1], route_idx[1:2], BF16)

    groups = [["conv_w", ("w_in", 0)]] + [[(nm, l)] for l in range(n_layers) for nm in BIG if (nm, l) != ("w_in", 0)]
    n_steps = len(groups) + 2
    pending = {}

    def start_step(st):
        keys = [groups[st - j] if 0 <= st - j < len(groups) else [] for j in range(3)]
        flat, n_sems, build = _gather_step([[buf[k] for k in ks] for ks in keys])
        pending["keys"] = [k for ks in keys for k in ks]
        pending["h"] = _exchange_start(f"gather_step{st}", flat, n_sems, build)

    def wait_step():
        for k, b in zip(pending["keys"], _exchange_wait(pending["h"])):
            buf[k] = b

    later = [k for grp in groups[1:] for k in grp]
    for k in groups[0]:
        place(k)
    for st, upto in enumerate((2, 5, len(later))):
        start_step(st)
        for k in later[:upto]:
            if k not in buf:
                place(k)
        if st < 2:
            wait_step()

    def first_group_done():
        conv_w_full = buf["conv_w"].reshape(N_CHIPS, n_layers, HALO, cwl).transpose(1, 2, 0, 3).reshape(
            n_layers, HALO, cw)
        for l in range(n_layers):
            params[l]["conv_w"] = conv_w_full[l]

    class LayerWeights:
        def __init__(self, l):
            self.l = l

        def __getitem__(self, nm):
            return buf[(nm, self.l)]

    wgs = [LayerWeights(l) for l in range(n_layers)]
    fwd_points = [(l, pt) for l in range(n_layers) for pt in ("fwd_ln1", "fwd_attn", "fwd_ln2", "fwd_up", "fwd_end")
                  if (pt != "fwd_ln1" or l == 0) and (pt != "fwd_end" or l + 1 < n_layers)]
    assert len(fwd_points) == n_steps - 3 + 1, "one hook point per pipeline step, and one to wait for the last"
    fwd_tables = [{} for _ in range(n_layers)]
    for i, (l, pt) in enumerate(fwd_points):
        fwd_tables[l].setdefault(pt, []).append(wait_step)
        if i == 0:
            fwd_tables[l][pt].append(first_group_done)
        if 3 + i < n_steps:
            fwd_tables[l][pt].append(functools.partial(start_step, 3 + i))
    params = []
    for l in range(n_layers):
        p = {nm: given[nm][l] for nm in SMALL if nm != "conv_w"}
        for nm in ("conv_b", "conv_ln_g", "conv_ln_b", "sgu_ln_g", "sgu_ln_b"):
            p[nm] = p[nm].reshape(1, -1)
        p["sgu_bexp"] = jnp.repeat(sgu_b[l].T, HEAD_DIM, axis=1)
        params.append(p)

    def make_hook(table):
        def hook(point, carry):
            for fn in table.get(point, ()):
                fn()
            return carry
        return hook

    h = x.reshape(s, d)
    saved = []
    for l in range(n_layers):
        h, sv = _layer_fwd(l, h, params[l], wgs[l], make_hook(fwd_tables[l]))
        saved.append(sv)
    dh, dh_b, loss_part = _loss_head(h, loss_target.reshape(s, d))
    loss = lax.psum(loss_part[0, 0], ("x", "y", "c"))

    big_grads = [{} for _ in range(n_layers)]
    small_grads = [None] * n_layers
    halves = {}

    def rs_group(tag, l, names):
        phases = {}

        def start():
            phases["p"] = _reduce_scatter(tag, [f"{nm}_{l}" for nm in names], [big_grads[l][nm] for nm in names],
                                          route_idx)
            phases["p"][0]()

        def step(k):
            return lambda: phases["p"][k]()

        def finish():
            for nm, fh in zip(names, phases["p"][4]()):
                halves[(nm, l)] = fh

        return [start, step(1), step(2), step(3), finish]

    early = rs_group("l0a", 0, ("w_down", "w_up", "w_out"))
    for l in reversed(range(n_layers)):
        table = {}
        if l + 1 < n_layers:
            above = rs_group(f"l{l + 1}", l + 1, BIG)
            for point, fn in zip(("bwd_start", "bwd_dup", "bwd_mix", "bwd_attn", "bwd_sgu"), above):
                table.setdefault(point, []).append(fn)
        if l == 0:
            for point, fn in zip(("bwd_mix", "bwd_attn", "bwd_end"), early[:3]):
                table.setdefault(point, []).append(fn)
        dh, dh_b, small_grads[l] = _layer_bwd(l, dh, dh_b, params[l], wgs[l], saved[l], big_grads[l],
                                              make_hook(table))
    grad_x = dh.reshape(x.shape)

    grads, delta, new_m, new_v = {}, {}, {}, {}
    adam_state = {nm: () for nm in BIG}

    def adam(nm, l):
        f, h = halves[(nm, l)]
        adam_state[nm] = _adamw_big(f"adamw_{nm}_{l}", given[nm], given["m_" + nm], given["v_" + nm], f, h, l,
                                    route_idx[0:1], adam_state[nm])

    def small_update():
        small_shapes = [(n_layers,) + small_grads[0][nm].shape for nm in SMALL]
        small_sum = _small_allreduce(_pack([jnp.stack([small_grads[l][nm] for l in range(n_layers)])
                                            for nm in SMALL]))
        for nm, g in zip(SMALL, _unpack(small_sum, small_shapes)):
            grads[nm] = g.reshape((n_layers,) + given[nm].shape[1:]) if nm != "conv_w" else g
        grads["conv_w"] = lax.dynamic_slice_in_dim(grads["conv_w"], chip * cwl, cwl, axis=2)
        results = _adamw_many("adamw_small", [given[nm] for nm in SMALL], [grads[nm] for nm in SMALL],
                              [given["m_" + nm] for nm in SMALL], [given["v_" + nm] for nm in SMALL])
        for dst, arrs in zip((delta, new_m, new_v), results):
            dst.update(zip(SMALL, arrs))

    upper = [(nm, l) for l in reversed(range(1, n_layers)) for nm in reversed(BIG)]
    late = rs_group("l0b", 0, ("w_in",))
    late[0]()
    for task in upper[:1]:
        adam(*task)
    late[1]()
    for task in upper[1:]:
        adam(*task)
    early[3]()
    late[2]()
    small_update()
    early[4]()
    for nm in ("w_down", "w_up"):
        adam(nm, 0)
    late[3]()
    adam("w_out", 0)
    late[4]()
    adam("w_in", 0)
    for nm in BIG:
        grads[nm], delta[nm], new_m[nm], new_v[nm] = adam_state[nm]
    return (loss, grad_x, *[grads[nm] for nm in WEIGHTS], *[delta[nm] for nm in WEIGHTS],
            *[new_m[nm] for nm in WEIGHTS], *[new_v[nm] for nm in WEIGHTS])
```

```python
import functools

import jax
import jax.numpy as jnp
from jax import lax
from jax.experimental import pallas as pl
from jax.experimental.pallas import tpu as pltpu

F32 = jnp.float32
BF16 = jnp.bfloat16
EPS = 1e-6
NEG_INF = -1e30
HEAD_DIM = 64
WINDOW = 128
CONV_KERNEL = 31
HALO = 32
GQA = 4
N_CHIPS = 4
ADAM_LR, ADAM_B1, ADAM_B2, ADAM_EPS, ADAM_WD, ADAM_STEP = 0.001, 0.9, 0.999, 1e-08, 0.01, 10
VMEM_LIMIT = 56 * 1024 * 1024
TILE_K = 2048
MESH = pl.DeviceIdType.MESH
ANY = pl.BlockSpec(memory_space=pl.ANY)

NN = (((1,), (0,)), ((), ()))
NT = (((1,), (1,)), ((), ()))
TN = (((0,), (0,)), ((), ()))


def _cp(*sem):
    return pltpu.CompilerParams(dimension_semantics=sem, vmem_limit_bytes=VMEM_LIMIT)


_LAST = [None]
TOKEN = jax.ShapeDtypeStruct((8, 128), F32)


def _ordered_call(body, *, out_shape, out_specs=None, in_specs=None, grid_spec=None, grid=None, **kw):
    single = not isinstance(out_shape, (tuple, list))
    shapes = (out_shape,) if single else tuple(out_shape)

    def run(*operands):
        dep = _LAST[0]
        n = len(operands)
        n_dep = 0 if dep is None else 1

        def fn(*refs):
            outs = refs[n + n_dep:n + n_dep + len(shapes)]
            token = refs[n + n_dep + len(shapes)]
            body(*refs[:n], *outs, *refs[n + n_dep + len(shapes) + 1:])
            token[...] = jnp.zeros_like(token)

        specs_in = list(grid_spec.in_specs if grid_spec is not None else in_specs) + [ANY] * n_dep
        specs_out = grid_spec.out_specs if grid_spec is not None else out_specs
        specs_out = tuple(specs_out) if isinstance(specs_out, (tuple, list)) else (specs_out,)
        if grid_spec is not None or grid:
            specs_out += (pl.BlockSpec(TOKEN.shape, lambda *_: (0, 0)),)
        else:
            specs_out += (pl.BlockSpec(memory_space=pltpu.VMEM),)
        args = operands + ((dep,) if n_dep else ())
        if grid_spec is not None:
            spec = pltpu.PrefetchScalarGridSpec(num_scalar_prefetch=grid_spec.num_scalar_prefetch, grid=grid_spec.grid,
                                                in_specs=specs_in, out_specs=specs_out)
            out = pl.pallas_call(fn, grid_spec=spec, out_shape=shapes + (TOKEN,), **kw)(*args)
        else:
            if grid:
                kw["grid"] = grid
            out = pl.pallas_call(fn, in_specs=specs_in, out_specs=specs_out, out_shape=shapes + (TOKEN,), **kw)(*args)
        _LAST[0] = out[-1]
        return out[0] if single else tuple(out[:-1])

    return run


def _tile(dim, pref):
    if dim <= pref:
        return dim
    for t in range(pref, 0, -128):
        if dim % t == 0:
            return t
    while dim % pref:
        pref //= 2
    return pref


def _dot(a, b, dims=NN):
    return lax.dot_general(a, b, dims, preferred_element_type=F32)


def _colsum(v):
    return jnp.sum(v, axis=0, keepdims=True)


def _sigmoid(x):
    return 1.0 / (1.0 + jnp.exp(-x))


def _matmul(name, operands, in_specs, out_shape, out_specs, grid, dims, acc_shape, epilogue, split_k=False):
    nk = grid[2]
    n_in = len(operands)

    def product(a_ref, b_ref):
        if split_k:
            ck = b_ref.shape[2]
            out = _dot(a_ref[:, 0:ck], b_ref[0], dims)
            for j in range(1, N_CHIPS):
                out = out + _dot(a_ref[:, j * ck:(j + 1) * ck], b_ref[j], dims)
            return out
        bv = b_ref[...]
        return _dot(a_ref[...], bv.reshape(-1, bv.shape[-1]) if bv.ndim == 3 else bv, dims)

    def body(*refs):
        a_ref, b_ref = refs[0], refs[1]
        extra = refs[2:n_in]
        if nk == 1:
            epilogue(product(a_ref, b_ref), extra, refs[n_in:])
            return
        outs = refs[n_in:-1]
        acc = refs[-1]
        k = pl.program_id(2)

        @pl.when(k == 0)
        def _():
            acc[...] = product(a_ref, b_ref)

        @pl.when((k > 0) & (k < nk - 1))
        def _():
            acc[...] += product(a_ref, b_ref)

        @pl.when(k == nk - 1)
        def _():
            epilogue(acc[...] + product(a_ref, b_ref), extra, outs)

    return _ordered_call(
        body, name=name, grid=grid, in_specs=in_specs, out_specs=out_specs, out_shape=out_shape,
        scratch_shapes=[pltpu.VMEM(acc_shape, F32)] if nk > 1 else [],
        compiler_params=_cp("parallel", "parallel", "arbitrary"),
    )(*operands)


def _ep_store(acc, extra, outs):
    outs[0][...] = acc.astype(outs[0].dtype)


def _ep_residual(acc, extra, outs):
    outs[0][...] = extra[0][...] + acc


def _ep_residual_norm(acc, extra, outs):
    xm = extra[0][...] + acc
    outs[0][...] = xm
    r = lax.rsqrt(jnp.mean(xm * xm, axis=-1, keepdims=True) + EPS)
    outs[1][...] = (xm * r * extra[1][...]).astype(BF16)


def _ep_up(acc, extra, outs):
    outs[0][...] = acc.astype(BF16)
    r = jnp.maximum(acc, 0.0)
    outs[1][...] = (r * r).astype(BF16)


def _ep_dup(acc, extra, outs):
    outs[0][...] = (acc * (2.0 * jnp.maximum(extra[0][...].astype(F32), 0.0))).astype(BF16)


def _whole_rows(wg):
    return N_CHIPS * wg.shape[1] <= TILE_K


def _ep_loss(n_features):
    def epilogue(acc, extra, outs):
        e = extra[0][...] + acc - extra[1][...]
        dy = e * (1.0 / n_features)
        outs[0][...] = dy
        outs[1][...] = dy.astype(BF16)
        outs[2][...] = jnp.broadcast_to(jnp.sum(jnp.sum(e * e, axis=0, keepdims=True), axis=1, keepdims=True),
                                        outs[2].shape)
    return epilogue


def _mm_act_w(name, a, wg, col_sharded, epilogue, extra=(), out_dtypes=(F32,), row_vectors=(), tile_sums=0):
    m, kdim = a.shape
    _, r, c = wg.shape
    tm = _tile(m, 1024)
    if col_sharded:
        n = N_CHIPS * c
        tn = _tile(c, 1024)
        tk = _tile(kdim, TILE_K)
        per = c // tn
        b_spec = pl.BlockSpec((None, tk, tn), lambda i, j, k: (j // per, k, j % per))
    elif _whole_rows(wg):
        n = c
        tm, tn, tk = _tile(m, 512), n, kdim
        b_spec = pl.BlockSpec((N_CHIPS, r, tn), lambda i, j, k: (0, 0, j))
    else:
        n = c
        tn = _tile(n, 1024)
        tk = _tile(r, TILE_K)
        per = r // tk
        b_spec = pl.BlockSpec((None, tk, tn), lambda i, j, k: (k // per, k % per, j))
    grid = (m // tm, n // tn, kdim // tk)
    o_spec = pl.BlockSpec((tm, tn), lambda i, j, k: (i, j))
    in_specs = ([pl.BlockSpec((tm, tk), lambda i, j, k: (i, k)), b_spec] + [o_spec] * len(extra)
                + [pl.BlockSpec((1, tn), lambda i, j, k: (0, j))] * len(row_vectors))
    sums_shape = jax.ShapeDtypeStruct((grid[0] * 8, grid[1] * 128), F32)
    sums_spec = pl.BlockSpec((8, 128), lambda i, j, k: (i, j))
    return _matmul(name, (a, wg) + tuple(extra) + tuple(row_vectors), in_specs,
                   tuple(jax.ShapeDtypeStruct((m, n), d) for d in out_dtypes) + (sums_shape,) * tile_sums,
                   tuple(o_spec for _ in out_dtypes) + (sums_spec,) * tile_sums, grid, NN, (tm, tn), epilogue)


def _mm_act_wt(name, a, wg, col_sharded, epilogue, extra=(), out_dtypes=(F32,)):
    m, kdim = a.shape
    _, r, c = wg.shape
    tm = _tile(m, 1024)
    split_k = False
    if col_sharded and N_CHIPS * c <= 2 * TILE_K:
        n = r
        tm, tn, tk, split_k = _tile(m, 512), n, kdim, True
        b_spec = pl.BlockSpec((N_CHIPS, tn, c), lambda i, j, k: (0, j, 0))
    elif col_sharded:
        n = r
        tn = _tile(n, 1024)
        tk = _tile(c, TILE_K)
        per = c // tk
        b_spec = pl.BlockSpec((None, tn, tk), lambda i, j, k: (k // per, j, k % per))
    elif N_CHIPS * r <= TILE_K:
        n = N_CHIPS * r
        tm, tn, tk = _tile(m, 512), n, _tile(c, TILE_K)
        b_spec = pl.BlockSpec((N_CHIPS, r, tk), lambda i, j, k: (0, 0, k))
    else:
        n = N_CHIPS * r
        tn = _tile(r, 1024)
        tk = _tile(c, TILE_K)
        per = r // tn
        b_spec = pl.BlockSpec((None, tn, tk), lambda i, j, k: (j // per, j % per, k))
    grid = (m // tm, n // tn, kdim // tk)
    o_spec = pl.BlockSpec((tm, tn), lambda i, j, k: (i, j))
    in_specs = [pl.BlockSpec((tm, tk), lambda i, j, k: (i, k)), b_spec] + [o_spec] * len(extra)
    return _matmul(name, (a, wg) + tuple(extra), in_specs,
                   tuple(jax.ShapeDtypeStruct((m, n), d) for d in out_dtypes),
                   tuple(o_spec for _ in out_dtypes), grid, NT, (tm, tn), epilogue, split_k)


def _mm_wgrad(name, a, g, col_sharded, c):
    s, kdim = a.shape
    _, n = g.shape
    ts = _tile(s, TILE_K)
    if col_sharded:
        r = kdim
        tm = _tile(kdim, 1024)
        tn = _tile(c, 1024)
        per = c // tn
        o_spec = pl.BlockSpec((None, tm, tn), lambda i, j, k: (j // per, i, j % per))
    else:
        r = kdim // N_CHIPS
        tm = _tile(r, 512)
        tn = _tile(c, 2048)
        per = r // tm
        o_spec = pl.BlockSpec((None, tm, tn), lambda i, j, k: (i // per, i % per, j))
    grid = (kdim // tm, n // tn, s // ts)
    in_specs = [pl.BlockSpec((ts, tm), lambda i, j, k: (k, i)), pl.BlockSpec((ts, tn), lambda i, j, k: (k, j))]
    return _matmul(name, (a, g), in_specs, (jax.ShapeDtypeStruct((N_CHIPS, r, c), BF16),), (o_spec,),
                   grid, TN, (tm, tn), _ep_store)[0]


def _rms_fwd(name, x, g):
    s, d = x.shape
    tb = _tile(s, 256)

    def body(x_ref, g_ref, o_ref):
        xv = x_ref[...]
        r = lax.rsqrt(jnp.mean(xv * xv, axis=-1, keepdims=True) + EPS)
        o_ref[...] = (xv * r * g_ref[...]).astype(BF16)

    return _ordered_call(
        body, name=name, grid=(s // tb,),
        in_specs=[pl.BlockSpec((tb, d), lambda i: (i, 0)), pl.BlockSpec((1, d), lambda i: (0, 0))],
        out_specs=pl.BlockSpec((tb, d), lambda i: (i, 0)),
        out_shape=jax.ShapeDtypeStruct((s, d), BF16), compiler_params=_cp("parallel"),
    )(x, g.reshape(1, d))


def _rms_bwd(name, dh, x, g, dres):
    s, d = x.shape
    tb = _tile(s, 256)

    def body(dh_ref, x_ref, g_ref, dres_ref, dx_ref, dxb_ref, dg_ref):
        i = pl.program_id(0)
        xv = x_ref[...]
        r = lax.rsqrt(jnp.mean(xv * xv, axis=-1, keepdims=True) + EPS)
        xhat = xv * r
        dhv = dh_ref[...]
        dxhat = dhv * g_ref[...]
        dx = dres_ref[...] + r * (dxhat - xhat * jnp.mean(dxhat * xhat, axis=-1, keepdims=True))
        dx_ref[...] = dx
        dxb_ref[...] = dx.astype(BF16)

        @pl.when(i == 0)
        def _():
            dg_ref[...] = jnp.zeros_like(dg_ref)

        dg_ref[...] += _colsum(dhv * xhat)

    row = pl.BlockSpec((tb, d), lambda i: (i, 0))
    vec = pl.BlockSpec((1, d), lambda i: (0, 0))
    return _ordered_call(
        body, name=name, grid=(s // tb,), in_specs=[row, row, vec, row], out_specs=(row, row, vec),
        out_shape=(jax.ShapeDtypeStruct((s, d), F32), jax.ShapeDtypeStruct((s, d), BF16),
                   jax.ShapeDtypeStruct((1, d), F32)),
        compiler_params=_cp("arbitrary"),
    )(dh, x, g.reshape(1, d), dres)


def _loss_head(y, t):
    s, d = y.shape
    tb = _tile(s, 256)

    def body(y_ref, t_ref, dy_ref, dyb_ref, loss_ref, acc):
        i = pl.program_id(0)
        e = y_ref[...] - t_ref[...]
        dy = e * (1.0 / d)
        dy_ref[...] = dy
        dyb_ref[...] = dy.astype(BF16)

        @pl.when(i == 0)
        def _():
            acc[...] = jnp.zeros_like(acc)

        acc[...] += _colsum(e * e)

        @pl.when(i == pl.num_programs(0) - 1)
        def _():
            loss_ref[...] = jnp.sum(acc[...], axis=-1, keepdims=True) * (0.5 / d)

    row = pl.BlockSpec((tb, d), lambda i: (i, 0))
    return _ordered_call(
        body, name="loss_head", grid=(s // tb,), in_specs=[row, row],
        out_specs=(row, row, pl.BlockSpec((1, 1), lambda i: (0, 0))),
        out_shape=(jax.ShapeDtypeStruct((s, d), F32), jax.ShapeDtypeStruct((s, d), BF16),
                   jax.ShapeDtypeStruct((1, 1), F32)),
        scratch_shapes=[pltpu.VMEM((1, d), F32)], compiler_params=_cp("arbitrary"),
    )(y, t)


def _mixnorm_fwd(name, ya, yc, ys, g):
    s, aw = ya.shape
    cw, sw = yc.shape[1], ys.shape[1]
    d = aw + cw + sw
    tb = _tile(s, 256)

    def body(ya_ref, yc_ref, ys_ref, g_ref, o_ref):
        off = 0
        for ref, w in ((ya_ref, aw), (yc_ref, cw), (ys_ref, sw)):
            v = ref[...]
            r = lax.rsqrt(jnp.mean(v * v, axis=-1, keepdims=True) + EPS)
            o_ref[:, off:off + w] = (v * r * g_ref[:, off:off + w]).astype(BF16)
            off += w

    def row(w):
        return pl.BlockSpec((tb, w), lambda i: (i, 0))

    return _ordered_call(
        body, name=name, grid=(s // tb,),
        in_specs=[row(aw), row(cw), row(sw), pl.BlockSpec((1, d), lambda i: (0, 0))], out_specs=row(d),
        out_shape=jax.ShapeDtypeStruct((s, d), BF16), compiler_params=_cp("parallel"),
    )(ya, yc, ys, g.reshape(1, d))


def _mixnorm_bwd(name, dmix, ya, yc, ys, g):
    s, aw = ya.shape
    cw, sw = yc.shape[1], ys.shape[1]
    d = aw + cw + sw
    tb = _tile(s, 256)

    def body(dm_ref, ya_ref, yc_ref, ys_ref, g_ref, dya_ref, dyc_ref, dys_ref, dg_ref):
        i = pl.program_id(0)

        @pl.when(i == 0)
        def _():
            dg_ref[...] = jnp.zeros_like(dg_ref)

        off = 0
        for ref, dref, w in ((ya_ref, dya_ref, aw), (yc_ref, dyc_ref, cw), (ys_ref, dys_ref, sw)):
            v = ref[...]
            r = lax.rsqrt(jnp.mean(v * v, axis=-1, keepdims=True) + EPS)
            vhat = v * r
            dm = dm_ref[:, off:off + w]
            dvhat = dm * g_ref[:, off:off + w]
            dref[...] = r * (dvhat - vhat * jnp.mean(dvhat * vhat, axis=-1, keepdims=True))
            dg_ref[:, off:off + w] += _colsum(dm * vhat)
            off += w

    def row(w):
        return pl.BlockSpec((tb, w), lambda i: (i, 0))

    vec = pl.BlockSpec((1, d), lambda i: (0, 0))
    return _ordered_call(
        body, name=name, grid=(s // tb,), in_specs=[row(d), row(aw), row(cw), row(sw), vec],
        out_specs=(row(aw), row(cw), row(sw), vec),
        out_shape=(jax.ShapeDtypeStruct((s, aw), F32), jax.ShapeDtypeStruct((s, cw), F32),
                   jax.ShapeDtypeStruct((s, sw), F32), jax.ShapeDtypeStruct((1, d), F32)),
        compiler_params=_cp("arbitrary"),
    )(dmix, ya, yc, ys, g.reshape(1, d))


def _head_rms(x):
    r = lax.rsqrt(jnp.mean(x * x, axis=-1, keepdims=True) + EPS)
    return x * r, r


def _attn_mask(n):
    qi = lax.broadcasted_iota(jnp.int32, (GQA * WINDOW, 2 * WINDOW), 0) & (WINDOW - 1)
    sj = lax.broadcasted_iota(jnp.int32, (GQA * WINDOW, 2 * WINDOW), 1)
    rel = qi + WINDOW - sj
    return (rel >= 0) & (rel < WINDOW) & ((sj >= WINDOW) | (n > 0))


def _attn_specs(nq, nkv, nb):
    qspec = pl.BlockSpec((nq, WINDOW, HEAD_DIM), lambda n: (0, n, 0))
    cur = pl.BlockSpec((nkv, WINDOW, HEAD_DIM), lambda n: (0, n, 0))
    prev = pl.BlockSpec((nkv, WINDOW, HEAD_DIM), lambda n: (0, jnp.maximum(n - 1, 0), 0))
    nxt = pl.BlockSpec((nkv, WINDOW, HEAD_DIM), lambda n: (0, jnp.minimum(n + 1, nb - 1), 0))
    gain = pl.BlockSpec((1, HEAD_DIM), lambda n: (0, 0))
    sink = pl.BlockSpec((nq, 1, 128), lambda n: (0, 0, 0))
    return qspec, cur, prev, nxt, gain, sink


def _group_sinks(s_ref, g):
    return jnp.concatenate([jnp.broadcast_to(s_ref[g * GQA + i][:, :1], (WINDOW, 1)) for i in range(GQA)], axis=0)


ATTN_SCALE = HEAD_DIM ** -0.5
assert ATTN_SCALE == 2.0 ** -3


def _attn_probs(qs_b, kn_b, valid, sink):
    logits = jnp.where(valid, _dot(qs_b, kn_b, NT), NEG_INF)
    m = jnp.maximum(jnp.max(logits, axis=-1, keepdims=True), sink)
    p = jnp.exp(logits - m)
    es = jnp.exp(sink - m)
    inv = 1.0 / (jnp.sum(p, axis=-1, keepdims=True) + es)
    return p * inv, es * inv


def _attn_fwd(name, q, k, v, gq, gk, sinks_b):
    nq, s, _ = q.shape
    nkv = k.shape[0]
    nb = s // WINDOW
    qspec, cur, prev, _, gain, sink = _attn_specs(nq, nkv, nb)

    def body(q_ref, kc_ref, kp_ref, vc_ref, vp_ref, gq_ref, gk_ref, s_ref, o_ref):
        gkv = gk_ref[...]
        valid = _attn_mask(pl.program_id(0))
        for g in range(nkv):
            kn_b = jnp.concatenate([_head_rms(kp_ref[g])[0] * gkv, _head_rms(kc_ref[g])[0] * gkv],
                                   axis=0).astype(BF16)
            vv_b = jnp.concatenate([vp_ref[g], vc_ref[g]], axis=0).astype(BF16)
            heads = pl.ds(g * GQA, GQA)
            q4 = q_ref[heads].reshape(GQA * WINDOW, HEAD_DIM)
            qs_b = (_head_rms(q4)[0] * gq_ref[...] * ATTN_SCALE).astype(BF16)
            probs, _ = _attn_probs(qs_b, kn_b, valid, _group_sinks(s_ref, g))
            o_ref[heads] = _dot(probs.astype(BF16), vv_b).reshape(GQA, WINDOW, HEAD_DIM)

    return _ordered_call(
        body, name=name, grid=(nb,), in_specs=[qspec, cur, prev, cur, prev, gain, gain, sink], out_specs=qspec,
        out_shape=jax.ShapeDtypeStruct((nq, s, HEAD_DIM), F32), compiler_params=_cp("parallel"),
    )(q, k, k, v, v, gq.reshape(1, HEAD_DIM), gk.reshape(1, HEAD_DIM), sinks_b)


def _attn_bwd(name, q, k, v, gq, gk, sinks_b, do):
    nq, s, _ = q.shape
    nkv = k.shape[0]
    nb = s // WINDOW
    qspec, cur, prev, _, gain, sink = _attn_specs(nq, nkv, nb)

    def body(q_ref, kc_ref, kp_ref, vc_ref, vp_ref, gq_ref, gk_ref, s_ref, do_ref,
             dq_ref, dkc_ref, dkp_ref, dvc_ref, dvp_ref, dgq_ref, ds_ref):
        n = pl.program_id(0)

        @pl.when(n == 0)
        def _():
            dgq_ref[...] = jnp.zeros_like(dgq_ref)
            ds_ref[...] = jnp.zeros_like(ds_ref)

        gkv = gk_ref[...]
        gqv = gq_ref[...]
        valid = _attn_mask(n)
        dgq = jnp.zeros((1, HEAD_DIM), F32)
        for g in range(nkv):
            kn_b = jnp.concatenate([_head_rms(kp_ref[g])[0] * gkv, _head_rms(kc_ref[g])[0] * gkv],
                                   axis=0).astype(BF16)
            vv_b = jnp.concatenate([vp_ref[g], vc_ref[g]], axis=0).astype(BF16)
            heads = pl.ds(g * GQA, GQA)
            qhat, r = _head_rms(q_ref[heads].reshape(GQA * WINDOW, HEAD_DIM))
            qs = qhat * gqv * ATTN_SCALE
            qs_b = qs.astype(BF16)
            probs, psink = _attn_probs(qs_b, kn_b, valid, _group_sinks(s_ref, g))
            do = do_ref[heads].reshape(GQA * WINDOW, HEAD_DIM)
            do_b = do.astype(BF16)
            dp = _dot(do_b, vv_b, NT)
            delta = jnp.sum(probs * dp, axis=-1, keepdims=True)
            dl_b = (probs * (dp - delta)).astype(BF16)
            sink_term = psink * delta
            for i in range(GQA):
                ds_ref[g * GQA + i] += jnp.broadcast_to(
                    -jnp.sum(sink_term[i * WINDOW:(i + 1) * WINDOW], axis=0, keepdims=True), (1, 128))
            dqn = _dot(dl_b, kn_b) * ATTN_SCALE
            dkn = _dot(qs.T.astype(BF16), dl_b).T
            dvv = _dot(do.T.astype(BF16), probs.astype(BF16)).T
            dgq += _colsum(dqn * qhat)
            dqhat = dqn * gqv
            dq_ref[heads] = (r * (dqhat - qhat * jnp.mean(dqhat * qhat, axis=-1, keepdims=True))).astype(
                BF16).reshape(GQA, WINDOW, HEAD_DIM)
            dkp_ref[g] = dkn[:WINDOW]
            dkc_ref[g] = dkn[WINDOW:]
            dvp_ref[g] = dvv[:WINDOW]
            dvc_ref[g] = dvv[WINDOW:]
        dgq_ref[...] += dgq

    kv_shape = jax.ShapeDtypeStruct((nkv, s, HEAD_DIM), F32)
    return _ordered_call(
        body, name=name, grid=(nb,), in_specs=[qspec, cur, prev, cur, prev, gain, gain, sink, qspec],
        out_specs=(qspec, cur, cur, cur, cur, gain, sink),
        out_shape=(jax.ShapeDtypeStruct((nq, s, HEAD_DIM), BF16), kv_shape, kv_shape, kv_shape, kv_shape,
                   jax.ShapeDtypeStruct((1, HEAD_DIM), F32), jax.ShapeDtypeStruct((nq, 1, 128), F32)),
        compiler_params=_cp("arbitrary"),
    )(q, k, k, v, v, gq.reshape(1, HEAD_DIM), gk.reshape(1, HEAD_DIM), sinks_b, do)


def _attn_bwd_kv(name, k, gk, dkc, dkp, dvc, dvp):
    nkv, s, _ = k.shape
    nb = s // WINDOW
    _, cur, _, nxt, gain, _ = _attn_specs(GQA * nkv, nkv, nb)

    def body(k_ref, gk_ref, dkc_ref, dkp_ref, dvc_ref, dvp_ref, dk_ref, dv_ref, dgk_ref):
        n = pl.program_id(0)

        @pl.when(n == 0)
        def _():
            dgk_ref[...] = jnp.zeros_like(dgk_ref)

        has_next = n < nb - 1
        dgk = jnp.zeros((1, HEAD_DIM), F32)
        for g in range(nkv):
            dkn = dkc_ref[g] + jnp.where(has_next, dkp_ref[g], 0.0)
            dv_ref[g] = (dvc_ref[g] + jnp.where(has_next, dvp_ref[g], 0.0)).astype(BF16)
            khat, r = _head_rms(k_ref[g])
            dgk += _colsum(dkn * khat)
            dkhat = dkn * gk_ref[...]
            dk_ref[g] = (r * (dkhat - khat * jnp.mean(dkhat * khat, axis=-1, keepdims=True))).astype(BF16)
        dgk_ref[...] += dgk

    kv_shape = jax.ShapeDtypeStruct((nkv, s, HEAD_DIM), BF16)
    return _ordered_call(
        body, name=name, grid=(nb,), in_specs=[cur, gain, cur, nxt, cur, nxt], out_specs=(cur, cur, gain),
        out_shape=(kv_shape, kv_shape, jax.ShapeDtypeStruct((1, HEAD_DIM), F32)),
        compiler_params=_cp("arbitrary"),
    )(k, gk.reshape(1, HEAD_DIM), dkc, dkp, dvc, dvp)


SUBLANES = 8


def _fill_shifts(buf, shifts, tb):
    rows = tb + HALO - SUBLANES
    for b in range(1, SUBLANES):
        shifts[b - 1, pl.ds(0, rows), :] = buf[pl.ds(b, rows), :]


def _window(buf, shifts, off, tb):
    b = off % SUBLANES
    return buf[pl.ds(off, tb), :] if b == 0 else shifts[b - 1, pl.ds(off - b, tb), :]


def _conv_recompute(i, a_ref, gt_ref, ap_ref, gp_ref, w_ref, b_ref, hbuf, shifts, tb):
    hbuf[pl.ds(HALO, tb), :] = a_ref[...] * _sigmoid(gt_ref[...])
    tail = ap_ref[pl.ds(tb - HALO, HALO), :] * _sigmoid(gp_ref[pl.ds(tb - HALO, HALO), :])
    hbuf[pl.ds(0, HALO), :] = jnp.where(i > 0, tail, 0.0)
    _fill_shifts(hbuf, shifts, tb)
    acc = jnp.broadcast_to(b_ref[...], a_ref.shape)
    for kk in range(CONV_KERNEL):
        acc = acc + w_ref[pl.ds(kk, 1), :] * _window(hbuf, shifts, HALO - (CONV_KERNEL - 1) + kk, tb)
    return acc


def _layer_norm_stats(c):
    mu = jnp.mean(c, axis=-1, keepdims=True)
    xc = c - mu
    r = lax.rsqrt(jnp.mean(xc * xc, axis=-1, keepdims=True) + EPS)
    return xc * r, r


def _conv_specs(s, cw, tb, a_blk):
    cur = lambda off: pl.BlockSpec((tb, cw), lambda i: (i, a_blk + off))
    prev = lambda off: pl.BlockSpec((tb, cw), lambda i: (jnp.maximum(i - 1, 0), a_blk + off))
    wspec = pl.BlockSpec((HALO, cw), lambda i: (0, 0))
    vec = pl.BlockSpec((1, cw), lambda i: (0, 0))
    row = pl.BlockSpec((tb, cw), lambda i: (i, 0))
    return cur, prev, wspec, vec, row


def _conv_fwd(name, proj, a_blk, w, b, lg, lb):
    s = proj.shape[0]
    cw = w.shape[1]
    tb = _tile(s, 256)
    cur, prev, wspec, vec, row = _conv_specs(s, cw, tb, a_blk)

    def body(a_ref, gt_ref, ap_ref, gp_ref, w_ref, b_ref, lg_ref, lb_ref, y_ref, hbuf, shifts):
        c = _conv_recompute(pl.program_id(0), a_ref, gt_ref, ap_ref, gp_ref, w_ref, b_ref, hbuf, shifts, tb)
        chat, _ = _layer_norm_stats(c)
        z = chat * lg_ref[...] + lb_ref[...]
        y_ref[...] = z * _sigmoid(z)

    return _ordered_call(
        body, name=name, grid=(s // tb,), in_specs=[cur(0), cur(1), prev(0), prev(1), wspec, vec, vec, vec],
        out_specs=row, out_shape=jax.ShapeDtypeStruct((s, cw), F32),
        scratch_shapes=[pltpu.VMEM((tb + HALO, cw), F32), pltpu.VMEM((SUBLANES - 1, tb + HALO, cw), F32)],
        compiler_params=_cp("arbitrary"),
    )(proj, proj, proj, proj, w, b, lg, lb)


def _conv_bwd1(name, proj, a_blk, w, b, lg, lb, dy):
    s = proj.shape[0]
    cw = w.shape[1]
    tb = _tile(s, 256)
    cur, prev, wspec, vec, row = _conv_specs(s, cw, tb, a_blk)

    def body(a_ref, gt_ref, ap_ref, gp_ref, w_ref, b_ref, lg_ref, lb_ref, dy_ref,
             dc_ref, dw_ref, db_ref, dlg_ref, dlb_ref, hbuf, shifts):
        i = pl.program_id(0)

        @pl.when(i == 0)
        def _():
            dw_ref[...] = jnp.zeros_like(dw_ref)
            db_ref[...] = jnp.zeros_like(db_ref)
            dlg_ref[...] = jnp.zeros_like(dlg_ref)
            dlb_ref[...] = jnp.zeros_like(dlb_ref)

        c = _conv_recompute(i, a_ref, gt_ref, ap_ref, gp_ref, w_ref, b_ref, hbuf, shifts, tb)
        chat, r = _layer_norm_stats(c)
        z = chat * lg_ref[...] + lb_ref[...]
        sg = _sigmoid(z)
        dz = dy_ref[...] * (sg + z * sg * (1.0 - sg))
        dlg_ref[...] += _colsum(dz * chat)
        dlb_ref[...] += _colsum(dz)
        dzg = dz * lg_ref[...]
        dc = r * (dzg - jnp.mean(dzg, axis=-1, keepdims=True) - chat * jnp.mean(dzg * chat, axis=-1, keepdims=True))
        dc_ref[...] = dc
        db_ref[...] += _colsum(dc)
        for kk in range(CONV_KERNEL):
            dw_ref[pl.ds(kk, 1), :] += _colsum(dc * _window(hbuf, shifts, HALO - (CONV_KERNEL - 1) + kk, tb))

    return _ordered_call(
        body, name=name, grid=(s // tb,), in_specs=[cur(0), cur(1), prev(0), prev(1), wspec, vec, vec, vec, row],
        out_specs=(row, wspec, vec, vec, vec),
        out_shape=(jax.ShapeDtypeStruct((s, cw), F32), jax.ShapeDtypeStruct((HALO, cw), F32),
                   jax.ShapeDtypeStruct((1, cw), F32), jax.ShapeDtypeStruct((1, cw), F32),
                   jax.ShapeDtypeStruct((1, cw), F32)),
        scratch_shapes=[pltpu.VMEM((tb + HALO, cw), F32), pltpu.VMEM((SUBLANES - 1, tb + HALO, cw), F32)],
        compiler_params=_cp("arbitrary"),
    )(proj, proj, proj, proj, w, b, lg, lb, dy)


def _conv_bwd2(name, proj, a_blk, w, dc):
    s = proj.shape[0]
    cw = w.shape[1]
    tb = _tile(s, 256)
    nblk = s // tb
    cur, _, wspec, _, row = _conv_specs(s, cw, tb, a_blk)
    nxt = pl.BlockSpec((tb, cw), lambda i: (jnp.minimum(i + 1, nblk - 1), 0))

    def body(a_ref, gt_ref, w_ref, dc_ref, dn_ref, o_ref, dbuf, shifts):
        i = pl.program_id(0)
        dbuf[pl.ds(0, tb), :] = dc_ref[...]
        dbuf[pl.ds(tb, HALO), :] = jnp.where(i < nblk - 1, dn_ref[pl.ds(0, HALO), :], 0.0)
        _fill_shifts(dbuf, shifts, tb)
        dh = jnp.zeros((tb, cw), F32)
        for kk in range(CONV_KERNEL):
            dh = dh + w_ref[pl.ds(kk, 1), :] * _window(dbuf, shifts, CONV_KERNEL - 1 - kk, tb)
        sg = _sigmoid(gt_ref[...])
        o_ref[:, 0:cw] = (dh * sg).astype(BF16)
        o_ref[:, cw:2 * cw] = (dh * a_ref[...] * sg * (1.0 - sg)).astype(BF16)

    return _ordered_call(
        body, name=name, grid=(nblk,), in_specs=[cur(0), cur(1), wspec, row, nxt],
        out_specs=pl.BlockSpec((tb, 2 * cw), lambda i: (i, 0)), out_shape=jax.ShapeDtypeStruct((s, 2 * cw), BF16),
        scratch_shapes=[pltpu.VMEM((tb + HALO, cw), F32), pltpu.VMEM((SUBLANES - 1, tb + HALO, cw), F32)],
        compiler_params=_cp("arbitrary"),
    )(proj, proj, w, dc, dc)


def _sgu_common(v_ref, lg_ref, lb_ref, w_ref, bexp_ref, sw):
    vhat, r = _layer_norm_stats(v_ref[...])
    vn_b = (vhat * lg_ref[...] + lb_ref[...]).astype(BF16)
    ii = lax.broadcasted_iota(jnp.int32, (WINDOW, WINDOW), 0)
    jj = lax.broadcasted_iota(jnp.int32, (WINDOW, WINDOW), 1)
    tril = jj <= ii
    head_of = lax.broadcasted_iota(jnp.int32, (WINDOW, sw), 1) // HEAD_DIM
    wts = [jnp.where(tril, w_ref[h], 0.0).astype(BF16) for h in range(sw // HEAD_DIM)]
    sv = bexp_ref[...]
    for h, wt in enumerate(wts):
        sv = sv + jnp.where(head_of == h, _dot(wt, vn_b), 0.0)
    return vhat, r, vn_b, tril, head_of, wts, sv


def _sgu_specs(sw, u_blk):
    nh = sw // HEAD_DIM
    u = pl.BlockSpec((WINDOW, sw), lambda n: (n, u_blk))
    v = pl.BlockSpec((WINDOW, sw), lambda n: (n, u_blk + 1))
    vec = pl.BlockSpec((1, sw), lambda n: (0, 0))
    wspec = pl.BlockSpec((nh, WINDOW, WINDOW), lambda n: (0, 0, 0))
    bspec = pl.BlockSpec((WINDOW, sw), lambda n: (0, 0))
    row = pl.BlockSpec((WINDOW, sw), lambda n: (n, 0))
    return u, v, vec, wspec, bspec, row


def _sgu_fwd(name, proj, u_blk, lg, lb, w, bexp):
    s = proj.shape[0]
    sw = lg.shape[1]
    u, v, vec, wspec, bspec, row = _sgu_specs(sw, u_blk)

    def body(u_ref, v_ref, lg_ref, lb_ref, w_ref, bexp_ref, y_ref):
        sv = _sgu_common(v_ref, lg_ref, lb_ref, w_ref, bexp_ref, sw)[-1]
        y_ref[...] = u_ref[...] * sv

    return _ordered_call(
        body, name=name, grid=(s // WINDOW,), in_specs=[u, v, vec, vec, wspec, bspec], out_specs=row,
        out_shape=jax.ShapeDtypeStruct((s, sw), F32), compiler_params=_cp("parallel"),
    )(proj, proj, lg, lb, w, bexp)


def _sgu_bwd(name, proj, u_blk, lg, lb, w, bexp, dy):
    s = proj.shape[0]
    sw = lg.shape[1]
    nh = sw // HEAD_DIM
    u, v, vec, wspec, bspec, row = _sgu_specs(sw, u_blk)
    dbspec = pl.BlockSpec((nh, WINDOW), lambda n: (0, 0))

    def body(u_ref, v_ref, lg_ref, lb_ref, w_ref, bexp_ref, dy_ref, o_ref, dw_ref, db_ref, dlg_ref, dlb_ref):
        n = pl.program_id(0)

        @pl.when(n == 0)
        def _():
            dw_ref[...] = jnp.zeros_like(dw_ref)
            db_ref[...] = jnp.zeros_like(db_ref)
            dlg_ref[...] = jnp.zeros_like(dlg_ref)
            dlb_ref[...] = jnp.zeros_like(dlb_ref)

        vhat, r, vn_b, tril, head_of, wts, sv = _sgu_common(v_ref, lg_ref, lb_ref, w_ref, bexp_ref, sw)
        dyv = dy_ref[...]
        o_ref[:, 0:sw] = (dyv * sv).astype(BF16)
        ds = dyv * u_ref[...]
        dvn = jnp.zeros((WINDOW, sw), F32)
        for h, wt in enumerate(wts):
            dsm_b = jnp.where(head_of == h, ds, 0.0).astype(BF16)
            dvn = dvn + _dot(wt, dsm_b, TN)
            dw_ref[h] += jnp.where(tril, _dot(dsm_b, vn_b, NT), 0.0)
        hmask = (lax.broadcasted_iota(jnp.int32, (nh, sw), 1) // HEAD_DIM
                 == lax.broadcasted_iota(jnp.int32, (nh, sw), 0)).astype(F32)
        db_ref[...] += lax.dot_general(hmask, ds, NT, precision=lax.Precision.HIGHEST, preferred_element_type=F32)
        dlg_ref[...] += _colsum(dvn * vhat)
        dlb_ref[...] += _colsum(dvn)
        dvg = dvn * lg_ref[...]
        dv = r * (dvg - jnp.mean(dvg, axis=-1, keepdims=True) - vhat * jnp.mean(dvg * vhat, axis=-1, keepdims=True))
        o_ref[:, sw:2 * sw] = dv.astype(BF16)

    return _ordered_call(
        body, name=name, grid=(s // WINDOW,), in_specs=[u, v, vec, vec, wspec, bspec, row],
        out_specs=(pl.BlockSpec((WINDOW, 2 * sw), lambda n: (n, 0)), wspec, dbspec, vec, vec),
        out_shape=(jax.ShapeDtypeStruct((s, 2 * sw), BF16), jax.ShapeDtypeStruct((nh, WINDOW, WINDOW), F32),
                   jax.ShapeDtypeStruct((nh, WINDOW), F32), jax.ShapeDtypeStruct((1, sw), F32),
                   jax.ShapeDtypeStruct((1, sw), F32)),
        compiler_params=_cp("arbitrary"),
    )(proj, proj, lg, lb, w, bexp, dy)


def _adamw_many(name, ws, gs, ms, vs):
    n = len(ws)

    def body(*refs):
        for t in range(n):
            w_ref, g_ref, m_ref, v_ref = (refs[k * n + t] for k in range(4))
            d_ref, nm_ref, nv_ref = (refs[(4 + k) * n + t] for k in range(3))
            gv = g_ref[...]
            mv = ADAM_B1 * m_ref[...] + (1.0 - ADAM_B1) * gv
            vv = ADAM_B2 * v_ref[...] + (1.0 - ADAM_B2) * (gv * gv)
            m_hat = mv / (1.0 - ADAM_B1 ** ADAM_STEP)
            v_hat = vv / (1.0 - ADAM_B2 ** ADAM_STEP)
            d_ref[...] = -ADAM_LR * (m_hat / (jnp.sqrt(v_hat) + ADAM_EPS) + ADAM_WD * w_ref[...])
            nm_ref[...] = mv
            nv_ref[...] = vv

    whole = pl.BlockSpec(memory_space=pltpu.VMEM)
    out = _ordered_call(
        body, name=name, in_specs=[whole] * (4 * n), out_specs=(whole,) * (3 * n),
        out_shape=tuple(jax.ShapeDtypeStruct(w.shape, F32) for w in ws) * 3,
        compiler_params=pltpu.CompilerParams(vmem_limit_bytes=VMEM_LIMIT),
    )(*ws, *gs, *ms, *vs)
    return out[:n], out[n:2 * n], out[2 * n:]


def _route():
    x, y, c = lax.axis_index("x"), lax.axis_index("y"), lax.axis_index("c")
    n1 = (jnp.where(c == 0, 1 - x, x), jnp.where(c == 0, y, 1 - y))
    n2 = (jnp.where(c == 0, x, 1 - x), jnp.where(c == 0, 1 - y, y))
    return x, y, c, n1, n2, (1 - x, 1 - y)


def _cidx(chip):
    return 2 * chip[0] + chip[1]


def _remote(src, dst, sems, k, device):
    send_sems, recv_sems = sems
    return pltpu.make_async_remote_copy(src_ref=src, dst_ref=dst, send_sem=send_sems.at[k], recv_sem=recv_sems.at[k],
                                        device_id=device, device_id_type=MESH)


HBM_SPEC = pl.BlockSpec(memory_space=pltpu.HBM)
SEM_SPEC = pl.BlockSpec(memory_space=pltpu.SEMAPHORE)
DATAFLOW = pltpu.SideEffectType.DATAFLOW_SIDE_EFFECTING


def _exchange_start(name, bufs, n_sems, build):
    n = len(bufs)

    def body(*refs):
        for cp in build(refs[:n], (refs[n], refs[n + 1])):
            cp.start()

    out = _ordered_call(
        body, name=name,
        out_shape=(pltpu.SemaphoreType.DMA((n_sems,)), pltpu.SemaphoreType.DMA((n_sems,)))
        + tuple(pltpu.HBM(b.shape, b.dtype) for b in bufs),
        in_specs=[HBM_SPEC] * n, out_specs=(SEM_SPEC, SEM_SPEC) + (HBM_SPEC,) * n,
        input_output_aliases={i: 2 + i for i in range(n)},
        compiler_params=pltpu.CompilerParams(has_side_effects=DATAFLOW),
    )(*[pltpu.with_memory_space_constraint(b, pltpu.HBM) for b in bufs])
    return dict(name=name, send=out[0], recv=out[1], bufs=list(out[2:2 + n]), build=build)


def _exchange_wait(handle):
    n = len(handle["bufs"])

    def body(*refs):
        for cp in handle["build"](refs[:n], (refs[n], refs[n + 1])):
            cp.wait_send()
            cp.wait_recv()

    return list(_ordered_call(
        body, name=handle["name"] + "_wait", out_shape=tuple(pltpu.HBM(b.shape, b.dtype) for b in handle["bufs"]),
        in_specs=[HBM_SPEC] * n + [SEM_SPEC, SEM_SPEC], out_specs=(HBM_SPEC,) * n,
        input_output_aliases={i: i for i in range(n)},
        compiler_params=pltpu.CompilerParams(has_side_effects=DATAFLOW),
    )(*handle["bufs"], handle["send"], handle["recv"]))


def _cast_place(name, w, l, me_idx, dtype):
    _, r, c = w.shape
    tr = _tile(r, 512)

    def body(me_ref, w_ref, o_ref):
        o_ref[...] = w_ref[...].astype(dtype)

    grid_spec = pltpu.PrefetchScalarGridSpec(
        num_scalar_prefetch=1, grid=(r // tr,),
        in_specs=[pl.BlockSpec((None, tr, c), lambda i, me_ref: (l, i, 0))],
        out_specs=pl.BlockSpec((None, tr, c), lambda i, me_ref: (me_ref[0], i, 0)))
    return _ordered_call(
        body, name=name, grid_spec=grid_spec, out_shape=jax.ShapeDtypeStruct((N_CHIPS, r, c), dtype),
        compiler_params=_cp("arbitrary"),
    )(me_idx, w)


def _my_half(ref, blk, c):
    hr = ref.shape[1] // 2
    return ref.at[blk, pl.ds(c * hr, hr), :]


def _gather_step(entering):
    lens = [len(e) for e in entering]
    flat = [b for e in entering for b in e]

    def build(refs, sems):
        x, y, c, n1, n2, dg = _route()
        me = _cidx((x, y))
        plan = ([(r, (me,), (*n1, c)) for r in refs[:lens[0]]]
                + [(r, (me, _cidx(n1)), (*n2, c)) for r in refs[lens[0]:lens[0] + lens[1]]]
                + [(r, (_cidx(n1), _cidx(n2), _cidx(dg)), (x, y, 1 - c)) for r in refs[lens[0] + lens[1]:]])
        cps = []
        for ref, blocks, peer in plan:
            for blk in blocks:
                cps.append(_remote(_my_half(ref, blk, c), _my_half(ref, blk, c), sems, len(cps), peer))
        return cps

    return flat, lens[0] + 2 * lens[1] + 3 * lens[2], build


RI_C, RI_ME, RI_N2, RI_N1 = 0, 1, 2, 3


def _pair_sum(name, g, sib, route_idx):
    _, rows, cols = g.shape
    hr = rows // 2
    tr = _tile(hr, 512)
    per = hr // tr

    def body(ri, g_ref, s_ref, o_ref):
        o_ref[...] = (g_ref[...].astype(F32) + s_ref[...].astype(F32)).astype(BF16)

    blk = (None, tr, cols)
    grid_spec = pltpu.PrefetchScalarGridSpec(
        num_scalar_prefetch=1, grid=(2, per),
        in_specs=[pl.BlockSpec(blk, lambda j, i, ri: (ri[RI_N1 + j], ri[RI_C] * per + i, 0)),
                  pl.BlockSpec(blk, lambda j, i, ri: (ri[RI_N1 + j], i, 0))],
        out_specs=pl.BlockSpec(blk, lambda j, i, ri: (j, i, 0)))
    return _ordered_call(
        body, name=name, grid_spec=grid_spec, out_shape=jax.ShapeDtypeStruct((2, hr, cols), BF16),
        compiler_params=_cp("parallel", "parallel"),
    )(route_idx, g, sib)


def _sum_stage1(name, g, sib, got, route_idx):
    _, hr, cols = sib.shape
    tr = _tile(hr, 512)
    per = hr // tr

    def body(ri, gm_ref, sm_ref, gn_ref, sn_ref, g0_ref, g1_ref, keep_ref, send_ref):
        keep_ref[...] = (gm_ref[...].astype(F32) + sm_ref[...].astype(F32)) + g0_ref[...].astype(F32)
        send_ref[...] = ((gn_ref[...].astype(F32) + sn_ref[...].astype(F32)) + g1_ref[...].astype(F32)).astype(BF16)

    blk = (None, tr, cols)
    row = pl.BlockSpec((tr, cols), lambda i, ri: (i, 0))

    def mine(which):
        return pl.BlockSpec(blk, lambda i, ri: (ri[which], ri[RI_C] * per + i, 0))

    def theirs(which):
        return pl.BlockSpec(blk, lambda i, ri: (ri[which], i, 0))

    grid_spec = pltpu.PrefetchScalarGridSpec(
        num_scalar_prefetch=1, grid=(per,),
        in_specs=[mine(RI_ME), theirs(RI_ME), mine(RI_N2), theirs(RI_N2),
                  pl.BlockSpec(blk, lambda i, ri: (0, i, 0)), pl.BlockSpec(blk, lambda i, ri: (1, i, 0))],
        out_specs=(row, row))
    return _ordered_call(
        body, name=name, grid_spec=grid_spec,
        out_shape=(jax.ShapeDtypeStruct((hr, cols), F32), jax.ShapeDtypeStruct((hr, cols), BF16)),
        compiler_params=_cp("parallel"),
    )(route_idx, g, sib, g, sib, got, got)


def _sum_stage2(name, keep, got):
    hr, cols = keep.shape
    tr = _tile(hr, 512)

    def body(k_ref, g_ref, o_ref):
        o_ref[...] = k_ref[...] + g_ref[...].astype(F32)

    row = pl.BlockSpec((tr, cols), lambda i: (i, 0))
    return _ordered_call(
        body, name=name, grid=(hr // tr,), in_specs=[row, row], out_specs=row,
        out_shape=jax.ShapeDtypeStruct((hr, cols), F32), compiler_params=_cp("parallel"),
    )(keep, got)


def _reduce_scatter(tag, names, grads, route_idx):
    n = len(grads)
    hrs = [g.shape[1] // 2 for g in grads]

    def empty(t, lead, dtype):
        return lax.empty(lead + (hrs[t], grads[t].shape[2]), dtype)

    def pair_stage(refs, sems):
        x, y, c, n1, n2, dg = _route()
        return [_remote(refs[t].at[:, pl.ds((1 - c) * hrs[t], hrs[t]), :], refs[n + t], sems, t, (x, y, 1 - c))
                for t in range(n)]

    def stage1(refs, sems):
        x, y, c, n1, n2, dg = _route()
        return [_remote(refs[t].at[slot], refs[n + t].at[slot], sems, 2 * t + slot, (*n1, c))
                for t in range(n) for slot in range(2)]

    def stage2(refs, sems):
        x, y, c, n1, n2, dg = _route()
        return [_remote(refs[t], refs[n + t], sems, t, (*n2, c)) for t in range(n)]

    def stage3(refs, sems):
        x, y, c, n1, n2, dg = _route()
        return [_remote(refs[t], refs[n + t], sems, t, (x, y, 1 - c)) for t in range(n)]

    state = {}

    def start():
        state["h"] = _exchange_start(f"rs_pair_{tag}", list(grads) + [empty(t, (N_CHIPS,), BF16) for t in range(n)],
                                     n, pair_stage)

    def pair_done():
        state["pair"] = _exchange_wait(state["h"])
        psum = [_pair_sum(f"rs_psum_{names[t]}", state["pair"][t], state["pair"][n + t], route_idx) for t in range(n)]
        state["h"] = _exchange_start(f"rs_x1_{tag}", psum + [empty(t, (2,), BF16) for t in range(n)], 2 * n, stage1)

    def x1_done():
        out = _exchange_wait(state["h"])
        state["keep"], send = zip(*[_sum_stage1(f"rs_sum1_{names[t]}", state["pair"][t], state["pair"][n + t],
                                                out[n + t], route_idx) for t in range(n)])
        state["h"] = _exchange_start(f"rs_x2_{tag}", list(send) + [empty(t, (), BF16) for t in range(n)], n, stage2)

    def x2_done():
        out = _exchange_wait(state["h"])
        mine = [_sum_stage2(f"rs_sum2_{names[t]}", state["keep"][t], out[n + t]) for t in range(n)]
        state["h"] = _exchange_start(f"rs_half_{tag}", mine + [empty(t, (), F32) for t in range(n)], n, stage3)

    def finish():
        out = _exchange_wait(state["h"])
        return list(zip(out[:n], out[n:]))

    return start, pair_done, x1_done, x2_done, finish


def _adamw_big(name, w, m, v, f, h, l, c_idx, prev):
    n_l, r, cols = w.shape
    hr = r // 2
    tr = _tile(hr, 256)
    per = hr // tr

    def body(c_ref, w_ref, m_ref, v_ref, f_ref, h_ref, *rest):
        g_ref, d_ref, nm_ref, nv_ref = rest[-4:]
        gv = jnp.where(pl.program_id(0) == c_ref[0], f_ref[...], h_ref[...])
        mv = ADAM_B1 * m_ref[...] + (1.0 - ADAM_B1) * gv
        vv = ADAM_B2 * v_ref[...] + (1.0 - ADAM_B2) * (gv * gv)
        m_hat = mv / (1.0 - ADAM_B1 ** ADAM_STEP)
        v_hat = vv / (1.0 - ADAM_B2 ** ADAM_STEP)
        g_ref[...] = gv
        d_ref[...] = -ADAM_LR * (m_hat / (jnp.sqrt(v_hat) + ADAM_EPS) + ADAM_WD * w_ref[...])
        nm_ref[...] = mv
        nv_ref[...] = vv

    big = pl.BlockSpec((None, tr, cols), lambda hf, i, c_ref: (l, hf * per + i, 0))
    fspec = pl.BlockSpec((tr, cols), lambda hf, i, c_ref: (jnp.where(hf == c_ref[0], i, 0), 0))
    hspec = pl.BlockSpec((tr, cols), lambda hf, i, c_ref: (jnp.where(hf == c_ref[0], 0, i), 0))
    grid_spec = pltpu.PrefetchScalarGridSpec(
        num_scalar_prefetch=1, grid=(2, per), in_specs=[big] * 3 + [fspec, hspec] + [ANY] * len(prev),
        out_specs=(big,) * 4)
    return _ordered_call(
        body, name=name, grid_spec=grid_spec, out_shape=(jax.ShapeDtypeStruct(w.shape, F32),) * 4,
        input_output_aliases={6 + k: k for k in range(len(prev))}, compiler_params=_cp("arbitrary", "arbitrary"),
    )(c_idx, w, m, v, f, h, *prev)


def _small_allreduce(buf):
    rows = buf.shape[0]
    hr = rows // 2

    def body(in_ref, out_ref, pair, acc, got1, got2, send_sems, recv_sems):
        x, y, c, n1, n2, dg = _route()
        sems = (send_sems, recv_sems)
        sibling = (x, y, 1 - c)
        mine = pl.ds(pl.multiple_of(c * hr, 8), hr)
        pair[c] = in_ref[...]
        cp = _remote(in_ref, pair.at[c], sems, 0, sibling)
        cp.start()
        cp.wait()
        acc[...] = pair[0, mine, :] + pair[1, mine, :]
        cp = _remote(acc, got1, sems, 1, (*n1, c))
        cp.start()
        cp.wait()
        acc[...] = acc[...] + got1[...]
        cp = _remote(acc, got2, sems, 2, (*n2, c))
        cp.start()
        cp.wait()
        out_ref[mine, :] = acc[...] + got2[...]
        cp = _remote(out_ref.at[mine, :], out_ref.at[mine, :], sems, 3, sibling)
        cp.start()
        cp.wait()

    half = pltpu.VMEM((hr, 128), F32)
    return _ordered_call(
        body, name="small_allreduce", in_specs=[pl.BlockSpec(memory_space=pltpu.VMEM)],
        out_specs=pl.BlockSpec(memory_space=pltpu.VMEM), out_shape=jax.ShapeDtypeStruct((rows, 128), F32),
        scratch_shapes=[pltpu.VMEM((2, rows, 128), F32), half, half, half,
                        pltpu.SemaphoreType.DMA((4,)), pltpu.SemaphoreType.DMA((4,))],
        compiler_params=pltpu.CompilerParams(has_side_effects=True, vmem_limit_bytes=VMEM_LIMIT),
    )(buf)


BIG = ("w_in", "w_out", "w_up", "w_down")
COL_SHARDED = {"w_in": True, "w_out": False, "w_up": True, "w_down": False}
SMALL = ("ln1_g", "q_norm_g", "k_norm_g", "sinks", "conv_w", "conv_b", "conv_ln_g", "conv_ln_b", "sgu_ln_g",
         "sgu_ln_b", "sgu_w", "sgu_b", "out_norm_g", "ln2_g")
WEIGHTS = ("ln1_g", "w_in", "q_norm_g", "k_norm_g", "sinks", "conv_w", "conv_b", "conv_ln_g", "conv_ln_b",
           "sgu_ln_g", "sgu_ln_b", "sgu_w", "sgu_b", "out_norm_g", "w_out", "ln2_g", "w_up", "w_down")
PACK_QUANTUM = 8 * 128
PACK_ROWS = 512


def _pack(arrs):
    parts = []
    for a in arrs:
        f = a.reshape(-1)
        parts.append(jnp.pad(f, (0, -f.shape[0] % PACK_QUANTUM)).reshape(-1, 128))
    rows = sum(p.shape[0] for p in parts)
    parts.append(jnp.zeros((-rows % PACK_ROWS, 128), F32))
    return jnp.concatenate(parts, axis=0)


def _unpack(buf, shapes):
    out, off = [], 0
    for shp in shapes:
        n = 1
        for dd in shp:
            n *= dd
        rows = (n + PACK_QUANTUM - 1) // PACK_QUANTUM * 8
        out.append(buf[off:off + rows].reshape(-1)[:n].reshape(shp))
        off += rows
    return out


def _to_heads(t, nh):
    return t.reshape(t.shape[0], nh, HEAD_DIM).transpose(1, 0, 2)


def _from_heads(t):
    return t.transpose(1, 0, 2).reshape(t.shape[1], t.shape[0] * HEAD_DIM)


def _no_hook(point, carry):
    return carry


def _layer_fwd(l, x, p, wg, hook=_no_hook, target=None):
    d = x.shape[1]
    aw, cw = d // 2, d // 4
    nq = aw // HEAD_DIM
    nkv = nq // GQA
    kvw = nkv * HEAD_DIM
    h1 = _rms_fwd(f"ln1_fwd_{l}", x, p["ln1_g"])
    h1 = hook("fwd_ln1", h1)
    proj = _mm_act_w(f"proj_{l}", h1, wg["w_in"], True, _ep_store)[0]
    proj = hook("fwd_proj", proj)
    q = _to_heads(proj[:, :aw], nq)
    k = _to_heads(proj[:, aw:aw + kvw], nkv)
    v = _to_heads(proj[:, aw + kvw:aw + 2 * kvw], nkv)
    sinks_b = jnp.broadcast_to(p["sinks"][:, None, None], (nq, 1, 128))
    ya = _from_heads(_attn_fwd(f"attn_fwd_{l}", q, k, v, p["q_norm_g"], p["k_norm_g"], sinks_b))
    ya = hook("fwd_attn", ya)
    yc = _conv_fwd(f"conv_fwd_{l}", proj, 3, p["conv_w"], p["conv_b"], p["conv_ln_g"], p["conv_ln_b"])
    ys = _sgu_fwd(f"sgu_fwd_{l}", proj, 5, p["sgu_ln_g"], p["sgu_ln_b"], p["sgu_w"], p["sgu_bexp"])
    mix = _mixnorm_fwd(f"mixnorm_fwd_{l}", ya, yc, ys, p["out_norm_g"])
    mix = hook("fwd_mid", mix)
    if _whole_rows(wg["w_out"]):
        xm, h2 = _mm_act_w(f"out_{l}", mix, wg["w_out"], False, _ep_residual_norm, extra=(x,),
                           out_dtypes=(F32, BF16), row_vectors=(p["ln2_g"].reshape(1, d),))
    else:
        xm = _mm_act_w(f"out_{l}", mix, wg["w_out"], False, _ep_residual, extra=(x,))[0]
        h2 = _rms_fwd(f"ln2_fwd_{l}", xm, p["ln2_g"])
    h2 = hook("fwd_ln2", h2)
    up_b, act_b = _mm_act_w(f"up_{l}", h2, wg["w_up"], True, _ep_up, out_dtypes=(BF16, BF16))
    act_b = hook("fwd_up", act_b)
    saved = dict(x=x, h1=h1, proj=proj, q=q, k=k, v=v, sinks_b=sinks_b, ya=ya, yc=yc, ys=ys, mix=mix, xm=xm, h2=h2,
                 up_b=up_b, act_b=act_b)
    if target is not None:
        return _mm_act_w(f"down_{l}", act_b, wg["w_down"], False, _ep_loss(d), extra=(xm, target),
                         out_dtypes=(F32, BF16), tile_sums=1), saved
    xo = _mm_act_w(f"down_{l}", act_b, wg["w_down"], False, _ep_residual, extra=(xm,))[0]
    xo = hook("fwd_end", xo)
    return xo, saved


def _layer_bwd(l, dxo, dxo_b, p, wg, sv, big, hook=_no_hook):
    d = dxo.shape[1]
    nq = (d // 2) // HEAD_DIM
    small = {}
    dxo_b = hook("bwd_start", dxo_b)
    big["w_down"] = _mm_wgrad(f"dw_down_{l}", sv["act_b"], dxo_b, False, d)
    dup_b = _mm_act_wt(f"dup_{l}", dxo_b, wg["w_down"], False, _ep_dup, extra=(sv["up_b"],), out_dtypes=(BF16,))[0]
    dup_b = hook("bwd_dup", dup_b)
    big["w_up"] = _mm_wgrad(f"dw_up_{l}", sv["h2"], dup_b, True, wg["w_up"].shape[2])
    dh2 = _mm_act_wt(f"dh2_{l}", dup_b, wg["w_up"], True, _ep_store)[0]
    dh2 = hook("bwd_dh2", dh2)
    dxm, dxm_b, small["ln2_g"] = _rms_bwd(f"ln2_bwd_{l}", dh2, sv["xm"], p["ln2_g"], dxo)
    big["w_out"] = _mm_wgrad(f"dw_out_{l}", sv["mix"], dxm_b, False, d)
    dmix = _mm_act_wt(f"dmix_{l}", dxm_b, wg["w_out"], False, _ep_store)[0]
    dya, dyc, dys, small["out_norm_g"] = _mixnorm_bwd(f"mixnorm_bwd_{l}", dmix, sv["ya"], sv["yc"], sv["ys"],
                                                      p["out_norm_g"])
    dya = hook("bwd_mix", dya)
    dq, dkc, dkp, dvc, dvp, small["q_norm_g"], dsink = _attn_bwd(
        f"attn_bwd_{l}", sv["q"], sv["k"], sv["v"], p["q_norm_g"], p["k_norm_g"], sv["sinks_b"], _to_heads(dya, nq))
    dkc = hook("bwd_attn", dkc)
    small["sinks"] = dsink[:, 0, 0]
    dk, dv, small["k_norm_g"] = _attn_bwd_kv(f"attn_bwd_kv_{l}", sv["k"], p["k_norm_g"], dkc, dkp, dvc, dvp)
    dc, dcw, small["conv_b"], small["conv_ln_g"], small["conv_ln_b"] = _conv_bwd1(
        f"conv_bwd1_{l}", sv["proj"], 3, p["conv_w"], p["conv_b"], p["conv_ln_g"], p["conv_ln_b"], dyc)
    small["conv_w"] = dcw[:CONV_KERNEL]
    dxc_b = _conv_bwd2(f"conv_bwd2_{l}", sv["proj"], 3, p["conv_w"], dc)
    dxs_b, small["sgu_w"], small["sgu_b"], small["sgu_ln_g"], small["sgu_ln_b"] = _sgu_bwd(
        f"sgu_bwd_{l}", sv["proj"], 5, p["sgu_ln_g"], p["sgu_ln_b"], p["sgu_w"], p["sgu_bexp"], dys)
    dxs_b = hook("bwd_sgu", dxs_b)
    dproj_b = jnp.concatenate([_from_heads(dq), _from_heads(dk), _from_heads(dv), dxc_b, dxs_b], axis=1)
    big["w_in"] = _mm_wgrad(f"dw_in_{l}", sv["h1"], dproj_b, True, wg["w_in"].shape[2])
    dh1 = _mm_act_wt(f"dh1_{l}", dproj_b, wg["w_in"], True, _ep_store)[0]
    dx, dx_b, small["ln1_g"] = _rms_bwd(f"ln1_bwd_{l}", dh1, sv["x"], p["ln1_g"], dxm)
    dx_b = hook("bwd_end", dx_b)
    return dx, dx_b, small


def kernel(x, ln1_g, w_in, q_norm_g, k_norm_g, sinks, conv_w, conv_b, conv_ln_g, conv_ln_b, sgu_ln_g, sgu_ln_b, sgu_w, sgu_b, out_norm_g, w_out, ln2_g, w_up, w_down, loss_target, m_ln1_g, m_w_in, m_q_norm_g, m_k_norm_g, m_sinks, m_conv_w, m_conv_b, m_conv_ln_g, m_conv_ln_b, m_sgu_ln_g, m_sgu_ln_b, m_sgu_w, m_sgu_b, m_out_norm_g, m_w_out, m_ln2_g, m_w_up, m_w_down, v_ln1_g, v_w_in, v_q_norm_g, v_k_norm_g, v_sinks, v_conv_w, v_conv_b, v_conv_ln_g, v_conv_ln_b, v_sgu_ln_g, v_sgu_ln_b, v_sgu_w, v_sgu_b, v_out_norm_g, v_w_out, v_ln2_g, v_w_up, v_w_down):
    given = dict(locals())
    _LAST[0] = None
    n_layers = ln1_g.shape[0]
    s, d = x.shape[1], x.shape[2]
    cw = d // 4
    xi, yi, core = lax.axis_index("x"), lax.axis_index("y"), lax.axis_index("c")
    chip = 2 * xi + yi
    first_partner = jnp.where(core == 0, 2 * (1 - xi) + yi, 2 * xi + (1 - yi))
    second_partner = jnp.where(core == 0, 2 * xi + (1 - yi), 2 * (1 - xi) + yi)
    route_idx = jnp.stack([core, chip, second_partner, first_partner, 3 - chip]).astype(jnp.int32)

    conv_w_pad = jnp.pad(conv_w, ((0, 0), (0, HALO - CONV_KERNEL), (0, 0))).reshape(1, n_layers * HALO, -1)
    cwl = conv_w_pad.shape[2]
    buf = {}

    def place(key):
        if key == "conv_w":
            buf[key] = _cast_place("place_conv_w", conv_w_pad, 0, route_idx[1:2], F32)
        else:
            buf[key] = _cast_place(f"place_{key[0]}_{key[1]}", given[key[0]], key[1], route_idx[1:2], BF16)

    groups = [["conv_w", ("w_in", 0)]] + [[(nm, l)] for l in range(n_layers) for nm in BIG if (nm, l) != ("w_in", 0)]
    n_steps = len(groups) + 2
    pending = {}

    def start_step(st):
        keys = [groups[st - j] if 0 <= st - j < len(groups) else [] for j in range(3)]
        flat, n_sems, build = _gather_step([[buf[k] for k in ks] for ks in keys])
        pending["keys"] = [k for ks in keys for k in ks]
        pending["h"] = _exchange_start(f"gather_step{st}", flat, n_sems, build)

    def wait_step():
        for k, b in zip(pending["keys"], _exchange_wait(pending["h"])):
            buf[k] = b

    later = [k for grp in groups[1:] for k in grp]
    for k in groups[0]:
        place(k)
    for st, upto in enumerate((2, 5, len(later))):
        start_step(st)
        for k in later[:upto]:
            if k not in buf:
                place(k)
        if st < 2:
            wait_step()

    def first_group_done():
        conv_w_full = buf["conv_w"].reshape(N_CHIPS, n_layers, HALO, cwl).transpose(1, 2, 0, 3).reshape(
            n_layers, HALO, cw)
        for l in range(n_layers):
            params[l]["conv_w"] = conv_w_full[l]

    class LayerWeights:
        def __init__(self, l):
            self.l = l

        def __getitem__(self, nm):
            return buf[(nm, self.l)]

    wgs = [LayerWeights(l) for l in range(n_layers)]
    fwd_points = [(l, pt) for l in range(n_layers) for pt in ("fwd_ln1", "fwd_attn", "fwd_ln2", "fwd_up", "fwd_end")
                  if (pt != "fwd_ln1" or l == 0) and (pt != "fwd_end" or l + 1 < n_layers)]
    assert len(fwd_points) == n_steps - 3 + 1, "one hook point per pipeline step, and one to wait for the last"
    fwd_tables = [{} for _ in range(n_layers)]
    for i, (l, pt) in enumerate(fwd_points):
        fwd_tables[l].setdefault(pt, []).append(wait_step)
        if i == 0:
            fwd_tables[l][pt].append(first_group_done)
        if 3 + i < n_steps:
            fwd_tables[l][pt].append(functools.partial(start_step, 3 + i))
    params = []
    for l in range(n_layers):
        p = {nm: given[nm][l] for nm in SMALL if nm != "conv_w"}
        for nm in ("conv_b", "conv_ln_g", "conv_ln_b", "sgu_ln_g", "sgu_ln_b"):
            p[nm] = p[nm].reshape(1, -1)
        p["sgu_bexp"] = jnp.repeat(sgu_b[l].T, HEAD_DIM, axis=1)
        params.append(p)

    def make_hook(table):
        def hook(point, carry):
            for fn in table.get(point, ()):
                fn()
            return carry
        return hook

    h = x.reshape(s, d)
    saved = []
    for l in range(n_layers):
        h, sv = _layer_fwd(l, h, params[l], wgs[l], make_hook(fwd_tables[l]),
                           loss_target.reshape(s, d) if l + 1 == n_layers else None)
        saved.append(sv)
    dh, dh_b, tile_sums = h
    loss = lax.psum(jnp.sum(tile_sums[::8, ::128]) * (0.5 / d), ("x", "y", "c"))

    big_grads = [{} for _ in range(n_layers)]
    small_grads = [None] * n_layers
    halves = {}

    def rs_group(tag, l, names):
        phases = {}

        def start():
            phases["p"] = _reduce_scatter(tag, [f"{nm}_{l}" for nm in names], [big_grads[l][nm] for nm in names],
                                          route_idx)
            phases["p"][0]()

        def step(k):
            return lambda: phases["p"][k]()

        def finish():
            for nm, fh in zip(names, phases["p"][4]()):
                halves[(nm, l)] = fh

        return [start, step(1), step(2), step(3), finish]

    early = rs_group("l0a", 0, ("w_down", "w_up", "w_out"))
    for l in reversed(range(n_layers)):
        table = {}
        if l + 1 < n_layers:
            above = rs_group(f"l{l + 1}", l + 1, BIG)
            for point, fn in zip(("bwd_start", "bwd_dup", "bwd_mix", "bwd_attn", "bwd_sgu"), above):
                table.setdefault(point, []).append(fn)
        if l == 0:
            for point, fn in zip(("bwd_mix", "bwd_attn", "bwd_end"), early[:3]):
                table.setdefault(point, []).append(fn)
        dh, dh_b, small_grads[l] = _layer_bwd(l, dh, dh_b, params[l], wgs[l], saved[l], big_grads[l],
                                              make_hook(table))
    grad_x = dh.reshape(x.shape)

    grads, delta, new_m, new_v = {}, {}, {}, {}
    adam_state = {nm: () for nm in BIG}

    def adam(nm, l):
        f, h = halves[(nm, l)]
        adam_state[nm] = _adamw_big(f"adamw_{nm}_{l}", given[nm], given["m_" + nm], given["v_" + nm], f, h, l,
                                    route_idx[0:1], adam_state[nm])

    def small_update():
        small_shapes = [(n_layers,) + small_grads[0][nm].shape for nm in SMALL]
        small_sum = _small_allreduce(_pack([jnp.stack([small_grads[l][nm] for l in range(n_layers)])
                                            for nm in SMALL]))
        for nm, g in zip(SMALL, _unpack(small_sum, small_shapes)):
            grads[nm] = g.reshape((n_layers,) + given[nm].shape[1:]) if nm != "conv_w" else g
        grads["conv_w"] = lax.dynamic_slice_in_dim(grads["conv_w"], chip * cwl, cwl, axis=2)
        results = _adamw_many("adamw_small", [given[nm] for nm in SMALL], [grads[nm] for nm in SMALL],
                              [given["m_" + nm] for nm in SMALL], [given["v_" + nm] for nm in SMALL])
        for dst, arrs in zip((delta, new_m, new_v), results):
            dst.update(zip(SMALL, arrs))

    upper = [(nm, l) for l in reversed(range(1, n_layers)) for nm in reversed(BIG)]
    late = rs_group("l0b", 0, ("w_in",))
    late[0]()
    for task in upper[:1]:
        adam(*task)
    late[1]()
    for task in upper[1:]:
        adam(*task)
    early[3]()
    late[2]()
    small_update()
    early[4]()
    for nm in ("w_down", "w_up"):
        adam(nm, 0)
    late[3]()
    adam("w_out", 0)
    late[4]()
    adam("w_in", 0)
    for nm in BIG:
        grads[nm], delta[nm], new_m[nm], new_v[nm] = adam_state[nm]
    return (loss, grad_x, *[grads[nm] for nm in WEIGHTS], *[delta[nm] for nm in WEIGHTS],
            *[new_m[nm] for nm in WEIGHTS], *[new_v[nm] for nm in WEIGHTS])
```
